```python
import math
import jax, jax.numpy as jnp
from jax import lax
import numpy as np

D_MODEL = 1024
BATCH = 16
SEQ = 2048
DEPTH = 2

HEAD_DIM = 64
A_HEADS = 4
A_QK = 2 * HEAD_DIM
A_V = 2 * HEAD_DIM
A_W = A_HEADS * A_V
B_GROUPS = ((128, 1), (512, 4), (2048, 16))
B_HEADS = 6
B_QW = len(B_GROUPS) * B_HEADS * HEAD_DIM
B_W = B_HEADS * HEAD_DIM
C_HEADS = 6
C_W = C_HEADS * HEAD_DIM
N_BRANCH = 3
BLOCK = 128
N_BUCKETS = 32
REL_MAX_DIST = 2048
N_BIAS_HEADS = 2 * A_HEADS + len(B_GROUPS) * B_HEADS
D_FF = 2816
N_EXPERTS = 8
TOP_K = 2
D_FF_EXPERT = 3584
N_DENSE = (DEPTH + 1) // 2
N_MOE = DEPTH // 2
EPS = 1e-6
IN_SIZES = (A_HEADS * A_QK, A_HEADS * A_QK, A_W, B_QW, B_QW, B_QW, C_W, C_W, C_W, C_HEADS, N_BRANCH * D_MODEL)
N_IN = sum(IN_SIZES)

kernel_name = 'hybrid_diff_dilated_fox_moe_block'


def rms_norm(x, g):
    xf = x.astype(jnp.float32)
    y = xf * lax.rsqrt(jnp.mean(xf * xf, axis=-1, keepdims=True) + EPS)
    return (y * g.astype(jnp.float32)).astype(x.dtype)


def t5_bucket(dist):
    n = jnp.maximum(dist, 0)
    max_exact = N_BUCKETS // 2
    nf = jnp.maximum(n, 1).astype(jnp.float32)
    large = max_exact + (jnp.log(nf / max_exact) / math.log(REL_MAX_DIST / max_exact)
                         * (N_BUCKETS - max_exact)).astype(jnp.int32)
    large = jnp.minimum(large, N_BUCKETS - 1)
    return jnp.where(n < max_exact, n, large)


def diff_attention(q, k, v, lam, bias_tab):
    S = q.shape[1]
    H = q.shape[2]
    scale = HEAD_DIM ** -0.5
    outs = []
    for i in range(S // BLOCK):
        q0 = i * BLOCK
        L = q0 + BLOCK
        s = jnp.einsum('bqhmd,bkhmd->bhmqk', q[:, q0:L], k[:, :L]).astype(jnp.float32) * scale
        dist = (q0 + jnp.arange(BLOCK))[:, None] - jnp.arange(L)[None, :]
        bias = bias_tab[t5_bucket(dist)].reshape(BLOCK, L, H, 2).transpose(2, 3, 0, 1)
        s = jnp.where(dist >= 0, s + bias.astype(jnp.float32), -jnp.inf)
        p = jax.nn.softmax(s, axis=-1)
        w = p[:, :, 0] - lam * p[:, :, 1]
        outs.append(jnp.einsum('bhqk,bkhe->bqhe', w.astype(v.dtype), v[:, :L]))
    return jnp.concatenate(outs, axis=1)


def forgetting_attention(q, k, v, log_f):
    S = q.shape[1]
    scale = HEAD_DIM ** -0.5
    F = jnp.moveaxis(jnp.cumsum(log_f, axis=1), 1, 2)
    outs = []
    for i in range(S // BLOCK):
        q0 = i * BLOCK
        L = q0 + BLOCK
        s = jnp.einsum('bqhd,bkhd->bhqk', q[:, q0:L], k[:, :L]).astype(jnp.float32) * scale
        decay = F[:, :, q0:L, None] - F[:, :, None, :L]
        causal = (q0 + jnp.arange(BLOCK))[:, None] >= jnp.arange(L)[None, :]
        s = jnp.where(causal, s + decay, -jnp.inf)
        p = jax.nn.softmax(s, axis=-1)
        outs.append(jnp.einsum('bhqk,bkhd->bqhd', p.astype(v.dtype), v[:, :L]))
    return jnp.concatenate(outs, axis=1)


def dilated_attention(q, k, v, bias_tab):
    B_, S = q.shape[0], q.shape[1]
    scale = HEAD_DIM ** -0.5
    outs, lses = [], []
    for g, (window, dil) in enumerate(B_GROUPS):
        n_back = window // dil
        M = S // dil
        nblk = -(-M // BLOCK)
        Mp = nblk * BLOCK

        def strided(t):
            return jnp.moveaxis(t.reshape(B_, M, dil, B_HEADS, HEAD_DIM), 2, 1)

        def blk(t):
            return t.reshape(B_, dil, nblk, BLOCK, B_HEADS, HEAD_DIM)

        qg, kg, vg = strided(q[:, :, g]), strided(k[:, :, g]), strided(v[:, :, g])
        qblk = blk(jnp.pad(qg, ((0, 0), (0, 0), (0, Mp - M), (0, 0), (0, 0))))
        kp = jnp.pad(kg, ((0, 0), (0, 0), (BLOCK, Mp - M), (0, 0), (0, 0)))
        vp = jnp.pad(vg, ((0, 0), (0, 0), (BLOCK, Mp - M), (0, 0), (0, 0)))
        kblk = jnp.concatenate([blk(kp[:, :, :Mp]), blk(kp[:, :, BLOCK:])], axis=3)
        vblk = jnp.concatenate([blk(vp[:, :, :Mp]), blk(vp[:, :, BLOCK:])], axis=3)
        s = jnp.einsum('brnqhd,brnkhd->brnhqk', qblk, kblk).astype(jnp.float32) * scale
        qi = jnp.arange(BLOCK)[:, None]
        kj = jnp.arange(2 * BLOCK)[None, :]
        dm = qi + BLOCK - kj
        key_m = (jnp.arange(nblk)[:, None, None] - 1) * BLOCK + kj[None]
        valid = (dm >= 0) & (dm <= n_back) & (key_m >= 0)
        bias = bias_tab[:, g * B_HEADS:(g + 1) * B_HEADS][t5_bucket(dm * dil)]
        bias = jnp.moveaxis(bias, -1, 0).astype(jnp.float32)
        s = jnp.where(valid[None, None, :, None], s + bias, -jnp.inf)
        m = jnp.max(s, axis=-1, keepdims=True)
        e = jnp.exp(s - m)
        den = jnp.sum(e, axis=-1)
        o = jnp.einsum('brnhqk,brnkhd->brnqhd', e.astype(v.dtype), vblk).astype(jnp.float32)
        o = o / jnp.moveaxis(den, -1, -2)[..., None]
        lse = jnp.moveaxis(m[..., 0] + jnp.log(den), -1, -2)

        def unstrided(t):
            t = t.reshape(B_, dil, Mp, *t.shape[4:])[:, :, :M]
            return jnp.moveaxis(t, 1, 2).reshape(B_, S, *t.shape[3:])

        outs.append(unstrided(o))
        lses.append(unstrided(lse))
    alpha = jax.nn.softmax(jnp.stack(lses), axis=0)
    return jnp.sum(alpha[..., None] * jnp.stack(outs), axis=0)


def hybrid_mixer(h, w_in, b_f, lq1, lk1, lq2, lk2, subln_g, lam_init, rel_bias,
                 w_br_a, w_br_b, w_br_c, w_out):
    B_, S, _ = h.shape
    proj = h @ w_in
    offsets = np.cumsum(IN_SIZES)[:-1].tolist()
    qa, ka, va, qb, kb, vb, qc, kc, vc, f_logit, gate_logit = jnp.split(proj, offsets, axis=-1)
    lam = (jnp.exp(jnp.sum((lq1 * lk1).astype(jnp.float32)))
           - jnp.exp(jnp.sum((lq2 * lk2).astype(jnp.float32))) + lam_init)
    oa = diff_attention(qa.reshape(B_, S, A_HEADS, 2, HEAD_DIM), ka.reshape(B_, S, A_HEADS, 2, HEAD_DIM),
                        va.reshape(B_, S, A_HEADS, A_V), lam, rel_bias[:, :2 * A_HEADS])
    oa = (rms_norm(oa, subln_g) * (1.0 - lam_init)).reshape(B_, S, A_W)
    G = len(B_GROUPS)
    ob = dilated_attention(qb.reshape(B_, S, G, B_HEADS, HEAD_DIM), kb.reshape(B_, S, G, B_HEADS, HEAD_DIM),
                           vb.reshape(B_, S, G, B_HEADS, HEAD_DIM), rel_bias[:, 2 * A_HEADS:])
    ob = ob.astype(h.dtype).reshape(B_, S, B_W)
    log_f = jax.nn.log_sigmoid(f_logit.astype(jnp.float32) + b_f.astype(jnp.float32))
    oc = forgetting_attention(qc.reshape(B_, S, C_HEADS, HEAD_DIM), kc.reshape(B_, S, C_HEADS, HEAD_DIM),
                              vc.reshape(B_, S, C_HEADS, HEAD_DIM), log_f).reshape(B_, S, C_W)
    gates = jax.nn.sigmoid(gate_logit).reshape(B_, S, N_BRANCH, D_MODEL)
    merged = (gates[:, :, 0] * (oa @ w_br_a) + gates[:, :, 1] * (ob @ w_br_b)
              + gates[:, :, 2] * (oc @ w_br_c))
    return merged @ w_out


def swiglu(h, wg, wu, wd):
    return (jax.nn.silu(h @ wg) * (h @ wu)) @ wd


def moe_swiglu(h, w_router, wg, wu, wd):
    logits = (h @ w_router).astype(jnp.float32)
    top_val, top_idx = lax.top_k(logits, TOP_K)
    top_w = jax.nn.softmax(top_val, axis=-1)
    gate = jnp.sum(jax.nn.one_hot(top_idx, N_EXPERTS, dtype=jnp.float32) * top_w[..., None], axis=-2)
    out = jnp.zeros_like(h)
    for e in range(N_EXPERTS):
        out = out + gate[..., e:e + 1].astype(h.dtype) * swiglu(h, wg[e], wu[e], wd[e])
    return out


def setup_inputs(seed: int = 0) -> dict:
    key = jax.random.key(seed)
    ks = jax.random.split(key, 26)

    def nrm(k, shape, s):
        return jax.random.normal(k, shape, jnp.float32) * s

    D = D_MODEL
    return {
        'x': nrm(ks[0], (BATCH, SEQ, D), 1.0),
        'c': nrm(ks[1], (BATCH, D), 1.0),
        'norm_mix_g': 1.0 + nrm(ks[2], (DEPTH, D), 0.02),
        'norm_ffn_g': 1.0 + nrm(ks[3], (DEPTH, D), 0.02),
        'w_mod': nrm(ks[4], (DEPTH, D, 6 * D), 0.5 * D ** -0.5),
        'b_mod': nrm(ks[5], (DEPTH, 6 * D), 0.02),
        'w_in': nrm(ks[6], (DEPTH, D, N_IN), D ** -0.5),
        'b_forget': 2.0 + nrm(ks[7], (DEPTH, C_HEADS), 0.5),
        'lam_q1': nrm(ks[8], (DEPTH, HEAD_DIM), 0.1),
        'lam_k1': nrm(ks[9], (DEPTH, HEAD_DIM), 0.1),
        'lam_q2': nrm(ks[10], (DEPTH, HEAD_DIM), 0.1),
        'lam_k2': nrm(ks[11], (DEPTH, HEAD_DIM), 0.1),
        'subln_g': 1.0 + nrm(ks[12], (DEPTH, A_V), 0.02),
        'rel_bias': nrm(ks[13], (N_BUCKETS, N_BIAS_HEADS), 0.5),
        'w_br_a': nrm(ks[14], (DEPTH, A_W, D), A_W ** -0.5),
        'w_br_b': nrm(ks[15], (DEPTH, B_W, D), B_W ** -0.5),
        'w_br_c': nrm(ks[16], (DEPTH, C_W, D), C_W ** -0.5),
        'w_out': nrm(ks[17], (DEPTH, D, D), D ** -0.5),
        'w_ff_gate': nrm(ks[18], (N_DENSE, D, D_FF), D ** -0.5),
        'w_ff_up': nrm(ks[19], (N_DENSE, D, D_FF), D ** -0.5),
        'w_ff_down': nrm(ks[20], (N_DENSE, D_FF, D), D_FF ** -0.5),
        'w_router': nrm(ks[21], (N_MOE, D, N_EXPERTS), D ** -0.5),
        'w_exp_gate': nrm(ks[22], (N_MOE, N_EXPERTS, D, D_FF_EXPERT), D ** -0.5),
        'w_exp_up': nrm(ks[23], (N_MOE, N_EXPERTS, D, D_FF_EXPERT), D ** -0.5),
        'w_exp_down': nrm(ks[24], (N_MOE, N_EXPERTS, D_FF_EXPERT, D), D_FF_EXPERT ** -0.5),
        'final_norm_g': 1.0 + nrm(ks[25], (D,), 0.02),
    }


def reference(x, c, norm_mix_g, norm_ffn_g, w_mod, b_mod, w_in, b_forget, lam_q1, lam_k1, lam_q2,
              lam_k2, subln_g, rel_bias, w_br_a, w_br_b, w_br_c, w_out, w_ff_gate, w_ff_up, w_ff_down,
              w_router, w_exp_gate, w_exp_up, w_exp_down, final_norm_g):
    for l in range(DEPTH):
        lam_init = 0.8 - 0.6 * math.exp(-0.3 * l)
        mod = jax.nn.silu(c) @ w_mod[l] + b_mod[l]
        sh1, sc1, g1, sh2, sc2, g2 = jnp.split(mod[:, None, :], 6, axis=-1)
        h = rms_norm(x, norm_mix_g[l]) * (1.0 + sc1) + sh1
        x = x + g1 * hybrid_mixer(h, w_in[l], b_forget[l], lam_q1[l], lam_k1[l], lam_q2[l], lam_k2[l],
                                  subln_g[l], lam_init, rel_bias, w_br_a[l], w_br_b[l], w_br_c[l], w_out[l])
        h = rms_norm(x, norm_ffn_g[l]) * (1.0 + sc2) + sh2
        if l % 2 == 0:
            f = swiglu(h, w_ff_gate[l // 2], w_ff_up[l // 2], w_ff_down[l // 2])
        else:
            f = moe_swiglu(h, w_router[l // 2], w_exp_gate[l // 2], w_exp_up[l // 2], w_exp_down[l // 2])
        x = x + g2 * f
    return rms_norm(x, final_norm_g)
```

```python
import functools
import math

import jax
import jax.numpy as jnp
from jax import lax
from jax.experimental import pallas as pl
from jax.experimental.pallas import tpu as pltpu

F32 = jnp.float32
BF16 = jnp.bfloat16

D_MODEL = 1024
HEAD_DIM = 64
LANES = 128
A_HEADS = 4
A_W = A_HEADS * 2 * HEAD_DIM
B_GROUPS = ((128, 1), (512, 4), (2048, 16))
B_HEADS = 6
B_W = B_HEADS * HEAD_DIM
B_QW = len(B_GROUPS) * B_W
C_HEADS = 6
C_W = C_HEADS * HEAD_DIM
N_BRANCH = 3
BLOCK = 128
N_BUCKETS = 32
REL_MAX_DIST = 2048
N_EXPERTS = 8
EPS = 1e-6
QKV_W = 3 * A_W + 3 * B_QW + 3 * C_W
GATE_W = N_BRANCH * D_MODEL
NEG = -1e30
SCALE = HEAD_DIM ** -0.5
VMEM_LIMIT = 56 * 1024 * 1024


def _params(*sem):
    return pltpu.CompilerParams(dimension_semantics=sem, vmem_limit_bytes=VMEM_LIMIT)


def _rms_mod(x, g, sc, sh):
    y = x * lax.rsqrt(jnp.mean(x * x, axis=-1, keepdims=True) + EPS)
    return (y * g) * (1.0 + sc) + sh


def _dot(a, b):
    return jnp.dot(a, b, preferred_element_type=F32)


def _dot_nt(a, b):
    return lax.dot_general(a, b, (((1,), (1,)), ((), ())), preferred_element_type=F32)


def _silu(a):
    return a * jax.nn.sigmoid(a)


def _mod_kernel(c_ref, w_ref, b_ref, o_ref):
    a = _silu(c_ref[...]).astype(BF16)
    o_ref[0] = _dot(a, w_ref[0].astype(BF16)) + b_ref[0]


def _modulation(c, w_mod, b_mod):
    depth, d, n = w_mod.shape
    bsz = c.shape[0]
    tn = 1536
    return pl.pallas_call(
        _mod_kernel,
        grid=(depth, n // tn),
        in_specs=[
            pl.BlockSpec((bsz, d), lambda l, j: (0, 0)),
            pl.BlockSpec((1, d, tn), lambda l, j: (l, 0, j)),
            pl.BlockSpec((1, 1, tn), lambda l, j: (l, 0, j)),
        ],
        out_specs=pl.BlockSpec((1, bsz, tn), lambda l, j: (l, 0, j)),
        out_shape=jax.ShapeDtypeStruct((depth, bsz, n), F32),
        compiler_params=_params("parallel", "parallel"),
        name="modulation",
    )(c, w_mod, b_mod.reshape(depth, 1, n))


def _inproj_kernel(x_ref, mod_ref, g_ref, w_ref, o_ref, h_scr, *, n_sigmoid, tn):
    j = pl.program_id(1)

    @pl.when(j == 0)
    def _():
        m = mod_ref[0]
        h_scr[...] = _rms_mod(x_ref[...], g_ref[...], m[1:2], m[0:1]).astype(BF16)

    y = _dot(h_scr[...], w_ref[...])
    if n_sigmoid:
        col = j * tn + lax.broadcasted_iota(jnp.int32, y.shape, 1)
        y = jnp.where(col < n_sigmoid, jax.nn.sigmoid(y), y)
    o_ref[...] = y.astype(o_ref.dtype)


def _inproj(x2d, mod3, g, w, *, seq, tm, tn, out_dtype, n_sigmoid=0):
    rows, d = x2d.shape
    n = w.shape[1]
    per_b = seq // tm
    return pl.pallas_call(
        functools.partial(_inproj_kernel, n_sigmoid=n_sigmoid, tn=tn),
        grid=(rows // tm, n // tn),
        in_specs=[
            pl.BlockSpec((tm, d), lambda i, j: (i, 0)),
            pl.BlockSpec((1, 6, d), lambda i, j: (i // per_b, 0, 0)),
            pl.BlockSpec((1, d), lambda i, j: (0, 0)),
            pl.BlockSpec((d, tn), lambda i, j: (0, j)),
        ],
        out_specs=pl.BlockSpec((tm, tn), lambda i, j: (i, j)),
        out_shape=jax.ShapeDtypeStruct((rows, n), out_dtype),
        scratch_shapes=[pltpu.VMEM((tm, d), BF16)],
        compiler_params=_params("parallel", "arbitrary"),
        name="inproj",
    )(x2d, mod3, g, w)


def _fcum_kernel(f_ref, b_ref, o_ref):
    z = f_ref[0].T[:8] + b_ref[...]
    x = jnp.minimum(z, 0.0) - jnp.log1p(jnp.exp(-jnp.abs(z)))
    s = x.shape[1]
    lane = lax.broadcasted_iota(jnp.int32, x.shape, 1)
    k = 1
    while k < s:
        x = x + jnp.where(lane >= k, pltpu.roll(x, k, 1), 0.0)
        k *= 2
    o_ref[0] = x


def _forget_cumsum(gf, b_f8, *, col_block):
    bsz, seq, _ = gf.shape
    return pl.pallas_call(
        _fcum_kernel,
        grid=(bsz,),
        in_specs=[
            pl.BlockSpec((1, seq, LANES), lambda b: (b, 0, col_block)),
            pl.BlockSpec((8, 1), lambda b: (0, 0)),
        ],
        out_specs=pl.BlockSpec((1, 8, seq), lambda b: (b, 0, 0)),
        out_shape=jax.ShapeDtypeStruct((bsz, 8, seq), F32),
        compiler_params=_params("parallel"),
        name="forget_cumsum",
    )(gf, b_f8)


def _half_masks(q):
    lane = lax.broadcasted_iota(jnp.int32, q.shape, 1)
    zero = jnp.zeros_like(q)
    return jnp.where(lane < HEAD_DIM, q, zero), jnp.where(lane >= HEAD_DIM, q, zero)


def _flash_init(m_scr, l_scr, acc_scr):
    m_scr[...] = jnp.full(m_scr.shape, -jnp.inf, F32)
    l_scr[...] = jnp.zeros(l_scr.shape, F32)
    acc_scr[...] = jnp.zeros(acc_scr.shape, F32)


def _flash_update(s, vblk, rows, m_scr, l_scr, acc_scr):
    m_prev = m_scr[rows]
    m_new = jnp.maximum(m_prev, jnp.max(s, axis=-1, keepdims=True))
    alpha = jnp.exp(m_prev - m_new)
    p = jnp.exp(s - m_new)
    l_scr[rows] = alpha * l_scr[rows] + jnp.sum(p, axis=-1, keepdims=True)
    acc_scr[rows] = alpha * acc_scr[rows] + _dot(p.astype(BF16), vblk)
    m_scr[rows] = m_new


def _attn_a_kernel(q_ref, k_ref, v_ref, bias_ref, lq1, lk1, lq2, lk2, sg_ref, o_ref,
                   m_scr, l_scr, acc_scr, *, t, lam_init):
    qi = pl.program_id(2)
    qh = _half_masks(q_ref[0] * SCALE)
    _flash_init(m_scr, l_scr, acc_scr)

    def step(kb, delta):
        off = pl.multiple_of(kb * t, t)
        kblk = k_ref[0, pl.ds(off, t), :]
        vblk = v_ref[0, pl.ds(off, t), :]
        for hh in range(2):
            rows = pl.ds(hh * t, t)
            s = _dot_nt(qh[hh], kblk) + bias_ref[0, delta, hh * t:(hh + 1) * t, :]
            _flash_update(s, vblk, rows, m_scr, l_scr, acc_scr)

    def body(kb, carry):
        step(kb, qi - kb)
        return carry

    lax.fori_loop(0, qi, body, 0)
    step(qi, 0)

    o = acc_scr[...] / l_scr[...]
    lam = (jnp.exp(jnp.sum(lq1[...] * lk1[...], axis=-1, keepdims=True))
           - jnp.exp(jnp.sum(lq2[...] * lk2[...], axis=-1, keepdims=True)) + lam_init)
    o = o[:t] - lam * o[t:]
    y = o * lax.rsqrt(jnp.mean(o * o, axis=-1, keepdims=True) + EPS)
    o_ref[0] = ((y * sg_ref[...]) * (1.0 - lam_init)).astype(o_ref.dtype)


def _attn_a(qkv, bias_a, lq1, lk1, lq2, lk2, subln_g, *, lam_init, t):
    bsz, seq, _ = qkv.shape
    nq = seq // t
    vec = lambda n: pl.BlockSpec((1, n), lambda h, b, i: (0, 0))
    return pl.pallas_call(
        functools.partial(_attn_a_kernel, t=t, lam_init=lam_init),
        grid=(A_HEADS, bsz, nq),
        in_specs=[
            pl.BlockSpec((1, t, LANES), lambda h, b, i: (b, i, h)),
            pl.BlockSpec((1, seq, LANES), lambda h, b, i: (b, 0, A_HEADS + h)),
            pl.BlockSpec((1, seq, LANES), lambda h, b, i: (b, 0, 2 * A_HEADS + h)),
            pl.BlockSpec((1, nq, 2 * t, t), lambda h, b, i: (h, 0, 0, 0)),
            vec(HEAD_DIM), vec(HEAD_DIM), vec(HEAD_DIM), vec(HEAD_DIM), vec(LANES),
        ],
        out_specs=pl.BlockSpec((1, t, LANES), lambda h, b, i: (b, i, h)),
        out_shape=jax.ShapeDtypeStruct((bsz, seq, A_W), BF16),
        scratch_shapes=[pltpu.VMEM((2 * t, 1), F32), pltpu.VMEM((2 * t, 1), F32),
                        pltpu.VMEM((2 * t, LANES), F32)],
        compiler_params=_params("parallel", "parallel", "arbitrary"),
        name="attn_diff",
    )(qkv, qkv, qkv, bias_a, lq1, lk1, lq2, lk2, subln_g)


def _attn_c_kernel(q_ref, k_ref, v_ref, f_ref, o_ref, m_scr, l_scr, acc_scr, *, t):
    qi = pl.program_id(2)
    qh = _half_masks(q_ref[0] * SCALE)
    _flash_init(m_scr, l_scr, acc_scr)
    q_off = pl.multiple_of(qi * t, t)
    f_anchor = f_ref[0, 0, :, pl.ds(q_off, LANES)][:, :1]

    def step(kb, diagonal):
        off = pl.multiple_of(kb * t, t)
        kblk = k_ref[0, pl.ds(off, t), :]
        vblk = v_ref[0, pl.ds(off, t), :]
        dec = f_anchor - f_ref[0, 0, :, pl.ds(off, t)]
        for hh in range(2):
            rows = pl.ds(hh * t, t)
            s = _dot_nt(qh[hh], kblk) + dec[hh:hh + 1]
            if diagonal:
                r = lax.broadcasted_iota(jnp.int32, s.shape, 0)
                c = lax.broadcasted_iota(jnp.int32, s.shape, 1)
                s = jnp.where(r >= c, s, NEG)
            _flash_update(s, vblk, rows, m_scr, l_scr, acc_scr)

    def body(kb, carry):
        step(kb, False)
        return carry

    lax.fori_loop(0, qi, body, 0)
    step(qi, True)

    o = acc_scr[...] / l_scr[...]
    lane = lax.broadcasted_iota(jnp.int32, (t, LANES), 1)
    o_ref[0] = jnp.where(lane < HEAD_DIM, o[:t], o[t:]).astype(o_ref.dtype)


def _attn_c(qkv, fcum, *, t):
    bsz, seq, _ = qkv.shape
    pairs = C_HEADS // 2
    q0 = (3 * A_W + 3 * B_QW) // LANES
    return pl.pallas_call(
        functools.partial(_attn_c_kernel, t=t),
        grid=(pairs, bsz, seq // t),
        in_specs=[
            pl.BlockSpec((1, t, LANES), lambda p, b, i: (b, i, q0 + p)),
            pl.BlockSpec((1, seq, LANES), lambda p, b, i: (b, 0, q0 + pairs + p)),
            pl.BlockSpec((1, seq, LANES), lambda p, b, i: (b, 0, q0 + 2 * pairs + p)),
            pl.BlockSpec((1, 1, 2, seq), lambda p, b, i: (b, p, 0, 0)),
        ],
        out_specs=pl.BlockSpec((1, t, LANES), lambda p, b, i: (b, i, p)),
        out_shape=jax.ShapeDtypeStruct((bsz, seq, C_W), BF16),
        scratch_shapes=[pltpu.VMEM((2 * t, 1), F32), pltpu.VMEM((2 * t, 1), F32),
                        pltpu.VMEM((2 * t, LANES), F32)],
        compiler_params=_params("parallel", "parallel", "arbitrary"),
        name="attn_forget",
    )(qkv, qkv, qkv, fcum)


def _attn_b_kernel(q_ref, kp_ref, kc_ref, vp_ref, vc_ref, bias_ref, o_ref, lse_ref):
    lane = lax.broadcasted_iota(jnp.int32, (BLOCK, LANES), 1)
    for hp in range(B_HEADS // 2):
        cols = slice(hp * LANES, (hp + 1) * LANES)
        qh = _half_masks(q_ref[0, :, cols] * SCALE)
        kcat = jnp.concatenate([kp_ref[0, :, cols], kc_ref[0, :, cols]], axis=0)
        vcat = jnp.concatenate([vp_ref[0, :, cols], vc_ref[0, :, cols]], axis=0)
        outs, lses = [], []
        for hh in range(2):
            s = _dot_nt(qh[hh], kcat) + bias_ref[0, hp, hh * BLOCK:(hh + 1) * BLOCK, :]
            m = jnp.max(s, axis=-1, keepdims=True)
            e = jnp.exp(s - m)
            den = jnp.sum(e, axis=-1, keepdims=True)
            outs.append(_dot(e.astype(BF16), vcat) / den)
            lses.append(jnp.broadcast_to(m + jnp.log(den), (BLOCK, LANES)))
        o_ref[0, :, cols] = jnp.where(lane < HEAD_DIM, outs[0], outs[1])
        lse_ref[0, :, cols] = jnp.where(lane < HEAD_DIM, lses[0], lses[1])


def _attn_b_group(qkv, bias_g, g, dil):
    bsz, seq, width = qkv.shape
    m_len = seq // dil
    nblk = m_len // BLOCK
    view = qkv.reshape(bsz, m_len, dil * width)
    per_r = width // B_W
    q0 = 3 * A_W // B_W + g
    k0 = q0 + len(B_GROUPS)
    v0 = k0 + len(B_GROUPS)
    cur = lambda c0: pl.BlockSpec((1, BLOCK, B_W), lambda b, r, n: (b, n, r * per_r + c0))
    prev = lambda c0: pl.BlockSpec((1, BLOCK, B_W),
                                   lambda b, r, n: (b, jnp.maximum(n - 1, 0), r * per_r + c0))
    out_spec = pl.BlockSpec((1, BLOCK, B_W), lambda b, r, n: (b, n, r))
    out_sds = jax.ShapeDtypeStruct((bsz, m_len, dil * B_W), F32)
    o, lse = pl.pallas_call(
        _attn_b_kernel,
        grid=(bsz, dil, nblk),
        in_specs=[
            cur(q0), prev(k0), cur(k0), prev(v0), cur(v0),
            pl.BlockSpec((1, B_HEADS // 2, 2 * BLOCK, 2 * BLOCK),
                         lambda b, r, n: (jnp.minimum(n, 1), 0, 0, 0)),
        ],
        out_specs=[out_spec, out_spec],
        out_shape=[out_sds, out_sds],
        compiler_params=_params("parallel", "parallel", "arbitrary"),
        name=f"attn_dilated_g{g}",
    )(view, view, view, view, view, bias_g)
    return o.reshape(bsz, seq, B_W), lse.reshape(bsz, seq, B_W)


def _merge_kernel(oa_ref, ob0, ob1, ob2, ls0, ls1, ls2, oc_ref, gate_ref, x_ref, mod_ref,
                  wa_ref, wb_ref, wc_ref, wo_ref, o_ref):
    l0, l1, l2 = ls0[...], ls1[...], ls2[...]
    mx = jnp.maximum(jnp.maximum(l0, l1), l2)
    e0, e1, e2 = jnp.exp(l0 - mx), jnp.exp(l1 - mx), jnp.exp(l2 - mx)
    den = e0 + e1 + e2
    ob = (e0 / den) * ob0[...] + (e1 / den) * ob1[...] + (e2 / den) * ob2[...]
    d = D_MODEL
    merged = (gate_ref[:, 0:d] * _dot(oa_ref[...], wa_ref[...])
              + gate_ref[:, d:2 * d] * _dot(ob.astype(BF16), wb_ref[...])
              + gate_ref[:, 2 * d:3 * d] * _dot(oc_ref[...], wc_ref[...]))
    y = _dot(merged.astype(BF16), wo_ref[...])
    o_ref[...] = x_ref[...] + mod_ref[0][2:3] * y


def _merge(oa, obs, lses, oc, gf, x2d, mod3, wa, wb, wc, wo, *, seq, tm):
    rows, d = x2d.shape
    per_b = seq // tm
    row = lambda w: pl.BlockSpec((tm, w), lambda i: (i, 0))
    full = lambda a: pl.BlockSpec(a.shape, lambda i: (0, 0))
    return pl.pallas_call(
        _merge_kernel,
        grid=(rows // tm,),
        in_specs=[row(A_W)] + [row(B_W)] * 6 + [row(C_W), row(GATE_W), row(d),
                  pl.BlockSpec((1, 6, d), lambda i: (i // per_b, 0, 0)),
                  full(wa), full(wb), full(wc), full(wo)],
        out_specs=row(d),
        out_shape=jax.ShapeDtypeStruct((rows, d), F32),
        compiler_params=_params("parallel"),
        name="merge_outproj",
    )(oa, *obs, *lses, oc, gf, x2d, mod3, wa, wb, wc, wo)


def _top2_gate(logits):
    idx = lax.broadcasted_iota(jnp.int32, logits.shape, 1)
    n = logits.shape[1]
    m1 = jnp.max(logits, axis=-1, keepdims=True)
    i1 = jnp.min(jnp.where(logits == m1, idx, n), axis=-1, keepdims=True)
    first = idx == i1
    rest = jnp.where(first, -jnp.inf, logits)
    m2 = jnp.max(rest, axis=-1, keepdims=True)
    i2 = jnp.min(jnp.where(rest == m2, idx, n), axis=-1, keepdims=True)
    second = idx == i2
    e = jnp.exp(m2 - m1)
    den = 1.0 + e
    return jnp.where(first, 1.0 / den, 0.0) + jnp.where(second, e / den, 0.0)


def _ffn_kernel(x_ref, mod_ref, g_ref, wg_ref, wu_ref, wd_ref, o_ref, h_scr, acc_scr, *, nf):
    f = pl.program_id(1)

    @pl.when(f == 0)
    def _():
        m = mod_ref[0]
        h_scr[...] = _rms_mod(x_ref[...], g_ref[...], m[4:5], m[3:4]).astype(BF16)
        acc_scr[...] = jnp.zeros(acc_scr.shape, F32)

    h = h_scr[...]
    act = (_silu(_dot(h, wg_ref[...])) * _dot(h, wu_ref[...])).astype(BF16)
    acc_scr[...] += _dot(act, wd_ref[...])

    @pl.when(f == nf - 1)
    def _():
        o_ref[...] = x_ref[...] + mod_ref[0][5:6] * acc_scr[...]


def _ffn(x2d, mod3, g, wg, wu, wd, *, seq, tm, tf):
    rows, d = x2d.shape
    dff = wg.shape[1]
    nf = dff // tf
    per_b = seq // tm
    return pl.pallas_call(
        functools.partial(_ffn_kernel, nf=nf),
        grid=(rows // tm, nf),
        in_specs=[
            pl.BlockSpec((tm, d), lambda i, f: (i, 0)),
            pl.BlockSpec((1, 6, d), lambda i, f: (i // per_b, 0, 0)),
            pl.BlockSpec((1, d), lambda i, f: (0, 0)),
            pl.BlockSpec((d, tf), lambda i, f: (0, f)),
            pl.BlockSpec((d, tf), lambda i, f: (0, f)),
            pl.BlockSpec((tf, d), lambda i, f: (f, 0)),
        ],
        out_specs=pl.BlockSpec((tm, d), lambda i, f: (i, 0)),
        out_shape=jax.ShapeDtypeStruct((rows, d), F32),
        scratch_shapes=[pltpu.VMEM((tm, d), BF16), pltpu.VMEM((tm, d), F32)],
        compiler_params=_params("parallel", "arbitrary"),
        name="ffn_dense",
    )(x2d, mod3, g, wg, wu, wd)


def _moe_kernel(x_ref, mod_ref, g_ref, wr_ref, wg_ref, wu_ref, wd_ref, o_ref,
                h_scr, gate_scr, acc_scr, *, nf, ne):
    e = pl.program_id(1)
    f = pl.program_id(2)

    @pl.when((e == 0) & (f == 0))
    def _():
        m = mod_ref[0]
        h = _rms_mod(x_ref[...], g_ref[...], m[4:5], m[3:4])
        h_scr[...] = h.astype(BF16)
        logits = jnp.dot(h, wr_ref[...], preferred_element_type=F32, precision=lax.Precision.HIGHEST)
        gate_scr[...] = _top2_gate(logits)
        o_ref[...] = jnp.zeros(o_ref.shape, F32)

    @pl.when(f == 0)
    def _():
        acc_scr[...] = jnp.zeros(acc_scr.shape, F32)

    h = h_scr[...]
    act = (_silu(_dot(h, wg_ref[0])) * _dot(h, wu_ref[0])).astype(BF16)
    acc_scr[...] += _dot(act, wd_ref[0])

    @pl.when(f == nf - 1)
    def _():
        gate = gate_scr[...]
        idx = lax.broadcasted_iota(jnp.int32, gate.shape, 1)
        ge = jnp.sum(jnp.where(idx == e, gate, 0.0), axis=-1, keepdims=True)
        o_ref[...] += ge * acc_scr[...]

    @pl.when((e == ne - 1) & (f == nf - 1))
    def _():
        o_ref[...] = x_ref[...] + mod_ref[0][5:6] * o_ref[...]


def _moe(x2d, mod3, g, wr, wg, wu, wd, *, seq, tm, tf):
    rows, d = x2d.shape
    ne, _, dff = wg.shape
    nf = dff // tf
    per_b = seq // tm
    return pl.pallas_call(
        functools.partial(_moe_kernel, nf=nf, ne=ne),
        grid=(rows // tm, ne, nf),
        in_specs=[
            pl.BlockSpec((tm, d), lambda i, e, f: (i, 0)),
            pl.BlockSpec((1, 6, d), lambda i, e, f: (i // per_b, 0, 0)),
            pl.BlockSpec((1, d), lambda i, e, f: (0, 0)),
            pl.BlockSpec((d, ne), lambda i, e, f: (0, 0)),
            pl.BlockSpec((1, d, tf), lambda i, e, f: (e, 0, f)),
            pl.BlockSpec((1, d, tf), lambda i, e, f: (e, 0, f)),
            pl.BlockSpec((1, tf, d), lambda i, e, f: (e, f, 0)),
        ],
        out_specs=pl.BlockSpec((tm, d), lambda i, e, f: (i, 0)),
        out_shape=jax.ShapeDtypeStruct((rows, d), F32),
        scratch_shapes=[pltpu.VMEM((tm, d), BF16), pltpu.VMEM((tm, ne), F32),
                        pltpu.VMEM((tm, d), F32)],
        compiler_params=_params("parallel", "arbitrary", "arbitrary"),
        name="moe_dense",
    )(x2d, mod3, g, wr, wg, wu, wd)


def _final_norm_kernel(x_ref, g_ref, o_ref):
    x = x_ref[...]
    o_ref[...] = (x * lax.rsqrt(jnp.mean(x * x, axis=-1, keepdims=True) + EPS)) * g_ref[...]


def _final_norm(x2d, g, *, tm):
    rows, d = x2d.shape
    return pl.pallas_call(
        _final_norm_kernel,
        grid=(rows // tm,),
        in_specs=[pl.BlockSpec((tm, d), lambda i: (i, 0)), pl.BlockSpec((1, d), lambda i: (0, 0))],
        out_specs=pl.BlockSpec((tm, d), lambda i: (i, 0)),
        out_shape=jax.ShapeDtypeStruct((rows, d), F32),
        compiler_params=_params("parallel"),
        name="final_norm",
    )(x2d, g)


def _t5_bucket(dist):
    n = jnp.maximum(dist, 0)
    max_exact = N_BUCKETS // 2
    nf = jnp.maximum(n, 1).astype(F32)
    large = max_exact + (jnp.log(nf / max_exact) / math.log(REL_MAX_DIST / max_exact)
                         * (N_BUCKETS - max_exact)).astype(jnp.int32)
    large = jnp.minimum(large, N_BUCKETS - 1)
    return jnp.where(n < max_exact, n, large)


def _bias_tiles_a(rel_bias, seq, t):
    nq = seq // t
    dist = (jnp.arange(nq)[:, None, None] * t + jnp.arange(t)[None, :, None]
            - jnp.arange(t)[None, None, :])
    vals = rel_bias[:, :2 * A_HEADS][_t5_bucket(dist)]
    vals = jnp.where((dist >= 0)[..., None], vals, NEG)
    vals = vals.reshape(nq, t, t, A_HEADS, 2).transpose(3, 0, 4, 1, 2)
    return vals.reshape(A_HEADS, nq, 2 * t, t).astype(F32)


def _bias_tiles_b(rel_bias, g, dil, n_back):
    qi = jnp.arange(BLOCK)[:, None]
    kj = jnp.arange(2 * BLOCK)[None, :]
    dm = qi + BLOCK - kj
    tab = rel_bias[:, 2 * A_HEADS + g * B_HEADS:2 * A_HEADS + (g + 1) * B_HEADS]
    vals = jnp.moveaxis(tab[_t5_bucket(dm * dil)], -1, 0)
    valid = (dm >= 0) & (dm <= n_back)
    later = jnp.where(valid[None], vals, NEG)
    first = jnp.where((valid & (kj >= BLOCK))[None], vals, NEG)
    both = jnp.stack([first, later])
    return both.reshape(2, B_HEADS // 2, 2 * BLOCK, 2 * BLOCK).astype(F32)


def kernel(x, c, norm_mix_g, norm_ffn_g, w_mod, b_mod, w_in, b_forget, lam_q1, lam_k1, lam_q2, lam_k2,
           subln_g, rel_bias, w_br_a, w_br_b, w_br_c, w_out, w_ff_gate, w_ff_up, w_ff_down, w_router,
           w_exp_gate, w_exp_up, w_exp_down, final_norm_g):
    bsz, seq, d = x.shape
    depth = w_mod.shape[0]
    rows = bsz * seq
    t_attn = 256
    x2d = x.reshape(rows, d)

    mod = _modulation(c, w_mod, b_mod)
    bias_a = _bias_tiles_a(rel_bias, seq, t_attn)
    bias_b = [_bias_tiles_b(rel_bias, g, dil, win // dil) for g, (win, dil) in enumerate(B_GROUPS)]
    f_col = QKV_W
    g_col = QKV_W + C_HEADS

    for l in range(depth):
        lam_init = 0.8 - 0.6 * math.exp(-0.3 * l)
        mod3 = mod[l].reshape(bsz, 6, d)
        w_l = w_in[l]
        w_qkv = w_l[:, :QKV_W].astype(BF16)
        w_f = jnp.pad(w_l[:, f_col:f_col + C_HEADS], ((0, 0), (0, LANES - C_HEADS)))
        w_gf = jnp.concatenate([w_l[:, g_col:], w_f], axis=1).astype(BF16)

        g_mix = norm_mix_g[l].reshape(1, d)
        qkv = _inproj(x2d, mod3, g_mix, w_qkv, seq=seq, tm=1024, tn=1536, out_dtype=BF16)
        gf = _inproj(x2d, mod3, g_mix, w_gf, seq=seq, tm=1024, tn=640, out_dtype=F32,
                     n_sigmoid=GATE_W)
        qkv = qkv.reshape(bsz, seq, QKV_W)

        b_f8 = jnp.pad(b_forget[l], (0, 8 - C_HEADS)).reshape(8, 1)
        fcum = _forget_cumsum(gf.reshape(bsz, seq, GATE_W + LANES), b_f8, col_block=GATE_W // LANES)
        fcum = fcum[:, :C_HEADS].reshape(bsz, C_HEADS // 2, 2, seq)

        oa = _attn_a(qkv, bias_a, lam_q1[l].reshape(1, -1), lam_k1[l].reshape(1, -1),
                     lam_q2[l].reshape(1, -1), lam_k2[l].reshape(1, -1), subln_g[l].reshape(1, -1),
                     lam_init=lam_init, t=t_attn)
        oc = _attn_c(qkv, fcum, t=t_attn)
        obs, lses = [], []
        for g, (win, dil) in enumerate(B_GROUPS):
            o_g, lse_g = _attn_b_group(qkv, bias_b[g], g, dil)
            obs.append(o_g.reshape(rows, B_W))
            lses.append(lse_g.reshape(rows, B_W))

        x2d = _merge(oa.reshape(rows, A_W), obs, lses, oc.reshape(rows, C_W), gf, x2d, mod3,
                     w_br_a[l].astype(BF16), w_br_b[l].astype(BF16), w_br_c[l].astype(BF16),
                     w_out[l].astype(BF16), seq=seq, tm=512)

        g_ffn = norm_ffn_g[l].reshape(1, d)
        if l % 2 == 0:
            j = l // 2
            x2d = _ffn(x2d, mod3, g_ffn, w_ff_gate[j].astype(BF16), w_ff_up[j].astype(BF16),
                       w_ff_down[j].astype(BF16), seq=seq, tm=512, tf=1408)
        else:
            j = l // 2
            x2d = _moe(x2d, mod3, g_ffn, w_router[j], w_exp_gate[j].astype(BF16),
                       w_exp_up[j].astype(BF16), w_exp_down[j].astype(BF16), seq=seq, tm=1024, tf=512)

    return _final_norm(x2d, final_norm_g.reshape(1, d), tm=1024).reshape(bsz, seq, d)
```

```python
import functools
import math

import jax
import jax.numpy as jnp
from jax import lax
from jax.experimental import pallas as pl
from jax.experimental.pallas import tpu as pltpu

F32 = jnp.float32
BF16 = jnp.bfloat16

D_MODEL = 1024
HEAD_DIM = 64
LANES = 128
A_HEADS = 4
A_W = A_HEADS * 2 * HEAD_DIM
B_GROUPS = ((128, 1), (512, 4), (2048, 16))
B_HEADS = 6
B_W = B_HEADS * HEAD_DIM
B_QW = len(B_GROUPS) * B_W
C_HEADS = 6
C_W = C_HEADS * HEAD_DIM
N_BRANCH = 3
BLOCK = 128
N_BUCKETS = 32
REL_MAX_DIST = 2048
N_EXPERTS = 8
EPS = 1e-6
QKV_W = 3 * A_W + 3 * B_QW + 3 * C_W
GATE_W = N_BRANCH * D_MODEL
NEG = -1e30
SCALE = HEAD_DIM ** -0.5
VMEM_LIMIT = 56 * 1024 * 1024


def _params(*sem):
    return pltpu.CompilerParams(dimension_semantics=sem, vmem_limit_bytes=VMEM_LIMIT)


def _rms_mod(x, g, sc, sh):
    y = x * lax.rsqrt(jnp.mean(x * x, axis=-1, keepdims=True) + EPS)
    return (y * g) * (1.0 + sc) + sh


def _dot(a, b):
    return jnp.dot(a, b, preferred_element_type=F32)


def _dot_nt(a, b):
    return lax.dot_general(a, b, (((1,), (1,)), ((), ())), preferred_element_type=F32)


def _silu(a):
    return a * jax.nn.sigmoid(a)


def _mod_kernel(c_ref, w_ref, b_ref, o_ref):
    a = _silu(c_ref[...]).astype(BF16)
    o_ref[0] = _dot(a, w_ref[0].astype(BF16)) + b_ref[0]


def _modulation(c, w_mod, b_mod):
    depth, d, n = w_mod.shape
    bsz = c.shape[0]
    tn = 1536
    return pl.pallas_call(
        _mod_kernel,
        grid=(depth, n // tn),
        in_specs=[
            pl.BlockSpec((bsz, d), lambda l, j: (0, 0)),
            pl.BlockSpec((1, d, tn), lambda l, j: (l, 0, j)),
            pl.BlockSpec((1, 1, tn), lambda l, j: (l, 0, j)),
        ],
        out_specs=pl.BlockSpec((1, bsz, tn), lambda l, j: (l, 0, j)),
        out_shape=jax.ShapeDtypeStruct((depth, bsz, n), F32),
        compiler_params=_params("parallel", "parallel"),
        name="modulation",
    )(c, w_mod, b_mod.reshape(depth, 1, n))


def _inproj_kernel(x_ref, mod_ref, g_ref, w_ref, o_ref, h_scr, *, n_sigmoid, tn):
    j = pl.program_id(1)

    @pl.when(j == 0)
    def _():
        m = mod_ref[0]
        h_scr[...] = _rms_mod(x_ref[...], g_ref[...], m[1:2], m[0:1]).astype(BF16)

    y = _dot(h_scr[...], w_ref[...])
    if n_sigmoid:
        col = j * tn + lax.broadcasted_iota(jnp.int32, y.shape, 1)
        y = jnp.where(col < n_sigmoid, jax.nn.sigmoid(y), y)
    o_ref[...] = y.astype(o_ref.dtype)


def _inproj(x2d, mod3, g, w, *, seq, tm, tn, out_dtype, n_sigmoid=0):
    rows, d = x2d.shape
    n = w.shape[1]
    per_b = seq // tm
    return pl.pallas_call(
        functools.partial(_inproj_kernel, n_sigmoid=n_sigmoid, tn=tn),
        grid=(rows // tm, n // tn),
        in_specs=[
            pl.BlockSpec((tm, d), lambda i, j: (i, 0)),
            pl.BlockSpec((1, 6, d), lambda i, j: (i // per_b, 0, 0)),
            pl.BlockSpec((1, d), lambda i, j: (0, 0)),
            pl.BlockSpec((d, tn), lambda i, j: (0, j)),
        ],
        out_specs=pl.BlockSpec((tm, tn), lambda i, j: (i, j)),
        out_shape=jax.ShapeDtypeStruct((rows, n), out_dtype),
        scratch_shapes=[pltpu.VMEM((tm, d), BF16)],
        compiler_params=_params("parallel", "arbitrary"),
        name="inproj",
    )(x2d, mod3, g, w)


def _fcum_kernel(f_ref, b_ref, o_ref):
    z = f_ref[0].T[:8] + b_ref[...]
    x = jnp.minimum(z, 0.0) - jnp.log1p(jnp.exp(-jnp.abs(z)))
    s = x.shape[1]
    lane = lax.broadcasted_iota(jnp.int32, x.shape, 1)
    k = 1
    while k < s:
        x = x + jnp.where(lane >= k, pltpu.roll(x, k, 1), 0.0)
        k *= 2
    o_ref[0] = x


def _forget_cumsum(gf, b_f8, *, col_block):
    bsz, seq, _ = gf.shape
    return pl.pallas_call(
        _fcum_kernel,
        grid=(bsz,),
        in_specs=[
            pl.BlockSpec((1, seq, LANES), lambda b: (b, 0, col_block)),
            pl.BlockSpec((8, 1), lambda b: (0, 0)),
        ],
        out_specs=pl.BlockSpec((1, 8, seq), lambda b: (b, 0, 0)),
        out_shape=jax.ShapeDtypeStruct((bsz, 8, seq), F32),
        compiler_params=_params("parallel"),
        name="forget_cumsum",
    )(gf, b_f8)


def _half_masks(q):
    lane = lax.broadcasted_iota(jnp.int32, q.shape, 1)
    zero = jnp.zeros_like(q)
    return jnp.where(lane < HEAD_DIM, q, zero), jnp.where(lane >= HEAD_DIM, q, zero)


def _flash_init(first, v_ref, vext_scr, m_scr, acc_scr):
    @pl.when(first)
    def _():
        vext_scr[:, :LANES] = v_ref[0]
        vext_scr[:, LANES:] = jnp.ones((vext_scr.shape[0], LANES), BF16)

    m_scr[...] = jnp.full(m_scr.shape, -jnp.inf, F32)
    acc_scr[...] = jnp.zeros(acc_scr.shape, F32)


def _lane_tile(a, n):
    return a if n == 1 else jnp.concatenate([a] * n, axis=1)


def _flash_update(s, vext, rows, m_scr, acc_scr):
    m_prev = m_scr[rows]
    m_new = jnp.maximum(m_prev, jnp.max(s, axis=-1, keepdims=True))
    alpha = jnp.exp(m_prev - m_new)
    p = jnp.exp(s - _lane_tile(m_new, s.shape[1] // LANES))
    acc_scr[rows] = _lane_tile(alpha, 2) * acc_scr[rows] + _dot(p.astype(BF16), vext)
    m_scr[rows] = m_new


def _flash_result(acc_scr):
    acc = acc_scr[...]
    return acc[:, :LANES] / acc[:, LANES:]


def _attn_a_kernel(q_ref, k_ref, v_ref, bias_ref, lq1, lk1, lq2, lk2, sg_ref, o_ref,
                   vext_scr, m_scr, acc_scr, *, t, nsb, lam_init):
    qi = pl.program_id(2)
    _flash_init(qi == 0, v_ref, vext_scr, m_scr, acc_scr)
    qh = [_half_masks(q_ref[0, sb * t:(sb + 1) * t, :] * SCALE) for sb in range(nsb)]

    def step(kb, sbs, delta_of):
        off = pl.multiple_of(kb * t, t)
        kblk = k_ref[0, pl.ds(off, t), :]
        vext = vext_scr[pl.ds(off, t), :]
        for sb in sbs:
            for hh in range(2):
                s = _dot_nt(qh[sb][hh], kblk) + bias_ref[0, delta_of(sb), hh * t:(hh + 1) * t, :]
                _flash_update(s, vext, pl.ds((sb * 2 + hh) * t, t), m_scr, acc_scr)

    def body(kb, carry):
        step(kb, range(nsb), lambda sb: qi * nsb + sb - kb)
        return carry

    lax.fori_loop(0, qi * nsb, body, 0)
    for j in range(nsb):
        step(qi * nsb + j, range(j, nsb), lambda sb: sb - j)

    o = _flash_result(acc_scr)
    lam = (jnp.exp(jnp.sum(lq1[...] * lk1[...], axis=-1, keepdims=True))
           - jnp.exp(jnp.sum(lq2[...] * lk2[...], axis=-1, keepdims=True)) + lam_init)
    for sb in range(nsb):
        d = o[2 * sb * t:(2 * sb + 1) * t] - lam * o[(2 * sb + 1) * t:(2 * sb + 2) * t]
        y = d * lax.rsqrt(jnp.mean(d * d, axis=-1, keepdims=True) + EPS)
        o_ref[0, sb * t:(sb + 1) * t, :] = ((y * sg_ref[...]) * (1.0 - lam_init)).astype(o_ref.dtype)


def _flash_scratch(seq, t, nsb):
    return [pltpu.VMEM((seq, 2 * LANES), BF16), pltpu.VMEM((2 * nsb * t, LANES), F32),
            pltpu.VMEM((2 * nsb * t, 2 * LANES), F32)]


def _attn_a(qkv, bias_a, lq1, lk1, lq2, lk2, subln_g, *, lam_init, t, nsb):
    bsz, seq, _ = qkv.shape
    tq = t * nsb
    vec = lambda n: pl.BlockSpec((1, n), lambda h, b, i: (0, 0))
    return pl.pallas_call(
        functools.partial(_attn_a_kernel, t=t, nsb=nsb, lam_init=lam_init),
        grid=(A_HEADS, bsz, seq // tq),
        in_specs=[
            pl.BlockSpec((1, tq, LANES), lambda h, b, i: (b, i, h)),
            pl.BlockSpec((1, seq, LANES), lambda h, b, i: (b, 0, A_HEADS + h)),
            pl.BlockSpec((1, seq, LANES), lambda h, b, i: (b, 0, 2 * A_HEADS + h)),
            pl.BlockSpec((1, seq // t, 2 * t, t), lambda h, b, i: (h, 0, 0, 0)),
            vec(HEAD_DIM), vec(HEAD_DIM), vec(HEAD_DIM), vec(HEAD_DIM), vec(LANES),
        ],
        out_specs=pl.BlockSpec((1, tq, LANES), lambda h, b, i: (b, i, h)),
        out_shape=jax.ShapeDtypeStruct((bsz, seq, A_W), BF16),
        scratch_shapes=_flash_scratch(seq, t, nsb),
        compiler_params=_params("parallel", "parallel", "arbitrary"),
        name="attn_diff",
    )(qkv, qkv, qkv, bias_a, lq1, lk1, lq2, lk2, subln_g)


def _attn_c_kernel(q_ref, k_ref, v_ref, f_ref, o_ref, vext_scr, m_scr, acc_scr, *, t, nsb):
    qi = pl.program_id(2)
    _flash_init(qi == 0, v_ref, vext_scr, m_scr, acc_scr)
    qh = [_half_masks(q_ref[0, sb * t:(sb + 1) * t, :] * SCALE) for sb in range(nsb)]
    q_off = pl.multiple_of(qi * (t * nsb), t * nsb)
    f_anchor = f_ref[0, 0, :, pl.ds(q_off, LANES)][:, :1]
    r = lax.broadcasted_iota(jnp.int32, (t, t), 0)
    c = lax.broadcasted_iota(jnp.int32, (t, t), 1)

    def step(kb, sbs, diag_sb):
        off = pl.multiple_of(kb * t, t)
        kblk = k_ref[0, pl.ds(off, t), :]
        vext = vext_scr[pl.ds(off, t), :]
        dec = f_anchor - f_ref[0, 0, :, pl.ds(off, t)]
        for sb in sbs:
            for hh in range(2):
                s = _dot_nt(qh[sb][hh], kblk) + dec[hh:hh + 1]
                if sb == diag_sb:
                    s = jnp.where(r >= c, s, NEG)
                _flash_update(s, vext, pl.ds((sb * 2 + hh) * t, t), m_scr, acc_scr)

    def body(kb, carry):
        step(kb, range(nsb), None)
        return carry

    lax.fori_loop(0, qi * nsb, body, 0)
    for j in range(nsb):
        step(qi * nsb + j, range(j, nsb), j)

    o = _flash_result(acc_scr)
    lane = lax.broadcasted_iota(jnp.int32, (t, LANES), 1)
    for sb in range(nsb):
        pair = jnp.where(lane < HEAD_DIM, o[2 * sb * t:(2 * sb + 1) * t], o[(2 * sb + 1) * t:(2 * sb + 2) * t])
        o_ref[0, sb * t:(sb + 1) * t, :] = pair.astype(o_ref.dtype)


def _attn_c(qkv, fcum, *, t, nsb):
    bsz, seq, _ = qkv.shape
    tq = t * nsb
    pairs = C_HEADS // 2
    q0 = (3 * A_W + 3 * B_QW) // LANES
    return pl.pallas_call(
        functools.partial(_attn_c_kernel, t=t, nsb=nsb),
        grid=(pairs, bsz, seq // tq),
        in_specs=[
            pl.BlockSpec((1, tq, LANES), lambda p, b, i: (b, i, q0 + p)),
            pl.BlockSpec((1, seq, LANES), lambda p, b, i: (b, 0, q0 + pairs + p)),
            pl.BlockSpec((1, seq, LANES), lambda p, b, i: (b, 0, q0 + 2 * pairs + p)),
            pl.BlockSpec((1, 1, 2, seq), lambda p, b, i: (b, p, 0, 0)),
        ],
        out_specs=pl.BlockSpec((1, tq, LANES), lambda p, b, i: (b, i, p)),
        out_shape=jax.ShapeDtypeStruct((bsz, seq, C_W), BF16),
        scratch_shapes=_flash_scratch(seq, t, nsb),
        compiler_params=_params("parallel", "parallel", "arbitrary"),
        name="attn_forget",
    )(qkv, qkv, qkv, fcum)


def _attn_b_kernel(q_ref, kp_ref, kc_ref, vp_ref, vc_ref, bias_ref, o_ref, lse_ref):
    lane = lax.broadcasted_iota(jnp.int32, (BLOCK, LANES), 1)
    for hp in range(B_HEADS // 2):
        cols = slice(hp * LANES, (hp + 1) * LANES)
        qh = _half_masks(q_ref[0, :, cols] * SCALE)
        kcat = jnp.concatenate([kp_ref[0, :, cols], kc_ref[0, :, cols]], axis=0)
        vcat = jnp.concatenate([vp_ref[0, :, cols], vc_ref[0, :, cols]], axis=0)
        outs, lses = [], []
        for hh in range(2):
            s = _dot_nt(qh[hh], kcat) + bias_ref[0, hp, hh * BLOCK:(hh + 1) * BLOCK, :]
            m = jnp.max(s, axis=-1, keepdims=True)
            e = jnp.exp(s - m)
            den = jnp.sum(e, axis=-1, keepdims=True)
            outs.append(_dot(e.astype(BF16), vcat) / den)
            lses.append(jnp.broadcast_to(m + jnp.log(den), (BLOCK, LANES)))
        o_ref[0, :, cols] = jnp.where(lane < HEAD_DIM, outs[0], outs[1])
        lse_ref[0, :, cols] = jnp.where(lane < HEAD_DIM, lses[0], lses[1])


def _attn_b_group(qkv, bias_g, g, dil):
    bsz, seq, width = qkv.shape
    m_len = seq // dil
    nblk = m_len // BLOCK
    view = qkv.reshape(bsz, m_len, dil * width)
    per_r = width // B_W
    q0 = 3 * A_W // B_W + g
    k0 = q0 + len(B_GROUPS)
    v0 = k0 + len(B_GROUPS)
    cur = lambda c0: pl.BlockSpec((1, BLOCK, B_W), lambda b, r, n: (b, n, r * per_r + c0))
    prev = lambda c0: pl.BlockSpec((1, BLOCK, B_W),
                                   lambda b, r, n: (b, jnp.maximum(n - 1, 0), r * per_r + c0))
    out_spec = pl.BlockSpec((1, BLOCK, B_W), lambda b, r, n: (b, n, r))
    out_sds = jax.ShapeDtypeStruct((bsz, m_len, dil * B_W), F32)
    o, lse = pl.pallas_call(
        _attn_b_kernel,
        grid=(bsz, dil, nblk),
        in_specs=[
            cur(q0), prev(k0), cur(k0), prev(v0), cur(v0),
            pl.BlockSpec((1, B_HEADS // 2, 2 * BLOCK, 2 * BLOCK),
                         lambda b, r, n: (jnp.minimum(n, 1), 0, 0, 0)),
        ],
        out_specs=[out_spec, out_spec],
        out_shape=[out_sds, out_sds],
        compiler_params=_params("parallel", "parallel", "arbitrary"),
        name=f"attn_dilated_g{g}",
    )(view, view, view, view, view, bias_g)
    return o.reshape(bsz, seq, B_W), lse.reshape(bsz, seq, B_W)


def _merge_kernel(oa_ref, ob0, ob1, ob2, ls0, ls1, ls2, oc_ref, gate_ref, x_ref, mod_ref,
                  wa_ref, wb_ref, wc_ref, wo_ref, o_ref):
    l0, l1, l2 = ls0[...], ls1[...], ls2[...]
    mx = jnp.maximum(jnp.maximum(l0, l1), l2)
    e0, e1, e2 = jnp.exp(l0 - mx), jnp.exp(l1 - mx), jnp.exp(l2 - mx)
    den = e0 + e1 + e2
    ob = (e0 / den) * ob0[...] + (e1 / den) * ob1[...] + (e2 / den) * ob2[...]
    d = D_MODEL
    merged = (gate_ref[:, 0:d] * _dot(oa_ref[...], wa_ref[...])
              + gate_ref[:, d:2 * d] * _dot(ob.astype(BF16), wb_ref[...])
              + gate_ref[:, 2 * d:3 * d] * _dot(oc_ref[...], wc_ref[...]))
    y = _dot(merged.astype(BF16), wo_ref[...])
    o_ref[...] = x_ref[...] + mod_ref[0][2:3] * y


def _merge(oa, obs, lses, oc, gf, x2d, mod3, wa, wb, wc, wo, *, seq, tm):
    rows, d = x2d.shape
    per_b = seq // tm
    row = lambda w: pl.BlockSpec((tm, w), lambda i: (i, 0))
    full = lambda a: pl.BlockSpec(a.shape, lambda i: (0, 0))
    return pl.pallas_call(
        _merge_kernel,
        grid=(rows // tm,),
        in_specs=[row(A_W)] + [row(B_W)] * 6 + [row(C_W), row(GATE_W), row(d),
                  pl.BlockSpec((1, 6, d), lambda i: (i // per_b, 0, 0)),
                  full(wa), full(wb), full(wc), full(wo)],
        out_specs=row(d),
        out_shape=jax.ShapeDtypeStruct((rows, d), F32),
        compiler_params=_params("parallel"),
        name="merge_outproj",
    )(oa, *obs, *lses, oc, gf, x2d, mod3, wa, wb, wc, wo)


def _top2_gate(logits):
    idx = lax.broadcasted_iota(jnp.int32, logits.shape, 1)
    n = logits.shape[1]
    m1 = jnp.max(logits, axis=-1, keepdims=True)
    i1 = jnp.min(jnp.where(logits == m1, idx, n), axis=-1, keepdims=True)
    first = idx == i1
    rest = jnp.where(first, -jnp.inf, logits)
    m2 = jnp.max(rest, axis=-1, keepdims=True)
    i2 = jnp.min(jnp.where(rest == m2, idx, n), axis=-1, keepdims=True)
    second = idx == i2
    e = jnp.exp(m2 - m1)
    den = 1.0 + e
    return jnp.where(first, 1.0 / den, 0.0) + jnp.where(second, e / den, 0.0)


def _ffn_kernel(x_ref, mod_ref, g_ref, wg_ref, wu_ref, wd_ref, o_ref, h_scr, acc_scr, *, nf):
    f = pl.program_id(1)

    @pl.when(f == 0)
    def _():
        m = mod_ref[0]
        h_scr[...] = _rms_mod(x_ref[...], g_ref[...], m[4:5], m[3:4]).astype(BF16)
        acc_scr[...] = jnp.zeros(acc_scr.shape, F32)

    h = h_scr[...]
    act = (_silu(_dot(h, wg_ref[...])) * _dot(h, wu_ref[...])).astype(BF16)
    acc_scr[...] += _dot(act, wd_ref[...])

    @pl.when(f == nf - 1)
    def _():
        o_ref[...] = x_ref[...] + mod_ref[0][5:6] * acc_scr[...]


def _ffn(x2d, mod3, g, wg, wu, wd, *, seq, tm, tf):
    rows, d = x2d.shape
    dff = wg.shape[1]
    nf = dff // tf
    per_b = seq // tm
    return pl.pallas_call(
        functools.partial(_ffn_kernel, nf=nf),
        grid=(rows // tm, nf),
        in_specs=[
            pl.BlockSpec((tm, d), lambda i, f: (i, 0)),
            pl.BlockSpec((1, 6, d), lambda i, f: (i // per_b, 0, 0)),
            pl.BlockSpec((1, d), lambda i, f: (0, 0)),
            pl.BlockSpec((d, tf), lambda i, f: (0, f)),
            pl.BlockSpec((d, tf), lambda i, f: (0, f)),
            pl.BlockSpec((tf, d), lambda i, f: (f, 0)),
        ],
        out_specs=pl.BlockSpec((tm, d), lambda i, f: (i, 0)),
        out_shape=jax.ShapeDtypeStruct((rows, d), F32),
        scratch_shapes=[pltpu.VMEM((tm, d), BF16), pltpu.VMEM((tm, d), F32)],
        compiler_params=_params("parallel", "arbitrary"),
        name="ffn_dense",
    )(x2d, mod3, g, wg, wu, wd)


def _moe_kernel(x_ref, mod_ref, g_ref, wr_ref, wg_ref, wu_ref, wd_ref, o_ref,
                h_scr, gate_scr, acc_scr, *, nf, ne):
    e = pl.program_id(1)
    f = pl.program_id(2)

    @pl.when((e == 0) & (f == 0))
    def _():
        m = mod_ref[0]
        h = _rms_mod(x_ref[...], g_ref[...], m[4:5], m[3:4])
        h_scr[...] = h.astype(BF16)
        logits = jnp.dot(h, wr_ref[...], preferred_element_type=F32, precision=lax.Precision.HIGHEST)
        gate_scr[...] = _top2_gate(logits)
        o_ref[...] = jnp.zeros(o_ref.shape, F32)

    @pl.when(f == 0)
    def _():
        acc_scr[...] = jnp.zeros(acc_scr.shape, F32)

    h = h_scr[...]
    act = (_silu(_dot(h, wg_ref[0])) * _dot(h, wu_ref[0])).astype(BF16)
    acc_scr[...] += _dot(act, wd_ref[0])

    @pl.when(f == nf - 1)
    def _():
        gate = gate_scr[...]
        idx = lax.broadcasted_iota(jnp.int32, gate.shape, 1)
        ge = jnp.sum(jnp.where(idx == e, gate, 0.0), axis=-1, keepdims=True)
        o_ref[...] += ge * acc_scr[...]

    @pl.when((e == ne - 1) & (f == nf - 1))
    def _():
        o_ref[...] = x_ref[...] + mod_ref[0][5:6] * o_ref[...]


def _moe(x2d, mod3, g, wr, wg, wu, wd, *, seq, tm, tf):
    rows, d = x2d.shape
    ne, _, dff = wg.shape
    nf = dff // tf
    per_b = seq // tm
    return pl.pallas_call(
        functools.partial(_moe_kernel, nf=nf, ne=ne),
        grid=(rows // tm, ne, nf),
        in_specs=[
            pl.BlockSpec((tm, d), lambda i, e, f: (i, 0)),
            pl.BlockSpec((1, 6, d), lambda i, e, f: (i // per_b, 0, 0)),
            pl.BlockSpec((1, d), lambda i, e, f: (0, 0)),
            pl.BlockSpec((d, ne), lambda i, e, f: (0, 0)),
            pl.BlockSpec((1, d, tf), lambda i, e, f: (e, 0, f)),
            pl.BlockSpec((1, d, tf), lambda i, e, f: (e, 0, f)),
            pl.BlockSpec((1, tf, d), lambda i, e, f: (e, f, 0)),
        ],
        out_specs=pl.BlockSpec((tm, d), lambda i, e, f: (i, 0)),
        out_shape=jax.ShapeDtypeStruct((rows, d), F32),
        scratch_shapes=[pltpu.VMEM((tm, d), BF16), pltpu.VMEM((tm, ne), F32),
                        pltpu.VMEM((tm, d), F32)],
        compiler_params=_params("parallel", "arbitrary", "arbitrary"),
        name="moe_dense",
    )(x2d, mod3, g, wr, wg, wu, wd)


def _final_norm_kernel(x_ref, g_ref, o_ref):
    x = x_ref[...]
    o_ref[...] = (x * lax.rsqrt(jnp.mean(x * x, axis=-1, keepdims=True) + EPS)) * g_ref[...]


def _final_norm(x2d, g, *, tm):
    rows, d = x2d.shape
    return pl.pallas_call(
        _final_norm_kernel,
        grid=(rows // tm,),
        in_specs=[pl.BlockSpec((tm, d), lambda i: (i, 0)), pl.BlockSpec((1, d), lambda i: (0, 0))],
        out_specs=pl.BlockSpec((tm, d), lambda i: (i, 0)),
        out_shape=jax.ShapeDtypeStruct((rows, d), F32),
        compiler_params=_params("parallel"),
        name="final_norm",
    )(x2d, g)


def _t5_bucket(dist):
    n = jnp.maximum(dist, 0)
    max_exact = N_BUCKETS // 2
    nf = jnp.maximum(n, 1).astype(F32)
    large = max_exact + (jnp.log(nf / max_exact) / math.log(REL_MAX_DIST / max_exact)
                         * (N_BUCKETS - max_exact)).astype(jnp.int32)
    large = jnp.minimum(large, N_BUCKETS - 1)
    return jnp.where(n < max_exact, n, large)


def _bias_tiles_a(rel_bias, seq, t):
    nq = seq // t
    ncol = 2 * A_HEADS
    tab = rel_bias[:, :ncol][_t5_bucket(jnp.arange(seq))].astype(F32)
    vneg = jnp.full((t, ncol), NEG, F32)
    v = jnp.concatenate([vneg, tab], axis=0)
    u = jnp.concatenate([v[1:seq + 1][::-1], vneg[:1], v[seq + 1:seq + t][::-1]], axis=0)
    n = seq + t - 1
    r = jnp.tile(u, (t, 1))[:t * n].reshape(t, n, ncol)[:, :seq]
    r = r.reshape(t, nq, t, A_HEADS, 2)[:, ::-1]
    return r.transpose(3, 1, 4, 0, 2).reshape(A_HEADS, nq, 2 * t, t)


def _bias_tiles_b(rel_bias, g, dil, n_back):
    qi = jnp.arange(BLOCK)[:, None]
    kj = jnp.arange(2 * BLOCK)[None, :]
    dm = qi + BLOCK - kj
    tab = rel_bias[:, 2 * A_HEADS + g * B_HEADS:2 * A_HEADS + (g + 1) * B_HEADS]
    vals = jnp.moveaxis(tab[_t5_bucket(dm * dil)], -1, 0)
    valid = (dm >= 0) & (dm <= n_back)
    later = jnp.where(valid[None], vals, NEG)
    first = jnp.where((valid & (kj >= BLOCK))[None], vals, NEG)
    both = jnp.stack([first, later])
    return both.reshape(2, B_HEADS // 2, 2 * BLOCK, 2 * BLOCK).astype(F32)


def kernel(x, c, norm_mix_g, norm_ffn_g, w_mod, b_mod, w_in, b_forget, lam_q1, lam_k1, lam_q2, lam_k2,
           subln_g, rel_bias, w_br_a, w_br_b, w_br_c, w_out, w_ff_gate, w_ff_up, w_ff_down, w_router,
           w_exp_gate, w_exp_up, w_exp_down, final_norm_g):
    bsz, seq, d = x.shape
    depth = w_mod.shape[0]
    rows = bsz * seq
    t_attn = 256
    x2d = x.reshape(rows, d)

    mod = _modulation(c, w_mod, b_mod)
    bias_a = _bias_tiles_a(rel_bias, seq, t_attn)
    bias_b = [_bias_tiles_b(rel_bias, g, dil, win // dil) for g, (win, dil) in enumerate(B_GROUPS)]
    f_col = QKV_W
    g_col = QKV_W + C_HEADS

    for l in range(depth):
        lam_init = 0.8 - 0.6 * math.exp(-0.3 * l)
        mod3 = mod[l].reshape(bsz, 6, d)
        w_l = w_in[l]
        w_qkv = w_l[:, :QKV_W].astype(BF16)
        w_f = jnp.pad(w_l[:, f_col:f_col + C_HEADS], ((0, 0), (0, LANES - C_HEADS)))
        w_gf = jnp.concatenate([w_l[:, g_col:], w_f], axis=1).astype(BF16)

        g_mix = norm_mix_g[l].reshape(1, d)
        qkv = _inproj(x2d, mod3, g_mix, w_qkv, seq=seq, tm=1024, tn=1536, out_dtype=BF16)
        gf = _inproj(x2d, mod3, g_mix, w_gf, seq=seq, tm=1024, tn=640, out_dtype=F32,
                     n_sigmoid=GATE_W)
        qkv = qkv.reshape(bsz, seq, QKV_W)

        b_f8 = jnp.pad(b_forget[l], (0, 8 - C_HEADS)).reshape(8, 1)
        fcum = _forget_cumsum(gf.reshape(bsz, seq, GATE_W + LANES), b_f8, col_block=GATE_W // LANES)
        fcum = fcum[:, :C_HEADS].reshape(bsz, C_HEADS // 2, 2, seq)

        oa = _attn_a(qkv, bias_a, lam_q1[l].reshape(1, -1), lam_k1[l].reshape(1, -1),
                     lam_q2[l].reshape(1, -1), lam_k2[l].reshape(1, -1), subln_g[l].reshape(1, -1),
                     lam_init=lam_init, t=t_attn, nsb=2)
        oc = _attn_c(qkv, fcum, t=t_attn, nsb=2)
        obs, lses = [], []
        for g, (win, dil) in enumerate(B_GROUPS):
            o_g, lse_g = _attn_b_group(qkv, bias_b[g], g, dil)
            obs.append(o_g.reshape(rows, B_W))
            lses.append(lse_g.reshape(rows, B_W))

        x2d = _merge(oa.reshape(rows, A_W), obs, lses, oc.reshape(rows, C_W), gf, x2d, mod3,
                     w_br_a[l].astype(BF16), w_br_b[l].astype(BF16), w_br_c[l].astype(BF16),
                     w_out[l].astype(BF16), seq=seq, tm=512)

        g_ffn = norm_ffn_g[l].reshape(1, d)
        if l % 2 == 0:
            j = l // 2
            x2d = _ffn(x2d, mod3, g_ffn, w_ff_gate[j].astype(BF16), w_ff_up[j].astype(BF16),
                       w_ff_down[j].astype(BF16), seq=seq, tm=512, tf=1408)
        else:
            j = l // 2
            x2d = _moe(x2d, mod3, g_ffn, w_router[j], w_exp_gate[j].astype(BF16),
                       w_exp_up[j].astype(BF16), w_exp_down[j].astype(BF16), seq=seq, tm=1024, tf=512)

    return _final_norm(x2d, final_norm_g.reshape(1, d), tm=1024).reshape(bsz, seq, d)
```

```python
import functools
import math

import jax
import jax.numpy as jnp
from jax import lax
from jax.experimental import pallas as pl
from jax.experimental.pallas import tpu as pltpu

F32 = jnp.float32
BF16 = jnp.bfloat16

D_MODEL = 1024
HEAD_DIM = 64
LANES = 128
A_HEADS = 4
A_W = A_HEADS * 2 * HEAD_DIM
B_GROUPS = ((128, 1), (512, 4), (2048, 16))
B_HEADS = 6
B_W = B_HEADS * HEAD_DIM
B_QW = len(B_GROUPS) * B_W
C_HEADS = 6
C_W = C_HEADS * HEAD_DIM
N_BRANCH = 3
BLOCK = 128
N_BUCKETS = 32
REL_MAX_DIST = 2048
N_EXPERTS = 8
EPS = 1e-6
QKV_W = 3 * A_W + 3 * B_QW + 3 * C_W
GATE_W = N_BRANCH * D_MODEL
NEG = -1e30
SCALE = HEAD_DIM ** -0.5
VMEM_LIMIT = 56 * 1024 * 1024


def _params(*sem):
    return pltpu.CompilerParams(dimension_semantics=sem, vmem_limit_bytes=VMEM_LIMIT)


def _rms_mod(x, g, sc, sh):
    y = x * lax.rsqrt(jnp.mean(x * x, axis=-1, keepdims=True) + EPS)
    return (y * g) * (1.0 + sc) + sh


def _dot(a, b):
    return jnp.dot(a, b, preferred_element_type=F32)


def _dot_nt(a, b):
    return lax.dot_general(a, b, (((1,), (1,)), ((), ())), preferred_element_type=F32)


def _silu(a):
    return a * jax.nn.sigmoid(a)


def _mod_kernel(c_ref, w_ref, b_ref, o_ref):
    a = _silu(c_ref[...]).astype(BF16)
    o_ref[0] = _dot(a, w_ref[0].astype(BF16)) + b_ref[0]


def _modulation(c, w_mod, b_mod):
    depth, d, n = w_mod.shape
    bsz = c.shape[0]
    tn = 1536
    return pl.pallas_call(
        _mod_kernel,
        grid=(depth, n // tn),
        in_specs=[
            pl.BlockSpec((bsz, d), lambda l, j: (0, 0)),
            pl.BlockSpec((1, d, tn), lambda l, j: (l, 0, j)),
            pl.BlockSpec((1, 1, tn), lambda l, j: (l, 0, j)),
        ],
        out_specs=pl.BlockSpec((1, bsz, tn), lambda l, j: (l, 0, j)),
        out_shape=jax.ShapeDtypeStruct((depth, bsz, n), F32),
        compiler_params=_params("parallel", "parallel"),
        name="modulation",
    )(c, w_mod, b_mod.reshape(depth, 1, n))


def _inproj_kernel(x_ref, mod_ref, g_ref, w_ref, o_ref, h_scr, *, n_sigmoid, tn):
    j = pl.program_id(1)

    @pl.when(j == 0)
    def _():
        m = mod_ref[0]
        h_scr[...] = _rms_mod(x_ref[...], g_ref[...], m[1:2], m[0:1]).astype(BF16)

    y = _dot(h_scr[...], w_ref[...])
    if n_sigmoid:
        col = j * tn + lax.broadcasted_iota(jnp.int32, y.shape, 1)
        y = jnp.where(col < n_sigmoid, jax.nn.sigmoid(y), y)
    o_ref[...] = y.astype(o_ref.dtype)


def _inproj(x2d, mod3, g, w, *, seq, tm, tn, out_dtype, n_sigmoid=0):
    rows, d = x2d.shape
    n = w.shape[1]
    per_b = seq // tm
    return pl.pallas_call(
        functools.partial(_inproj_kernel, n_sigmoid=n_sigmoid, tn=tn),
        grid=(rows // tm, n // tn),
        in_specs=[
            pl.BlockSpec((tm, d), lambda i, j: (i, 0)),
            pl.BlockSpec((1, 6, d), lambda i, j: (i // per_b, 0, 0)),
            pl.BlockSpec((1, d), lambda i, j: (0, 0)),
            pl.BlockSpec((d, tn), lambda i, j: (0, j)),
        ],
        out_specs=pl.BlockSpec((tm, tn), lambda i, j: (i, j)),
        out_shape=jax.ShapeDtypeStruct((rows, n), out_dtype),
        scratch_shapes=[pltpu.VMEM((tm, d), BF16)],
        compiler_params=_params("parallel", "arbitrary"),
        name="inproj",
    )(x2d, mod3, g, w)


def _fcum_kernel(f_ref, b_ref, o_ref):
    z = f_ref[0].T[:8] + b_ref[...]
    x = jnp.minimum(z, 0.0) - jnp.log1p(jnp.exp(-jnp.abs(z)))
    s = x.shape[1]
    lane = lax.broadcasted_iota(jnp.int32, x.shape, 1)
    k = 1
    while k < s:
        x = x + jnp.where(lane >= k, pltpu.roll(x, k, 1), 0.0)
        k *= 2
    o_ref[0] = x


def _forget_cumsum(gf, b_f8, *, col_block):
    bsz, seq, _ = gf.shape
    return pl.pallas_call(
        _fcum_kernel,
        grid=(bsz,),
        in_specs=[
            pl.BlockSpec((1, seq, LANES), lambda b: (b, 0, col_block)),
            pl.BlockSpec((8, 1), lambda b: (0, 0)),
        ],
        out_specs=pl.BlockSpec((1, 8, seq), lambda b: (b, 0, 0)),
        out_shape=jax.ShapeDtypeStruct((bsz, 8, seq), F32),
        compiler_params=_params("parallel"),
        name="forget_cumsum",
    )(gf, b_f8)


def _half_masks(q):
    lane = lax.broadcasted_iota(jnp.int32, q.shape, 1)
    zero = jnp.zeros_like(q)
    return jnp.where(lane < HEAD_DIM, q, zero), jnp.where(lane >= HEAD_DIM, q, zero)


def _flash_init(first, v_ref, vext_scr, m_scr, acc_scr):
    @pl.when(first)
    def _():
        vext_scr[:, :LANES] = v_ref[0]
        vext_scr[:, LANES:] = jnp.ones((vext_scr.shape[0], LANES), BF16)

    m_scr[...] = jnp.full(m_scr.shape, -jnp.inf, F32)
    acc_scr[...] = jnp.zeros(acc_scr.shape, F32)


def _lane_tile(a, n):
    return a if n == 1 else jnp.concatenate([a] * n, axis=1)


def _flash_update(s, vext, rows, m_scr, acc_scr):
    m_prev = m_scr[rows]
    m_new = jnp.maximum(m_prev, jnp.max(s, axis=-1, keepdims=True))
    alpha = jnp.exp(m_prev - m_new)
    p = jnp.exp(s - _lane_tile(m_new, s.shape[1] // LANES))
    acc_scr[rows] = _lane_tile(alpha, 2) * acc_scr[rows] + _dot(p.astype(BF16), vext)
    m_scr[rows] = m_new


def _flash_result(acc_scr):
    acc = acc_scr[...]
    return acc[:, :LANES] / acc[:, LANES:]


def _attn_a_kernel(q_ref, k_ref, v_ref, bias_ref, lq1, lk1, lq2, lk2, sg_ref, o_ref,
                   vext_scr, m_scr, acc_scr, *, t, nsb, lam_init):
    qi = pl.program_id(2)
    _flash_init(qi == 0, v_ref, vext_scr, m_scr, acc_scr)
    qh = [_half_masks(q_ref[0, sb * t:(sb + 1) * t, :] * SCALE) for sb in range(nsb)]

    def step(kb, sbs, delta_of):
        off = pl.multiple_of(kb * t, t)
        kblk = k_ref[0, pl.ds(off, t), :]
        vext = vext_scr[pl.ds(off, t), :]
        for sb in sbs:
            for hh in range(2):
                s = _dot_nt(qh[sb][hh], kblk) + bias_ref[0, delta_of(sb), hh * t:(hh + 1) * t, :]
                _flash_update(s, vext, pl.ds((sb * 2 + hh) * t, t), m_scr, acc_scr)

    def body(kb, carry):
        step(kb, range(nsb), lambda sb: qi * nsb + sb - kb)
        return carry

    lax.fori_loop(0, qi * nsb, body, 0)
    for j in range(nsb):
        step(qi * nsb + j, range(j, nsb), lambda sb: sb - j)

    o = _flash_result(acc_scr)
    lam = (jnp.exp(jnp.sum(lq1[...] * lk1[...], axis=-1, keepdims=True))
           - jnp.exp(jnp.sum(lq2[...] * lk2[...], axis=-1, keepdims=True)) + lam_init)
    for sb in range(nsb):
        d = o[2 * sb * t:(2 * sb + 1) * t] - lam * o[(2 * sb + 1) * t:(2 * sb + 2) * t]
        y = d * lax.rsqrt(jnp.mean(d * d, axis=-1, keepdims=True) + EPS)
        o_ref[0, sb * t:(sb + 1) * t, :] = ((y * sg_ref[...]) * (1.0 - lam_init)).astype(o_ref.dtype)


def _flash_scratch(seq, t, nsb):
    return [pltpu.VMEM((seq, 2 * LANES), BF16), pltpu.VMEM((2 * nsb * t, LANES), F32),
            pltpu.VMEM((2 * nsb * t, 2 * LANES), F32)]


def _attn_a(qkv, bias_a, lq1, lk1, lq2, lk2, subln_g, *, lam_init, t, nsb):
    bsz, seq, _ = qkv.shape
    tq = t * nsb
    vec = lambda n: pl.BlockSpec((1, n), lambda h, b, i: (0, 0))
    return pl.pallas_call(
        functools.partial(_attn_a_kernel, t=t, nsb=nsb, lam_init=lam_init),
        grid=(A_HEADS, bsz, seq // tq),
        in_specs=[
            pl.BlockSpec((1, tq, LANES), lambda h, b, i: (b, i, h)),
            pl.BlockSpec((1, seq, LANES), lambda h, b, i: (b, 0, A_HEADS + h)),
            pl.BlockSpec((1, seq, LANES), lambda h, b, i: (b, 0, 2 * A_HEADS + h)),
            pl.BlockSpec((1, seq // t, 2 * t, t), lambda h, b, i: (h, 0, 0, 0)),
            vec(HEAD_DIM), vec(HEAD_DIM), vec(HEAD_DIM), vec(HEAD_DIM), vec(LANES),
        ],
        out_specs=pl.BlockSpec((1, tq, LANES), lambda h, b, i: (b, i, h)),
        out_shape=jax.ShapeDtypeStruct((bsz, seq, A_W), BF16),
        scratch_shapes=_flash_scratch(seq, t, nsb),
        compiler_params=_params("parallel", "parallel", "arbitrary"),
        name="attn_diff",
    )(qkv, qkv, qkv, bias_a, lq1, lk1, lq2, lk2, subln_g)


def _attn_c_kernel(q_ref, k_ref, v_ref, f_ref, o_ref, vext_scr, m_scr, acc_scr, *, t, nsb):
    qi = pl.program_id(2)
    _flash_init(qi == 0, v_ref, vext_scr, m_scr, acc_scr)
    qh = [_half_masks(q_ref[0, sb * t:(sb + 1) * t, :] * SCALE) for sb in range(nsb)]
    q_off = pl.multiple_of(qi * (t * nsb), t * nsb)
    f_anchor = f_ref[0, 0, :, pl.ds(q_off, LANES)][:, :1]
    r = lax.broadcasted_iota(jnp.int32, (t, t), 0)
    c = lax.broadcasted_iota(jnp.int32, (t, t), 1)

    def step(kb, sbs, diag_sb):
        off = pl.multiple_of(kb * t, t)
        kblk = k_ref[0, pl.ds(off, t), :]
        vext = vext_scr[pl.ds(off, t), :]
        dec = f_anchor - f_ref[0, 0, :, pl.ds(off, t)]
        for sb in sbs:
            for hh in range(2):
                s = _dot_nt(qh[sb][hh], kblk) + dec[hh:hh + 1]
                if sb == diag_sb:
                    s = jnp.where(r >= c, s, NEG)
                _flash_update(s, vext, pl.ds((sb * 2 + hh) * t, t), m_scr, acc_scr)

    def body(kb, carry):
        step(kb, range(nsb), None)
        return carry

    lax.fori_loop(0, qi * nsb, body, 0)
    for j in range(nsb):
        step(qi * nsb + j, range(j, nsb), j)

    o = _flash_result(acc_scr)
    lane = lax.broadcasted_iota(jnp.int32, (t, LANES), 1)
    for sb in range(nsb):
        pair = jnp.where(lane < HEAD_DIM, o[2 * sb * t:(2 * sb + 1) * t], o[(2 * sb + 1) * t:(2 * sb + 2) * t])
        o_ref[0, sb * t:(sb + 1) * t, :] = pair.astype(o_ref.dtype)


def _attn_c(qkv, fcum, *, t, nsb):
    bsz, seq, _ = qkv.shape
    tq = t * nsb
    pairs = C_HEADS // 2
    q0 = (3 * A_W + 3 * B_QW) // LANES
    return pl.pallas_call(
        functools.partial(_attn_c_kernel, t=t, nsb=nsb),
        grid=(pairs, bsz, seq // tq),
        in_specs=[
            pl.BlockSpec((1, tq, LANES), lambda p, b, i: (b, i, q0 + p)),
            pl.BlockSpec((1, seq, LANES), lambda p, b, i: (b, 0, q0 + pairs + p)),
            pl.BlockSpec((1, seq, LANES), lambda p, b, i: (b, 0, q0 + 2 * pairs + p)),
            pl.BlockSpec((1, 1, 2, seq), lambda p, b, i: (b, p, 0, 0)),
        ],
        out_specs=pl.BlockSpec((1, tq, LANES), lambda p, b, i: (b, i, p)),
        out_shape=jax.ShapeDtypeStruct((bsz, seq, C_W), BF16),
        scratch_shapes=_flash_scratch(seq, t, nsb),
        compiler_params=_params("parallel", "parallel", "arbitrary"),
        name="attn_forget",
    )(qkv, qkv, qkv, fcum)


def _attn_b_kernel(q_ref, kp_ref, kc_ref, vp_ref, vc_ref, bias_ref, o_ref, lse_ref):
    lane = lax.broadcasted_iota(jnp.int32, (BLOCK, LANES), 1)
    for hp in range(B_HEADS // 2):
        cols = slice(hp * LANES, (hp + 1) * LANES)
        qh = _half_masks(q_ref[0, :, cols] * SCALE)
        kcat = jnp.concatenate([kp_ref[0, :, cols], kc_ref[0, :, cols]], axis=0)
        vcat = jnp.concatenate([vp_ref[0, :, cols], vc_ref[0, :, cols]], axis=0)
        outs, lses = [], []
        for hh in range(2):
            s = _dot_nt(qh[hh], kcat) + bias_ref[0, hp, hh * BLOCK:(hh + 1) * BLOCK, :]
            m = jnp.max(s, axis=-1, keepdims=True)
            e = jnp.exp(s - m)
            den = jnp.sum(e, axis=-1, keepdims=True)
            outs.append(_dot(e.astype(BF16), vcat) / den)
            lses.append(jnp.broadcast_to(m + jnp.log(den), (BLOCK, LANES)))
        o_ref[0, :, cols] = jnp.where(lane < HEAD_DIM, outs[0], outs[1])
        lse_ref[0, :, cols] = jnp.where(lane < HEAD_DIM, lses[0], lses[1])


def _attn_b_group(qkv, bias_g, g, dil):
    bsz, seq, width = qkv.shape
    m_len = seq // dil
    nblk = m_len // BLOCK
    view = qkv.reshape(bsz, m_len, dil * width)
    per_r = width // B_W
    q0 = 3 * A_W // B_W + g
    k0 = q0 + len(B_GROUPS)
    v0 = k0 + len(B_GROUPS)
    cur = lambda c0: pl.BlockSpec((1, BLOCK, B_W), lambda b, r, n: (b, n, r * per_r + c0))
    prev = lambda c0: pl.BlockSpec((1, BLOCK, B_W),
                                   lambda b, r, n: (b, jnp.maximum(n - 1, 0), r * per_r + c0))
    out_spec = pl.BlockSpec((1, BLOCK, B_W), lambda b, r, n: (b, n, r))
    out_sds = jax.ShapeDtypeStruct((bsz, m_len, dil * B_W), F32)
    o, lse = pl.pallas_call(
        _attn_b_kernel,
        grid=(bsz, dil, nblk),
        in_specs=[
            cur(q0), prev(k0), cur(k0), prev(v0), cur(v0),
            pl.BlockSpec((1, B_HEADS // 2, 2 * BLOCK, 2 * BLOCK),
                         lambda b, r, n: (jnp.minimum(n, 1), 0, 0, 0)),
        ],
        out_specs=[out_spec, out_spec],
        out_shape=[out_sds, out_sds],
        compiler_params=_params("parallel", "parallel", "arbitrary"),
        name=f"attn_dilated_g{g}",
    )(view, view, view, view, view, bias_g)
    return o.reshape(bsz, seq, B_W), lse.reshape(bsz, seq, B_W)


def _merge_kernel(oa_ref, ob0, ob1, ob2, ls0, ls1, ls2, oc_ref, gate_ref, x_ref, mod_ref,
                  wa_ref, wb_ref, wc_ref, wo_ref, o_ref):
    l0, l1, l2 = ls0[...], ls1[...], ls2[...]
    mx = jnp.maximum(jnp.maximum(l0, l1), l2)
    e0, e1, e2 = jnp.exp(l0 - mx), jnp.exp(l1 - mx), jnp.exp(l2 - mx)
    den = e0 + e1 + e2
    ob = (e0 / den) * ob0[...] + (e1 / den) * ob1[...] + (e2 / den) * ob2[...]
    d = D_MODEL
    merged = (gate_ref[:, 0:d] * _dot(oa_ref[...], wa_ref[...])
              + gate_ref[:, d:2 * d] * _dot(ob.astype(BF16), wb_ref[...])
              + gate_ref[:, 2 * d:3 * d] * _dot(oc_ref[...], wc_ref[...]))
    y = _dot(merged.astype(BF16), wo_ref[...])
    o_ref[...] = x_ref[...] + mod_ref[0][2:3] * y


def _merge(oa, obs, lses, oc, gf, x2d, mod3, wa, wb, wc, wo, *, seq, tm):
    rows, d = x2d.shape
    per_b = seq // tm
    row = lambda w: pl.BlockSpec((tm, w), lambda i: (i, 0))
    full = lambda a: pl.BlockSpec(a.shape, lambda i: (0, 0))
    return pl.pallas_call(
        _merge_kernel,
        grid=(rows // tm,),
        in_specs=[row(A_W)] + [row(B_W)] * 6 + [row(C_W), row(GATE_W), row(d),
                  pl.BlockSpec((1, 6, d), lambda i: (i // per_b, 0, 0)),
                  full(wa), full(wb), full(wc), full(wo)],
        out_specs=row(d),
        out_shape=jax.ShapeDtypeStruct((rows, d), F32),
        compiler_params=_params("parallel"),
        name="merge_outproj",
    )(oa, *obs, *lses, oc, gf, x2d, mod3, wa, wb, wc, wo)


def _ffn_kernel(x_ref, mod_ref, g_ref, wg_ref, wu_ref, wd_ref, o_ref, h_scr, acc_scr, *, nf):
    f = pl.program_id(1)

    @pl.when(f == 0)
    def _():
        m = mod_ref[0]
        h_scr[...] = _rms_mod(x_ref[...], g_ref[...], m[4:5], m[3:4]).astype(BF16)
        acc_scr[...] = jnp.zeros(acc_scr.shape, F32)

    h = h_scr[...]
    act = (_silu(_dot(h, wg_ref[...])) * _dot(h, wu_ref[...])).astype(BF16)
    acc_scr[...] += _dot(act, wd_ref[...])

    @pl.when(f == nf - 1)
    def _():
        o_ref[...] = x_ref[...] + mod_ref[0][5:6] * acc_scr[...]


def _ffn(x2d, mod3, g, wg, wu, wd, *, seq, tm, tf):
    rows, d = x2d.shape
    dff = wg.shape[1]
    nf = dff // tf
    per_b = seq // tm
    return pl.pallas_call(
        functools.partial(_ffn_kernel, nf=nf),
        grid=(rows // tm, nf),
        in_specs=[
            pl.BlockSpec((tm, d), lambda i, f: (i, 0)),
            pl.BlockSpec((1, 6, d), lambda i, f: (i // per_b, 0, 0)),
            pl.BlockSpec((1, d), lambda i, f: (0, 0)),
            pl.BlockSpec((d, tf), lambda i, f: (0, f)),
            pl.BlockSpec((d, tf), lambda i, f: (0, f)),
            pl.BlockSpec((tf, d), lambda i, f: (f, 0)),
        ],
        out_specs=pl.BlockSpec((tm, d), lambda i, f: (i, 0)),
        out_shape=jax.ShapeDtypeStruct((rows, d), F32),
        scratch_shapes=[pltpu.VMEM((tm, d), BF16), pltpu.VMEM((tm, d), F32)],
        compiler_params=_params("parallel", "arbitrary"),
        name="ffn_dense",
    )(x2d, mod3, g, wg, wu, wd)


def _route_kernel(x_ref, mod_ref, g_ref, wr_ref, tri_ref, h_ref, ridx_ref, rw_ref, cnt_ref, carry_scr):
    i = pl.program_id(0)

    @pl.when(i == 0)
    def _():
        carry_scr[...] = jnp.zeros(carry_scr.shape, F32)

    m = mod_ref[0]
    h = _rms_mod(x_ref[...], g_ref[...], m[4:5], m[3:4])
    h_ref[...] = h
    logits = jnp.dot(h, wr_ref[...], preferred_element_type=F32, precision=lax.Precision.HIGHEST)
    idx = lax.broadcasted_iota(jnp.int32, logits.shape, 1)
    n = logits.shape[1]
    m1 = jnp.max(logits, axis=-1, keepdims=True)
    i1 = jnp.min(jnp.where(logits == m1, idx, n), axis=-1, keepdims=True)
    first = idx == i1
    rest = jnp.where(first, -jnp.inf, logits)
    m2 = jnp.max(rest, axis=-1, keepdims=True)
    i2 = jnp.min(jnp.where(rest == m2, idx, n), axis=-1, keepdims=True)
    second = idx == i2
    e = jnp.exp(m2 - m1)
    den = 1.0 + e
    onehot = jnp.where(first | second, 1.0, 0.0)
    before = _dot(tri_ref[...], onehot.astype(BF16)) - onehot + carry_scr[...]
    rank1 = jnp.sum(jnp.where(first, before, 0.0), axis=-1, keepdims=True).astype(jnp.int32)
    rank2 = jnp.sum(jnp.where(second, before, 0.0), axis=-1, keepdims=True).astype(jnp.int32)
    carry_scr[...] += jnp.sum(onehot, axis=0, keepdims=True)
    ridx_ref[...] = jnp.where(idx == 0, i1, jnp.where(idx == 1, i2, jnp.where(idx == 2, rank1,
                              jnp.where(idx == 3, rank2, 0))))
    rw_ref[...] = jnp.where(idx == 0, 1.0 / den, jnp.where(idx == 1, e / den, 0.0))
    cnt_ref[...] = carry_scr[...]


def _route(x2d, mod3, g, wr, *, seq, tm):
    rows, d = x2d.shape
    ne = wr.shape[1]
    per_b = seq // tm
    tri = (jnp.arange(tm)[:, None] >= jnp.arange(tm)[None, :]).astype(BF16)
    row = lambda w: pl.BlockSpec((tm, w), lambda i: (i, 0))
    return pl.pallas_call(
        _route_kernel,
        grid=(rows // tm,),
        in_specs=[row(d), pl.BlockSpec((1, 6, d), lambda i: (i // per_b, 0, 0)),
                  pl.BlockSpec((1, d), lambda i: (0, 0)), pl.BlockSpec((d, ne), lambda i: (0, 0)),
                  pl.BlockSpec((tm, tm), lambda i: (0, 0))],
        out_specs=[row(d), row(ne), row(ne), pl.BlockSpec((1, ne), lambda i: (0, 0))],
        out_shape=[jax.ShapeDtypeStruct((rows, d), F32), jax.ShapeDtypeStruct((rows, ne), jnp.int32),
                   jax.ShapeDtypeStruct((rows, ne), F32), jax.ShapeDtypeStruct((1, ne), F32)],
        scratch_shapes=[pltpu.VMEM((1, ne), F32)],
        compiler_params=_params("arbitrary"),
        name="moe_route",
    )(x2d, mod3, g, wr, tri)


def _gather_rows(tok_of, n, src_hbm, dst, sem):
    def issue(r, carry):
        pltpu.make_async_copy(src_hbm.at[pl.ds(tok_of(r), 1)], dst.at[pl.ds(r, 1)], sem).start()
        return carry

    lax.fori_loop(0, n, issue, 0)


def _gather_wait(n, src_hbm, dst, sem):
    pltpu.make_async_copy(src_hbm.at[pl.ds(0, n)], dst, sem).wait()


def _expert_kernel(te_ref, src_ref, nu_ref, h_hbm, wg_ref, wu_ref, wd_ref, y_ref,
                   xbuf, xb_scr, acc_scr, sem, *, nf, r):
    i = pl.program_id(0)
    f = pl.program_id(1)
    active = i < nu_ref[0]
    slot = i % 2

    def gather(tile, s):
        _gather_rows(lambda j: src_ref[tile * r + j] >> 1, r, h_hbm, xbuf.at[s], sem.at[s])

    @pl.when((i == 0) & (f == 0))
    def _():
        gather(0, 0)

    @pl.when(active & (f == 0))
    def _():
        _gather_wait(r, h_hbm, xbuf.at[slot], sem.at[slot])

        @pl.when(i + 1 < nu_ref[0])
        def _():
            gather(i + 1, 1 - slot)

        xb_scr[...] = xbuf[slot].astype(BF16)
        acc_scr[...] = jnp.zeros(acc_scr.shape, F32)

    @pl.when(active)
    def _():
        xb = xb_scr[...]
        act = (_silu(_dot(xb, wg_ref[0])) * _dot(xb, wu_ref[0])).astype(BF16)
        acc_scr[...] += _dot(act, wd_ref[0])

    @pl.when(f == nf - 1)
    def _():
        y_ref[...] = jnp.where(active, acc_scr[...], 0.0)


def _experts(h2, tile_expert, src_pair, n_used, wg, wu, wd, *, r, tf):
    ne, d, dff = wg.shape
    nf = dff // tf
    p_rows = src_pair.shape[0]
    nt = p_rows // r
    fsel = lambda i, f, nu: jnp.where(i < nu[0], f, nf - 1)
    grid_spec = pltpu.PrefetchScalarGridSpec(
        num_scalar_prefetch=3,
        grid=(nt, nf),
        in_specs=[
            pl.BlockSpec(memory_space=pl.ANY),
            pl.BlockSpec((1, d, tf), lambda i, f, te, sp, nu: (te[i], 0, fsel(i, f, nu))),
            pl.BlockSpec((1, d, tf), lambda i, f, te, sp, nu: (te[i], 0, fsel(i, f, nu))),
            pl.BlockSpec((1, tf, d), lambda i, f, te, sp, nu: (te[i], fsel(i, f, nu), 0)),
        ],
        out_specs=pl.BlockSpec((r, d), lambda i, f, te, sp, nu: (i, 0)),
        scratch_shapes=[pltpu.VMEM((2, r, d), F32), pltpu.VMEM((r, d), BF16), pltpu.VMEM((r, d), F32),
                        pltpu.SemaphoreType.DMA((2,))],
    )
    return pl.pallas_call(
        functools.partial(_expert_kernel, nf=nf, r=r),
        grid_spec=grid_spec,
        out_shape=jax.ShapeDtypeStruct((p_rows, d), F32),
        compiler_params=_params("arbitrary", "arbitrary"),
        name="moe_experts",
    )(tile_expert, src_pair, n_used, h2, wg, wu, wd)


def _combine_kernel(pos_ref, y_hbm, x_ref, rw_ref, mod_ref, fg_ref, o_ref, ybuf, sem, *, tm, final):
    i = pl.program_id(0)
    n = pl.num_programs(0)
    slot = i % 2

    def gather(tile, s):
        for k in range(2):
            _gather_rows(lambda j: pos_ref[(tile * tm + j) * 2 + k], tm, y_hbm, ybuf.at[s, k], sem.at[s])

    @pl.when(i == 0)
    def _():
        gather(0, 0)

    for k in range(2):
        _gather_wait(tm, y_hbm, ybuf.at[slot, k], sem.at[slot])

    @pl.when(i + 1 < n)
    def _():
        gather(i + 1, 1 - slot)

    w = rw_ref[...]
    f = w[:, 0:1] * ybuf[slot, 0] + w[:, 1:2] * ybuf[slot, 1]
    x = x_ref[...] + mod_ref[0][5:6] * f
    if final:
        x = (x * lax.rsqrt(jnp.mean(x * x, axis=-1, keepdims=True) + EPS)) * fg_ref[...]
    o_ref[...] = x


def _combine(pos, y, x2d, rw, mod3, final_g, *, seq, tm, final):
    rows, d = x2d.shape
    per_b = seq // tm
    grid_spec = pltpu.PrefetchScalarGridSpec(
        num_scalar_prefetch=1,
        grid=(rows // tm,),
        in_specs=[
            pl.BlockSpec(memory_space=pl.ANY),
            pl.BlockSpec((tm, d), lambda i, p: (i, 0)),
            pl.BlockSpec((tm, rw.shape[1]), lambda i, p: (i, 0)),
            pl.BlockSpec((1, 6, d), lambda i, p: (i // per_b, 0, 0)),
            pl.BlockSpec((1, d), lambda i, p: (0, 0)),
        ],
        out_specs=pl.BlockSpec((tm, d), lambda i, p: (i, 0)),
        scratch_shapes=[pltpu.VMEM((2, 2, tm, d), F32), pltpu.SemaphoreType.DMA((2,))],
    )
    return pl.pallas_call(
        functools.partial(_combine_kernel, tm=tm, final=final),
        grid_spec=grid_spec,
        out_shape=jax.ShapeDtypeStruct((rows, d), F32),
        compiler_params=_params("arbitrary"),
        name="moe_combine",
    )(pos, y, x2d, rw, mod3, final_g)


def _moe(x2d, mod3, g, wr, wg, wu, wd, final_g, *, seq, final):
    rows, d = x2d.shape
    ne = wr.shape[1]
    r = 512
    h2, ridx, rw, cnt = _route(x2d, mod3, g, wr, seq=seq, tm=1024)
    counts = cnt[0].astype(jnp.int32)
    tiles_e = (counts + r - 1) // r
    tile_end = jnp.cumsum(tiles_e)
    start = (tile_end - tiles_e) * r
    pos = jnp.take(start, ridx[:, 0:2], axis=0) + ridx[:, 2:4]
    p_rows = 2 * rows + ne * r
    nt = p_rows // r
    pair_id = jnp.arange(2 * rows, dtype=jnp.int32)
    src_pair = jnp.zeros((p_rows,), jnp.int32).at[pos.reshape(-1)].set(pair_id)
    n_used = tile_end[-1:]
    tile_expert = jnp.searchsorted(tile_end, jnp.minimum(jnp.arange(nt), n_used[0] - 1), side="right")
    y = _experts(h2, tile_expert.astype(jnp.int32), src_pair, n_used.astype(jnp.int32), wg, wu, wd,
                 r=r, tf=512)
    return _combine(pos.reshape(-1).astype(jnp.int32), y, x2d, rw, mod3, final_g, seq=seq, tm=256,
                    final=final)


def _final_norm_kernel(x_ref, g_ref, o_ref):
    x = x_ref[...]
    o_ref[...] = (x * lax.rsqrt(jnp.mean(x * x, axis=-1, keepdims=True) + EPS)) * g_ref[...]


def _final_norm(x2d, g, *, tm):
    rows, d = x2d.shape
    return pl.pallas_call(
        _final_norm_kernel,
        grid=(rows // tm,),
        in_specs=[pl.BlockSpec((tm, d), lambda i: (i, 0)), pl.BlockSpec((1, d), lambda i: (0, 0))],
        out_specs=pl.BlockSpec((tm, d), lambda i: (i, 0)),
        out_shape=jax.ShapeDtypeStruct((rows, d), F32),
        compiler_params=_params("parallel"),
        name="final_norm",
    )(x2d, g)


def _t5_bucket(dist):
    n = jnp.maximum(dist, 0)
    max_exact = N_BUCKETS // 2
    nf = jnp.maximum(n, 1).astype(F32)
    large = max_exact + (jnp.log(nf / max_exact) / math.log(REL_MAX_DIST / max_exact)
                         * (N_BUCKETS - max_exact)).astype(jnp.int32)
    large = jnp.minimum(large, N_BUCKETS - 1)
    return jnp.where(n < max_exact, n, large)


def _bias_tiles_a(rel_bias, seq, t):
    nq = seq // t
    ncol = 2 * A_HEADS
    tab = rel_bias[:, :ncol][_t5_bucket(jnp.arange(seq))].astype(F32)
    vneg = jnp.full((t, ncol), NEG, F32)
    v = jnp.concatenate([vneg, tab], axis=0)
    u = jnp.concatenate([v[1:seq + 1][::-1], vneg[:1], v[seq + 1:seq + t][::-1]], axis=0)
    n = seq + t - 1
    r = jnp.tile(u, (t, 1))[:t * n].reshape(t, n, ncol)[:, :seq]
    r = r.reshape(t, nq, t, A_HEADS, 2)[:, ::-1]
    return r.transpose(3, 1, 4, 0, 2).reshape(A_HEADS, nq, 2 * t, t)


def _bias_tiles_b(rel_bias, g, dil, n_back):
    qi = jnp.arange(BLOCK)[:, None]
    kj = jnp.arange(2 * BLOCK)[None, :]
    dm = qi + BLOCK - kj
    tab = rel_bias[:, 2 * A_HEADS + g * B_HEADS:2 * A_HEADS + (g + 1) * B_HEADS]
    vals = jnp.moveaxis(tab[_t5_bucket(dm * dil)], -1, 0)
    valid = (dm >= 0) & (dm <= n_back)
    later = jnp.where(valid[None], vals, NEG)
    first = jnp.where((valid & (kj >= BLOCK))[None], vals, NEG)
    both = jnp.stack([first, later])
    return both.reshape(2, B_HEADS // 2, 2 * BLOCK, 2 * BLOCK).astype(F32)


def kernel(x, c, norm_mix_g, norm_ffn_g, w_mod, b_mod, w_in, b_forget, lam_q1, lam_k1, lam_q2, lam_k2,
           subln_g, rel_bias, w_br_a, w_br_b, w_br_c, w_out, w_ff_gate, w_ff_up, w_ff_down, w_router,
           w_exp_gate, w_exp_up, w_exp_down, final_norm_g):
    bsz, seq, d = x.shape
    depth = w_mod.shape[0]
    rows = bsz * seq
    t_attn = 256
    x2d = x.reshape(rows, d)

    mod = _modulation(c, w_mod, b_mod)
    bias_a = _bias_tiles_a(rel_bias, seq, t_attn)
    bias_b = [_bias_tiles_b(rel_bias, g, dil, win // dil) for g, (win, dil) in enumerate(B_GROUPS)]
    f_col = QKV_W
    g_col = QKV_W + C_HEADS

    final_g = final_norm_g.reshape(1, d)
    fused_final = False
    for l in range(depth):
        lam_init = 0.8 - 0.6 * math.exp(-0.3 * l)
        mod3 = mod[l].reshape(bsz, 6, d)
        w_l = w_in[l]
        w_qkv = w_l[:, :QKV_W].astype(BF16)
        w_f = jnp.pad(w_l[:, f_col:f_col + C_HEADS], ((0, 0), (0, LANES - C_HEADS)))
        w_gf = jnp.concatenate([w_l[:, g_col:], w_f], axis=1).astype(BF16)

        g_mix = norm_mix_g[l].reshape(1, d)
        qkv = _inproj(x2d, mod3, g_mix, w_qkv, seq=seq, tm=1024, tn=1536, out_dtype=BF16)
        gf = _inproj(x2d, mod3, g_mix, w_gf, seq=seq, tm=1024, tn=640, out_dtype=F32,
                     n_sigmoid=GATE_W)
        qkv = qkv.reshape(bsz, seq, QKV_W)

        b_f8 = jnp.pad(b_forget[l], (0, 8 - C_HEADS)).reshape(8, 1)
        fcum = _forget_cumsum(gf.reshape(bsz, seq, GATE_W + LANES), b_f8, col_block=GATE_W // LANES)
        fcum = fcum[:, :C_HEADS].reshape(bsz, C_HEADS // 2, 2, seq)

        oa = _attn_a(qkv, bias_a, lam_q1[l].reshape(1, -1), lam_k1[l].reshape(1, -1),
                     lam_q2[l].reshape(1, -1), lam_k2[l].reshape(1, -1), subln_g[l].reshape(1, -1),
                     lam_init=lam_init, t=t_attn, nsb=2)
        oc = _attn_c(qkv, fcum, t=t_attn, nsb=2)
        obs, lses = [], []
        for g, (win, dil) in enumerate(B_GROUPS):
            o_g, lse_g = _attn_b_group(qkv, bias_b[g], g, dil)
            obs.append(o_g.reshape(rows, B_W))
            lses.append(lse_g.reshape(rows, B_W))

        x2d = _merge(oa.reshape(rows, A_W), obs, lses, oc.reshape(rows, C_W), gf, x2d, mod3,
                     w_br_a[l].astype(BF16), w_br_b[l].astype(BF16), w_br_c[l].astype(BF16),
                     w_out[l].astype(BF16), seq=seq, tm=512)

        g_ffn = norm_ffn_g[l].reshape(1, d)
        if l % 2 == 0:
            j = l // 2
            x2d = _ffn(x2d, mod3, g_ffn, w_ff_gate[j].astype(BF16), w_ff_up[j].astype(BF16),
                       w_ff_down[j].astype(BF16), seq=seq, tm=512, tf=1408)
        else:
            j = l // 2
            fused_final = l == depth - 1
            x2d = _moe(x2d, mod3, g_ffn, w_router[j], w_exp_gate[j].astype(BF16),
                       w_exp_up[j].astype(BF16), w_exp_down[j].astype(BF16), final_g,
                       seq=seq, final=fused_final)

    if not fused_final:
        x2d = _final_norm(x2d, final_g, tm=1024)
    return x2d.reshape(bsz, seq, d)
```

```python
import functools
import math

import jax
import jax.numpy as jnp
from jax import lax
from jax.experimental import pallas as pl
from jax.experimental.pallas import tpu as pltpu

F32 = jnp.float32
BF16 = jnp.bfloat16

D_MODEL = 1024
HEAD_DIM = 64
LANES = 128
A_HEADS = 4
A_W = A_HEADS * 2 * HEAD_DIM
B_GROUPS = ((128, 1), (512, 4), (2048, 16))
B_HEADS = 6
B_W = B_HEADS * HEAD_DIM
B_QW = len(B_GROUPS) * B_W
C_HEADS = 6
C_W = C_HEADS * HEAD_DIM
N_BRANCH = 3
BLOCK = 128
N_BUCKETS = 32
REL_MAX_DIST = 2048
N_EXPERTS = 8
EPS = 1e-6
QKV_W = 3 * A_W + 3 * B_QW + 3 * C_W
GATE_W = N_BRANCH * D_MODEL
NEG = -1e30
SCALE = HEAD_DIM ** -0.5
VMEM_LIMIT = 56 * 1024 * 1024


def _params(*sem):
    return pltpu.CompilerParams(dimension_semantics=sem, vmem_limit_bytes=VMEM_LIMIT)


def _rms_mod(x, g, sc, sh):
    y = x * lax.rsqrt(jnp.mean(x * x, axis=-1, keepdims=True) + EPS)
    return (y * g) * (1.0 + sc) + sh


def _dot(a, b):
    return jnp.dot(a, b, preferred_element_type=F32)


def _dot_nt(a, b):
    return lax.dot_general(a, b, (((1,), (1,)), ((), ())), preferred_element_type=F32)


def _silu(a):
    return a * jax.nn.sigmoid(a)


def _mod_kernel(c_ref, w_ref, b_ref, o_ref):
    a = _silu(c_ref[...]).astype(BF16)
    o_ref[0] = _dot(a, w_ref[0].astype(BF16)) + b_ref[0]


def _modulation(c, w_mod, b_mod):
    depth, d, n = w_mod.shape
    bsz = c.shape[0]
    tn = 1536
    return pl.pallas_call(
        _mod_kernel,
        grid=(depth, n // tn),
        in_specs=[
            pl.BlockSpec((bsz, d), lambda l, j: (0, 0)),
            pl.BlockSpec((1, d, tn), lambda l, j: (l, 0, j)),
            pl.BlockSpec((1, 1, tn), lambda l, j: (l, 0, j)),
        ],
        out_specs=pl.BlockSpec((1, bsz, tn), lambda l, j: (l, 0, j)),
        out_shape=jax.ShapeDtypeStruct((depth, bsz, n), F32),
        compiler_params=_params("parallel", "parallel"),
        name="modulation",
    )(c, w_mod, b_mod.reshape(depth, 1, n))


def _inproj_kernel(x_ref, mod_ref, g_ref, w_ref, o_ref, h_scr, *, n_sigmoid, tn):
    j = pl.program_id(1)

    @pl.when(j == 0)
    def _():
        m = mod_ref[0]
        h_scr[...] = _rms_mod(x_ref[...], g_ref[...], m[1:2], m[0:1]).astype(BF16)

    y = _dot(h_scr[...], w_ref[...])
    if n_sigmoid:
        col = j * tn + lax.broadcasted_iota(jnp.int32, y.shape, 1)
        y = jnp.where(col < n_sigmoid, jax.nn.sigmoid(y), y)
    o_ref[...] = y.astype(o_ref.dtype)


def _inproj(x2d, mod3, g, w, *, seq, tm, tn, out_dtype, n_sigmoid=0):
    rows, d = x2d.shape
    n = w.shape[1]
    per_b = seq // tm
    return pl.pallas_call(
        functools.partial(_inproj_kernel, n_sigmoid=n_sigmoid, tn=tn),
        grid=(rows // tm, n // tn),
        in_specs=[
            pl.BlockSpec((tm, d), lambda i, j: (i, 0)),
            pl.BlockSpec((1, 6, d), lambda i, j: (i // per_b, 0, 0)),
            pl.BlockSpec((1, d), lambda i, j: (0, 0)),
            pl.BlockSpec((d, tn), lambda i, j: (0, j)),
        ],
        out_specs=pl.BlockSpec((tm, tn), lambda i, j: (i, j)),
        out_shape=jax.ShapeDtypeStruct((rows, n), out_dtype),
        scratch_shapes=[pltpu.VMEM((tm, d), BF16)],
        compiler_params=_params("parallel", "arbitrary"),
        name="inproj",
    )(x2d, mod3, g, w)


def _qkv_kernel(x_ref, mod_ref, g_ref, w_ref, ac_ref, b0_ref, b1_ref, b2_ref, h_scr, *, tm):
    m = mod_ref[0]
    h = _rms_mod(x_ref[...], g_ref[...], m[1:2], m[0:1])
    nc = h_scr.shape[0]
    for c in range(nc):
        h_scr[c] = h[:, c * LANES:(c + 1) * LANES]
    hb = h.astype(BF16)
    n_ac = ac_ref.shape[1]
    n_b = b0_ref.shape[1]
    ac_ref[...] = _dot(hb, w_ref[:, 0:n_ac]).astype(BF16)
    b0_ref[...] = _dot(hb, w_ref[:, n_ac:n_ac + n_b]).astype(BF16)
    for gi, ref in ((1, b1_ref), (2, b2_ref)):
        dil = B_GROUPS[gi][1]
        per = tm // dil
        hp = jnp.concatenate(
            [jnp.concatenate([h_scr[c, pl.ds(r, per, stride=dil), :] for c in range(nc)], axis=1).astype(BF16)
             for r in range(dil)], axis=0)
        y = _dot(hp, w_ref[:, n_ac + gi * n_b:n_ac + (gi + 1) * n_b]).astype(BF16)
        for r in range(dil):
            ref[0, r] = y[r * per:(r + 1) * per]


def _qkv_proj(x2d, mod3, g, w, *, bsz, seq, tm):
    rows, d = x2d.shape
    per_b = seq // tm
    n_b = 3 * B_W
    n_ac = w.shape[1] - 3 * n_b
    dil1, dil2 = B_GROUPS[1][1], B_GROUPS[2][1]
    strided = lambda dil: pl.BlockSpec((1, dil, tm // dil, n_b), lambda i: (i // per_b, 0, i % per_b, 0))
    return pl.pallas_call(
        functools.partial(_qkv_kernel, tm=tm),
        grid=(rows // tm,),
        in_specs=[
            pl.BlockSpec((tm, d), lambda i: (i, 0)),
            pl.BlockSpec((1, 6, d), lambda i: (i // per_b, 0, 0)),
            pl.BlockSpec((1, d), lambda i: (0, 0)),
            pl.BlockSpec(w.shape, lambda i: (0, 0), pipeline_mode=pl.Buffered(1)),
        ],
        out_specs=[pl.BlockSpec((tm, n_ac), lambda i: (i, 0)), pl.BlockSpec((tm, n_b), lambda i: (i, 0)),
                   strided(dil1), strided(dil2)],
        out_shape=[jax.ShapeDtypeStruct((rows, n_ac), BF16), jax.ShapeDtypeStruct((rows, n_b), BF16),
                   jax.ShapeDtypeStruct((bsz, dil1, seq // dil1, n_b), BF16),
                   jax.ShapeDtypeStruct((bsz, dil2, seq // dil2, n_b), BF16)],
        scratch_shapes=[pltpu.VMEM((d // LANES, tm, LANES), F32)],
        compiler_params=_params("parallel"),
        name="qkv_proj",
    )(x2d, mod3, g, w)


def _fcum_kernel(f_ref, b_ref, o_ref):
    z = f_ref[0].T[:8] + b_ref[...]
    x = jnp.minimum(z, 0.0) - jnp.log1p(jnp.exp(-jnp.abs(z)))
    s = x.shape[1]
    lane = lax.broadcasted_iota(jnp.int32, x.shape, 1)
    k = 1
    while k < s:
        x = x + jnp.where(lane >= k, pltpu.roll(x, k, 1), 0.0)
        k *= 2
    o_ref[0] = x


def _forget_cumsum(gf, b_f8, *, col_block):
    bsz, seq, _ = gf.shape
    return pl.pallas_call(
        _fcum_kernel,
        grid=(bsz,),
        in_specs=[
            pl.BlockSpec((1, seq, LANES), lambda b: (b, 0, col_block)),
            pl.BlockSpec((8, 1), lambda b: (0, 0)),
        ],
        out_specs=pl.BlockSpec((1, 8, seq), lambda b: (b, 0, 0)),
        out_shape=jax.ShapeDtypeStruct((bsz, 8, seq), F32),
        compiler_params=_params("parallel"),
        name="forget_cumsum",
    )(gf, b_f8)


def _half_masks(q):
    lane = lax.broadcasted_iota(jnp.int32, q.shape, 1)
    zero = jnp.zeros_like(q)
    return jnp.where(lane < HEAD_DIM, q, zero), jnp.where(lane >= HEAD_DIM, q, zero)


def _flash_init(first, v_ref, vext_scr, m_scr, acc_scr):
    @pl.when(first)
    def _():
        vext_scr[:, :LANES] = v_ref[0]
        vext_scr[:, LANES:] = jnp.ones((vext_scr.shape[0], LANES), BF16)

    m_scr[...] = jnp.full(m_scr.shape, -jnp.inf, F32)
    acc_scr[...] = jnp.zeros(acc_scr.shape, F32)


def _lane_tile(a, n):
    return a if n == 1 else jnp.concatenate([a] * n, axis=1)


def _flash_update(s, vext, rows, m_scr, acc_scr):
    m_prev = m_scr[rows]
    m_new = jnp.maximum(m_prev, jnp.max(s, axis=-1, keepdims=True))
    alpha = jnp.exp(m_prev - m_new)
    p = jnp.exp(s - _lane_tile(m_new, s.shape[1] // LANES))
    acc_scr[rows] = _lane_tile(alpha, 2) * acc_scr[rows] + _dot(p.astype(BF16), vext)
    m_scr[rows] = m_new


def _flash_result(acc_scr):
    acc = acc_scr[...]
    return acc[:, :LANES] / acc[:, LANES:]


def _attn_a_kernel(q_ref, k_ref, v_ref, bias_ref, lq1, lk1, lq2, lk2, sg_ref, o_ref,
                   vext_scr, m_scr, acc_scr, *, t, nsb, lam_init):
    qi = pl.program_id(2)
    _flash_init(qi == 0, v_ref, vext_scr, m_scr, acc_scr)
    qh = [_half_masks(q_ref[0, sb * t:(sb + 1) * t, :] * SCALE) for sb in range(nsb)]

    def step(kb, sbs, delta_of):
        off = pl.multiple_of(kb * t, t)
        kblk = k_ref[0, pl.ds(off, t), :]
        vext = vext_scr[pl.ds(off, t), :]
        for sb in sbs:
            for hh in range(2):
                s = _dot_nt(qh[sb][hh], kblk) + bias_ref[0, delta_of(sb), hh * t:(hh + 1) * t, :]
                _flash_update(s, vext, pl.ds((sb * 2 + hh) * t, t), m_scr, acc_scr)

    def body(kb, carry):
        step(kb, range(nsb), lambda sb: qi * nsb + sb - kb)
        return carry

    lax.fori_loop(0, qi * nsb, body, 0)
    for j in range(nsb):
        step(qi * nsb + j, range(j, nsb), lambda sb: sb - j)

    o = _flash_result(acc_scr)
    lam = (jnp.exp(jnp.sum(lq1[...] * lk1[...], axis=-1, keepdims=True))
           - jnp.exp(jnp.sum(lq2[...] * lk2[...], axis=-1, keepdims=True)) + lam_init)
    for sb in range(nsb):
        d = o[2 * sb * t:(2 * sb + 1) * t] - lam * o[(2 * sb + 1) * t:(2 * sb + 2) * t]
        y = d * lax.rsqrt(jnp.mean(d * d, axis=-1, keepdims=True) + EPS)
        o_ref[0, sb * t:(sb + 1) * t, :] = ((y * sg_ref[...]) * (1.0 - lam_init)).astype(o_ref.dtype)


def _flash_scratch(seq, t, nsb):
    return [pltpu.VMEM((seq, 2 * LANES), BF16), pltpu.VMEM((2 * nsb * t, LANES), F32),
            pltpu.VMEM((2 * nsb * t, 2 * LANES), F32)]


def _attn_a(qkv, bias_a, lq1, lk1, lq2, lk2, subln_g, *, lam_init, t, nsb):
    bsz, seq, _ = qkv.shape
    tq = t * nsb
    vec = lambda n: pl.BlockSpec((1, n), lambda h, b, i: (0, 0))
    return pl.pallas_call(
        functools.partial(_attn_a_kernel, t=t, nsb=nsb, lam_init=lam_init),
        grid=(A_HEADS, bsz, seq // tq),
        in_specs=[
            pl.BlockSpec((1, tq, LANES), lambda h, b, i: (b, i, h)),
            pl.BlockSpec((1, seq, LANES), lambda h, b, i: (b, 0, A_HEADS + h)),
            pl.BlockSpec((1, seq, LANES), lambda h, b, i: (b, 0, 2 * A_HEADS + h)),
            pl.BlockSpec((1, seq // t, 2 * t, t), lambda h, b, i: (h, 0, 0, 0)),
            vec(HEAD_DIM), vec(HEAD_DIM), vec(HEAD_DIM), vec(HEAD_DIM), vec(LANES),
        ],
        out_specs=pl.BlockSpec((1, tq, LANES), lambda h, b, i: (b, i, h)),
        out_shape=jax.ShapeDtypeStruct((bsz, seq, A_W), BF16),
        scratch_shapes=_flash_scratch(seq, t, nsb),
        compiler_params=_params("parallel", "parallel", "arbitrary"),
        name="attn_diff",
    )(qkv, qkv, qkv, bias_a, lq1, lk1, lq2, lk2, subln_g)


def _attn_c_kernel(q_ref, k_ref, v_ref, f_ref, o_ref, vext_scr, m_scr, acc_scr, *, t, nsb):
    qi = pl.program_id(2)
    _flash_init(qi == 0, v_ref, vext_scr, m_scr, acc_scr)
    qh = [_half_masks(q_ref[0, sb * t:(sb + 1) * t, :] * SCALE) for sb in range(nsb)]
    q_off = pl.multiple_of(qi * (t * nsb), t * nsb)
    f_anchor = f_ref[0, 0, :, pl.ds(q_off, LANES)][:, :1]
    r = lax.broadcasted_iota(jnp.int32, (t, t), 0)
    c = lax.broadcasted_iota(jnp.int32, (t, t), 1)

    def step(kb, sbs, diag_sb):
        off = pl.multiple_of(kb * t, t)
        kblk = k_ref[0, pl.ds(off, t), :]
        vext = vext_scr[pl.ds(off, t), :]
        dec = f_anchor - f_ref[0, 0, :, pl.ds(off, t)]
        for sb in sbs:
            for hh in range(2):
                s = _dot_nt(qh[sb][hh], kblk) + dec[hh:hh + 1]
                if sb == diag_sb:
                    s = jnp.where(r >= c, s, NEG)
                _flash_update(s, vext, pl.ds((sb * 2 + hh) * t, t), m_scr, acc_scr)

    def body(kb, carry):
        step(kb, range(nsb), None)
        return carry

    lax.fori_loop(0, qi * nsb, body, 0)
    for j in range(nsb):
        step(qi * nsb + j, range(j, nsb), j)

    o = _flash_result(acc_scr)
    lane = lax.broadcasted_iota(jnp.int32, (t, LANES), 1)
    for sb in range(nsb):
        pair = jnp.where(lane < HEAD_DIM, o[2 * sb * t:(2 * sb + 1) * t], o[(2 * sb + 1) * t:(2 * sb + 2) * t])
        o_ref[0, sb * t:(sb + 1) * t, :] = pair.astype(o_ref.dtype)


def _attn_c(qkv, fcum, *, t, nsb):
    bsz, seq, _ = qkv.shape
    tq = t * nsb
    pairs = C_HEADS // 2
    q0 = 3 * A_W // LANES
    return pl.pallas_call(
        functools.partial(_attn_c_kernel, t=t, nsb=nsb),
        grid=(pairs, bsz, seq // tq),
        in_specs=[
            pl.BlockSpec((1, tq, LANES), lambda p, b, i: (b, i, q0 + p)),
            pl.BlockSpec((1, seq, LANES), lambda p, b, i: (b, 0, q0 + pairs + p)),
            pl.BlockSpec((1, seq, LANES), lambda p, b, i: (b, 0, q0 + 2 * pairs + p)),
            pl.BlockSpec((1, 1, 2, seq), lambda p, b, i: (b, p, 0, 0)),
        ],
        out_specs=pl.BlockSpec((1, tq, LANES), lambda p, b, i: (b, i, p)),
        out_shape=jax.ShapeDtypeStruct((bsz, seq, C_W), BF16),
        scratch_shapes=_flash_scratch(seq, t, nsb),
        compiler_params=_params("parallel", "parallel", "arbitrary"),
        name="attn_forget",
    )(qkv, qkv, qkv, fcum)


def _attn_b_kernel(q_ref, kp_ref, kc_ref, vp_ref, vc_ref, bias_ref, o_ref, lse_ref, *, dil, nb):
    n = pl.program_id(1)
    lane = lax.broadcasted_iota(jnp.int32, (BLOCK, LANES), 1)
    first_variant = jnp.minimum(n, 1)

    def residue(r):
        for hp in range(B_HEADS // 2):
            cols = slice(hp * LANES, (hp + 1) * LANES)
            kcat = jnp.concatenate([kp_ref[0, r, :, cols], kc_ref[0, r, :, cols]], axis=0)
            vcat = jnp.concatenate([vp_ref[0, r, :, cols], vc_ref[0, r, :, cols]], axis=0)
            for jb in range(nb):
                qh = _half_masks(q_ref[0, r, jb * BLOCK:(jb + 1) * BLOCK, cols] * SCALE)
                kwin = kcat[jb * BLOCK:(jb + 2) * BLOCK]
                vwin = vcat[jb * BLOCK:(jb + 2) * BLOCK]
                variant = first_variant if jb == 0 else 1
                outs, lses = [], []
                for hh in range(2):
                    s = _dot_nt(qh[hh], kwin) + bias_ref[variant, hp, hh * BLOCK:(hh + 1) * BLOCK, :]
                    m = jnp.max(s, axis=-1, keepdims=True)
                    e = jnp.exp(s - m)
                    den = jnp.sum(e, axis=-1, keepdims=True)
                    outs.append(_dot(e.astype(BF16), vwin) / den)
                    lses.append(jnp.broadcast_to(m + jnp.log(den), (BLOCK, LANES)))
                if dil == 1:
                    rows = pl.ds(jb * BLOCK, BLOCK)
                else:
                    rows = pl.ds(jb * BLOCK * dil + r, BLOCK, stride=dil)
                o_ref[0, hp, rows, :] = jnp.where(lane < HEAD_DIM, outs[0], outs[1])
                lse_ref[0, hp, rows, :] = jnp.where(lane < HEAD_DIM, lses[0], lses[1])

    if dil == 1:
        residue(0)
    else:
        def body(r, carry):
            residue(r)
            return carry

        lax.fori_loop(0, dil, body, 0)


def _attn_b_group(qkv_g, bias_g, g, nb):
    bsz, dil, m_len, _ = qkv_g.shape
    tb = BLOCK * nb
    cur = lambda c: pl.BlockSpec((1, dil, tb, B_W), lambda b, n: (b, 0, n, c))
    prev = lambda c: pl.BlockSpec((1, dil, BLOCK, B_W), lambda b, n: (b, 0, jnp.maximum(n * nb - 1, 0), c))
    pairs = B_HEADS // 2
    out_spec = pl.BlockSpec((1, pairs, tb * dil, LANES), lambda b, n: (b, 0, n, 0))
    out_sds = jax.ShapeDtypeStruct((bsz, pairs, m_len * dil, LANES), F32)
    return pl.pallas_call(
        functools.partial(_attn_b_kernel, dil=dil, nb=nb),
        grid=(bsz, m_len // tb),
        in_specs=[cur(0), prev(1), cur(1), prev(2), cur(2),
                  pl.BlockSpec(bias_g.shape, lambda b, n: (0, 0, 0, 0))],
        out_specs=[out_spec, out_spec],
        out_shape=[out_sds, out_sds],
        compiler_params=_params("parallel", "arbitrary"),
        name=f"attn_dilated_g{g}",
    )(qkv_g, qkv_g, qkv_g, qkv_g, qkv_g, bias_g)


def _merge_kernel(oa_ref, ob0, ob1, ob2, ls0, ls1, ls2, oc_ref, gate_ref, x_ref, mod_ref,
                  wa_ref, wb_ref, wc_ref, wo_ref, o_ref):
    parts = []
    for hp in range(B_HEADS // 2):
        l0, l1, l2 = ls0[0, hp], ls1[0, hp], ls2[0, hp]
        mx = jnp.maximum(jnp.maximum(l0, l1), l2)
        e0, e1, e2 = jnp.exp(l0 - mx), jnp.exp(l1 - mx), jnp.exp(l2 - mx)
        den = e0 + e1 + e2
        parts.append((e0 / den) * ob0[0, hp] + (e1 / den) * ob1[0, hp] + (e2 / den) * ob2[0, hp])
    ob = jnp.concatenate(parts, axis=1)
    d = D_MODEL
    merged = (gate_ref[:, 0:d] * _dot(oa_ref[...], wa_ref[...])
              + gate_ref[:, d:2 * d] * _dot(ob.astype(BF16), wb_ref[...])
              + gate_ref[:, 2 * d:3 * d] * _dot(oc_ref[...], wc_ref[...]))
    y = _dot(merged.astype(BF16), wo_ref[...])
    o_ref[...] = x_ref[...] + mod_ref[0][2:3] * y


def _merge(oa, obs, lses, oc, gf, x2d, mod3, wa, wb, wc, wo, *, seq, tm):
    rows, d = x2d.shape
    per_b = seq // tm
    row = lambda w: pl.BlockSpec((tm, w), lambda i: (i, 0))
    full = lambda a: pl.BlockSpec(a.shape, lambda i: (0, 0))
    paired = pl.BlockSpec((1, B_HEADS // 2, tm, LANES), lambda i: (i // per_b, 0, i % per_b, 0))
    return pl.pallas_call(
        _merge_kernel,
        grid=(rows // tm,),
        in_specs=[row(A_W)] + [paired] * 6 + [row(C_W), row(GATE_W), row(d),
                  pl.BlockSpec((1, 6, d), lambda i: (i // per_b, 0, 0)),
                  full(wa), full(wb), full(wc), full(wo)],
        out_specs=row(d),
        out_shape=jax.ShapeDtypeStruct((rows, d), F32),
        compiler_params=_params("parallel"),
        name="merge_outproj",
    )(oa, *obs, *lses, oc, gf, x2d, mod3, wa, wb, wc, wo)


def _ffn_kernel(x_ref, mod_ref, g_ref, wg_ref, wu_ref, wd_ref, o_ref, h_scr, acc_scr, *, nf):
    f = pl.program_id(1)

    @pl.when(f == 0)
    def _():
        m = mod_ref[0]
        h_scr[...] = _rms_mod(x_ref[...], g_ref[...], m[4:5], m[3:4]).astype(BF16)
        acc_scr[...] = jnp.zeros(acc_scr.shape, F32)

    h = h_scr[...]
    act = (_silu(_dot(h, wg_ref[...])) * _dot(h, wu_ref[...])).astype(BF16)
    acc_scr[...] += _dot(act, wd_ref[...])

    @pl.when(f == nf - 1)
    def _():
        o_ref[...] = x_ref[...] + mod_ref[0][5:6] * acc_scr[...]


def _ffn(x2d, mod3, g, wg, wu, wd, *, seq, tm, tf):
    rows, d = x2d.shape
    dff = wg.shape[1]
    nf = dff // tf
    per_b = seq // tm
    return pl.pallas_call(
        functools.partial(_ffn_kernel, nf=nf),
        grid=(rows // tm, nf),
        in_specs=[
            pl.BlockSpec((tm, d), lambda i, f: (i, 0)),
            pl.BlockSpec((1, 6, d), lambda i, f: (i // per_b, 0, 0)),
            pl.BlockSpec((1, d), lambda i, f: (0, 0)),
            pl.BlockSpec((d, tf), lambda i, f: (0, f)),
            pl.BlockSpec((d, tf), lambda i, f: (0, f)),
            pl.BlockSpec((tf, d), lambda i, f: (f, 0)),
        ],
        out_specs=pl.BlockSpec((tm, d), lambda i, f: (i, 0)),
        out_shape=jax.ShapeDtypeStruct((rows, d), F32),
        scratch_shapes=[pltpu.VMEM((tm, d), BF16), pltpu.VMEM((tm, d), F32)],
        compiler_params=_params("parallel", "arbitrary"),
        name="ffn_dense",
    )(x2d, mod3, g, wg, wu, wd)


def _route_kernel(x_ref, mod_ref, g_ref, wr_ref, tri_ref, h_ref, ridx_ref, rw_ref, cnt_ref, carry_scr):
    i = pl.program_id(0)

    @pl.when(i == 0)
    def _():
        carry_scr[...] = jnp.zeros(carry_scr.shape, F32)

    m = mod_ref[0]
    h = _rms_mod(x_ref[...], g_ref[...], m[4:5], m[3:4])
    h_ref[...] = h
    logits = jnp.dot(h, wr_ref[...], preferred_element_type=F32, precision=lax.Precision.HIGHEST)
    idx = lax.broadcasted_iota(jnp.int32, logits.shape, 1)
    n = logits.shape[1]
    m1 = jnp.max(logits, axis=-1, keepdims=True)
    i1 = jnp.min(jnp.where(logits == m1, idx, n), axis=-1, keepdims=True)
    first = idx == i1
    rest = jnp.where(first, -jnp.inf, logits)
    m2 = jnp.max(rest, axis=-1, keepdims=True)
    i2 = jnp.min(jnp.where(rest == m2, idx, n), axis=-1, keepdims=True)
    second = idx == i2
    e = jnp.exp(m2 - m1)
    den = 1.0 + e
    onehot = jnp.where(first | second, 1.0, 0.0)
    before = _dot(tri_ref[...], onehot.astype(BF16)) - onehot + carry_scr[...]
    rank1 = jnp.sum(jnp.where(first, before, 0.0), axis=-1, keepdims=True).astype(jnp.int32)
    rank2 = jnp.sum(jnp.where(second, before, 0.0), axis=-1, keepdims=True).astype(jnp.int32)
    carry_scr[...] += jnp.sum(onehot, axis=0, keepdims=True)
    ridx_ref[...] = jnp.where(idx == 0, i1, jnp.where(idx == 1, i2, jnp.where(idx == 2, rank1,
                              jnp.where(idx == 3, rank2, 0))))
    rw_ref[...] = jnp.where(idx == 0, 1.0 / den, jnp.where(idx == 1, e / den, 0.0))
    cnt_ref[...] = carry_scr[...]


def _route(x2d, mod3, g, wr, *, seq, tm):
    rows, d = x2d.shape
    ne = wr.shape[1]
    per_b = seq // tm
    tri = (jnp.arange(tm)[:, None] >= jnp.arange(tm)[None, :]).astype(BF16)
    row = lambda w: pl.BlockSpec((tm, w), lambda i: (i, 0))
    return pl.pallas_call(
        _route_kernel,
        grid=(rows // tm,),
        in_specs=[row(d), pl.BlockSpec((1, 6, d), lambda i: (i // per_b, 0, 0)),
                  pl.BlockSpec((1, d), lambda i: (0, 0)), pl.BlockSpec((d, ne), lambda i: (0, 0)),
                  pl.BlockSpec((tm, tm), lambda i: (0, 0))],
        out_specs=[row(d), row(ne), row(ne), pl.BlockSpec((1, ne), lambda i: (0, 0))],
        out_shape=[jax.ShapeDtypeStruct((rows, d), F32), jax.ShapeDtypeStruct((rows, ne), jnp.int32),
                   jax.ShapeDtypeStruct((rows, ne), F32), jax.ShapeDtypeStruct((1, ne), F32)],
        scratch_shapes=[pltpu.VMEM((1, ne), F32)],
        compiler_params=_params("arbitrary"),
        name="moe_route",
    )(x2d, mod3, g, wr, tri)


def _gather_rows(tok_of, n, src_hbm, dst, sem):
    def issue(r, carry):
        pltpu.make_async_copy(src_hbm.at[pl.ds(tok_of(r), 1)], dst.at[pl.ds(r, 1)], sem).start()
        return carry

    lax.fori_loop(0, n, issue, 0)


def _gather_wait(n, src_hbm, dst, sem):
    pltpu.make_async_copy(src_hbm.at[pl.ds(0, n)], dst, sem).wait()


def _expert_kernel(te_ref, src_ref, nu_ref, h_hbm, wg_ref, wu_ref, wd_ref, y_ref,
                   xbuf, xb_scr, acc_scr, sem, *, nf, r):
    i = pl.program_id(0)
    f = pl.program_id(1)
    active = i < nu_ref[0]
    slot = i % 2

    def gather(tile, s):
        _gather_rows(lambda j: src_ref[tile * r + j] >> 1, r, h_hbm, xbuf.at[s], sem.at[s])

    @pl.when((i == 0) & (f == 0))
    def _():
        gather(0, 0)

    @pl.when(active & (f == 0))
    def _():
        _gather_wait(r, h_hbm, xbuf.at[slot], sem.at[slot])

        @pl.when(i + 1 < nu_ref[0])
        def _():
            gather(i + 1, 1 - slot)

        xb_scr[...] = xbuf[slot].astype(BF16)
        acc_scr[...] = jnp.zeros(acc_scr.shape, F32)

    @pl.when(active)
    def _():
        xb = xb_scr[...]
        act = (_silu(_dot(xb, wg_ref[0])) * _dot(xb, wu_ref[0])).astype(BF16)
        acc_scr[...] += _dot(act, wd_ref[0])

    @pl.when(f == nf - 1)
    def _():
        y_ref[...] = jnp.where(active, acc_scr[...], 0.0)


def _experts(h2, tile_expert, src_pair, n_used, wg, wu, wd, *, r, tf):
    ne, d, dff = wg.shape
    nf = dff // tf
    p_rows = src_pair.shape[0]
    nt = p_rows // r
    fsel = lambda i, f, nu: jnp.where(i < nu[0], f, nf - 1)
    grid_spec = pltpu.PrefetchScalarGridSpec(
        num_scalar_prefetch=3,
        grid=(nt, nf),
        in_specs=[
            pl.BlockSpec(memory_space=pl.ANY),
            pl.BlockSpec((1, d, tf), lambda i, f, te, sp, nu: (te[i], 0, fsel(i, f, nu))),
            pl.BlockSpec((1, d, tf), lambda i, f, te, sp, nu: (te[i], 0, fsel(i, f, nu))),
            pl.BlockSpec((1, tf, d), lambda i, f, te, sp, nu: (te[i], fsel(i, f, nu), 0)),
        ],
        out_specs=pl.BlockSpec((r, d), lambda i, f, te, sp, nu: (i, 0)),
        scratch_shapes=[pltpu.VMEM((2, r, d), F32), pltpu.VMEM((r, d), BF16), pltpu.VMEM((r, d), F32),
                        pltpu.SemaphoreType.DMA((2,))],
    )
    return pl.pallas_call(
        functools.partial(_expert_kernel, nf=nf, r=r),
        grid_spec=grid_spec,
        out_shape=jax.ShapeDtypeStruct((p_rows, d), F32),
        compiler_params=_params("arbitrary", "arbitrary"),
        name="moe_experts",
    )(tile_expert, src_pair, n_used, h2, wg, wu, wd)


def _combine_kernel(pos_ref, y_hbm, x_ref, rw_ref, mod_ref, fg_ref, o_ref, ybuf, sem, *, tm, final):
    i = pl.program_id(0)
    n = pl.num_programs(0)
    slot = i % 2

    def gather(tile, s):
        for k in range(2):
            _gather_rows(lambda j: pos_ref[(tile * tm + j) * 2 + k], tm, y_hbm, ybuf.at[s, k], sem.at[s])

    @pl.when(i == 0)
    def _():
        gather(0, 0)

    for k in range(2):
        _gather_wait(tm, y_hbm, ybuf.at[slot, k], sem.at[slot])

    @pl.when(i + 1 < n)
    def _():
        gather(i + 1, 1 - slot)

    w = rw_ref[...]
    f = w[:, 0:1] * ybuf[slot, 0] + w[:, 1:2] * ybuf[slot, 1]
    x = x_ref[...] + mod_ref[0][5:6] * f
    if final:
        x = (x * lax.rsqrt(jnp.mean(x * x, axis=-1, keepdims=True) + EPS)) * fg_ref[...]
    o_ref[...] = x


def _combine(pos, y, x2d, rw, mod3, final_g, *, seq, tm, final):
    rows, d = x2d.shape
    per_b = seq // tm
    grid_spec = pltpu.PrefetchScalarGridSpec(
        num_scalar_prefetch=1,
        grid=(rows // tm,),
        in_specs=[
            pl.BlockSpec(memory_space=pl.ANY),
            pl.BlockSpec((tm, d), lambda i, p: (i, 0)),
            pl.BlockSpec((tm, rw.shape[1]), lambda i, p: (i, 0)),
            pl.BlockSpec((1, 6, d), lambda i, p: (i // per_b, 0, 0)),
            pl.BlockSpec((1, d), lambda i, p: (0, 0)),
        ],
        out_specs=pl.BlockSpec((tm, d), lambda i, p: (i, 0)),
        scratch_shapes=[pltpu.VMEM((2, 2, tm, d), F32), pltpu.SemaphoreType.DMA((2,))],
    )
    return pl.pallas_call(
        functools.partial(_combine_kernel, tm=tm, final=final),
        grid_spec=grid_spec,
        out_shape=jax.ShapeDtypeStruct((rows, d), F32),
        compiler_params=_params("arbitrary"),
        name="moe_combine",
    )(pos, y, x2d, rw, mod3, final_g)


def _moe(x2d, mod3, g, wr, wg, wu, wd, final_g, *, seq, final):
    rows, d = x2d.shape
    ne = wr.shape[1]
    r = 512
    h2, ridx, rw, cnt = _route(x2d, mod3, g, wr, seq=seq, tm=1024)
    counts = cnt[0].astype(jnp.int32)
    tiles_e = (counts + r - 1) // r
    tile_end = jnp.cumsum(tiles_e)
    start = (tile_end - tiles_e) * r
    pos = jnp.take(start, ridx[:, 0:2], axis=0) + ridx[:, 2:4]
    p_rows = 2 * rows + ne * r
    nt = p_rows // r
    pair_id = jnp.arange(2 * rows, dtype=jnp.int32)
    src_pair = jnp.zeros((p_rows,), jnp.int32).at[pos.reshape(-1)].set(pair_id)
    n_used = tile_end[-1:]
    tile_idx = jnp.minimum(jnp.arange(nt), n_used[0] - 1)
    tile_expert = jnp.sum(tile_idx[:, None] >= tile_end[None, :], axis=1)
    y = _experts(h2, tile_expert.astype(jnp.int32), src_pair, n_used.astype(jnp.int32), wg, wu, wd,
                 r=r, tf=512)
    return _combine(pos.reshape(-1).astype(jnp.int32), y, x2d, rw, mod3, final_g, seq=seq, tm=256,
                    final=final)


def _final_norm_kernel(x_ref, g_ref, o_ref):
    x = x_ref[...]
    o_ref[...] = (x * lax.rsqrt(jnp.mean(x * x, axis=-1, keepdims=True) + EPS)) * g_ref[...]


def _final_norm(x2d, g, *, tm):
    rows, d = x2d.shape
    return pl.pallas_call(
        _final_norm_kernel,
        grid=(rows // tm,),
        in_specs=[pl.BlockSpec((tm, d), lambda i: (i, 0)), pl.BlockSpec((1, d), lambda i: (0, 0))],
        out_specs=pl.BlockSpec((tm, d), lambda i: (i, 0)),
        out_shape=jax.ShapeDtypeStruct((rows, d), F32),
        compiler_params=_params("parallel"),
        name="final_norm",
    )(x2d, g)


def _t5_bucket(dist):
    n = jnp.maximum(dist, 0)
    max_exact = N_BUCKETS // 2
    nf = jnp.maximum(n, 1).astype(F32)
    large = max_exact + (jnp.log(nf / max_exact) / math.log(REL_MAX_DIST / max_exact)
                         * (N_BUCKETS - max_exact)).astype(jnp.int32)
    large = jnp.minimum(large, N_BUCKETS - 1)
    return jnp.where(n < max_exact, n, large)


def _bias_tiles_a(rel_bias, seq, t):
    nq = seq // t
    ncol = 2 * A_HEADS
    tab = rel_bias[:, :ncol][_t5_bucket(jnp.arange(seq))].astype(F32).T
    vneg = jnp.full((ncol, t), NEG, F32)
    v = jnp.concatenate([vneg, tab], axis=1)
    u = jnp.concatenate([v[:, 1:seq + 1][:, ::-1], vneg[:, :1], v[:, seq + 1:seq + t][:, ::-1]], axis=1)
    n = seq + t - 1
    r = jnp.tile(u, (1, t))[:, :t * n].reshape(ncol, t, n)[:, :, :seq]
    r = r.reshape(ncol, t, nq, t)[:, :, ::-1].transpose(0, 2, 1, 3)
    r = r.reshape(A_HEADS, 2, nq, t, t).transpose(0, 2, 1, 3, 4)
    return r.reshape(A_HEADS, nq, 2 * t, t)


def _bias_tiles_b(rel_bias, g, dil, n_back):
    qi = jnp.arange(BLOCK)[:, None]
    kj = jnp.arange(2 * BLOCK)[None, :]
    dm = qi + BLOCK - kj
    tab = rel_bias[:, 2 * A_HEADS + g * B_HEADS:2 * A_HEADS + (g + 1) * B_HEADS]
    vals = jnp.moveaxis(tab[_t5_bucket(dm * dil)], -1, 0)
    valid = (dm >= 0) & (dm <= n_back)
    later = jnp.where(valid[None], vals, NEG)
    first = jnp.where((valid & (kj >= BLOCK))[None], vals, NEG)
    both = jnp.stack([first, later])
    return both.reshape(2, B_HEADS // 2, 2 * BLOCK, 2 * BLOCK).astype(F32)


def kernel(x, c, norm_mix_g, norm_ffn_g, w_mod, b_mod, w_in, b_forget, lam_q1, lam_k1, lam_q2, lam_k2,
           subln_g, rel_bias, w_br_a, w_br_b, w_br_c, w_out, w_ff_gate, w_ff_up, w_ff_down, w_router,
           w_exp_gate, w_exp_up, w_exp_down, final_norm_g):
    bsz, seq, d = x.shape
    depth = w_mod.shape[0]
    rows = bsz * seq
    t_attn = 256
    x2d = x.reshape(rows, d)

    mod = _modulation(c, w_mod, b_mod)
    bias_a = _bias_tiles_a(rel_bias, seq, t_attn)
    bias_b = [_bias_tiles_b(rel_bias, g, dil, win // dil) for g, (win, dil) in enumerate(B_GROUPS)]
    f_col = QKV_W
    g_col = QKV_W + C_HEADS

    final_g = final_norm_g.reshape(1, d)
    fused_final = False
    for l in range(depth):
        lam_init = 0.8 - 0.6 * math.exp(-0.3 * l)
        mod3 = mod[l].reshape(bsz, 6, d)
        w_l = w_in[l]
        a_end, b_end = 3 * A_W, 3 * A_W + 3 * B_QW
        w_b = [jnp.concatenate([w_l[:, a_end + s * B_QW + g * B_W:a_end + s * B_QW + (g + 1) * B_W]
                                for s in range(3)], axis=1) for g in range(len(B_GROUPS))]
        w_qkv = jnp.concatenate([w_l[:, :a_end], w_l[:, b_end:QKV_W]] + w_b, axis=1).astype(BF16)
        w_f = jnp.pad(w_l[:, f_col:f_col + C_HEADS], ((0, 0), (0, LANES - C_HEADS)))
        w_gf = jnp.concatenate([w_l[:, g_col:], w_f], axis=1).astype(BF16)

        g_mix = norm_mix_g[l].reshape(1, d)
        qkv_ac, qkv_b0, qkv_b1, qkv_b2 = _qkv_proj(x2d, mod3, g_mix, w_qkv, bsz=bsz, seq=seq, tm=512)
        gf = _inproj(x2d, mod3, g_mix, w_gf, seq=seq, tm=1024, tn=640, out_dtype=F32,
                     n_sigmoid=GATE_W)
        qkv_ac = qkv_ac.reshape(bsz, seq, -1)

        b_f8 = jnp.pad(b_forget[l], (0, 8 - C_HEADS)).reshape(8, 1)
        fcum = _forget_cumsum(gf.reshape(bsz, seq, GATE_W + LANES), b_f8, col_block=GATE_W // LANES)
        fcum = fcum[:, :C_HEADS].reshape(bsz, C_HEADS // 2, 2, seq)

        oa = _attn_a(qkv_ac, bias_a, lam_q1[l].reshape(1, -1), lam_k1[l].reshape(1, -1),
                     lam_q2[l].reshape(1, -1), lam_k2[l].reshape(1, -1), subln_g[l].reshape(1, -1),
                     lam_init=lam_init, t=t_attn, nsb=2)
        oc = _attn_c(qkv_ac, fcum, t=t_attn, nsb=2)
        obs, lses = [], []
        groups = (qkv_b0.reshape(bsz, 1, seq, 3 * B_W), qkv_b1, qkv_b2)
        for g, (qkv_g, nb) in enumerate(zip(groups, (4, 1, 1))):
            o_g, lse_g = _attn_b_group(qkv_g, bias_b[g], g, nb)
            obs.append(o_g)
            lses.append(lse_g)

        x2d = _merge(oa.reshape(rows, A_W), obs, lses, oc.reshape(rows, C_W), gf, x2d, mod3,
                     w_br_a[l].astype(BF16), w_br_b[l].astype(BF16), w_br_c[l].astype(BF16),
                     w_out[l].astype(BF16), seq=seq, tm=512)

        g_ffn = norm_ffn_g[l].reshape(1, d)
        if l % 2 == 0:
            j = l // 2
            x2d = _ffn(x2d, mod3, g_ffn, w_ff_gate[j].astype(BF16), w_ff_up[j].astype(BF16),
                       w_ff_down[j].astype(BF16), seq=seq, tm=512, tf=1408)
        else:
            j = l // 2
            fused_final = l == depth - 1
            x2d = _moe(x2d, mod3, g_ffn, w_router[j], w_exp_gate[j].astype(BF16),
                       w_exp_up[j].astype(BF16), w_exp_down[j].astype(BF16), final_g,
                       seq=seq, final=fused_final)

    if not fused_final:
        x2d = _final_norm(x2d, final_g, tm=1024)
    return x2d.reshape(bsz, seq, d)
```

```python
import functools
import math

import jax
import jax.numpy as jnp
from jax import lax
from jax.experimental import pallas as pl
from jax.experimental.pallas import tpu as pltpu

F32 = jnp.float32
BF16 = jnp.bfloat16

D_MODEL = 1024
HEAD_DIM = 64
LANES = 128
A_HEADS = 4
A_W = A_HEADS * 2 * HEAD_DIM
B_GROUPS = ((128, 1), (512, 4), (2048, 16))
B_HEADS = 6
B_W = B_HEADS * HEAD_DIM
B_QW = len(B_GROUPS) * B_W
C_HEADS = 6
C_W = C_HEADS * HEAD_DIM
N_BRANCH = 3
BLOCK = 128
N_BUCKETS = 32
REL_MAX_DIST = 2048
N_EXPERTS = 8
EPS = 1e-6
QKV_W = 3 * A_W + 3 * B_QW + 3 * C_W
GATE_W = N_BRANCH * D_MODEL
NEG = -1e30
SCALE = HEAD_DIM ** -0.5
VMEM_LIMIT = 56 * 1024 * 1024


def _params(*sem):
    return pltpu.CompilerParams(dimension_semantics=sem, vmem_limit_bytes=VMEM_LIMIT)


def _rms_mod(x, g, sc, sh):
    y = x * lax.rsqrt(jnp.mean(x * x, axis=-1, keepdims=True) + EPS)
    return (y * g) * (1.0 + sc) + sh


def _dot(a, b):
    return jnp.dot(a, b, preferred_element_type=F32)


def _dot_nt(a, b):
    return lax.dot_general(a, b, (((1,), (1,)), ((), ())), preferred_element_type=F32)


def _silu(a):
    return a * jax.nn.sigmoid(a)


def _mod_kernel(c_ref, w_ref, b_ref, o_ref):
    a = _silu(c_ref[...]).astype(BF16)
    o_ref[0] = _dot(a, w_ref[0].astype(BF16)) + b_ref[0]


def _modulation(c, w_mod, b_mod):
    depth, d, n = w_mod.shape
    bsz = c.shape[0]
    tn = 1536
    return pl.pallas_call(
        _mod_kernel,
        grid=(depth, n // tn),
        in_specs=[
            pl.BlockSpec((bsz, d), lambda l, j: (0, 0)),
            pl.BlockSpec((1, d, tn), lambda l, j: (l, 0, j)),
            pl.BlockSpec((1, 1, tn), lambda l, j: (l, 0, j)),
        ],
        out_specs=pl.BlockSpec((1, bsz, tn), lambda l, j: (l, 0, j)),
        out_shape=jax.ShapeDtypeStruct((depth, bsz, n), F32),
        compiler_params=_params("parallel", "parallel"),
        name="modulation",
    )(c, w_mod, b_mod.reshape(depth, 1, n))


def _inproj_kernel(x_ref, mod_ref, g_ref, w_ref, o_ref, h_scr, *, n_sigmoid, tn):
    j = pl.program_id(1)

    @pl.when(j == 0)
    def _():
        m = mod_ref[0]
        h_scr[...] = _rms_mod(x_ref[...], g_ref[...], m[1:2], m[0:1]).astype(BF16)

    y = _dot(h_scr[...], w_ref[...])
    if n_sigmoid:
        col = j * tn + lax.broadcasted_iota(jnp.int32, y.shape, 1)
        y = jnp.where(col < n_sigmoid, jax.nn.sigmoid(y), y)
    o_ref[...] = y.astype(o_ref.dtype)


def _inproj(x2d, mod3, g, w, *, seq, tm, tn, out_dtype, n_sigmoid=0):
    rows, d = x2d.shape
    n = w.shape[1]
    per_b = seq // tm
    return pl.pallas_call(
        functools.partial(_inproj_kernel, n_sigmoid=n_sigmoid, tn=tn),
        grid=(rows // tm, n // tn),
        in_specs=[
            pl.BlockSpec((tm, d), lambda i, j: (i, 0)),
            pl.BlockSpec((1, 6, d), lambda i, j: (i // per_b, 0, 0)),
            pl.BlockSpec((1, d), lambda i, j: (0, 0)),
            pl.BlockSpec((d, tn), lambda i, j: (0, j)),
        ],
        out_specs=pl.BlockSpec((tm, tn), lambda i, j: (i, j)),
        out_shape=jax.ShapeDtypeStruct((rows, n), out_dtype),
        scratch_shapes=[pltpu.VMEM((tm, d), BF16)],
        compiler_params=_params("parallel", "arbitrary"),
        name="inproj",
    )(x2d, mod3, g, w)


def _qkv_kernel(x_ref, mod_ref, g_ref, w_ref, ac_ref, b0_ref, b1_ref, b2_ref, h_scr, *, tm):
    m = mod_ref[0]
    h = _rms_mod(x_ref[...], g_ref[...], m[1:2], m[0:1])
    nc = h_scr.shape[0]
    for c in range(nc):
        h_scr[c] = h[:, c * LANES:(c + 1) * LANES]
    hb = h.astype(BF16)
    n_ac = ac_ref.shape[1]
    n_b = b0_ref.shape[1]
    ac_ref[...] = _dot(hb, w_ref[:, 0:n_ac]).astype(BF16)
    b0_ref[...] = _dot(hb, w_ref[:, n_ac:n_ac + n_b]).astype(BF16)
    for gi, ref in ((1, b1_ref), (2, b2_ref)):
        dil = B_GROUPS[gi][1]
        per = tm // dil
        hp = jnp.concatenate(
            [jnp.concatenate([h_scr[c, pl.ds(r, per, stride=dil), :] for c in range(nc)], axis=1).astype(BF16)
             for r in range(dil)], axis=0)
        y = _dot(hp, w_ref[:, n_ac + gi * n_b:n_ac + (gi + 1) * n_b]).astype(BF16)
        for r in range(dil):
            ref[0, r] = y[r * per:(r + 1) * per]


def _qkv_proj(x2d, mod3, g, w, *, bsz, seq, tm):
    rows, d = x2d.shape
    per_b = seq // tm
    n_b = 3 * B_W
    n_ac = w.shape[1] - 3 * n_b
    dil1, dil2 = B_GROUPS[1][1], B_GROUPS[2][1]
    strided = lambda dil: pl.BlockSpec((1, dil, tm // dil, n_b), lambda i: (i // per_b, 0, i % per_b, 0))
    return pl.pallas_call(
        functools.partial(_qkv_kernel, tm=tm),
        grid=(rows // tm,),
        in_specs=[
            pl.BlockSpec((tm, d), lambda i: (i, 0)),
            pl.BlockSpec((1, 6, d), lambda i: (i // per_b, 0, 0)),
            pl.BlockSpec((1, d), lambda i: (0, 0)),
            pl.BlockSpec(w.shape, lambda i: (0, 0), pipeline_mode=pl.Buffered(1)),
        ],
        out_specs=[pl.BlockSpec((tm, n_ac), lambda i: (i, 0)), pl.BlockSpec((tm, n_b), lambda i: (i, 0)),
                   strided(dil1), strided(dil2)],
        out_shape=[jax.ShapeDtypeStruct((rows, n_ac), BF16), jax.ShapeDtypeStruct((rows, n_b), BF16),
                   jax.ShapeDtypeStruct((bsz, dil1, seq // dil1, n_b), BF16),
                   jax.ShapeDtypeStruct((bsz, dil2, seq // dil2, n_b), BF16)],
        scratch_shapes=[pltpu.VMEM((d // LANES, tm, LANES), F32)],
        compiler_params=_params("parallel"),
        name="qkv_proj",
    )(x2d, mod3, g, w)


def _fcum_kernel(f_ref, b_ref, o_ref):
    z = f_ref[0].T[:8] + b_ref[...]
    x = jnp.minimum(z, 0.0) - jnp.log1p(jnp.exp(-jnp.abs(z)))
    s = x.shape[1]
    lane = lax.broadcasted_iota(jnp.int32, x.shape, 1)
    k = 1
    while k < s:
        x = x + jnp.where(lane >= k, pltpu.roll(x, k, 1), 0.0)
        k *= 2
    o_ref[0] = x


def _forget_cumsum(gf, b_f8, *, col_block):
    bsz, seq, _ = gf.shape
    return pl.pallas_call(
        _fcum_kernel,
        grid=(bsz,),
        in_specs=[
            pl.BlockSpec((1, seq, LANES), lambda b: (b, 0, col_block)),
            pl.BlockSpec((8, 1), lambda b: (0, 0)),
        ],
        out_specs=pl.BlockSpec((1, 8, seq), lambda b: (b, 0, 0)),
        out_shape=jax.ShapeDtypeStruct((bsz, 8, seq), F32),
        compiler_params=_params("parallel"),
        name="forget_cumsum",
    )(gf, b_f8)


def _half_masks(q):
    lane = lax.broadcasted_iota(jnp.int32, q.shape, 1)
    zero = jnp.zeros_like(q)
    return jnp.where(lane < HEAD_DIM, q, zero), jnp.where(lane >= HEAD_DIM, q, zero)


def _flash_init(first, v_ref, vext_scr, m_scr, acc_scr):
    @pl.when(first)
    def _():
        vext_scr[:, :LANES] = v_ref[0]
        vext_scr[:, LANES:] = jnp.ones((vext_scr.shape[0], LANES), BF16)

    m_scr[...] = jnp.full(m_scr.shape, -jnp.inf, F32)
    acc_scr[...] = jnp.zeros(acc_scr.shape, F32)


def _lane_tile(a, n):
    return a if n == 1 else jnp.concatenate([a] * n, axis=1)


def _flash_update(s, vext, rows, m_scr, acc_scr):
    m_prev = m_scr[rows]
    m_new = jnp.maximum(m_prev, jnp.max(s, axis=-1, keepdims=True))
    alpha = jnp.exp(m_prev - m_new)
    p = jnp.exp(s - _lane_tile(m_new, s.shape[1] // LANES))
    acc_scr[rows] = _lane_tile(alpha, 2) * acc_scr[rows] + _dot(p.astype(BF16), vext)
    m_scr[rows] = m_new


def _flash_result(acc_scr):
    acc = acc_scr[...]
    return acc[:, :LANES] / acc[:, LANES:]


def _attn_a_kernel(q_ref, k_ref, v_ref, bias_ref, lq1, lk1, lq2, lk2, sg_ref, o_ref,
                   vext_scr, m_scr, acc_scr, *, t, nsb, lam_init):
    qi = pl.program_id(2)
    _flash_init(qi == 0, v_ref, vext_scr, m_scr, acc_scr)
    qh = [_half_masks(q_ref[0, sb * t:(sb + 1) * t, :] * SCALE) for sb in range(nsb)]

    def step(kb, nkb, plan):
        off = pl.multiple_of(kb * t, t)
        kblk = k_ref[0, pl.ds(off, nkb * t), :]
        vext = vext_scr[pl.ds(off, nkb * t), :]
        for sb, deltas in plan:
            for hh in range(2):
                bias = [bias_ref[0, d, hh * t:(hh + 1) * t, :] for d in deltas]
                s = _dot_nt(qh[sb][hh], kblk) + (bias[0] if nkb == 1 else jnp.concatenate(bias, axis=1))
                _flash_update(s, vext, pl.ds((sb * 2 + hh) * t, t), m_scr, acc_scr)

    def body(kb2, carry):
        first = [qi * nsb + sb - 2 * kb2 for sb in range(nsb)]
        step(2 * kb2, 2, [(sb, (first[sb], first[sb] - 1)) for sb in range(nsb)])
        return carry

    lax.fori_loop(0, qi * (nsb // 2), body, 0)
    for j in range(0, nsb, 2):
        step(qi * nsb + j, 1, [(j, (0,))])
        step(qi * nsb + j, 2, [(sb, (sb - j, sb - j - 1)) for sb in range(j + 1, nsb)])

    o = _flash_result(acc_scr)
    lam = (jnp.exp(jnp.sum(lq1[...] * lk1[...], axis=-1, keepdims=True))
           - jnp.exp(jnp.sum(lq2[...] * lk2[...], axis=-1, keepdims=True)) + lam_init)
    for sb in range(nsb):
        d = o[2 * sb * t:(2 * sb + 1) * t] - lam * o[(2 * sb + 1) * t:(2 * sb + 2) * t]
        y = d * lax.rsqrt(jnp.mean(d * d, axis=-1, keepdims=True) + EPS)
        o_ref[0, sb * t:(sb + 1) * t, :] = ((y * sg_ref[...]) * (1.0 - lam_init)).astype(o_ref.dtype)


def _flash_scratch(seq, t, nsb):
    return [pltpu.VMEM((seq, 2 * LANES), BF16), pltpu.VMEM((2 * nsb * t, LANES), F32),
            pltpu.VMEM((2 * nsb * t, 2 * LANES), F32)]


def _attn_a(qkv, bias_a, lq1, lk1, lq2, lk2, subln_g, *, lam_init, t, nsb):
    bsz, seq, _ = qkv.shape
    tq = t * nsb
    vec = lambda n: pl.BlockSpec((1, n), lambda h, b, i: (0, 0))
    return pl.pallas_call(
        functools.partial(_attn_a_kernel, t=t, nsb=nsb, lam_init=lam_init),
        grid=(A_HEADS, bsz, seq // tq),
        in_specs=[
            pl.BlockSpec((1, tq, LANES), lambda h, b, i: (b, i, h)),
            pl.BlockSpec((1, seq, LANES), lambda h, b, i: (b, 0, A_HEADS + h)),
            pl.BlockSpec((1, seq, LANES), lambda h, b, i: (b, 0, 2 * A_HEADS + h)),
            pl.BlockSpec((1, seq // t, 2 * t, t), lambda h, b, i: (h, 0, 0, 0)),
            vec(HEAD_DIM), vec(HEAD_DIM), vec(HEAD_DIM), vec(HEAD_DIM), vec(LANES),
        ],
        out_specs=pl.BlockSpec((1, tq, LANES), lambda h, b, i: (b, i, h)),
        out_shape=jax.ShapeDtypeStruct((bsz, seq, A_W), BF16),
        scratch_shapes=_flash_scratch(seq, t, nsb),
        compiler_params=_params("parallel", "parallel", "arbitrary"),
        name="attn_diff",
    )(qkv, qkv, qkv, bias_a, lq1, lk1, lq2, lk2, subln_g)


def _attn_c_kernel(q_ref, k_ref, v_ref, f_ref, o_ref, vext_scr, m_scr, acc_scr, *, t, nsb):
    qi = pl.program_id(2)
    _flash_init(qi == 0, v_ref, vext_scr, m_scr, acc_scr)
    qh = [_half_masks(q_ref[0, sb * t:(sb + 1) * t, :] * SCALE) for sb in range(nsb)]
    q_off = pl.multiple_of(qi * (t * nsb), t * nsb)
    f_anchor = f_ref[0, 0, :, pl.ds(q_off, LANES)][:, :1]
    def causal(nkb):
        r = lax.broadcasted_iota(jnp.int32, (t, nkb * t), 0)
        c = lax.broadcasted_iota(jnp.int32, (t, nkb * t), 1)
        return r + (nkb - 1) * t >= c

    def step(kb, nkb, sbs, diag_sb):
        off = pl.multiple_of(kb * t, t)
        kblk = k_ref[0, pl.ds(off, nkb * t), :]
        vext = vext_scr[pl.ds(off, nkb * t), :]
        dec = [f_anchor[hh:hh + 1] - f_ref[0, 0, hh:hh + 1, pl.ds(off, nkb * t)] for hh in range(2)]
        for sb in sbs:
            for hh in range(2):
                s = _dot_nt(qh[sb][hh], kblk) + dec[hh]
                if sb == diag_sb:
                    s = jnp.where(causal(nkb), s, NEG)
                _flash_update(s, vext, pl.ds((sb * 2 + hh) * t, t), m_scr, acc_scr)

    def body(kb2, carry):
        step(2 * kb2, 2, range(nsb), None)
        return carry

    lax.fori_loop(0, qi * (nsb // 2), body, 0)
    for j in range(0, nsb, 2):
        step(qi * nsb + j, 1, [j], j)
        step(qi * nsb + j, 2, range(j + 1, nsb), j + 1)

    o = _flash_result(acc_scr)
    lane = lax.broadcasted_iota(jnp.int32, (t, LANES), 1)
    for sb in range(nsb):
        pair = jnp.where(lane < HEAD_DIM, o[2 * sb * t:(2 * sb + 1) * t], o[(2 * sb + 1) * t:(2 * sb + 2) * t])
        o_ref[0, sb * t:(sb + 1) * t, :] = pair.astype(o_ref.dtype)


def _attn_c(qkv, fcum, *, t, nsb):
    bsz, seq, _ = qkv.shape
    tq = t * nsb
    pairs = C_HEADS // 2
    q0 = 3 * A_W // LANES
    return pl.pallas_call(
        functools.partial(_attn_c_kernel, t=t, nsb=nsb),
        grid=(pairs, bsz, seq // tq),
        in_specs=[
            pl.BlockSpec((1, tq, LANES), lambda p, b, i: (b, i, q0 + p)),
            pl.BlockSpec((1, seq, LANES), lambda p, b, i: (b, 0, q0 + pairs + p)),
            pl.BlockSpec((1, seq, LANES), lambda p, b, i: (b, 0, q0 + 2 * pairs + p)),
            pl.BlockSpec((1, 1, 2, seq), lambda p, b, i: (b, p, 0, 0)),
        ],
        out_specs=pl.BlockSpec((1, tq, LANES), lambda p, b, i: (b, i, p)),
        out_shape=jax.ShapeDtypeStruct((bsz, seq, C_W), BF16),
        scratch_shapes=_flash_scratch(seq, t, nsb),
        compiler_params=_params("parallel", "parallel", "arbitrary"),
        name="attn_forget",
    )(qkv, qkv, qkv, fcum)


def _attn_b_kernel(q_ref, kp_ref, kc_ref, vp_ref, vc_ref, bias_ref, o_ref, lse_ref, *, dil, nb):
    n = pl.program_id(1)
    lane = lax.broadcasted_iota(jnp.int32, (BLOCK, LANES), 1)
    first_variant = jnp.minimum(n, 1)

    def residue(r):
        for hp in range(B_HEADS // 2):
            cols = slice(hp * LANES, (hp + 1) * LANES)
            kcat = jnp.concatenate([kp_ref[0, r, :, cols], kc_ref[0, r, :, cols]], axis=0)
            vcat = jnp.concatenate([vp_ref[0, r, :, cols], vc_ref[0, r, :, cols]], axis=0)
            for jb in range(nb):
                qh = _half_masks(q_ref[0, r, jb * BLOCK:(jb + 1) * BLOCK, cols] * SCALE)
                kwin = kcat[jb * BLOCK:(jb + 2) * BLOCK]
                vwin = vcat[jb * BLOCK:(jb + 2) * BLOCK]
                variant = first_variant if jb == 0 else 1
                outs, lses = [], []
                for hh in range(2):
                    s = _dot_nt(qh[hh], kwin) + bias_ref[variant, hp, hh * BLOCK:(hh + 1) * BLOCK, :]
                    m = jnp.max(s, axis=-1, keepdims=True)
                    e = jnp.exp(s - m)
                    den = jnp.sum(e, axis=-1, keepdims=True)
                    outs.append(_dot(e.astype(BF16), vwin) / den)
                    lses.append(jnp.broadcast_to(m + jnp.log(den), (BLOCK, LANES)))
                if dil == 1:
                    rows = pl.ds(jb * BLOCK, BLOCK)
                else:
                    rows = pl.ds(jb * BLOCK * dil + r, BLOCK, stride=dil)
                o_ref[0, hp, rows, :] = jnp.where(lane < HEAD_DIM, outs[0], outs[1])
                lse_ref[0, hp, rows, :] = jnp.where(lane < HEAD_DIM, lses[0], lses[1])

    if dil == 1:
        residue(0)
    else:
        def body(r, carry):
            residue(r)
            return carry

        lax.fori_loop(0, dil, body, 0)


def _attn_b_group(qkv_g, bias_g, g, nb):
    bsz, dil, m_len, _ = qkv_g.shape
    tb = BLOCK * nb
    cur = lambda c: pl.BlockSpec((1, dil, tb, B_W), lambda b, n: (b, 0, n, c))
    prev = lambda c: pl.BlockSpec((1, dil, BLOCK, B_W), lambda b, n: (b, 0, jnp.maximum(n * nb - 1, 0), c))
    pairs = B_HEADS // 2
    out_spec = pl.BlockSpec((1, pairs, tb * dil, LANES), lambda b, n: (b, 0, n, 0))
    out_sds = jax.ShapeDtypeStruct((bsz, pairs, m_len * dil, LANES), F32)
    return pl.pallas_call(
        functools.partial(_attn_b_kernel, dil=dil, nb=nb),
        grid=(bsz, m_len // tb),
        in_specs=[cur(0), prev(1), cur(1), prev(2), cur(2),
                  pl.BlockSpec(bias_g.shape, lambda b, n: (0, 0, 0, 0))],
        out_specs=[out_spec, out_spec],
        out_shape=[out_sds, out_sds],
        compiler_params=_params("parallel", "arbitrary"),
        name=f"attn_dilated_g{g}",
    )(qkv_g, qkv_g, qkv_g, qkv_g, qkv_g, bias_g)


def _merge_kernel(oa_ref, ob0, ob1, ob2, ls0, ls1, ls2, oc_ref, gate_ref, x_ref, mod_ref,
                  wa_ref, wb_ref, wc_ref, wo_ref, o_ref):
    parts = []
    for hp in range(B_HEADS // 2):
        l0, l1, l2 = ls0[0, hp], ls1[0, hp], ls2[0, hp]
        mx = jnp.maximum(jnp.maximum(l0, l1), l2)
        e0, e1, e2 = jnp.exp(l0 - mx), jnp.exp(l1 - mx), jnp.exp(l2 - mx)
        den = e0 + e1 + e2
        parts.append((e0 / den) * ob0[0, hp] + (e1 / den) * ob1[0, hp] + (e2 / den) * ob2[0, hp])
    ob = jnp.concatenate(parts, axis=1)
    d = D_MODEL
    merged = (gate_ref[:, 0:d] * _dot(oa_ref[...], wa_ref[...])
              + gate_ref[:, d:2 * d] * _dot(ob.astype(BF16), wb_ref[...])
              + gate_ref[:, 2 * d:3 * d] * _dot(oc_ref[...], wc_ref[...]))
    y = _dot(merged.astype(BF16), wo_ref[...])
    o_ref[...] = x_ref[...] + mod_ref[0][2:3] * y


def _merge(oa, obs, lses, oc, gf, x2d, mod3, wa, wb, wc, wo, *, seq, tm):
    rows, d = x2d.shape
    per_b = seq // tm
    row = lambda w: pl.BlockSpec((tm, w), lambda i: (i, 0))
    full = lambda a: pl.BlockSpec(a.shape, lambda i: (0, 0))
    paired = pl.BlockSpec((1, B_HEADS // 2, tm, LANES), lambda i: (i // per_b, 0, i % per_b, 0))
    return pl.pallas_call(
        _merge_kernel,
        grid=(rows // tm,),
        in_specs=[row(A_W)] + [paired] * 6 + [row(C_W), row(GATE_W), row(d),
                  pl.BlockSpec((1, 6, d), lambda i: (i // per_b, 0, 0)),
                  full(wa), full(wb), full(wc), full(wo)],
        out_specs=row(d),
        out_shape=jax.ShapeDtypeStruct((rows, d), F32),
        compiler_params=_params("parallel"),
        name="merge_outproj",
    )(oa, *obs, *lses, oc, gf, x2d, mod3, wa, wb, wc, wo)


def _ffn_kernel(x_ref, mod_ref, g_ref, wg_ref, wu_ref, wd_ref, o_ref, h_scr, acc_scr, *, nf):
    f = pl.program_id(1)

    @pl.when(f == 0)
    def _():
        m = mod_ref[0]
        h_scr[...] = _rms_mod(x_ref[...], g_ref[...], m[4:5], m[3:4]).astype(BF16)
        acc_scr[...] = jnp.zeros(acc_scr.shape, F32)

    h = h_scr[...]
    act = (_silu(_dot(h, wg_ref[...])) * _dot(h, wu_ref[...])).astype(BF16)
    acc_scr[...] += _dot(act, wd_ref[...])

    @pl.when(f == nf - 1)
    def _():
        o_ref[...] = x_ref[...] + mod_ref[0][5:6] * acc_scr[...]


def _ffn(x2d, mod3, g, wg, wu, wd, *, seq, tm, tf):
    rows, d = x2d.shape
    dff = wg.shape[1]
    nf = dff // tf
    per_b = seq // tm
    return pl.pallas_call(
        functools.partial(_ffn_kernel, nf=nf),
        grid=(rows // tm, nf),
        in_specs=[
            pl.BlockSpec((tm, d), lambda i, f: (i, 0)),
            pl.BlockSpec((1, 6, d), lambda i, f: (i // per_b, 0, 0)),
            pl.BlockSpec((1, d), lambda i, f: (0, 0)),
            pl.BlockSpec((d, tf), lambda i, f: (0, f)),
            pl.BlockSpec((d, tf), lambda i, f: (0, f)),
            pl.BlockSpec((tf, d), lambda i, f: (f, 0)),
        ],
        out_specs=pl.BlockSpec((tm, d), lambda i, f: (i, 0)),
        out_shape=jax.ShapeDtypeStruct((rows, d), F32),
        scratch_shapes=[pltpu.VMEM((tm, d), BF16), pltpu.VMEM((tm, d), F32)],
        compiler_params=_params("parallel", "arbitrary"),
        name="ffn_dense",
    )(x2d, mod3, g, wg, wu, wd)


def _route_kernel(x_ref, mod_ref, g_ref, wr_ref, tri_ref, h_ref, ridx_ref, rw_ref, cnt_ref, carry_scr):
    i = pl.program_id(0)

    @pl.when(i == 0)
    def _():
        carry_scr[...] = jnp.zeros(carry_scr.shape, F32)

    m = mod_ref[0]
    h = _rms_mod(x_ref[...], g_ref[...], m[4:5], m[3:4])
    h_ref[...] = h
    logits = jnp.dot(h, wr_ref[...], preferred_element_type=F32, precision=lax.Precision.HIGHEST)
    idx = lax.broadcasted_iota(jnp.int32, logits.shape, 1)
    n = logits.shape[1]
    m1 = jnp.max(logits, axis=-1, keepdims=True)
    i1 = jnp.min(jnp.where(logits == m1, idx, n), axis=-1, keepdims=True)
    first = idx == i1
    rest = jnp.where(first, -jnp.inf, logits)
    m2 = jnp.max(rest, axis=-1, keepdims=True)
    i2 = jnp.min(jnp.where(rest == m2, idx, n), axis=-1, keepdims=True)
    second = idx == i2
    e = jnp.exp(m2 - m1)
    den = 1.0 + e
    onehot = jnp.where(first | second, 1.0, 0.0)
    before = _dot(tri_ref[...], onehot.astype(BF16)) - onehot + carry_scr[...]
    rank1 = jnp.sum(jnp.where(first, before, 0.0), axis=-1, keepdims=True).astype(jnp.int32)
    rank2 = jnp.sum(jnp.where(second, before, 0.0), axis=-1, keepdims=True).astype(jnp.int32)
    carry_scr[...] += jnp.sum(onehot, axis=0, keepdims=True)
    ridx_ref[...] = jnp.where(idx == 0, i1, jnp.where(idx == 1, i2, jnp.where(idx == 2, rank1,
                              jnp.where(idx == 3, rank2, 0))))
    rw_ref[...] = jnp.where(idx == 0, 1.0 / den, jnp.where(idx == 1, e / den, 0.0))
    cnt_ref[...] = carry_scr[...]


def _route(x2d, mod3, g, wr, *, seq, tm):
    rows, d = x2d.shape
    ne = wr.shape[1]
    per_b = seq // tm
    tri = (jnp.arange(tm)[:, None] >= jnp.arange(tm)[None, :]).astype(BF16)
    row = lambda w: pl.BlockSpec((tm, w), lambda i: (i, 0))
    return pl.pallas_call(
        _route_kernel,
        grid=(rows // tm,),
        in_specs=[row(d), pl.BlockSpec((1, 6, d), lambda i: (i // per_b, 0, 0)),
                  pl.BlockSpec((1, d), lambda i: (0, 0)), pl.BlockSpec((d, ne), lambda i: (0, 0)),
                  pl.BlockSpec((tm, tm), lambda i: (0, 0))],
        out_specs=[row(d), row(ne), row(ne), pl.BlockSpec((1, ne), lambda i: (0, 0))],
        out_shape=[jax.ShapeDtypeStruct((rows, d), F32), jax.ShapeDtypeStruct((rows, ne), jnp.int32),
                   jax.ShapeDtypeStruct((rows, ne), F32), jax.ShapeDtypeStruct((1, ne), F32)],
        scratch_shapes=[pltpu.VMEM((1, ne), F32)],
        compiler_params=_params("arbitrary"),
        name="moe_route",
    )(x2d, mod3, g, wr, tri)


def _gather_rows(tok_of, n, src_hbm, dst, sem):
    def issue(r, carry):
        pltpu.make_async_copy(src_hbm.at[pl.ds(tok_of(r), 1)], dst.at[pl.ds(r, 1)], sem).start()
        return carry

    lax.fori_loop(0, n, issue, 0, unroll=8)


def _gather_wait(n, src_hbm, dst, sem):
    pltpu.make_async_copy(src_hbm.at[pl.ds(0, n)], dst, sem).wait()


def _expert_kernel(te_ref, src_ref, nu_ref, h_hbm, wg_ref, wu_ref, wd_ref, y_ref,
                   xbuf, xb_scr, acc_scr, sem, *, nf, r):
    i = pl.program_id(0)
    f = pl.program_id(1)
    active = i < nu_ref[0]
    slot = i % 2

    def gather(tile, s):
        _gather_rows(lambda j: src_ref[tile * r + j] >> 1, r, h_hbm, xbuf.at[s], sem.at[s])

    @pl.when((i == 0) & (f == 0))
    def _():
        gather(0, 0)

    @pl.when(active & (f == 0))
    def _():
        _gather_wait(r, h_hbm, xbuf.at[slot], sem.at[slot])

        @pl.when(i + 1 < nu_ref[0])
        def _():
            gather(i + 1, 1 - slot)

        xb_scr[...] = xbuf[slot].astype(BF16)
        acc_scr[...] = jnp.zeros(acc_scr.shape, F32)

    @pl.when(active)
    def _():
        xb = xb_scr[...]
        act = (_silu(_dot(xb, wg_ref[0])) * _dot(xb, wu_ref[0])).astype(BF16)
        acc_scr[...] += _dot(act, wd_ref[0])

    @pl.when(f == nf - 1)
    def _():
        y_ref[...] = jnp.where(active, acc_scr[...], 0.0)


def _experts(h2, tile_expert, src_pair, n_used, wg, wu, wd, *, r, tf):
    ne, d, dff = wg.shape
    nf = dff // tf
    p_rows = src_pair.shape[0]
    nt = p_rows // r
    fsel = lambda i, f, nu: jnp.where(i < nu[0], f, nf - 1)
    grid_spec = pltpu.PrefetchScalarGridSpec(
        num_scalar_prefetch=3,
        grid=(nt, nf),
        in_specs=[
            pl.BlockSpec(memory_space=pl.ANY),
            pl.BlockSpec((1, d, tf), lambda i, f, te, sp, nu: (te[i], 0, fsel(i, f, nu))),
            pl.BlockSpec((1, d, tf), lambda i, f, te, sp, nu: (te[i], 0, fsel(i, f, nu))),
            pl.BlockSpec((1, tf, d), lambda i, f, te, sp, nu: (te[i], fsel(i, f, nu), 0)),
        ],
        out_specs=pl.BlockSpec((r, d), lambda i, f, te, sp, nu: (i, 0)),
        scratch_shapes=[pltpu.VMEM((2, r, d), F32), pltpu.VMEM((r, d), BF16), pltpu.VMEM((r, d), F32),
                        pltpu.SemaphoreType.DMA((2,))],
    )
    return pl.pallas_call(
        functools.partial(_expert_kernel, nf=nf, r=r),
        grid_spec=grid_spec,
        out_shape=jax.ShapeDtypeStruct((p_rows, d), F32),
        compiler_params=_params("arbitrary", "arbitrary"),
        name="moe_experts",
    )(tile_expert, src_pair, n_used, h2, wg, wu, wd)


def _combine_kernel(pos_ref, y_hbm, x_ref, rw_ref, mod_ref, fg_ref, o_ref, ybuf, sem, *, tm, final):
    i = pl.program_id(0)
    n = pl.num_programs(0)
    slot = i % 2

    def gather(tile, s):
        for k in range(2):
            _gather_rows(lambda j: pos_ref[(tile * tm + j) * 2 + k], tm, y_hbm, ybuf.at[s, k], sem.at[s])

    @pl.when(i == 0)
    def _():
        gather(0, 0)

    for k in range(2):
        _gather_wait(tm, y_hbm, ybuf.at[slot, k], sem.at[slot])

    @pl.when(i + 1 < n)
    def _():
        gather(i + 1, 1 - slot)

    w = rw_ref[...]
    f = w[:, 0:1] * ybuf[slot, 0] + w[:, 1:2] * ybuf[slot, 1]
    x = x_ref[...] + mod_ref[0][5:6] * f
    if final:
        x = (x * lax.rsqrt(jnp.mean(x * x, axis=-1, keepdims=True) + EPS)) * fg_ref[...]
    o_ref[...] = x


def _combine(pos, y, x2d, rw, mod3, final_g, *, seq, tm, final):
    rows, d = x2d.shape
    per_b = seq // tm
    grid_spec = pltpu.PrefetchScalarGridSpec(
        num_scalar_prefetch=1,
        grid=(rows // tm,),
        in_specs=[
            pl.BlockSpec(memory_space=pl.ANY),
            pl.BlockSpec((tm, d), lambda i, p: (i, 0)),
            pl.BlockSpec((tm, rw.shape[1]), lambda i, p: (i, 0)),
            pl.BlockSpec((1, 6, d), lambda i, p: (i // per_b, 0, 0)),
            pl.BlockSpec((1, d), lambda i, p: (0, 0)),
        ],
        out_specs=pl.BlockSpec((tm, d), lambda i, p: (i, 0)),
        scratch_shapes=[pltpu.VMEM((2, 2, tm, d), F32), pltpu.SemaphoreType.DMA((2,))],
    )
    return pl.pallas_call(
        functools.partial(_combine_kernel, tm=tm, final=final),
        grid_spec=grid_spec,
        out_shape=jax.ShapeDtypeStruct((rows, d), F32),
        compiler_params=_params("arbitrary"),
        name="moe_combine",
    )(pos, y, x2d, rw, mod3, final_g)


def _moe(x2d, mod3, g, wr, wg, wu, wd, final_g, *, seq, final):
    rows, d = x2d.shape
    ne = wr.shape[1]
    r = 512
    h2, ridx, rw, cnt = _route(x2d, mod3, g, wr, seq=seq, tm=1024)
    counts = cnt[0].astype(jnp.int32)
    tiles_e = (counts + r - 1) // r
    tile_end = jnp.cumsum(tiles_e)
    start = (tile_end - tiles_e) * r
    pos = jnp.take(start, ridx[:, 0:2], axis=0) + ridx[:, 2:4]
    p_rows = 2 * rows + ne * r
    nt = p_rows // r
    pair_id = jnp.arange(2 * rows, dtype=jnp.int32)
    src_pair = jnp.zeros((p_rows,), jnp.int32).at[pos.reshape(-1)].set(pair_id)
    n_used = tile_end[-1:]
    tile_idx = jnp.minimum(jnp.arange(nt), n_used[0] - 1)
    tile_expert = jnp.sum(tile_idx[:, None] >= tile_end[None, :], axis=1)
    y = _experts(h2, tile_expert.astype(jnp.int32), src_pair, n_used.astype(jnp.int32), wg, wu, wd,
                 r=r, tf=512)
    return _combine(pos.reshape(-1).astype(jnp.int32), y, x2d, rw, mod3, final_g, seq=seq, tm=256,
                    final=final)


def _final_norm_kernel(x_ref, g_ref, o_ref):
    x = x_ref[...]
    o_ref[...] = (x * lax.rsqrt(jnp.mean(x * x, axis=-1, keepdims=True) + EPS)) * g_ref[...]


def _final_norm(x2d, g, *, tm):
    rows, d = x2d.shape
    return pl.pallas_call(
        _final_norm_kernel,
        grid=(rows // tm,),
        in_specs=[pl.BlockSpec((tm, d), lambda i: (i, 0)), pl.BlockSpec((1, d), lambda i: (0, 0))],
        out_specs=pl.BlockSpec((tm, d), lambda i: (i, 0)),
        out_shape=jax.ShapeDtypeStruct((rows, d), F32),
        compiler_params=_params("parallel"),
        name="final_norm",
    )(x2d, g)


def _t5_bucket(dist):
    n = jnp.maximum(dist, 0)
    max_exact = N_BUCKETS // 2
    nf = jnp.maximum(n, 1).astype(F32)
    large = max_exact + (jnp.log(nf / max_exact) / math.log(REL_MAX_DIST / max_exact)
                         * (N_BUCKETS - max_exact)).astype(jnp.int32)
    large = jnp.minimum(large, N_BUCKETS - 1)
    return jnp.where(n < max_exact, n, large)


def _bias_tiles_a(rel_bias, seq, t):
    nq = seq // t
    ncol = 2 * A_HEADS
    tab = rel_bias[:, :ncol][_t5_bucket(jnp.arange(seq))].astype(F32).T
    vneg = jnp.full((ncol, t), NEG, F32)
    v = jnp.concatenate([vneg, tab], axis=1)
    u = jnp.concatenate([v[:, 1:seq + 1][:, ::-1], vneg[:, :1], v[:, seq + 1:seq + t][:, ::-1]], axis=1)

    def toeplitz_kernel(u_ref, o_ref):
        x = jnp.broadcast_to(u_ref[0], (t, seq + t))
        r = pltpu.roll(x, 0, 1, stride=1, stride_axis=0)
        for delta in range(nq):
            c0 = (nq - 1 - delta) * t
            o_ref[0, delta] = r[:, c0:c0 + t]

    return pl.pallas_call(
        toeplitz_kernel,
        grid=(ncol,),
        in_specs=[pl.BlockSpec((1, 1, seq + t), lambda c: (c, 0, 0))],
        out_specs=pl.BlockSpec((1, nq, t, t), lambda c: (c // 2, 0, c % 2, 0)),
        out_shape=jax.ShapeDtypeStruct((A_HEADS, nq, 2 * t, t), F32),
        compiler_params=_params("parallel"),
        name="bias_tiles_diff",
    )(u.reshape(ncol, 1, seq + t))


def _bias_tiles_b(rel_bias):
    ng = len(B_GROUPS)
    period = 3 * BLOCK
    rows = []
    for g, (win, dil) in enumerate(B_GROUPS):
        n_back = win // dil
        tab = rel_bias[:, 2 * A_HEADS + g * B_HEADS:2 * A_HEADS + (g + 1) * B_HEADS]
        vals = tab[_t5_bucket(jnp.arange(n_back, -1, -1) * dil)].astype(F32).T
        rows.append(jnp.concatenate([vals, jnp.full((B_HEADS, period - n_back - 1), NEG, F32)], axis=1))
    u = jnp.concatenate(rows, axis=0)

    def toeplitz_kernel(u_ref, o_ref):
        x = jnp.broadcast_to(u_ref[0], (BLOCK, period))
        r = pltpu.roll(x, 0, 1, stride=1, stride_axis=0)[:, :2 * BLOCK]
        col = lax.broadcasted_iota(jnp.int32, r.shape, 1)
        o_ref[0, 0, 0] = jnp.where(col >= BLOCK, r, NEG)
        o_ref[0, 1, 0] = r

    return pl.pallas_call(
        toeplitz_kernel,
        grid=(ng * B_HEADS,),
        in_specs=[pl.BlockSpec((1, 1, period), lambda c: (c, 0, 0))],
        out_specs=pl.BlockSpec((1, 2, 1, BLOCK, 2 * BLOCK),
                               lambda c: (c // B_HEADS, 0, (c % B_HEADS) // 2, c % 2, 0)),
        out_shape=jax.ShapeDtypeStruct((ng, 2, B_HEADS // 2, 2 * BLOCK, 2 * BLOCK), F32),
        compiler_params=_params("parallel"),
        name="bias_tiles_dilated",
    )(u.reshape(ng * B_HEADS, 1, period))


def kernel(x, c, norm_mix_g, norm_ffn_g, w_mod, b_mod, w_in, b_forget, lam_q1, lam_k1, lam_q2, lam_k2,
           subln_g, rel_bias, w_br_a, w_br_b, w_br_c, w_out, w_ff_gate, w_ff_up, w_ff_down, w_router,
           w_exp_gate, w_exp_up, w_exp_down, final_norm_g):
    bsz, seq, d = x.shape
    depth = w_mod.shape[0]
    rows = bsz * seq
    t_attn = 256
    x2d = x.reshape(rows, d)

    mod = _modulation(c, w_mod, b_mod)
    bias_a = _bias_tiles_a(rel_bias, seq, t_attn)
    bias_b = _bias_tiles_b(rel_bias)
    f_col = QKV_W
    g_col = QKV_W + C_HEADS

    final_g = final_norm_g.reshape(1, d)
    fused_final = False
    for l in range(depth):
        lam_init = 0.8 - 0.6 * math.exp(-0.3 * l)
        mod3 = mod[l].reshape(bsz, 6, d)
        w_l = w_in[l]
        a_end, b_end = 3 * A_W, 3 * A_W + 3 * B_QW
        w_b = [jnp.concatenate([w_l[:, a_end + s * B_QW + g * B_W:a_end + s * B_QW + (g + 1) * B_W]
                                for s in range(3)], axis=1) for g in range(len(B_GROUPS))]
        w_qkv = jnp.concatenate([w_l[:, :a_end], w_l[:, b_end:QKV_W]] + w_b, axis=1).astype(BF16)
        w_f = jnp.pad(w_l[:, f_col:f_col + C_HEADS], ((0, 0), (0, LANES - C_HEADS)))
        w_gf = jnp.concatenate([w_l[:, g_col:], w_f], axis=1).astype(BF16)

        g_mix = norm_mix_g[l].reshape(1, d)
        qkv_ac, qkv_b0, qkv_b1, qkv_b2 = _qkv_proj(x2d, mod3, g_mix, w_qkv, bsz=bsz, seq=seq, tm=512)
        gf = _inproj(x2d, mod3, g_mix, w_gf, seq=seq, tm=1024, tn=640, out_dtype=F32,
                     n_sigmoid=GATE_W)
        qkv_ac = qkv_ac.reshape(bsz, seq, -1)

        b_f8 = jnp.pad(b_forget[l], (0, 8 - C_HEADS)).reshape(8, 1)
        fcum = _forget_cumsum(gf.reshape(bsz, seq, GATE_W + LANES), b_f8, col_block=GATE_W // LANES)
        fcum = fcum[:, :C_HEADS].reshape(bsz, C_HEADS // 2, 2, seq)

        oa = _attn_a(qkv_ac, bias_a, lam_q1[l].reshape(1, -1), lam_k1[l].reshape(1, -1),
                     lam_q2[l].reshape(1, -1), lam_k2[l].reshape(1, -1), subln_g[l].reshape(1, -1),
                     lam_init=lam_init, t=t_attn, nsb=4)
        oc = _attn_c(qkv_ac, fcum, t=t_attn, nsb=4)
        obs, lses = [], []
        groups = (qkv_b0.reshape(bsz, 1, seq, 3 * B_W), qkv_b1, qkv_b2)
        for g, (qkv_g, nb) in enumerate(zip(groups, (4, 1, 1))):
            o_g, lse_g = _attn_b_group(qkv_g, bias_b[g], g, nb)
            obs.append(o_g)
            lses.append(lse_g)

        x2d = _merge(oa.reshape(rows, A_W), obs, lses, oc.reshape(rows, C_W), gf, x2d, mod3,
                     w_br_a[l].astype(BF16), w_br_b[l].astype(BF16), w_br_c[l].astype(BF16),
                     w_out[l].astype(BF16), seq=seq, tm=512)

        g_ffn = norm_ffn_g[l].reshape(1, d)
        if l % 2 == 0:
            j = l // 2
            x2d = _ffn(x2d, mod3, g_ffn, w_ff_gate[j].astype(BF16), w_ff_up[j].astype(BF16),
                       w_ff_down[j].astype(BF16), seq=seq, tm=512, tf=1408)
        else:
            j = l // 2
            fused_final = l == depth - 1
            x2d = _moe(x2d, mod3, g_ffn, w_router[j], w_exp_gate[j].astype(BF16),
                       w_exp_up[j].astype(BF16), w_exp_down[j].astype(BF16), final_g,
                       seq=seq, final=fused_final)

    if not fused_final:
        x2d = _final_norm(x2d, final_g, tm=1024)
    return x2d.reshape(bsz, seq, d)
```

```python
import functools
import math

import jax
import jax.numpy as jnp
from jax import lax
from jax.experimental import pallas as pl
from jax.experimental.pallas import tpu as pltpu

F32 = jnp.float32
BF16 = jnp.bfloat16

D_MODEL = 1024
HEAD_DIM = 64
LANES = 128
A_HEADS = 4
A_W = A_HEADS * 2 * HEAD_DIM
B_GROUPS = ((128, 1), (512, 4), (2048, 16))
B_HEADS = 6
B_W = B_HEADS * HEAD_DIM
B_QW = len(B_GROUPS) * B_W
C_HEADS = 6
C_W = C_HEADS * HEAD_DIM
N_BRANCH = 3
BLOCK = 128
N_BUCKETS = 32
REL_MAX_DIST = 2048
N_EXPERTS = 8
EPS = 1e-6
QKV_W = 3 * A_W + 3 * B_QW + 3 * C_W
GATE_W = N_BRANCH * D_MODEL
NEG = -1e30
SCALE = HEAD_DIM ** -0.5
VMEM_LIMIT = 56 * 1024 * 1024


def _params(*sem):
    return pltpu.CompilerParams(dimension_semantics=sem, vmem_limit_bytes=VMEM_LIMIT)


def _rms_mod(x, g, sc, sh):
    y = x * lax.rsqrt(jnp.mean(x * x, axis=-1, keepdims=True) + EPS)
    return (y * g) * (1.0 + sc) + sh


def _dot(a, b):
    return jnp.dot(a, b, preferred_element_type=F32)


def _dot_nt(a, b):
    return lax.dot_general(a, b, (((1,), (1,)), ((), ())), preferred_element_type=F32)


def _silu(a):
    return a * jax.nn.sigmoid(a)


def _mod_kernel(c_ref, w_ref, b_ref, o_ref):
    a = _silu(c_ref[...]).astype(BF16)
    o_ref[0] = _dot(a, w_ref[0].astype(BF16)) + b_ref[0]


def _modulation(c, w_mod, b_mod):
    depth, d, n = w_mod.shape
    bsz = c.shape[0]
    tn = 1536
    return pl.pallas_call(
        _mod_kernel,
        grid=(depth, n // tn),
        in_specs=[
            pl.BlockSpec((bsz, d), lambda l, j: (0, 0)),
            pl.BlockSpec((1, d, tn), lambda l, j: (l, 0, j)),
            pl.BlockSpec((1, 1, tn), lambda l, j: (l, 0, j)),
        ],
        out_specs=pl.BlockSpec((1, bsz, tn), lambda l, j: (l, 0, j)),
        out_shape=jax.ShapeDtypeStruct((depth, bsz, n), F32),
        compiler_params=_params("parallel", "parallel"),
        name="modulation",
    )(c, w_mod, b_mod.reshape(depth, 1, n))


_A_BLOCKS = 3 * A_W // B_W
_B_BLOCKS = 3 * B_QW // B_W
_QKV_SRC_BLOCKS = (list(range(_A_BLOCKS))
                   + list(range(_A_BLOCKS + _B_BLOCKS, QKV_W // B_W))
                   + [_A_BLOCKS + s * len(B_GROUPS) + g for g in range(len(B_GROUPS)) for s in range(3)])


def _prep_w_in_kernel(*refs):
    n = len(_QKV_SRC_BLOCKS)
    piece_refs, x_ref, y_ref, qkv_ref, gf_ref = refs[:n], refs[n], refs[n + 1], refs[n + 2], refs[n + 3]
    for j, ref in enumerate(piece_refs):
        qkv_ref[0, :, j * B_W:(j + 1) * B_W] = ref[0].astype(BF16)
    x = x_ref[0]
    rolled = pltpu.roll(x, GATE_W - C_HEADS, 1)
    tail = pltpu.roll(y_ref[0], LANES - C_HEADS, 1)
    lane = lax.broadcasted_iota(jnp.int32, tail.shape, 1)
    gf_ref[0, :, :GATE_W - LANES] = rolled[:, :GATE_W - LANES].astype(BF16)
    gf_ref[0, :, GATE_W - LANES:GATE_W] = jnp.where(lane < LANES - C_HEADS, rolled[:, GATE_W - LANES:],
                                                     tail).astype(BF16)
    gf_ref[0, :, GATE_W:] = jnp.where(lane < C_HEADS, x[:, :LANES], 0.0).astype(BF16)


def _prep_w_in(w_in):
    depth, d, _ = w_in.shape
    tr = 256
    piece = lambda c: pl.BlockSpec((1, tr, B_W), lambda l, i: (l, i, c))
    return pl.pallas_call(
        _prep_w_in_kernel,
        grid=(depth, d // tr),
        in_specs=[piece(c) for c in _QKV_SRC_BLOCKS]
        + [pl.BlockSpec((1, tr, GATE_W), lambda l, i: (l, i, QKV_W // GATE_W)),
           pl.BlockSpec((1, tr, LANES), lambda l, i: (l, i, (QKV_W + GATE_W) // LANES))],
        out_specs=[pl.BlockSpec((1, tr, QKV_W), lambda l, i: (l, i, 0)),
                   pl.BlockSpec((1, tr, GATE_W + LANES), lambda l, i: (l, i, 0))],
        out_shape=[jax.ShapeDtypeStruct((depth, d, QKV_W), BF16),
                   jax.ShapeDtypeStruct((depth, d, GATE_W + LANES), BF16)],
        compiler_params=_params("parallel", "parallel"),
        name="prep_w_in",
    )(*([w_in] * (len(_QKV_SRC_BLOCKS) + 2)))


def _inproj_kernel(x_ref, mod_ref, g_ref, w_ref, o_ref, h_scr, *, n_sigmoid, tn):
    j = pl.program_id(1)

    @pl.when(j == 0)
    def _():
        m = mod_ref[0]
        h_scr[...] = _rms_mod(x_ref[...], g_ref[...], m[1:2], m[0:1]).astype(BF16)

    y = _dot(h_scr[...], w_ref[...])
    if n_sigmoid:
        col = j * tn + lax.broadcasted_iota(jnp.int32, y.shape, 1)
        y = jnp.where(col < n_sigmoid, jax.nn.sigmoid(y), y)
    o_ref[...] = y.astype(o_ref.dtype)


def _inproj(x2d, mod3, g, w, *, seq, tm, tn, out_dtype, n_sigmoid=0):
    rows, d = x2d.shape
    n = w.shape[1]
    per_b = seq // tm
    return pl.pallas_call(
        functools.partial(_inproj_kernel, n_sigmoid=n_sigmoid, tn=tn),
        grid=(rows // tm, n // tn),
        in_specs=[
            pl.BlockSpec((tm, d), lambda i, j: (i, 0)),
            pl.BlockSpec((1, 6, d), lambda i, j: (i // per_b, 0, 0)),
            pl.BlockSpec((1, d), lambda i, j: (0, 0)),
            pl.BlockSpec((d, tn), lambda i, j: (0, j)),
        ],
        out_specs=pl.BlockSpec((tm, tn), lambda i, j: (i, j)),
        out_shape=jax.ShapeDtypeStruct((rows, n), out_dtype),
        scratch_shapes=[pltpu.VMEM((tm, d), BF16)],
        compiler_params=_params("parallel", "arbitrary"),
        name="inproj",
    )(x2d, mod3, g, w)


def _qkv_kernel(x_ref, mod_ref, g_ref, w_ref, ac_ref, b0_ref, b1_ref, b2_ref, h_scr, *, tm):
    m = mod_ref[0]
    h = _rms_mod(x_ref[...], g_ref[...], m[1:2], m[0:1])
    nc = h_scr.shape[0]
    for c in range(nc):
        h_scr[c] = h[:, c * LANES:(c + 1) * LANES]
    hb = h.astype(BF16)
    n_ac = ac_ref.shape[1]
    n_b = b0_ref.shape[1]
    ac_ref[...] = _dot(hb, w_ref[:, 0:n_ac]).astype(BF16)
    b0_ref[...] = _dot(hb, w_ref[:, n_ac:n_ac + n_b]).astype(BF16)
    for gi, ref in ((1, b1_ref), (2, b2_ref)):
        dil = B_GROUPS[gi][1]
        per = tm // dil
        hp = jnp.concatenate(
            [jnp.concatenate([h_scr[c, pl.ds(r, per, stride=dil), :] for c in range(nc)], axis=1).astype(BF16)
             for r in range(dil)], axis=0)
        y = _dot(hp, w_ref[:, n_ac + gi * n_b:n_ac + (gi + 1) * n_b]).astype(BF16)
        for r in range(dil):
            ref[0, r] = y[r * per:(r + 1) * per]


def _qkv_proj(x2d, mod3, g, w, *, bsz, seq, tm):
    rows, d = x2d.shape
    per_b = seq // tm
    n_b = 3 * B_W
    n_ac = w.shape[1] - 3 * n_b
    dil1, dil2 = B_GROUPS[1][1], B_GROUPS[2][1]
    strided = lambda dil: pl.BlockSpec((1, dil, tm // dil, n_b), lambda i: (i // per_b, 0, i % per_b, 0))
    return pl.pallas_call(
        functools.partial(_qkv_kernel, tm=tm),
        grid=(rows // tm,),
        in_specs=[
            pl.BlockSpec((tm, d), lambda i: (i, 0)),
            pl.BlockSpec((1, 6, d), lambda i: (i // per_b, 0, 0)),
            pl.BlockSpec((1, d), lambda i: (0, 0)),
            pl.BlockSpec(w.shape, lambda i: (0, 0), pipeline_mode=pl.Buffered(1)),
        ],
        out_specs=[pl.BlockSpec((tm, n_ac), lambda i: (i, 0)), pl.BlockSpec((tm, n_b), lambda i: (i, 0)),
                   strided(dil1), strided(dil2)],
        out_shape=[jax.ShapeDtypeStruct((rows, n_ac), BF16), jax.ShapeDtypeStruct((rows, n_b), BF16),
                   jax.ShapeDtypeStruct((bsz, dil1, seq // dil1, n_b), BF16),
                   jax.ShapeDtypeStruct((bsz, dil2, seq // dil2, n_b), BF16)],
        scratch_shapes=[pltpu.VMEM((d // LANES, tm, LANES), F32)],
        compiler_params=_params("parallel"),
        name="qkv_proj",
    )(x2d, mod3, g, w)


def _fcum_kernel(f_ref, b_ref, o_ref):
    z = f_ref[0].T[:8] + b_ref[...]
    x = jnp.minimum(z, 0.0) - jnp.log1p(jnp.exp(-jnp.abs(z)))
    s = x.shape[1]
    lane = lax.broadcasted_iota(jnp.int32, x.shape, 1)
    k = 1
    while k < s:
        x = x + jnp.where(lane >= k, pltpu.roll(x, k, 1), 0.0)
        k *= 2
    o_ref[0] = x


def _forget_cumsum(gf, b_f8, *, col_block):
    bsz, seq, _ = gf.shape
    return pl.pallas_call(
        _fcum_kernel,
        grid=(bsz,),
        in_specs=[
            pl.BlockSpec((1, seq, LANES), lambda b: (b, 0, col_block)),
            pl.BlockSpec((8, 1), lambda b: (0, 0)),
        ],
        out_specs=pl.BlockSpec((1, 8, seq), lambda b: (b, 0, 0)),
        out_shape=jax.ShapeDtypeStruct((bsz, 8, seq), F32),
        compiler_params=_params("parallel"),
        name="forget_cumsum",
    )(gf, b_f8)


def _half_masks(q):
    lane = lax.broadcasted_iota(jnp.int32, q.shape, 1)
    zero = jnp.zeros_like(q)
    return jnp.where(lane < HEAD_DIM, q, zero), jnp.where(lane >= HEAD_DIM, q, zero)


def _flash_init(first, v_ref, vext_scr, m_scr, acc_scr):
    @pl.when(first)
    def _():
        vext_scr[:, :LANES] = v_ref[0]
        vext_scr[:, LANES:] = jnp.ones((vext_scr.shape[0], LANES), BF16)

    m_scr[...] = jnp.full(m_scr.shape, -jnp.inf, F32)
    acc_scr[...] = jnp.zeros(acc_scr.shape, F32)


def _lane_tile(a, n):
    return a if n == 1 else jnp.concatenate([a] * n, axis=1)


def _flash_update(s, vext, rows, m_scr, acc_scr):
    m_prev = m_scr[rows]
    m_new = jnp.maximum(m_prev, jnp.max(s, axis=-1, keepdims=True))
    alpha = jnp.exp(m_prev - m_new)
    p = jnp.exp(s - _lane_tile(m_new, s.shape[1] // LANES))
    acc_scr[rows] = _lane_tile(alpha, 2) * acc_scr[rows] + _dot(p.astype(BF16), vext)
    m_scr[rows] = m_new


def _flash_result(acc_scr):
    acc = acc_scr[...]
    return acc[:, :LANES] / acc[:, LANES:]


def _attn_a_kernel(q_ref, k_ref, v_ref, bias_ref, lq1, lk1, lq2, lk2, sg_ref, o_ref,
                   vext_scr, m_scr, acc_scr, *, t, nsb, lam_init):
    qi = pl.program_id(2)
    _flash_init(qi == 0, v_ref, vext_scr, m_scr, acc_scr)
    qh = [_half_masks(q_ref[0, sb * t:(sb + 1) * t, :] * SCALE) for sb in range(nsb)]

    def step(kb, nkb, plan):
        off = pl.multiple_of(kb * t, t)
        kblk = k_ref[0, pl.ds(off, nkb * t), :]
        vext = vext_scr[pl.ds(off, nkb * t), :]
        for sb, deltas in plan:
            for hh in range(2):
                bias = [bias_ref[0, d, hh * t:(hh + 1) * t, :] for d in deltas]
                s = _dot_nt(qh[sb][hh], kblk) + (bias[0] if nkb == 1 else jnp.concatenate(bias, axis=1))
                _flash_update(s, vext, pl.ds((sb * 2 + hh) * t, t), m_scr, acc_scr)

    def body(kb2, carry):
        first = [qi * nsb + sb - 2 * kb2 for sb in range(nsb)]
        step(2 * kb2, 2, [(sb, (first[sb], first[sb] - 1)) for sb in range(nsb)])
        return carry

    lax.fori_loop(0, qi * (nsb // 2), body, 0)
    for j in range(0, nsb, 2):
        step(qi * nsb + j, 1, [(j, (0,))])
        step(qi * nsb + j, 2, [(sb, (sb - j, sb - j - 1)) for sb in range(j + 1, nsb)])

    o = _flash_result(acc_scr)
    lam = (jnp.exp(jnp.sum(lq1[...] * lk1[...], axis=-1, keepdims=True))
           - jnp.exp(jnp.sum(lq2[...] * lk2[...], axis=-1, keepdims=True)) + lam_init)
    for sb in range(nsb):
        d = o[2 * sb * t:(2 * sb + 1) * t] - lam * o[(2 * sb + 1) * t:(2 * sb + 2) * t]
        y = d * lax.rsqrt(jnp.mean(d * d, axis=-1, keepdims=True) + EPS)
        o_ref[0, sb * t:(sb + 1) * t, :] = ((y * sg_ref[...]) * (1.0 - lam_init)).astype(o_ref.dtype)


def _flash_scratch(seq, t, nsb):
    return [pltpu.VMEM((seq, 2 * LANES), BF16), pltpu.VMEM((2 * nsb * t, LANES), F32),
            pltpu.VMEM((2 * nsb * t, 2 * LANES), F32)]


def _attn_a(qkv, bias_a, lq1, lk1, lq2, lk2, subln_g, *, lam_init, t, nsb):
    bsz, seq, _ = qkv.shape
    tq = t * nsb
    vec = lambda n: pl.BlockSpec((1, n), lambda h, b, i: (0, 0))
    return pl.pallas_call(
        functools.partial(_attn_a_kernel, t=t, nsb=nsb, lam_init=lam_init),
        grid=(A_HEADS, bsz, seq // tq),
        in_specs=[
            pl.BlockSpec((1, tq, LANES), lambda h, b, i: (b, i, h)),
            pl.BlockSpec((1, seq, LANES), lambda h, b, i: (b, 0, A_HEADS + h)),
            pl.BlockSpec((1, seq, LANES), lambda h, b, i: (b, 0, 2 * A_HEADS + h)),
            pl.BlockSpec((1, seq // t, 2 * t, t), lambda h, b, i: (h, 0, 0, 0)),
            vec(HEAD_DIM), vec(HEAD_DIM), vec(HEAD_DIM), vec(HEAD_DIM), vec(LANES),
        ],
        out_specs=pl.BlockSpec((1, tq, LANES), lambda h, b, i: (b, i, h)),
        out_shape=jax.ShapeDtypeStruct((bsz, seq, A_W), BF16),
        scratch_shapes=_flash_scratch(seq, t, nsb),
        compiler_params=_params("parallel", "parallel", "arbitrary"),
        name="attn_diff",
    )(qkv, qkv, qkv, bias_a, lq1, lk1, lq2, lk2, subln_g)


def _attn_c_kernel(q_ref, k_ref, v_ref, f_ref, o_ref, vext_scr, m_scr, acc_scr, *, t, nsb):
    qi = pl.program_id(2)
    _flash_init(qi == 0, v_ref, vext_scr, m_scr, acc_scr)
    qh = [_half_masks(q_ref[0, sb * t:(sb + 1) * t, :] * SCALE) for sb in range(nsb)]
    q_off = pl.multiple_of(qi * (t * nsb), t * nsb)
    f_anchor = f_ref[0, 0, :, pl.ds(q_off, LANES)][:, :1]
    def causal(nkb):
        r = lax.broadcasted_iota(jnp.int32, (t, nkb * t), 0)
        c = lax.broadcasted_iota(jnp.int32, (t, nkb * t), 1)
        return r + (nkb - 1) * t >= c

    def step(kb, nkb, sbs, diag_sb):
        off = pl.multiple_of(kb * t, t)
        kblk = k_ref[0, pl.ds(off, nkb * t), :]
        vext = vext_scr[pl.ds(off, nkb * t), :]
        dec = [f_anchor[hh:hh + 1] - f_ref[0, 0, hh:hh + 1, pl.ds(off, nkb * t)] for hh in range(2)]
        for sb in sbs:
            for hh in range(2):
                s = _dot_nt(qh[sb][hh], kblk) + dec[hh]
                if sb == diag_sb:
                    s = jnp.where(causal(nkb), s, NEG)
                _flash_update(s, vext, pl.ds((sb * 2 + hh) * t, t), m_scr, acc_scr)

    def body(kb2, carry):
        step(2 * kb2, 2, range(nsb), None)
        return carry

    lax.fori_loop(0, qi * (nsb // 2), body, 0)
    for j in range(0, nsb, 2):
        step(qi * nsb + j, 1, [j], j)
        step(qi * nsb + j, 2, range(j + 1, nsb), j + 1)

    o = _flash_result(acc_scr)
    lane = lax.broadcasted_iota(jnp.int32, (t, LANES), 1)
    for sb in range(nsb):
        pair = jnp.where(lane < HEAD_DIM, o[2 * sb * t:(2 * sb + 1) * t], o[(2 * sb + 1) * t:(2 * sb + 2) * t])
        o_ref[0, sb * t:(sb + 1) * t, :] = pair.astype(o_ref.dtype)


def _attn_c(qkv, fcum, *, t, nsb):
    bsz, seq, _ = qkv.shape
    tq = t * nsb
    pairs = C_HEADS // 2
    q0 = 3 * A_W // LANES
    return pl.pallas_call(
        functools.partial(_attn_c_kernel, t=t, nsb=nsb),
        grid=(pairs, bsz, seq // tq),
        in_specs=[
            pl.BlockSpec((1, tq, LANES), lambda p, b, i: (b, i, q0 + p)),
            pl.BlockSpec((1, seq, LANES), lambda p, b, i: (b, 0, q0 + pairs + p)),
            pl.BlockSpec((1, seq, LANES), lambda p, b, i: (b, 0, q0 + 2 * pairs + p)),
            pl.BlockSpec((1, 1, 2, seq), lambda p, b, i: (b, p, 0, 0)),
        ],
        out_specs=pl.BlockSpec((1, tq, LANES), lambda p, b, i: (b, i, p)),
        out_shape=jax.ShapeDtypeStruct((bsz, seq, C_W), BF16),
        scratch_shapes=_flash_scratch(seq, t, nsb),
        compiler_params=_params("parallel", "parallel", "arbitrary"),
        name="attn_forget",
    )(qkv, qkv, qkv, fcum)


def _attn_b_kernel(q_ref, kp_ref, kc_ref, vp_ref, vc_ref, bias_ref, o_ref, lse_ref, *, dil, nb):
    n = pl.program_id(1)
    lane = lax.broadcasted_iota(jnp.int32, (BLOCK, LANES), 1)
    first_variant = jnp.minimum(n, 1)

    def residue(r):
        for hp in range(B_HEADS // 2):
            cols = slice(hp * LANES, (hp + 1) * LANES)
            kcat = jnp.concatenate([kp_ref[0, r, :, cols], kc_ref[0, r, :, cols]], axis=0)
            vcat = jnp.concatenate([vp_ref[0, r, :, cols], vc_ref[0, r, :, cols]], axis=0)
            for jb in range(nb):
                qh = _half_masks(q_ref[0, r, jb * BLOCK:(jb + 1) * BLOCK, cols] * SCALE)
                kwin = kcat[jb * BLOCK:(jb + 2) * BLOCK]
                vwin = vcat[jb * BLOCK:(jb + 2) * BLOCK]
                variant = first_variant if jb == 0 else 1
                outs, lses = [], []
                for hh in range(2):
                    s = _dot_nt(qh[hh], kwin) + bias_ref[variant, hp, hh * BLOCK:(hh + 1) * BLOCK, :]
                    m = jnp.max(s, axis=-1, keepdims=True)
                    e = jnp.exp(s - m)
                    den = jnp.sum(e, axis=-1, keepdims=True)
                    outs.append(_dot(e.astype(BF16), vwin) / den)
                    lses.append(jnp.broadcast_to(m + jnp.log(den), (BLOCK, LANES)))
                if dil == 1:
                    rows = pl.ds(jb * BLOCK, BLOCK)
                else:
                    rows = pl.ds(jb * BLOCK * dil + r, BLOCK, stride=dil)
                o_ref[0, hp, rows, :] = jnp.where(lane < HEAD_DIM, outs[0], outs[1])
                lse_ref[0, hp, rows, :] = jnp.where(lane < HEAD_DIM, lses[0], lses[1])

    unroll = min(dil, 4)
    if dil == unroll:
        for r in range(dil):
            residue(r)
    else:
        def body(i, carry):
            for j in range(unroll):
                residue(i * unroll + j)
            return carry

        lax.fori_loop(0, dil // unroll, body, 0)


def _attn_b_group(qkv_g, bias_g, g, nb):
    bsz, dil, m_len, _ = qkv_g.shape
    tb = BLOCK * nb
    cur = lambda c: pl.BlockSpec((1, dil, tb, B_W), lambda b, n: (b, 0, n, c))
    prev = lambda c: pl.BlockSpec((1, dil, BLOCK, B_W), lambda b, n: (b, 0, jnp.maximum(n * nb - 1, 0), c))
    pairs = B_HEADS // 2
    out_spec = pl.BlockSpec((1, pairs, tb * dil, LANES), lambda b, n: (b, 0, n, 0))
    out_sds = jax.ShapeDtypeStruct((bsz, pairs, m_len * dil, LANES), F32)
    return pl.pallas_call(
        functools.partial(_attn_b_kernel, dil=dil, nb=nb),
        grid=(bsz, m_len // tb),
        in_specs=[cur(0), prev(1), cur(1), prev(2), cur(2),
                  pl.BlockSpec(bias_g.shape, lambda b, n: (0, 0, 0, 0))],
        out_specs=[out_spec, out_spec],
        out_shape=[out_sds, out_sds],
        compiler_params=_params("parallel", "arbitrary"),
        name=f"attn_dilated_g{g}",
    )(qkv_g, qkv_g, qkv_g, qkv_g, qkv_g, bias_g)


def _merge_kernel(oa_ref, ob0, ob1, ob2, ls0, ls1, ls2, oc_ref, gate_ref, x_ref, mod_ref,
                  wa_ref, wb_ref, wc_ref, wo_ref, o_ref):
    parts = []
    for hp in range(B_HEADS // 2):
        l0, l1, l2 = ls0[0, hp], ls1[0, hp], ls2[0, hp]
        mx = jnp.maximum(jnp.maximum(l0, l1), l2)
        e0, e1, e2 = jnp.exp(l0 - mx), jnp.exp(l1 - mx), jnp.exp(l2 - mx)
        den = e0 + e1 + e2
        parts.append((e0 / den) * ob0[0, hp] + (e1 / den) * ob1[0, hp] + (e2 / den) * ob2[0, hp])
    ob = jnp.concatenate(parts, axis=1)
    d = D_MODEL
    merged = (gate_ref[:, 0:d] * _dot(oa_ref[...], wa_ref[...])
              + gate_ref[:, d:2 * d] * _dot(ob.astype(BF16), wb_ref[...])
              + gate_ref[:, 2 * d:3 * d] * _dot(oc_ref[...], wc_ref[...]))
    y = _dot(merged.astype(BF16), wo_ref[...])
    o_ref[...] = x_ref[...] + mod_ref[0][2:3] * y


def _merge(oa, obs, lses, oc, gf, x2d, mod3, wa, wb, wc, wo, *, seq, tm):
    rows, d = x2d.shape
    per_b = seq // tm
    row = lambda w: pl.BlockSpec((tm, w), lambda i: (i, 0))
    full = lambda a: pl.BlockSpec(a.shape, lambda i: (0, 0))
    paired = pl.BlockSpec((1, B_HEADS // 2, tm, LANES), lambda i: (i // per_b, 0, i % per_b, 0))
    return pl.pallas_call(
        _merge_kernel,
        grid=(rows // tm,),
        in_specs=[row(A_W)] + [paired] * 6 + [row(C_W), row(GATE_W), row(d),
                  pl.BlockSpec((1, 6, d), lambda i: (i // per_b, 0, 0)),
                  full(wa), full(wb), full(wc), full(wo)],
        out_specs=row(d),
        out_shape=jax.ShapeDtypeStruct((rows, d), F32),
        compiler_params=_params("parallel"),
        name="merge_outproj",
    )(oa, *obs, *lses, oc, gf, x2d, mod3, wa, wb, wc, wo)


def _ffn_kernel(x_ref, mod_ref, g_ref, wg_ref, wu_ref, wd_ref, o_ref, h_scr, acc_scr, *, nf):
    f = pl.program_id(1)

    @pl.when(f == 0)
    def _():
        m = mod_ref[0]
        h_scr[...] = _rms_mod(x_ref[...], g_ref[...], m[4:5], m[3:4]).astype(BF16)
        acc_scr[...] = jnp.zeros(acc_scr.shape, F32)

    h = h_scr[...]
    act = (_silu(_dot(h, wg_ref[...])) * _dot(h, wu_ref[...])).astype(BF16)
    acc_scr[...] += _dot(act, wd_ref[...])

    @pl.when(f == nf - 1)
    def _():
        o_ref[...] = x_ref[...] + mod_ref[0][5:6] * acc_scr[...]


def _ffn(x2d, mod3, g, wg, wu, wd, *, seq, tm, tf):
    rows, d = x2d.shape
    dff = wg.shape[1]
    nf = dff // tf
    per_b = seq // tm
    return pl.pallas_call(
        functools.partial(_ffn_kernel, nf=nf),
        grid=(rows // tm, nf),
        in_specs=[
            pl.BlockSpec((tm, d), lambda i, f: (i, 0)),
            pl.BlockSpec((1, 6, d), lambda i, f: (i // per_b, 0, 0)),
            pl.BlockSpec((1, d), lambda i, f: (0, 0)),
            pl.BlockSpec((d, tf), lambda i, f: (0, f)),
            pl.BlockSpec((d, tf), lambda i, f: (0, f)),
            pl.BlockSpec((tf, d), lambda i, f: (f, 0)),
        ],
        out_specs=pl.BlockSpec((tm, d), lambda i, f: (i, 0)),
        out_shape=jax.ShapeDtypeStruct((rows, d), F32),
        scratch_shapes=[pltpu.VMEM((tm, d), BF16), pltpu.VMEM((tm, d), F32)],
        compiler_params=_params("parallel", "arbitrary"),
        name="ffn_dense",
    )(x2d, mod3, g, wg, wu, wd)


def _route_kernel(x_ref, mod_ref, g_ref, wr_ref, tri_ref, h_ref, ridx_ref, rw_ref, cnt_ref, carry_scr):
    i = pl.program_id(0)

    @pl.when(i == 0)
    def _():
        carry_scr[...] = jnp.zeros(carry_scr.shape, F32)

    m = mod_ref[0]
    h = _rms_mod(x_ref[...], g_ref[...], m[4:5], m[3:4])
    h_ref[...] = h
    logits = jnp.dot(h, wr_ref[...], preferred_element_type=F32, precision=lax.Precision.HIGHEST)
    idx = lax.broadcasted_iota(jnp.int32, logits.shape, 1)
    n = logits.shape[1]
    m1 = jnp.max(logits, axis=-1, keepdims=True)
    i1 = jnp.min(jnp.where(logits == m1, idx, n), axis=-1, keepdims=True)
    first = idx == i1
    rest = jnp.where(first, -jnp.inf, logits)
    m2 = jnp.max(rest, axis=-1, keepdims=True)
    i2 = jnp.min(jnp.where(rest == m2, idx, n), axis=-1, keepdims=True)
    second = idx == i2
    e = jnp.exp(m2 - m1)
    den = 1.0 + e
    onehot = jnp.where(first | second, 1.0, 0.0)
    before = _dot(tri_ref[...], onehot.astype(BF16)) - onehot + carry_scr[...]
    rank1 = jnp.sum(jnp.where(first, before, 0.0), axis=-1, keepdims=True).astype(jnp.int32)
    rank2 = jnp.sum(jnp.where(second, before, 0.0), axis=-1, keepdims=True).astype(jnp.int32)
    carry_scr[...] += jnp.sum(onehot, axis=0, keepdims=True)
    ridx_ref[...] = jnp.where(idx == 0, i1, jnp.where(idx == 1, i2, jnp.where(idx == 2, rank1,
                              jnp.where(idx == 3, rank2, 0))))
    rw_ref[...] = jnp.where(idx == 0, 1.0 / den, jnp.where(idx == 1, e / den, 0.0))
    cnt_ref[...] = carry_scr[...]


def _route(x2d, mod3, g, wr, *, seq, tm):
    rows, d = x2d.shape
    ne = wr.shape[1]
    per_b = seq // tm
    tri = (jnp.arange(tm)[:, None] >= jnp.arange(tm)[None, :]).astype(BF16)
    row = lambda w: pl.BlockSpec((tm, w), lambda i: (i, 0))
    return pl.pallas_call(
        _route_kernel,
        grid=(rows // tm,),
        in_specs=[row(d), pl.BlockSpec((1, 6, d), lambda i: (i // per_b, 0, 0)),
                  pl.BlockSpec((1, d), lambda i: (0, 0)), pl.BlockSpec((d, ne), lambda i: (0, 0)),
                  pl.BlockSpec((tm, tm), lambda i: (0, 0))],
        out_specs=[row(d), row(ne), row(ne), pl.BlockSpec((1, ne), lambda i: (0, 0))],
        out_shape=[jax.ShapeDtypeStruct((rows, d), F32), jax.ShapeDtypeStruct((rows, ne), jnp.int32),
                   jax.ShapeDtypeStruct((rows, ne), F32), jax.ShapeDtypeStruct((1, ne), F32)],
        scratch_shapes=[pltpu.VMEM((1, ne), F32)],
        compiler_params=_params("arbitrary"),
        name="moe_route",
    )(x2d, mod3, g, wr, tri)


def _gather_rows(tok_of, n, src_hbm, dst, sem):
    def issue(r, carry):
        pltpu.make_async_copy(src_hbm.at[pl.ds(tok_of(r), 1)], dst.at[pl.ds(r, 1)], sem).start()
        return carry

    lax.fori_loop(0, n, issue, 0, unroll=8)


def _gather_wait(n, src_hbm, dst, sem):
    pltpu.make_async_copy(src_hbm.at[pl.ds(0, n)], dst, sem).wait()


def _expert_kernel(te_ref, src_ref, nu_ref, h_hbm, wg_ref, wu_ref, wd_ref, y_ref,
                   xbuf, xb_scr, acc_scr, sem, *, nf, r, nt):
    i = pl.program_id(0)
    f = pl.program_id(1)
    active = i < nu_ref[0]
    slot = i % 2

    def gather(tile, s):
        _gather_rows(lambda j: src_ref[tile * r + j] >> 1, r, h_hbm, xbuf.at[s], sem.at[s])

    @pl.when((i == 0) & (f == 0))
    def _():
        gather(0, 0)

    @pl.when(active & (f == 0))
    def _():
        _gather_wait(r, h_hbm, xbuf.at[slot], sem.at[slot])
        xb_scr[...] = xbuf[slot].astype(BF16)
        acc_scr[...] = jnp.zeros(acc_scr.shape, F32)

    @pl.when(active)
    def _():
        chunk = r // nf
        base = jnp.minimum(i + 1, nt - 1) * r + f * chunk
        for j in range(chunk):
            pltpu.make_async_copy(h_hbm.at[pl.ds(src_ref[base + j] >> 1, 1)],
                                  xbuf.at[1 - slot, pl.ds(f * chunk + j, 1)], sem.at[1 - slot]).start()
        xb = xb_scr[...]
        act = (_silu(_dot(xb, wg_ref[0])) * _dot(xb, wu_ref[0])).astype(BF16)
        acc_scr[...] += _dot(act, wd_ref[0])

    @pl.when((i == nu_ref[0] - 1) & (f == nf - 1))
    def _():
        _gather_wait(r, h_hbm, xbuf.at[1 - slot], sem.at[1 - slot])

    @pl.when(f == nf - 1)
    def _():
        y_ref[...] = jnp.where(active, acc_scr[...], 0.0)


def _experts(h2, tile_expert, src_pair, n_used, wg, wu, wd, *, r, tf):
    ne, d, dff = wg.shape
    nf = dff // tf
    p_rows = src_pair.shape[0]
    nt = p_rows // r
    fsel = lambda i, f, nu: jnp.where(i < nu[0], f, nf - 1)
    grid_spec = pltpu.PrefetchScalarGridSpec(
        num_scalar_prefetch=3,
        grid=(nt, nf),
        in_specs=[
            pl.BlockSpec(memory_space=pl.ANY),
            pl.BlockSpec((1, d, tf), lambda i, f, te, sp, nu: (te[i], 0, fsel(i, f, nu))),
            pl.BlockSpec((1, d, tf), lambda i, f, te, sp, nu: (te[i], 0, fsel(i, f, nu))),
            pl.BlockSpec((1, tf, d), lambda i, f, te, sp, nu: (te[i], fsel(i, f, nu), 0)),
        ],
        out_specs=pl.BlockSpec((r, d), lambda i, f, te, sp, nu: (i, 0)),
        scratch_shapes=[pltpu.VMEM((2, r, d), F32), pltpu.VMEM((r, d), BF16), pltpu.VMEM((r, d), F32),
                        pltpu.SemaphoreType.DMA((2,))],
    )
    return pl.pallas_call(
        functools.partial(_expert_kernel, nf=nf, r=r, nt=nt),
        grid_spec=grid_spec,
        out_shape=jax.ShapeDtypeStruct((p_rows, d), F32),
        compiler_params=_params("arbitrary", "arbitrary"),
        name="moe_experts",
    )(tile_expert, src_pair, n_used, h2, wg, wu, wd)


def _combine_kernel(pos_ref, y_hbm, x_ref, rw_ref, mod_ref, fg_ref, o_ref, ybuf, sem, *, tm, final):
    i = pl.program_id(0)
    n = pl.num_programs(0)
    slot = i % 2

    def gather(tile, s):
        for k in range(2):
            _gather_rows(lambda j: pos_ref[(tile * tm + j) * 2 + k], tm, y_hbm, ybuf.at[s, k], sem.at[s])

    @pl.when(i == 0)
    def _():
        gather(0, 0)

    for k in range(2):
        _gather_wait(tm, y_hbm, ybuf.at[slot, k], sem.at[slot])

    @pl.when(i + 1 < n)
    def _():
        gather(i + 1, 1 - slot)

    w = rw_ref[...]
    f = w[:, 0:1] * ybuf[slot, 0] + w[:, 1:2] * ybuf[slot, 1]
    x = x_ref[...] + mod_ref[0][5:6] * f
    if final:
        x = (x * lax.rsqrt(jnp.mean(x * x, axis=-1, keepdims=True) + EPS)) * fg_ref[...]
    o_ref[...] = x


def _combine(pos, y, x2d, rw, mod3, final_g, *, seq, tm, final):
    rows, d = x2d.shape
    per_b = seq // tm
    grid_spec = pltpu.PrefetchScalarGridSpec(
        num_scalar_prefetch=1,
        grid=(rows // tm,),
        in_specs=[
            pl.BlockSpec(memory_space=pl.ANY),
            pl.BlockSpec((tm, d), lambda i, p: (i, 0)),
            pl.BlockSpec((tm, rw.shape[1]), lambda i, p: (i, 0)),
            pl.BlockSpec((1, 6, d), lambda i, p: (i // per_b, 0, 0)),
            pl.BlockSpec((1, d), lambda i, p: (0, 0)),
        ],
        out_specs=pl.BlockSpec((tm, d), lambda i, p: (i, 0)),
        scratch_shapes=[pltpu.VMEM((2, 2, tm, d), F32), pltpu.SemaphoreType.DMA((2,))],
    )
    return pl.pallas_call(
        functools.partial(_combine_kernel, tm=tm, final=final),
        grid_spec=grid_spec,
        out_shape=jax.ShapeDtypeStruct((rows, d), F32),
        compiler_params=_params("arbitrary"),
        name="moe_combine",
    )(pos, y, x2d, rw, mod3, final_g)


def _moe(x2d, mod3, g, wr, wg, wu, wd, final_g, *, seq, final):
    rows, d = x2d.shape
    ne = wr.shape[1]
    r = 512
    h2, ridx, rw, cnt = _route(x2d, mod3, g, wr, seq=seq, tm=1024)
    counts = cnt[0].astype(jnp.int32)
    tiles_e = (counts + r - 1) // r
    tile_end = jnp.cumsum(tiles_e)
    start = (tile_end - tiles_e) * r
    pos = jnp.take(start, ridx[:, 0:2], axis=0) + ridx[:, 2:4]
    p_rows = 2 * rows + ne * r
    nt = p_rows // r
    pair_id = jnp.arange(2 * rows, dtype=jnp.int32)
    src_pair = jnp.zeros((p_rows,), jnp.int32).at[pos.reshape(-1)].set(pair_id)
    n_used = tile_end[-1:]
    tile_idx = jnp.minimum(jnp.arange(nt), n_used[0] - 1)
    tile_expert = jnp.sum(tile_idx[:, None] >= tile_end[None, :], axis=1)
    y = _experts(h2, tile_expert.astype(jnp.int32), src_pair, n_used.astype(jnp.int32), wg, wu, wd,
                 r=r, tf=896)
    return _combine(pos.reshape(-1).astype(jnp.int32), y, x2d, rw, mod3, final_g, seq=seq, tm=256,
                    final=final)


def _final_norm_kernel(x_ref, g_ref, o_ref):
    x = x_ref[...]
    o_ref[...] = (x * lax.rsqrt(jnp.mean(x * x, axis=-1, keepdims=True) + EPS)) * g_ref[...]


def _final_norm(x2d, g, *, tm):
    rows, d = x2d.shape
    return pl.pallas_call(
        _final_norm_kernel,
        grid=(rows // tm,),
        in_specs=[pl.BlockSpec((tm, d), lambda i: (i, 0)), pl.BlockSpec((1, d), lambda i: (0, 0))],
        out_specs=pl.BlockSpec((tm, d), lambda i: (i, 0)),
        out_shape=jax.ShapeDtypeStruct((rows, d), F32),
        compiler_params=_params("parallel"),
        name="final_norm",
    )(x2d, g)


def _t5_bucket(dist):
    n = jnp.maximum(dist, 0)
    max_exact = N_BUCKETS // 2
    nf = jnp.maximum(n, 1).astype(F32)
    large = max_exact + (jnp.log(nf / max_exact) / math.log(REL_MAX_DIST / max_exact)
                         * (N_BUCKETS - max_exact)).astype(jnp.int32)
    large = jnp.minimum(large, N_BUCKETS - 1)
    return jnp.where(n < max_exact, n, large)


def _bias_tiles_a(rel_bias, seq, t):
    nq = seq // t
    ncol = 2 * A_HEADS
    tab = rel_bias[:, :ncol][_t5_bucket(jnp.arange(seq))].astype(F32).T
    vneg = jnp.full((ncol, t), NEG, F32)
    v = jnp.concatenate([vneg, tab], axis=1)
    u = jnp.concatenate([v[:, 1:seq + 1][:, ::-1], vneg[:, :1], v[:, seq + 1:seq + t][:, ::-1]], axis=1)

    def toeplitz_kernel(u_ref, o_ref):
        x = jnp.broadcast_to(u_ref[0], (t, seq + t))
        r = pltpu.roll(x, 0, 1, stride=1, stride_axis=0)
        for delta in range(nq):
            c0 = (nq - 1 - delta) * t
            o_ref[0, delta] = r[:, c0:c0 + t]

    return pl.pallas_call(
        toeplitz_kernel,
        grid=(ncol,),
        in_specs=[pl.BlockSpec((1, 1, seq + t), lambda c: (c, 0, 0))],
        out_specs=pl.BlockSpec((1, nq, t, t), lambda c: (c // 2, 0, c % 2, 0)),
        out_shape=jax.ShapeDtypeStruct((A_HEADS, nq, 2 * t, t), F32),
        compiler_params=_params("parallel"),
        name="bias_tiles_diff",
    )(u.reshape(ncol, 1, seq + t))


def _bias_tiles_b(rel_bias):
    ng = len(B_GROUPS)
    period = 3 * BLOCK
    rows = []
    for g, (win, dil) in enumerate(B_GROUPS):
        n_back = win // dil
        tab = rel_bias[:, 2 * A_HEADS + g * B_HEADS:2 * A_HEADS + (g + 1) * B_HEADS]
        vals = tab[_t5_bucket(jnp.arange(n_back, -1, -1) * dil)].astype(F32).T
        rows.append(jnp.concatenate([vals, jnp.full((B_HEADS, period - n_back - 1), NEG, F32)], axis=1))
    u = jnp.concatenate(rows, axis=0)

    def toeplitz_kernel(u_ref, o_ref):
        x = jnp.broadcast_to(u_ref[0], (BLOCK, period))
        r = pltpu.roll(x, 0, 1, stride=1, stride_axis=0)[:, :2 * BLOCK]
        col = lax.broadcasted_iota(jnp.int32, r.shape, 1)
        o_ref[0, 0, 0] = jnp.where(col >= BLOCK, r, NEG)
        o_ref[0, 1, 0] = r

    return pl.pallas_call(
        toeplitz_kernel,
        grid=(ng * B_HEADS,),
        in_specs=[pl.BlockSpec((1, 1, period), lambda c: (c, 0, 0))],
        out_specs=pl.BlockSpec((1, 2, 1, BLOCK, 2 * BLOCK),
                               lambda c: (c // B_HEADS, 0, (c % B_HEADS) // 2, c % 2, 0)),
        out_shape=jax.ShapeDtypeStruct((ng, 2, B_HEADS // 2, 2 * BLOCK, 2 * BLOCK), F32),
        compiler_params=_params("parallel"),
        name="bias_tiles_dilated",
    )(u.reshape(ng * B_HEADS, 1, period))


def kernel(x, c, norm_mix_g, norm_ffn_g, w_mod, b_mod, w_in, b_forget, lam_q1, lam_k1, lam_q2, lam_k2,
           subln_g, rel_bias, w_br_a, w_br_b, w_br_c, w_out, w_ff_gate, w_ff_up, w_ff_down, w_router,
           w_exp_gate, w_exp_up, w_exp_down, final_norm_g):
    bsz, seq, d = x.shape
    depth = w_mod.shape[0]
    rows = bsz * seq
    t_attn = 256
    x2d = x.reshape(rows, d)

    mod = _modulation(c, w_mod, b_mod)
    bias_a = _bias_tiles_a(rel_bias, seq, t_attn)
    bias_b = _bias_tiles_b(rel_bias)
    w_qkv_all, w_gf_all = _prep_w_in(w_in)

    final_g = final_norm_g.reshape(1, d)
    fused_final = False
    for l in range(depth):
        lam_init = 0.8 - 0.6 * math.exp(-0.3 * l)
        mod3 = mod[l].reshape(bsz, 6, d)
        w_qkv, w_gf = w_qkv_all[l], w_gf_all[l]

        g_mix = norm_mix_g[l].reshape(1, d)
        qkv_ac, qkv_b0, qkv_b1, qkv_b2 = _qkv_proj(x2d, mod3, g_mix, w_qkv, bsz=bsz, seq=seq, tm=512)
        gf = _inproj(x2d, mod3, g_mix, w_gf, seq=seq, tm=1024, tn=640, out_dtype=F32,
                     n_sigmoid=GATE_W)
        qkv_ac = qkv_ac.reshape(bsz, seq, -1)

        b_f8 = jnp.pad(b_forget[l], (0, 8 - C_HEADS)).reshape(8, 1)
        fcum = _forget_cumsum(gf.reshape(bsz, seq, GATE_W + LANES), b_f8, col_block=GATE_W // LANES)
        fcum = fcum[:, :C_HEADS].reshape(bsz, C_HEADS // 2, 2, seq)

        oa = _attn_a(qkv_ac, bias_a, lam_q1[l].reshape(1, -1), lam_k1[l].reshape(1, -1),
                     lam_q2[l].reshape(1, -1), lam_k2[l].reshape(1, -1), subln_g[l].reshape(1, -1),
                     lam_init=lam_init, t=t_attn, nsb=4)
        oc = _attn_c(qkv_ac, fcum, t=t_attn, nsb=4)
        obs, lses = [], []
        groups = (qkv_b0.reshape(bsz, 1, seq, 3 * B_W), qkv_b1, qkv_b2)
        for g, (qkv_g, nb) in enumerate(zip(groups, (4, 1, 1))):
            o_g, lse_g = _attn_b_group(qkv_g, bias_b[g], g, nb)
            obs.append(o_g)
            lses.append(lse_g)

        x2d = _merge(oa.reshape(rows, A_W), obs, lses, oc.reshape(rows, C_W), gf, x2d, mod3,
                     w_br_a[l].astype(BF16), w_br_b[l].astype(BF16), w_br_c[l].astype(BF16),
                     w_out[l].astype(BF16), seq=seq, tm=512)

        g_ffn = norm_ffn_g[l].reshape(1, d)
        if l % 2 == 0:
            j = l // 2
            x2d = _ffn(x2d, mod3, g_ffn, w_ff_gate[j].astype(BF16), w_ff_up[j].astype(BF16),
                       w_ff_down[j].astype(BF16), seq=seq, tm=512, tf=1408)
        else:
            j = l // 2
            fused_final = l == depth - 1
            x2d = _moe(x2d, mod3, g_ffn, w_router[j], w_exp_gate[j].astype(BF16),
                       w_exp_up[j].astype(BF16), w_exp_down[j].astype(BF16), final_g,
                       seq=seq, final=fused_final)

    if not fused_final:
        x2d = _final_norm(x2d, final_g, tm=1024)
    return x2d.reshape(bsz, seq, d)
```

```python
import functools
import math

import jax
import jax.numpy as jnp
from jax import lax
from jax.experimental import pallas as pl
from jax.experimental.pallas import tpu as pltpu

F32 = jnp.float32
BF16 = jnp.bfloat16

D_MODEL = 1024
HEAD_DIM = 64
LANES = 128
A_HEADS = 4
A_W = A_HEADS * 2 * HEAD_DIM
B_GROUPS = ((128, 1), (512, 4), (2048, 16))
B_HEADS = 6
B_W = B_HEADS * HEAD_DIM
B_QW = len(B_GROUPS) * B_W
C_HEADS = 6
C_W = C_HEADS * HEAD_DIM
N_BRANCH = 3
BLOCK = 128
N_BUCKETS = 32
REL_MAX_DIST = 2048
N_EXPERTS = 8
EPS = 1e-6
QKV_W = 3 * A_W + 3 * B_QW + 3 * C_W
GATE_W = N_BRANCH * D_MODEL
NEG = -1e30
SCALE = HEAD_DIM ** -0.5
VMEM_LIMIT = 56 * 1024 * 1024


def _params(*sem):
    return pltpu.CompilerParams(dimension_semantics=sem, vmem_limit_bytes=VMEM_LIMIT)


def _resident(a):
    return pl.BlockSpec(a.shape, lambda *_: (0, 0), pipeline_mode=pl.Buffered(1))


def _rms_mod(x, g, sc, sh):
    y = x * lax.rsqrt(jnp.mean(x * x, axis=-1, keepdims=True) + EPS)
    return (y * g) * (1.0 + sc) + sh


def _dot(a, b):
    return jnp.dot(a, b, preferred_element_type=F32)


def _dot_nt(a, b):
    return lax.dot_general(a, b, (((1,), (1,)), ((), ())), preferred_element_type=F32)


def _silu(a):
    return a * jax.nn.sigmoid(a)


def _mod_kernel(c_ref, w_ref, b_ref, o_ref):
    a = _silu(c_ref[...]).astype(BF16)
    o_ref[0] = _dot(a, w_ref[0].astype(BF16)) + b_ref[0]


def _modulation(c, w_mod, b_mod):
    depth, d, n = w_mod.shape
    bsz = c.shape[0]
    tn = 1536
    return pl.pallas_call(
        _mod_kernel,
        grid=(depth, n // tn),
        in_specs=[
            pl.BlockSpec((bsz, d), lambda l, j: (0, 0)),
            pl.BlockSpec((1, d, tn), lambda l, j: (l, 0, j)),
            pl.BlockSpec((1, 1, tn), lambda l, j: (l, 0, j)),
        ],
        out_specs=pl.BlockSpec((1, bsz, tn), lambda l, j: (l, 0, j)),
        out_shape=jax.ShapeDtypeStruct((depth, bsz, n), F32),
        compiler_params=_params("parallel", "parallel"),
        name="modulation",
    )(c, w_mod, b_mod.reshape(depth, 1, n))


_A_BLOCKS = 3 * A_W // B_W
_B_BLOCKS = 3 * B_QW // B_W
_QKV_SRC_BLOCKS = (list(range(_A_BLOCKS))
                   + list(range(_A_BLOCKS + _B_BLOCKS, QKV_W // B_W))
                   + [_A_BLOCKS + s * len(B_GROUPS) + g for g in range(len(B_GROUPS)) for s in range(3)])


def _prep_w_in_kernel(*refs):
    n = len(_QKV_SRC_BLOCKS)
    piece_refs, x_ref, y_ref, qkv_ref, gf_ref = refs[:n], refs[n], refs[n + 1], refs[n + 2], refs[n + 3]
    for j, ref in enumerate(piece_refs):
        qkv_ref[0, :, j * B_W:(j + 1) * B_W] = ref[0].astype(BF16)
    x = x_ref[0]
    rolled = pltpu.roll(x, GATE_W - C_HEADS, 1)
    tail = pltpu.roll(y_ref[0], LANES - C_HEADS, 1)
    lane = lax.broadcasted_iota(jnp.int32, tail.shape, 1)
    gf_ref[0, :, :GATE_W - LANES] = rolled[:, :GATE_W - LANES].astype(BF16)
    gf_ref[0, :, GATE_W - LANES:GATE_W] = jnp.where(lane < LANES - C_HEADS, rolled[:, GATE_W - LANES:],
                                                     tail).astype(BF16)
    gf_ref[0, :, GATE_W:] = jnp.where(lane < C_HEADS, x[:, :LANES], 0.0).astype(BF16)


def _prep_w_in(w_in):
    depth, d, _ = w_in.shape
    tr = 256
    piece = lambda c: pl.BlockSpec((1, tr, B_W), lambda l, i: (l, i, c))
    return pl.pallas_call(
        _prep_w_in_kernel,
        grid=(depth, d // tr),
        in_specs=[piece(c) for c in _QKV_SRC_BLOCKS]
        + [pl.BlockSpec((1, tr, GATE_W), lambda l, i: (l, i, QKV_W // GATE_W)),
           pl.BlockSpec((1, tr, LANES), lambda l, i: (l, i, (QKV_W + GATE_W) // LANES))],
        out_specs=[pl.BlockSpec((1, tr, QKV_W), lambda l, i: (l, i, 0)),
                   pl.BlockSpec((1, tr, GATE_W + LANES), lambda l, i: (l, i, 0))],
        out_shape=[jax.ShapeDtypeStruct((depth, d, QKV_W), BF16),
                   jax.ShapeDtypeStruct((depth, d, GATE_W + LANES), BF16)],
        compiler_params=_params("parallel", "parallel"),
        name="prep_w_in",
    )(*([w_in] * (len(_QKV_SRC_BLOCKS) + 2)))


def _qkv_kernel(x_ref, mod_ref, g_ref, w_ref, wf_ref, ac_ref, b0_ref, b1_ref, b2_ref, f_ref, h_scr, *, tm):
    m = mod_ref[0]
    h = _rms_mod(x_ref[...], g_ref[...], m[1:2], m[0:1])
    nc = h_scr.shape[0]
    for c in range(nc):
        h_scr[c] = h[:, c * LANES:(c + 1) * LANES]
    hb = h.astype(BF16)
    n_ac = ac_ref.shape[1]
    n_b = b0_ref.shape[1]
    f_ref[...] = _dot(hb, wf_ref[...])
    ac_ref[...] = _dot(hb, w_ref[:, 0:n_ac]).astype(BF16)
    b0_ref[...] = _dot(hb, w_ref[:, n_ac:n_ac + n_b]).astype(BF16)
    for gi, ref in ((1, b1_ref), (2, b2_ref)):
        dil = B_GROUPS[gi][1]
        per = tm // dil
        hp = jnp.concatenate(
            [jnp.concatenate([h_scr[c, pl.ds(r, per, stride=dil), :] for c in range(nc)], axis=1).astype(BF16)
             for r in range(dil)], axis=0)
        y = _dot(hp, w_ref[:, n_ac + gi * n_b:n_ac + (gi + 1) * n_b]).astype(BF16)
        for r in range(dil):
            ref[0, r] = y[r * per:(r + 1) * per]


def _qkv_proj(x2d, mod3, g, w, w_f, *, bsz, seq, tm):
    rows, d = x2d.shape
    per_b = seq // tm
    n_b = 3 * B_W
    n_ac = w.shape[1] - 3 * n_b
    dil1, dil2 = B_GROUPS[1][1], B_GROUPS[2][1]
    strided = lambda dil: pl.BlockSpec((1, dil, tm // dil, n_b), lambda i: (i // per_b, 0, i % per_b, 0))
    return pl.pallas_call(
        functools.partial(_qkv_kernel, tm=tm),
        grid=(rows // tm,),
        in_specs=[
            pl.BlockSpec((tm, d), lambda i: (i, 0)),
            pl.BlockSpec((1, 6, d), lambda i: (i // per_b, 0, 0)),
            pl.BlockSpec((1, d), lambda i: (0, 0)),
            _resident(w), _resident(w_f),
        ],
        out_specs=[pl.BlockSpec((tm, n_ac), lambda i: (i, 0)), pl.BlockSpec((tm, n_b), lambda i: (i, 0)),
                   strided(dil1), strided(dil2), pl.BlockSpec((tm, LANES), lambda i: (i, 0))],
        out_shape=[jax.ShapeDtypeStruct((rows, n_ac), BF16), jax.ShapeDtypeStruct((rows, n_b), BF16),
                   jax.ShapeDtypeStruct((bsz, dil1, seq // dil1, n_b), BF16),
                   jax.ShapeDtypeStruct((bsz, dil2, seq // dil2, n_b), BF16),
                   jax.ShapeDtypeStruct((rows, LANES), F32)],
        scratch_shapes=[pltpu.VMEM((d // LANES, tm, LANES), F32)],
        compiler_params=_params("parallel"),
        name="qkv_proj",
    )(x2d, mod3, g, w, w_f)


def _fcum_kernel(f_ref, b_ref, o_ref):
    z = f_ref[0].T[:8] + b_ref[...]
    x = jnp.minimum(z, 0.0) - jnp.log1p(jnp.exp(-jnp.abs(z)))
    s = x.shape[1]
    lane = lax.broadcasted_iota(jnp.int32, x.shape, 1)
    k = 1
    while k < s:
        x = x + jnp.where(lane >= k, pltpu.roll(x, k, 1), 0.0)
        k *= 2
    o_ref[0] = x


def _forget_cumsum(gf, b_f8, *, col_block):
    bsz, seq, _ = gf.shape
    return pl.pallas_call(
        _fcum_kernel,
        grid=(bsz,),
        in_specs=[
            pl.BlockSpec((1, seq, LANES), lambda b: (b, 0, col_block)),
            pl.BlockSpec((8, 1), lambda b: (0, 0)),
        ],
        out_specs=pl.BlockSpec((1, 8, seq), lambda b: (b, 0, 0)),
        out_shape=jax.ShapeDtypeStruct((bsz, 8, seq), F32),
        compiler_params=_params("parallel"),
        name="forget_cumsum",
    )(gf, b_f8)


def _half_masks(q):
    lane = lax.broadcasted_iota(jnp.int32, q.shape, 1)
    zero = jnp.zeros_like(q)
    return jnp.where(lane < HEAD_DIM, q, zero), jnp.where(lane >= HEAD_DIM, q, zero)


def _flash_init(first, v_ref, vext_scr, m_scr, acc_scr):
    @pl.when(first)
    def _():
        vext_scr[:, :LANES] = v_ref[0]
        vext_scr[:, LANES:] = jnp.ones((vext_scr.shape[0], LANES), BF16)

    m_scr[...] = jnp.full(m_scr.shape, -jnp.inf, F32)
    acc_scr[...] = jnp.zeros(acc_scr.shape, F32)


def _lane_tile(a, n):
    return a if n == 1 else jnp.concatenate([a] * n, axis=1)


def _flash_update(s, vext, rows, m_scr, acc_scr):
    m_prev = m_scr[rows]
    m_new = jnp.maximum(m_prev, jnp.max(s, axis=-1, keepdims=True))
    alpha = jnp.exp(m_prev - m_new)
    p = jnp.exp(s - _lane_tile(m_new, s.shape[1] // LANES))
    acc_scr[rows] = _lane_tile(alpha, 2) * acc_scr[rows] + _dot(p.astype(BF16), vext)
    m_scr[rows] = m_new


def _flash_result(acc_scr):
    acc = acc_scr[...]
    return acc[:, :LANES] / acc[:, LANES:]


def _attn_a_kernel(q_ref, k_ref, v_ref, bias_ref, lq1, lk1, lq2, lk2, sg_ref, o_ref,
                   vext_scr, m_scr, acc_scr, *, t, nsb, lam_init):
    qi = pl.program_id(2)
    _flash_init(qi == 0, v_ref, vext_scr, m_scr, acc_scr)
    qh = [_half_masks(q_ref[0, sb * t:(sb + 1) * t, :] * SCALE) for sb in range(nsb)]

    def step(kb, nkb, plan):
        off = pl.multiple_of(kb * t, t)
        kblk = k_ref[0, pl.ds(off, nkb * t), :]
        vext = vext_scr[pl.ds(off, nkb * t), :]
        for sb, deltas in plan:
            for hh in range(2):
                bias = [bias_ref[0, d, hh * t:(hh + 1) * t, :] for d in deltas]
                s = _dot_nt(qh[sb][hh], kblk) + (bias[0] if nkb == 1 else jnp.concatenate(bias, axis=1))
                _flash_update(s, vext, pl.ds((sb * 2 + hh) * t, t), m_scr, acc_scr)

    def body(kb2, carry):
        first = [qi * nsb + sb - 2 * kb2 for sb in range(nsb)]
        step(2 * kb2, 2, [(sb, (first[sb], first[sb] - 1)) for sb in range(nsb)])
        return carry

    lax.fori_loop(0, qi * (nsb // 2), body, 0)
    for j in range(0, nsb, 2):
        step(qi * nsb + j, 1, [(j, (0,))])
        step(qi * nsb + j, 2, [(sb, (sb - j, sb - j - 1)) for sb in range(j + 1, nsb)])

    o = _flash_result(acc_scr)
    lam = (jnp.exp(jnp.sum(lq1[...] * lk1[...], axis=-1, keepdims=True))
           - jnp.exp(jnp.sum(lq2[...] * lk2[...], axis=-1, keepdims=True)) + lam_init)
    for sb in range(nsb):
        d = o[2 * sb * t:(2 * sb + 1) * t] - lam * o[(2 * sb + 1) * t:(2 * sb + 2) * t]
        y = d * lax.rsqrt(jnp.mean(d * d, axis=-1, keepdims=True) + EPS)
        o_ref[0, sb * t:(sb + 1) * t, :] = ((y * sg_ref[...]) * (1.0 - lam_init)).astype(o_ref.dtype)


def _flash_scratch(seq, t, nsb):
    return [pltpu.VMEM((seq, 2 * LANES), BF16), pltpu.VMEM((2 * nsb * t, LANES), F32),
            pltpu.VMEM((2 * nsb * t, 2 * LANES), F32)]


def _attn_a(qkv, bias_a, lq1, lk1, lq2, lk2, subln_g, *, lam_init, t, nsb):
    bsz, seq, _ = qkv.shape
    tq = t * nsb
    vec = lambda n: pl.BlockSpec((1, n), lambda h, b, i: (0, 0))
    return pl.pallas_call(
        functools.partial(_attn_a_kernel, t=t, nsb=nsb, lam_init=lam_init),
        grid=(A_HEADS, bsz, seq // tq),
        in_specs=[
            pl.BlockSpec((1, tq, LANES), lambda h, b, i: (b, i, h)),
            pl.BlockSpec((1, seq, LANES), lambda h, b, i: (b, 0, A_HEADS + h)),
            pl.BlockSpec((1, seq, LANES), lambda h, b, i: (b, 0, 2 * A_HEADS + h)),
            pl.BlockSpec((1, seq // t, 2 * t, t), lambda h, b, i: (h, 0, 0, 0)),
            vec(HEAD_DIM), vec(HEAD_DIM), vec(HEAD_DIM), vec(HEAD_DIM), vec(LANES),
        ],
        out_specs=pl.BlockSpec((1, tq, LANES), lambda h, b, i: (b, i, h)),
        out_shape=jax.ShapeDtypeStruct((bsz, seq, A_W), BF16),
        scratch_shapes=_flash_scratch(seq, t, nsb),
        compiler_params=_params("parallel", "parallel", "arbitrary"),
        name="attn_diff",
    )(qkv, qkv, qkv, bias_a, lq1, lk1, lq2, lk2, subln_g)


def _attn_c_kernel(q_ref, k_ref, v_ref, f_ref, o_ref, vext_scr, m_scr, acc_scr, *, t, nsb):
    qi = pl.program_id(2)
    _flash_init(qi == 0, v_ref, vext_scr, m_scr, acc_scr)
    qh = [_half_masks(q_ref[0, sb * t:(sb + 1) * t, :] * SCALE) for sb in range(nsb)]
    q_off = pl.multiple_of(qi * (t * nsb), t * nsb)
    f_anchor = f_ref[0, 0, :, pl.ds(q_off, LANES)][:, :1]
    def causal(nkb):
        r = lax.broadcasted_iota(jnp.int32, (t, nkb * t), 0)
        c = lax.broadcasted_iota(jnp.int32, (t, nkb * t), 1)
        return r + (nkb - 1) * t >= c

    def step(kb, nkb, sbs, diag_sb):
        off = pl.multiple_of(kb * t, t)
        kblk = k_ref[0, pl.ds(off, nkb * t), :]
        vext = vext_scr[pl.ds(off, nkb * t), :]
        dec = [f_anchor[hh:hh + 1] - f_ref[0, 0, hh:hh + 1, pl.ds(off, nkb * t)] for hh in range(2)]
        for sb in sbs:
            for hh in range(2):
                s = _dot_nt(qh[sb][hh], kblk) + dec[hh]
                if sb == diag_sb:
                    s = jnp.where(causal(nkb), s, NEG)
                _flash_update(s, vext, pl.ds((sb * 2 + hh) * t, t), m_scr, acc_scr)

    def body(kb2, carry):
        step(2 * kb2, 2, range(nsb), None)
        return carry

    lax.fori_loop(0, qi * (nsb // 2), body, 0)
    for j in range(0, nsb, 2):
        step(qi * nsb + j, 1, [j], j)
        step(qi * nsb + j, 2, range(j + 1, nsb), j + 1)

    o = _flash_result(acc_scr)
    lane = lax.broadcasted_iota(jnp.int32, (t, LANES), 1)
    for sb in range(nsb):
        pair = jnp.where(lane < HEAD_DIM, o[2 * sb * t:(2 * sb + 1) * t], o[(2 * sb + 1) * t:(2 * sb + 2) * t])
        o_ref[0, sb * t:(sb + 1) * t, :] = pair.astype(o_ref.dtype)


def _attn_c(qkv, fcum, *, t, nsb):
    bsz, seq, _ = qkv.shape
    tq = t * nsb
    pairs = C_HEADS // 2
    q0 = 3 * A_W // LANES
    return pl.pallas_call(
        functools.partial(_attn_c_kernel, t=t, nsb=nsb),
        grid=(pairs, bsz, seq // tq),
        in_specs=[
            pl.BlockSpec((1, tq, LANES), lambda p, b, i: (b, i, q0 + p)),
            pl.BlockSpec((1, seq, LANES), lambda p, b, i: (b, 0, q0 + pairs + p)),
            pl.BlockSpec((1, seq, LANES), lambda p, b, i: (b, 0, q0 + 2 * pairs + p)),
            pl.BlockSpec((1, 1, 2, seq), lambda p, b, i: (b, p, 0, 0)),
        ],
        out_specs=pl.BlockSpec((1, tq, LANES), lambda p, b, i: (b, i, p)),
        out_shape=jax.ShapeDtypeStruct((bsz, seq, C_W), BF16),
        scratch_shapes=_flash_scratch(seq, t, nsb),
        compiler_params=_params("parallel", "parallel", "arbitrary"),
        name="attn_forget",
    )(qkv, qkv, qkv, fcum)


def _attn_b_kernel(q_ref, kp_ref, kc_ref, vp_ref, vc_ref, bias_ref, o_ref, lse_ref, *, dil, nb):
    n = pl.program_id(1)
    lane = lax.broadcasted_iota(jnp.int32, (BLOCK, LANES), 1)
    first_variant = jnp.minimum(n, 1)

    def residue(r):
        for hp in range(B_HEADS // 2):
            cols = slice(hp * LANES, (hp + 1) * LANES)
            kcat = jnp.concatenate([kp_ref[0, r, :, cols], kc_ref[0, r, :, cols]], axis=0)
            vcat = jnp.concatenate([vp_ref[0, r, :, cols], vc_ref[0, r, :, cols]], axis=0)
            for jb in range(nb):
                qh = _half_masks(q_ref[0, r, jb * BLOCK:(jb + 1) * BLOCK, cols] * SCALE)
                kwin = kcat[jb * BLOCK:(jb + 2) * BLOCK]
                vwin = vcat[jb * BLOCK:(jb + 2) * BLOCK]
                variant = first_variant if jb == 0 else 1
                outs, lses = [], []
                for hh in range(2):
                    s = _dot_nt(qh[hh], kwin) + bias_ref[variant, hp, hh * BLOCK:(hh + 1) * BLOCK, :]
                    m = jnp.max(s, axis=-1, keepdims=True)
                    e = jnp.exp(s - m)
                    den = jnp.sum(e, axis=-1, keepdims=True)
                    outs.append(_dot(e.astype(BF16), vwin) / den)
                    lses.append(jnp.broadcast_to(m + jnp.log(den), (BLOCK, LANES)))
                if dil == 1:
                    rows = pl.ds(jb * BLOCK, BLOCK)
                else:
                    rows = pl.ds(jb * BLOCK * dil + r, BLOCK, stride=dil)
                o_ref[0, hp, rows, :] = jnp.where(lane < HEAD_DIM, outs[0], outs[1])
                lse_ref[0, hp, rows, :] = jnp.where(lane < HEAD_DIM, lses[0], lses[1])

    unroll = min(dil, 4)
    if dil == unroll:
        for r in range(dil):
            residue(r)
    else:
        def body(i, carry):
            for j in range(unroll):
                residue(i * unroll + j)
            return carry

        lax.fori_loop(0, dil // unroll, body, 0)


def _attn_b_group(qkv_g, bias_g, g, nb):
    bsz, dil, m_len, _ = qkv_g.shape
    tb = BLOCK * nb
    cur = lambda c: pl.BlockSpec((1, dil, tb, B_W), lambda b, n: (b, 0, n, c))
    prev = lambda c: pl.BlockSpec((1, dil, BLOCK, B_W), lambda b, n: (b, 0, jnp.maximum(n * nb - 1, 0), c))
    pairs = B_HEADS // 2
    out_spec = pl.BlockSpec((1, pairs, tb * dil, LANES), lambda b, n: (b, 0, n, 0))
    out_sds = jax.ShapeDtypeStruct((bsz, pairs, m_len * dil, LANES), F32)
    return pl.pallas_call(
        functools.partial(_attn_b_kernel, dil=dil, nb=nb),
        grid=(bsz, m_len // tb),
        in_specs=[cur(0), prev(1), cur(1), prev(2), cur(2),
                  pl.BlockSpec(bias_g.shape, lambda b, n: (0, 0, 0, 0))],
        out_specs=[out_spec, out_spec],
        out_shape=[out_sds, out_sds],
        compiler_params=_params("parallel", "arbitrary"),
        name=f"attn_dilated_g{g}",
    )(qkv_g, qkv_g, qkv_g, qkv_g, qkv_g, bias_g)


def _merge_kernel(oa_ref, ob0, ob1, ob2, ls0, ls1, ls2, oc_ref, x_ref, mod_ref, g_ref,
                  wg_ref, wa_ref, wb_ref, wc_ref, wo_ref, o_ref):
    x = x_ref[...]
    m = mod_ref[0]
    hb = _rms_mod(x, g_ref[...], m[1:2], m[0:1]).astype(BF16)
    parts = []
    for hp in range(B_HEADS // 2):
        l0, l1, l2 = ls0[0, hp], ls1[0, hp], ls2[0, hp]
        mx = jnp.maximum(jnp.maximum(l0, l1), l2)
        e0, e1, e2 = jnp.exp(l0 - mx), jnp.exp(l1 - mx), jnp.exp(l2 - mx)
        den = e0 + e1 + e2
        parts.append((e0 / den) * ob0[0, hp] + (e1 / den) * ob1[0, hp] + (e2 / den) * ob2[0, hp])
    ob = jnp.concatenate(parts, axis=1)
    d = D_MODEL
    gate = lambda k: jax.nn.sigmoid(_dot(hb, wg_ref[:, k * d:(k + 1) * d]))
    merged = (gate(0) * _dot(oa_ref[...], wa_ref[...])
              + gate(1) * _dot(ob.astype(BF16), wb_ref[...])
              + gate(2) * _dot(oc_ref[...], wc_ref[...]))
    y = _dot(merged.astype(BF16), wo_ref[...])
    o_ref[...] = x + m[2:3] * y


def _merge(oa, obs, lses, oc, x2d, mod3, g, w_gate, wa, wb, wc, wo, *, seq, tm):
    rows, d = x2d.shape
    per_b = seq // tm
    row = lambda w: pl.BlockSpec((tm, w), lambda i: (i, 0))
    paired = pl.BlockSpec((1, B_HEADS // 2, tm, LANES), lambda i: (i // per_b, 0, i % per_b, 0))
    return pl.pallas_call(
        _merge_kernel,
        grid=(rows // tm,),
        in_specs=[row(A_W)] + [paired] * 6 + [row(C_W), row(d),
                  pl.BlockSpec((1, 6, d), lambda i: (i // per_b, 0, 0)),
                  pl.BlockSpec((1, d), lambda i: (0, 0)),
                  _resident(w_gate), _resident(wa), _resident(wb), _resident(wc), _resident(wo)],
        out_specs=row(d),
        out_shape=jax.ShapeDtypeStruct((rows, d), F32),
        compiler_params=_params("parallel"),
        name="merge_outproj",
    )(oa, *obs, *lses, oc, x2d, mod3, g, w_gate, wa, wb, wc, wo)


MXU_TILE = 256


def _ffn_kernel(x_ref, mod_ref, g_ref, wg_ref, wu_ref, wd_ref, o_ref, *, bounds):
    x = x_ref[...]
    m = mod_ref[0]
    h = _rms_mod(x, g_ref[...], m[4:5], m[3:4]).astype(BF16)
    acc = None
    for lo, hi in bounds:
        act = (_silu(_dot(h, wg_ref[:, lo:hi])) * _dot(h, wu_ref[:, lo:hi])).astype(BF16)
        part = _dot(act, wd_ref[lo:hi, :])
        acc = part if acc is None else acc + part
    o_ref[...] = x + m[5:6] * acc


def _ffn(x2d, mod3, g, wg, wu, wd, *, seq, tm):
    rows, d = x2d.shape
    dff = wg.shape[1]
    half = (dff // MXU_TILE + 1) // 2 * MXU_TILE
    per_b = seq // tm
    return pl.pallas_call(
        functools.partial(_ffn_kernel, bounds=((0, half), (half, dff))),
        grid=(rows // tm,),
        in_specs=[
            pl.BlockSpec((tm, d), lambda i: (i, 0)),
            pl.BlockSpec((1, 6, d), lambda i: (i // per_b, 0, 0)),
            pl.BlockSpec((1, d), lambda i: (0, 0)),
            _resident(wg), _resident(wu), _resident(wd),
        ],
        out_specs=pl.BlockSpec((tm, d), lambda i: (i, 0)),
        out_shape=jax.ShapeDtypeStruct((rows, d), F32),
        compiler_params=_params("parallel"),
        name="ffn_dense",
    )(x2d, mod3, g, wg, wu, wd)


def _route_kernel(x_ref, mod_ref, g_ref, wr_ref, tri_ref, h_ref, ridx_ref, rw_ref, cnt_ref, carry_scr):
    i = pl.program_id(0)

    @pl.when(i == 0)
    def _():
        carry_scr[...] = jnp.zeros(carry_scr.shape, F32)

    m = mod_ref[0]
    h = _rms_mod(x_ref[...], g_ref[...], m[4:5], m[3:4])
    h_ref[...] = h
    logits = jnp.dot(h, wr_ref[...], preferred_element_type=F32, precision=lax.Precision.HIGHEST)
    idx = lax.broadcasted_iota(jnp.int32, logits.shape, 1)
    n = logits.shape[1]
    m1 = jnp.max(logits, axis=-1, keepdims=True)
    i1 = jnp.min(jnp.where(logits == m1, idx, n), axis=-1, keepdims=True)
    first = idx == i1
    rest = jnp.where(first, -jnp.inf, logits)
    m2 = jnp.max(rest, axis=-1, keepdims=True)
    i2 = jnp.min(jnp.where(rest == m2, idx, n), axis=-1, keepdims=True)
    second = idx == i2
    e = jnp.exp(m2 - m1)
    den = 1.0 + e
    onehot = jnp.where(first | second, 1.0, 0.0)
    before = _dot(tri_ref[...], onehot.astype(BF16)) - onehot + carry_scr[...]
    rank1 = jnp.sum(jnp.where(first, before, 0.0), axis=-1, keepdims=True).astype(jnp.int32)
    rank2 = jnp.sum(jnp.where(second, before, 0.0), axis=-1, keepdims=True).astype(jnp.int32)
    carry_scr[...] += jnp.sum(onehot, axis=0, keepdims=True)
    ridx_ref[...] = jnp.where(idx == 0, i1, jnp.where(idx == 1, i2, jnp.where(idx == 2, rank1,
                              jnp.where(idx == 3, rank2, 0))))
    rw_ref[...] = jnp.where(idx == 0, 1.0 / den, jnp.where(idx == 1, e / den, 0.0))
    cnt_ref[...] = carry_scr[...]


def _route(x2d, mod3, g, wr, *, seq, tm):
    rows, d = x2d.shape
    ne = wr.shape[1]
    per_b = seq // tm
    tri = (jnp.arange(tm)[:, None] >= jnp.arange(tm)[None, :]).astype(BF16)
    row = lambda w: pl.BlockSpec((tm, w), lambda i: (i, 0))
    return pl.pallas_call(
        _route_kernel,
        grid=(rows // tm,),
        in_specs=[row(d), pl.BlockSpec((1, 6, d), lambda i: (i // per_b, 0, 0)),
                  pl.BlockSpec((1, d), lambda i: (0, 0)), pl.BlockSpec((d, ne), lambda i: (0, 0)),
                  pl.BlockSpec((tm, tm), lambda i: (0, 0))],
        out_specs=[row(d), row(ne), row(ne), pl.BlockSpec((1, ne), lambda i: (0, 0))],
        out_shape=[jax.ShapeDtypeStruct((rows, d), F32), jax.ShapeDtypeStruct((rows, ne), jnp.int32),
                   jax.ShapeDtypeStruct((rows, ne), F32), jax.ShapeDtypeStruct((1, ne), F32)],
        scratch_shapes=[pltpu.VMEM((1, ne), F32)],
        compiler_params=_params("arbitrary"),
        name="moe_route",
    )(x2d, mod3, g, wr, tri)


def _gather_rows(tok_of, n, src_hbm, dst, sem):
    def issue(r, carry):
        pltpu.make_async_copy(src_hbm.at[pl.ds(tok_of(r), 1)], dst.at[pl.ds(r, 1)], sem).start()
        return carry

    lax.fori_loop(0, n, issue, 0, unroll=8)


def _gather_wait(n, src_hbm, dst, sem):
    pltpu.make_async_copy(src_hbm.at[pl.ds(0, n)], dst, sem).wait()


def _expert_kernel(te_ref, src_ref, nu_ref, h_hbm, wg_ref, wu_ref, wd_ref, y_ref,
                   xbuf, xb_scr, acc_scr, sem, *, nf, r, nt):
    i = pl.program_id(0)
    f = pl.program_id(1)
    active = i < nu_ref[0]
    slot = i % 2

    def gather(tile, s):
        _gather_rows(lambda j: src_ref[tile * r + j] >> 1, r, h_hbm, xbuf.at[s], sem.at[s])

    @pl.when((i == 0) & (f == 0))
    def _():
        gather(0, 0)

    @pl.when(active & (f == 0))
    def _():
        _gather_wait(r, h_hbm, xbuf.at[slot], sem.at[slot])
        xb_scr[...] = xbuf[slot].astype(BF16)
        acc_scr[...] = jnp.zeros(acc_scr.shape, F32)

    @pl.when(active)
    def _():
        chunk = r // nf
        base = jnp.minimum(i + 1, nt - 1) * r + f * chunk
        for j in range(chunk):
            pltpu.make_async_copy(h_hbm.at[pl.ds(src_ref[base + j] >> 1, 1)],
                                  xbuf.at[1 - slot, pl.ds(f * chunk + j, 1)], sem.at[1 - slot]).start()
        xb = xb_scr[...]
        act = (_silu(_dot(xb, wg_ref[0])) * _dot(xb, wu_ref[0])).astype(BF16)
        acc_scr[...] += _dot(act, wd_ref[0])

    @pl.when((i == nu_ref[0] - 1) & (f == nf - 1))
    def _():
        _gather_wait(r, h_hbm, xbuf.at[1 - slot], sem.at[1 - slot])

    @pl.when(f == nf - 1)
    def _():
        y_ref[...] = jnp.where(active, acc_scr[...], 0.0)


def _experts(h2, tile_expert, src_pair, n_used, wg, wu, wd, *, r, tf):
    ne, d, dff = wg.shape
    nf = dff // tf
    p_rows = src_pair.shape[0]
    nt = p_rows // r
    fsel = lambda i, f, nu: jnp.where(i < nu[0], f, nf - 1)
    grid_spec = pltpu.PrefetchScalarGridSpec(
        num_scalar_prefetch=3,
        grid=(nt, nf),
        in_specs=[
            pl.BlockSpec(memory_space=pl.ANY),
            pl.BlockSpec((1, d, tf), lambda i, f, te, sp, nu: (te[i], 0, fsel(i, f, nu))),
            pl.BlockSpec((1, d, tf), lambda i, f, te, sp, nu: (te[i], 0, fsel(i, f, nu))),
            pl.BlockSpec((1, tf, d), lambda i, f, te, sp, nu: (te[i], fsel(i, f, nu), 0)),
        ],
        out_specs=pl.BlockSpec((r, d), lambda i, f, te, sp, nu: (i, 0)),
        scratch_shapes=[pltpu.VMEM((2, r, d), F32), pltpu.VMEM((r, d), BF16), pltpu.VMEM((r, d), F32),
                        pltpu.SemaphoreType.DMA((2,))],
    )
    return pl.pallas_call(
        functools.partial(_expert_kernel, nf=nf, r=r, nt=nt),
        grid_spec=grid_spec,
        out_shape=jax.ShapeDtypeStruct((p_rows, d), F32),
        compiler_params=_params("arbitrary", "arbitrary"),
        name="moe_experts",
    )(tile_expert, src_pair, n_used, h2, wg, wu, wd)


def _combine_kernel(pos_ref, y_hbm, x_ref, rw_ref, mod_ref, fg_ref, o_ref, ybuf, sem, *, tm, final):
    i = pl.program_id(0)
    n = pl.num_programs(0)
    slot = i % 2

    def gather(tile, s):
        for k in range(2):
            _gather_rows(lambda j: pos_ref[(tile * tm + j) * 2 + k], tm, y_hbm, ybuf.at[s, k], sem.at[s])

    @pl.when(i == 0)
    def _():
        gather(0, 0)

    for k in range(2):
        _gather_wait(tm, y_hbm, ybuf.at[slot, k], sem.at[slot])

    @pl.when(i + 1 < n)
    def _():
        gather(i + 1, 1 - slot)

    w = rw_ref[...]
    f = w[:, 0:1] * ybuf[slot, 0] + w[:, 1:2] * ybuf[slot, 1]
    x = x_ref[...] + mod_ref[0][5:6] * f
    if final:
        x = (x * lax.rsqrt(jnp.mean(x * x, axis=-1, keepdims=True) + EPS)) * fg_ref[...]
    o_ref[...] = x


def _combine(pos, y, x2d, rw, mod3, final_g, *, seq, tm, final):
    rows, d = x2d.shape
    per_b = seq // tm
    grid_spec = pltpu.PrefetchScalarGridSpec(
        num_scalar_prefetch=1,
        grid=(rows // tm,),
        in_specs=[
            pl.BlockSpec(memory_space=pl.ANY),
            pl.BlockSpec((tm, d), lambda i, p: (i, 0)),
            pl.BlockSpec((tm, rw.shape[1]), lambda i, p: (i, 0)),
            pl.BlockSpec((1, 6, d), lambda i, p: (i // per_b, 0, 0)),
            pl.BlockSpec((1, d), lambda i, p: (0, 0)),
        ],
        out_specs=pl.BlockSpec((tm, d), lambda i, p: (i, 0)),
        scratch_shapes=[pltpu.VMEM((2, 2, tm, d), F32), pltpu.SemaphoreType.DMA((2,))],
    )
    return pl.pallas_call(
        functools.partial(_combine_kernel, tm=tm, final=final),
        grid_spec=grid_spec,
        out_shape=jax.ShapeDtypeStruct((rows, d), F32),
        compiler_params=_params("arbitrary"),
        name="moe_combine",
    )(pos, y, x2d, rw, mod3, final_g)


def _moe(x2d, mod3, g, wr, wg, wu, wd, final_g, *, seq, final):
    rows, d = x2d.shape
    ne = wr.shape[1]
    r = 512
    h2, ridx, rw, cnt = _route(x2d, mod3, g, wr, seq=seq, tm=1024)
    counts = cnt[0].astype(jnp.int32)
    tiles_e = (counts + r - 1) // r
    tile_end = jnp.cumsum(tiles_e)
    start = (tile_end - tiles_e) * r
    pos = jnp.take(start, ridx[:, 0:2], axis=0) + ridx[:, 2:4]
    p_rows = 2 * rows + ne * r
    nt = p_rows // r
    pair_id = jnp.arange(2 * rows, dtype=jnp.int32)
    src_pair = jnp.zeros((p_rows,), jnp.int32).at[pos.reshape(-1)].set(pair_id)
    n_used = tile_end[-1:]
    tile_idx = jnp.minimum(jnp.arange(nt), n_used[0] - 1)
    tile_expert = jnp.sum(tile_idx[:, None] >= tile_end[None, :], axis=1)
    y = _experts(h2, tile_expert.astype(jnp.int32), src_pair, n_used.astype(jnp.int32), wg, wu, wd,
                 r=r, tf=1792)
    return _combine(pos.reshape(-1).astype(jnp.int32), y, x2d, rw, mod3, final_g, seq=seq, tm=256,
                    final=final)


def _final_norm_kernel(x_ref, g_ref, o_ref):
    x = x_ref[...]
    o_ref[...] = (x * lax.rsqrt(jnp.mean(x * x, axis=-1, keepdims=True) + EPS)) * g_ref[...]


def _final_norm(x2d, g, *, tm):
    rows, d = x2d.shape
    return pl.pallas_call(
        _final_norm_kernel,
        grid=(rows // tm,),
        in_specs=[pl.BlockSpec((tm, d), lambda i: (i, 0)), pl.BlockSpec((1, d), lambda i: (0, 0))],
        out_specs=pl.BlockSpec((tm, d), lambda i: (i, 0)),
        out_shape=jax.ShapeDtypeStruct((rows, d), F32),
        compiler_params=_params("parallel"),
        name="final_norm",
    )(x2d, g)


def _t5_bucket(dist):
    n = jnp.maximum(dist, 0)
    max_exact = N_BUCKETS // 2
    nf = jnp.maximum(n, 1).astype(F32)
    large = max_exact + (jnp.log(nf / max_exact) / math.log(REL_MAX_DIST / max_exact)
                         * (N_BUCKETS - max_exact)).astype(jnp.int32)
    large = jnp.minimum(large, N_BUCKETS - 1)
    return jnp.where(n < max_exact, n, large)


def _bias_tiles_a(rel_bias, seq, t):
    nq = seq // t
    ncol = 2 * A_HEADS
    tab = rel_bias[:, :ncol][_t5_bucket(jnp.arange(seq))].astype(F32).T
    vneg = jnp.full((ncol, t), NEG, F32)
    v = jnp.concatenate([vneg, tab], axis=1)
    u = jnp.concatenate([v[:, 1:seq + 1][:, ::-1], vneg[:, :1], v[:, seq + 1:seq + t][:, ::-1]], axis=1)

    def toeplitz_kernel(u_ref, o_ref):
        x = jnp.broadcast_to(u_ref[0], (t, seq + t))
        r = pltpu.roll(x, 0, 1, stride=1, stride_axis=0)
        for delta in range(nq):
            c0 = (nq - 1 - delta) * t
            o_ref[0, delta] = r[:, c0:c0 + t]

    return pl.pallas_call(
        toeplitz_kernel,
        grid=(ncol,),
        in_specs=[pl.BlockSpec((1, 1, seq + t), lambda c: (c, 0, 0))],
        out_specs=pl.BlockSpec((1, nq, t, t), lambda c: (c // 2, 0, c % 2, 0)),
        out_shape=jax.ShapeDtypeStruct((A_HEADS, nq, 2 * t, t), F32),
        compiler_params=_params("parallel"),
        name="bias_tiles_diff",
    )(u.reshape(ncol, 1, seq + t))


def _bias_tiles_b(rel_bias):
    ng = len(B_GROUPS)
    period = 3 * BLOCK
    rows = []
    for g, (win, dil) in enumerate(B_GROUPS):
        n_back = win // dil
        tab = rel_bias[:, 2 * A_HEADS + g * B_HEADS:2 * A_HEADS + (g + 1) * B_HEADS]
        vals = tab[_t5_bucket(jnp.arange(n_back, -1, -1) * dil)].astype(F32).T
        rows.append(jnp.concatenate([vals, jnp.full((B_HEADS, period - n_back - 1), NEG, F32)], axis=1))
    u = jnp.concatenate(rows, axis=0)

    def toeplitz_kernel(u_ref, o_ref):
        x = jnp.broadcast_to(u_ref[0], (BLOCK, period))
        r = pltpu.roll(x, 0, 1, stride=1, stride_axis=0)[:, :2 * BLOCK]
        col = lax.broadcasted_iota(jnp.int32, r.shape, 1)
        o_ref[0, 0, 0] = jnp.where(col >= BLOCK, r, NEG)
        o_ref[0, 1, 0] = r

    return pl.pallas_call(
        toeplitz_kernel,
        grid=(ng * B_HEADS,),
        in_specs=[pl.BlockSpec((1, 1, period), lambda c: (c, 0, 0))],
        out_specs=pl.BlockSpec((1, 2, 1, BLOCK, 2 * BLOCK),
                               lambda c: (c // B_HEADS, 0, (c % B_HEADS) // 2, c % 2, 0)),
        out_shape=jax.ShapeDtypeStruct((ng, 2, B_HEADS // 2, 2 * BLOCK, 2 * BLOCK), F32),
        compiler_params=_params("parallel"),
        name="bias_tiles_dilated",
    )(u.reshape(ng * B_HEADS, 1, period))


def kernel(x, c, norm_mix_g, norm_ffn_g, w_mod, b_mod, w_in, b_forget, lam_q1, lam_k1, lam_q2, lam_k2,
           subln_g, rel_bias, w_br_a, w_br_b, w_br_c, w_out, w_ff_gate, w_ff_up, w_ff_down, w_router,
           w_exp_gate, w_exp_up, w_exp_down, final_norm_g):
    bsz, seq, d = x.shape
    depth = w_mod.shape[0]
    rows = bsz * seq
    t_attn = 256
    x2d = x.reshape(rows, d)

    mod = _modulation(c, w_mod, b_mod)
    bias_a = _bias_tiles_a(rel_bias, seq, t_attn)
    bias_b = _bias_tiles_b(rel_bias)
    w_qkv_all, w_gf_all = _prep_w_in(w_in)

    final_g = final_norm_g.reshape(1, d)
    fused_final = False
    for l in range(depth):
        lam_init = 0.8 - 0.6 * math.exp(-0.3 * l)
        mod3 = mod[l].reshape(bsz, 6, d)
        w_qkv, w_gate, w_f = w_qkv_all[l], w_gf_all[l, :, :GATE_W], w_gf_all[l, :, GATE_W:]

        g_mix = norm_mix_g[l].reshape(1, d)
        qkv_ac, qkv_b0, qkv_b1, qkv_b2, f_logit = _qkv_proj(x2d, mod3, g_mix, w_qkv, w_f,
                                                            bsz=bsz, seq=seq, tm=512)
        qkv_ac = qkv_ac.reshape(bsz, seq, -1)

        b_f8 = jnp.pad(b_forget[l], (0, 8 - C_HEADS)).reshape(8, 1)
        fcum = _forget_cumsum(f_logit.reshape(bsz, seq, LANES), b_f8, col_block=0)
        fcum = fcum[:, :C_HEADS].reshape(bsz, C_HEADS // 2, 2, seq)

        oa = _attn_a(qkv_ac, bias_a, lam_q1[l].reshape(1, -1), lam_k1[l].reshape(1, -1),
                     lam_q2[l].reshape(1, -1), lam_k2[l].reshape(1, -1), subln_g[l].reshape(1, -1),
                     lam_init=lam_init, t=t_attn, nsb=4)
        oc = _attn_c(qkv_ac, fcum, t=t_attn, nsb=4)
        obs, lses = [], []
        groups = (qkv_b0.reshape(bsz, 1, seq, 3 * B_W), qkv_b1, qkv_b2)
        for g, (qkv_g, nb) in enumerate(zip(groups, (4, 1, 1))):
            o_g, lse_g = _attn_b_group(qkv_g, bias_b[g], g, nb)
            obs.append(o_g)
            lses.append(lse_g)

        x2d = _merge(oa.reshape(rows, A_W), obs, lses, oc.reshape(rows, C_W), x2d, mod3, g_mix, w_gate,
                     w_br_a[l].astype(BF16), w_br_b[l].astype(BF16), w_br_c[l].astype(BF16),
                     w_out[l].astype(BF16), seq=seq, tm=512)

        g_ffn = norm_ffn_g[l].reshape(1, d)
        if l % 2 == 0:
            j = l // 2
            x2d = _ffn(x2d, mod3, g_ffn, w_ff_gate[j].astype(BF16), w_ff_up[j].astype(BF16),
                       w_ff_down[j].astype(BF16), seq=seq, tm=512)
        else:
            j = l // 2
            fused_final = l == depth - 1
            x2d = _moe(x2d, mod3, g_ffn, w_router[j], w_exp_gate[j].astype(BF16),
                       w_exp_up[j].astype(BF16), w_exp_down[j].astype(BF16), final_g,
                       seq=seq, final=fused_final)

    if not fused_final:
        x2d = _final_norm(x2d, final_g, tm=1024)
    return x2d.reshape(bsz, seq, d)
```

```python
import functools
import math

import jax
import jax.numpy as jnp
from jax import lax
from jax.experimental import pallas as pl
from jax.experimental.pallas import tpu as pltpu

F32 = jnp.float32
BF16 = jnp.bfloat16

D_MODEL = 1024
HEAD_DIM = 64
LANES = 128
A_HEADS = 4
A_W = A_HEADS * 2 * HEAD_DIM
B_GROUPS = ((128, 1), (512, 4), (2048, 16))
B_HEADS = 6
B_W = B_HEADS * HEAD_DIM
B_QW = len(B_GROUPS) * B_W
C_HEADS = 6
C_W = C_HEADS * HEAD_DIM
N_BRANCH = 3
BLOCK = 128
N_BUCKETS = 32
REL_MAX_DIST = 2048
N_EXPERTS = 8
EPS = 1e-6
QKV_W = 3 * A_W + 3 * B_QW + 3 * C_W
GATE_W = N_BRANCH * D_MODEL
NEG = -1e30
SCALE = HEAD_DIM ** -0.5
VMEM_LIMIT = 56 * 1024 * 1024


def _params(*sem):
    return pltpu.CompilerParams(dimension_semantics=sem, vmem_limit_bytes=VMEM_LIMIT)


def _resident(a):
    return pl.BlockSpec(a.shape, lambda *_: (0, 0), pipeline_mode=pl.Buffered(1))


def _rms_mod(x, g, sc, sh):
    y = x * lax.rsqrt(jnp.mean(x * x, axis=-1, keepdims=True) + EPS)
    return (y * g) * (1.0 + sc) + sh


def _dot(a, b):
    return jnp.dot(a, b, preferred_element_type=F32)


def _dot_nt(a, b):
    return lax.dot_general(a, b, (((1,), (1,)), ((), ())), preferred_element_type=F32)


def _silu(a):
    return a * jax.nn.sigmoid(a)


def _mod_kernel(c_ref, w_ref, b_ref, o_ref):
    a = _silu(c_ref[...]).astype(BF16)
    o_ref[0] = _dot(a, w_ref[0].astype(BF16)) + b_ref[0]


def _modulation(c, w_mod, b_mod):
    depth, d, n = w_mod.shape
    bsz = c.shape[0]
    tn = 1536
    return pl.pallas_call(
        _mod_kernel,
        grid=(depth, n // tn),
        in_specs=[
            pl.BlockSpec((bsz, d), lambda l, j: (0, 0)),
            pl.BlockSpec((1, d, tn), lambda l, j: (l, 0, j)),
            pl.BlockSpec((1, 1, tn), lambda l, j: (l, 0, j)),
        ],
        out_specs=pl.BlockSpec((1, bsz, tn), lambda l, j: (l, 0, j)),
        out_shape=jax.ShapeDtypeStruct((depth, bsz, n), F32),
        compiler_params=_params("parallel", "parallel"),
        name="modulation",
    )(c, w_mod, b_mod.reshape(depth, 1, n))


_A_BLOCKS = 3 * A_W // B_W
_B_BLOCKS = 3 * B_QW // B_W
_QKV_SRC_BLOCKS = (list(range(_A_BLOCKS))
                   + list(range(_A_BLOCKS + _B_BLOCKS, QKV_W // B_W))
                   + [_A_BLOCKS + s * len(B_GROUPS) + g for g in range(len(B_GROUPS)) for s in range(3)])


def _prep_w_in_kernel(*refs):
    n = len(_QKV_SRC_BLOCKS)
    piece_refs, x_ref, y_ref, qkv_ref, gf_ref = refs[:n], refs[n], refs[n + 1], refs[n + 2], refs[n + 3]
    for j, ref in enumerate(piece_refs):
        qkv_ref[0, :, j * B_W:(j + 1) * B_W] = ref[0].astype(BF16)
    x = x_ref[0]
    rolled = pltpu.roll(x, GATE_W - C_HEADS, 1)
    tail = pltpu.roll(y_ref[0], LANES - C_HEADS, 1)
    lane = lax.broadcasted_iota(jnp.int32, tail.shape, 1)
    gf_ref[0, :, :GATE_W - LANES] = rolled[:, :GATE_W - LANES].astype(BF16)
    gf_ref[0, :, GATE_W - LANES:GATE_W] = jnp.where(lane < LANES - C_HEADS, rolled[:, GATE_W - LANES:],
                                                     tail).astype(BF16)
    gf_ref[0, :, GATE_W:] = jnp.where(lane < C_HEADS, x[:, :LANES], 0.0).astype(BF16)


def _prep_w_in(w_in):
    depth, d, _ = w_in.shape
    tr = 256
    piece = lambda c: pl.BlockSpec((1, tr, B_W), lambda l, i: (l, i, c))
    return pl.pallas_call(
        _prep_w_in_kernel,
        grid=(depth, d // tr),
        in_specs=[piece(c) for c in _QKV_SRC_BLOCKS]
        + [pl.BlockSpec((1, tr, GATE_W), lambda l, i: (l, i, QKV_W // GATE_W)),
           pl.BlockSpec((1, tr, LANES), lambda l, i: (l, i, (QKV_W + GATE_W) // LANES))],
        out_specs=[pl.BlockSpec((1, tr, QKV_W), lambda l, i: (l, i, 0)),
                   pl.BlockSpec((1, tr, GATE_W + LANES), lambda l, i: (l, i, 0))],
        out_shape=[jax.ShapeDtypeStruct((depth, d, QKV_W), BF16),
                   jax.ShapeDtypeStruct((depth, d, GATE_W + LANES), BF16)],
        compiler_params=_params("parallel", "parallel"),
        name="prep_w_in",
    )(*([w_in] * (len(_QKV_SRC_BLOCKS) + 2)))


def _qkv_kernel(x_ref, mod_ref, g_ref, w_ref, wf_ref, ac_ref, b0_ref, b1_ref, b2_ref, f_ref, h_scr, *, tm):
    m = mod_ref[0]
    h = _rms_mod(x_ref[...], g_ref[...], m[1:2], m[0:1])
    nc = h_scr.shape[0]
    for c in range(nc):
        h_scr[c] = h[:, c * LANES:(c + 1) * LANES]
    hb = h.astype(BF16)
    n_ac = ac_ref.shape[1]
    n_b = b0_ref.shape[1]
    f_ref[...] = _dot(hb, wf_ref[...])
    ac_ref[...] = _dot(hb, w_ref[:, 0:n_ac]).astype(BF16)
    b0_ref[...] = _dot(hb, w_ref[:, n_ac:n_ac + n_b]).astype(BF16)
    for gi, ref in ((1, b1_ref), (2, b2_ref)):
        dil = B_GROUPS[gi][1]
        per = tm // dil
        hp = jnp.concatenate(
            [jnp.concatenate([h_scr[c, pl.ds(r, per, stride=dil), :] for c in range(nc)], axis=1).astype(BF16)
             for r in range(dil)], axis=0)
        y = _dot(hp, w_ref[:, n_ac + gi * n_b:n_ac + (gi + 1) * n_b]).astype(BF16)
        for r in range(dil):
            ref[0, r] = y[r * per:(r + 1) * per]


def _qkv_proj(x2d, mod3, g, w, w_f, *, bsz, seq, tm):
    rows, d = x2d.shape
    per_b = seq // tm
    n_b = 3 * B_W
    n_ac = w.shape[1] - 3 * n_b
    dil1, dil2 = B_GROUPS[1][1], B_GROUPS[2][1]
    strided = lambda dil: pl.BlockSpec((1, dil, tm // dil, n_b), lambda i: (i // per_b, 0, i % per_b, 0))
    return pl.pallas_call(
        functools.partial(_qkv_kernel, tm=tm),
        grid=(rows // tm,),
        in_specs=[
            pl.BlockSpec((tm, d), lambda i: (i, 0)),
            pl.BlockSpec((1, 6, d), lambda i: (i // per_b, 0, 0)),
            pl.BlockSpec((1, d), lambda i: (0, 0)),
            _resident(w), _resident(w_f),
        ],
        out_specs=[pl.BlockSpec((tm, n_ac), lambda i: (i, 0)), pl.BlockSpec((tm, n_b), lambda i: (i, 0)),
                   strided(dil1), strided(dil2), pl.BlockSpec((tm, LANES), lambda i: (i, 0))],
        out_shape=[jax.ShapeDtypeStruct((rows, n_ac), BF16), jax.ShapeDtypeStruct((rows, n_b), BF16),
                   jax.ShapeDtypeStruct((bsz, dil1, seq // dil1, n_b), BF16),
                   jax.ShapeDtypeStruct((bsz, dil2, seq // dil2, n_b), BF16),
                   jax.ShapeDtypeStruct((rows, LANES), F32)],
        scratch_shapes=[pltpu.VMEM((d // LANES, tm, LANES), F32)],
        compiler_params=_params("parallel"),
        name="qkv_proj",
    )(x2d, mod3, g, w, w_f)


def _fcum_kernel(f_ref, b_ref, o_ref):
    z = f_ref[0].T[:8] + b_ref[...]
    x = jnp.minimum(z, 0.0) - jnp.log1p(jnp.exp(-jnp.abs(z)))
    s = x.shape[1]
    lane = lax.broadcasted_iota(jnp.int32, x.shape, 1)
    k = 1
    while k < s:
        x = x + jnp.where(lane >= k, pltpu.roll(x, k, 1), 0.0)
        k *= 2
    o_ref[0] = x


def _forget_cumsum(gf, b_f8, *, col_block):
    bsz, seq, _ = gf.shape
    return pl.pallas_call(
        _fcum_kernel,
        grid=(bsz,),
        in_specs=[
            pl.BlockSpec((1, seq, LANES), lambda b: (b, 0, col_block)),
            pl.BlockSpec((8, 1), lambda b: (0, 0)),
        ],
        out_specs=pl.BlockSpec((1, 8, seq), lambda b: (b, 0, 0)),
        out_shape=jax.ShapeDtypeStruct((bsz, 8, seq), F32),
        compiler_params=_params("parallel"),
        name="forget_cumsum",
    )(gf, b_f8)


def _half_masks(q):
    lane = lax.broadcasted_iota(jnp.int32, q.shape, 1)
    zero = jnp.zeros_like(q)
    return jnp.where(lane < HEAD_DIM, q, zero), jnp.where(lane >= HEAD_DIM, q, zero)


def _flash_init(first, v_ref, vext_scr, m_scr, acc_scr):
    @pl.when(first)
    def _():
        for bb in range(vext_scr.shape[0]):
            vext_scr[bb, :, :LANES] = v_ref[bb]
            vext_scr[bb, :, LANES:] = jnp.ones((vext_scr.shape[1], LANES), BF16)

    m_scr[...] = jnp.full(m_scr.shape, -jnp.inf, F32)
    acc_scr[...] = jnp.zeros(acc_scr.shape, F32)


def _lane_tile(a, n):
    return a if n == 1 else jnp.concatenate([a] * n, axis=1)


def _flash_update(s, vext, rows, m_scr, acc_scr):
    m_prev = m_scr[rows]
    m_new = jnp.maximum(m_prev, jnp.max(s, axis=-1, keepdims=True))
    alpha = jnp.exp(m_prev - m_new)
    p = jnp.exp(s - _lane_tile(m_new, s.shape[1] // LANES))
    acc_scr[rows] = _lane_tile(alpha, 2) * acc_scr[rows] + _dot(p.astype(BF16), vext)
    m_scr[rows] = m_new


def _flash_result(acc_scr):
    acc = acc_scr[...]
    return acc[:, :LANES] / acc[:, LANES:]


def _attn_a_kernel(q_ref, k_ref, v_ref, bias_ref, lq1, lk1, lq2, lk2, sg_ref, o_ref,
                   vext_scr, m_scr, acc_scr, *, t, nsb, nbb, lam_init):
    qi = pl.program_id(2)
    _flash_init(qi == 0, v_ref, vext_scr, m_scr, acc_scr)
    qh = [[_half_masks(q_ref[bb, sb * t:(sb + 1) * t, :] * SCALE) for sb in range(nsb)] for bb in range(nbb)]
    chain_rows = lambda bb, sb, hh: pl.ds(((bb * nsb + sb) * 2 + hh) * t, t)

    def step(kb, nkb, plan):
        off = pl.multiple_of(kb * t, t)
        for bb in range(nbb):
            kblk = k_ref[bb, pl.ds(off, nkb * t), :]
            vext = vext_scr[bb, pl.ds(off, nkb * t), :]
            for sb, deltas in plan:
                for hh in range(2):
                    bias = [bias_ref[0, d, hh * t:(hh + 1) * t, :] for d in deltas]
                    s = _dot_nt(qh[bb][sb][hh], kblk) + (bias[0] if nkb == 1 else jnp.concatenate(bias, axis=1))
                    _flash_update(s, vext, chain_rows(bb, sb, hh), m_scr, acc_scr)

    def body(kb2, carry):
        first = [qi * nsb + sb - 2 * kb2 for sb in range(nsb)]
        step(2 * kb2, 2, [(sb, (first[sb], first[sb] - 1)) for sb in range(nsb)])
        return carry

    lax.fori_loop(0, qi * (nsb // 2), body, 0)
    for j in range(0, nsb, 2):
        step(qi * nsb + j, 1, [(j, (0,))])
        step(qi * nsb + j, 2, [(sb, (sb - j, sb - j - 1)) for sb in range(j + 1, nsb)])

    o = _flash_result(acc_scr)
    lam = (jnp.exp(jnp.sum(lq1[...] * lk1[...], axis=-1, keepdims=True))
           - jnp.exp(jnp.sum(lq2[...] * lk2[...], axis=-1, keepdims=True)) + lam_init)
    for bb in range(nbb):
        for sb in range(nsb):
            c0 = (bb * nsb + sb) * 2 * t
            d = o[c0:c0 + t] - lam * o[c0 + t:c0 + 2 * t]
            y = d * lax.rsqrt(jnp.mean(d * d, axis=-1, keepdims=True) + EPS)
            o_ref[bb, sb * t:(sb + 1) * t, :] = ((y * sg_ref[...]) * (1.0 - lam_init)).astype(o_ref.dtype)


def _flash_scratch(seq, t, nsb, nbb):
    chains = 2 * nsb * nbb
    return [pltpu.VMEM((nbb, seq, 2 * LANES), BF16), pltpu.VMEM((chains * t, LANES), F32),
            pltpu.VMEM((chains * t, 2 * LANES), F32)]


def _attn_a(qkv, bias_a, lq1, lk1, lq2, lk2, subln_g, *, lam_init, t, nsb, nbb):
    bsz, seq, _ = qkv.shape
    tq = t * nsb
    assert bsz % nbb == 0 and seq % tq == 0 and nsb % 2 == 0
    vec = lambda n: pl.BlockSpec((1, n), lambda h, b, i: (0, 0))
    return pl.pallas_call(
        functools.partial(_attn_a_kernel, t=t, nsb=nsb, nbb=nbb, lam_init=lam_init),
        grid=(A_HEADS, bsz // nbb, seq // tq),
        in_specs=[
            pl.BlockSpec((nbb, tq, LANES), lambda h, b, i: (b, i, h)),
            pl.BlockSpec((nbb, seq, LANES), lambda h, b, i: (b, 0, A_HEADS + h)),
            pl.BlockSpec((nbb, seq, LANES), lambda h, b, i: (b, 0, 2 * A_HEADS + h)),
            pl.BlockSpec((1, seq // t, 2 * t, t), lambda h, b, i: (h, 0, 0, 0)),
            vec(HEAD_DIM), vec(HEAD_DIM), vec(HEAD_DIM), vec(HEAD_DIM), vec(LANES),
        ],
        out_specs=pl.BlockSpec((nbb, tq, LANES), lambda h, b, i: (b, i, h)),
        out_shape=jax.ShapeDtypeStruct((bsz, seq, A_W), BF16),
        scratch_shapes=_flash_scratch(seq, t, nsb, nbb),
        compiler_params=_params("parallel", "parallel", "arbitrary"),
        name="attn_diff",
    )(qkv, qkv, qkv, bias_a, lq1, lk1, lq2, lk2, subln_g)


def _attn_c_kernel(q_ref, k_ref, v_ref, f_ref, o_ref, vext_scr, m_scr, acc_scr, *, t, nsb, nbb):
    qi = pl.program_id(2)
    _flash_init(qi == 0, v_ref, vext_scr, m_scr, acc_scr)
    qh = [[_half_masks(q_ref[bb, sb * t:(sb + 1) * t, :] * SCALE) for sb in range(nsb)] for bb in range(nbb)]
    chain_rows = lambda bb, sb, hh: pl.ds(((bb * nsb + sb) * 2 + hh) * t, t)
    q_off = pl.multiple_of(qi * (t * nsb), t * nsb)
    f_anchor = [f_ref[bb, 0, :, pl.ds(q_off, LANES)][:, :1] for bb in range(nbb)]
    def causal(nkb):
        r = lax.broadcasted_iota(jnp.int32, (t, nkb * t), 0)
        c = lax.broadcasted_iota(jnp.int32, (t, nkb * t), 1)
        return r + (nkb - 1) * t >= c

    def step(kb, nkb, sbs, diag_sb):
        off = pl.multiple_of(kb * t, t)
        for bb in range(nbb):
            kblk = k_ref[bb, pl.ds(off, nkb * t), :]
            vext = vext_scr[bb, pl.ds(off, nkb * t), :]
            dec = [f_anchor[bb][hh:hh + 1] - f_ref[bb, 0, hh:hh + 1, pl.ds(off, nkb * t)] for hh in range(2)]
            for sb in sbs:
                for hh in range(2):
                    s = _dot_nt(qh[bb][sb][hh], kblk) + dec[hh]
                    if sb == diag_sb:
                        s = jnp.where(causal(nkb), s, NEG)
                    _flash_update(s, vext, chain_rows(bb, sb, hh), m_scr, acc_scr)

    def body(kb2, carry):
        step(2 * kb2, 2, range(nsb), None)
        return carry

    lax.fori_loop(0, qi * (nsb // 2), body, 0)
    for j in range(0, nsb, 2):
        step(qi * nsb + j, 1, [j], j)
        step(qi * nsb + j, 2, range(j + 1, nsb), j + 1)

    o = _flash_result(acc_scr)
    lane = lax.broadcasted_iota(jnp.int32, (t, LANES), 1)
    for bb in range(nbb):
        for sb in range(nsb):
            c0 = (bb * nsb + sb) * 2 * t
            pair = jnp.where(lane < HEAD_DIM, o[c0:c0 + t], o[c0 + t:c0 + 2 * t])
            o_ref[bb, sb * t:(sb + 1) * t, :] = pair.astype(o_ref.dtype)


def _attn_c(qkv, fcum, *, t, nsb, nbb):
    bsz, seq, _ = qkv.shape
    tq = t * nsb
    assert bsz % nbb == 0 and seq % tq == 0 and nsb % 2 == 0
    pairs = C_HEADS // 2
    q0 = 3 * A_W // LANES
    return pl.pallas_call(
        functools.partial(_attn_c_kernel, t=t, nsb=nsb, nbb=nbb),
        grid=(pairs, bsz // nbb, seq // tq),
        in_specs=[
            pl.BlockSpec((nbb, tq, LANES), lambda p, b, i: (b, i, q0 + p)),
            pl.BlockSpec((nbb, seq, LANES), lambda p, b, i: (b, 0, q0 + pairs + p)),
            pl.BlockSpec((nbb, seq, LANES), lambda p, b, i: (b, 0, q0 + 2 * pairs + p)),
            pl.BlockSpec((nbb, 1, 2, seq), lambda p, b, i: (b, p, 0, 0)),
        ],
        out_specs=pl.BlockSpec((nbb, tq, LANES), lambda p, b, i: (b, i, p)),
        out_shape=jax.ShapeDtypeStruct((bsz, seq, C_W), BF16),
        scratch_shapes=_flash_scratch(seq, t, nsb, nbb),
        compiler_params=_params("parallel", "parallel", "arbitrary"),
        name="attn_forget",
    )(qkv, qkv, qkv, fcum)


def _attn_b_kernel(q_ref, kp_ref, kc_ref, vp_ref, vc_ref, bias_ref, o_ref, lse_ref, *, dil, nb):
    n = pl.program_id(1)
    lane = lax.broadcasted_iota(jnp.int32, (BLOCK, LANES), 1)
    first_variant = jnp.minimum(n, 1)

    def residue(r):
        for hp in range(B_HEADS // 2):
            cols = slice(hp * LANES, (hp + 1) * LANES)
            kcat = jnp.concatenate([kp_ref[0, r, :, cols], kc_ref[0, r, :, cols]], axis=0)
            vcat = jnp.concatenate([vp_ref[0, r, :, cols], vc_ref[0, r, :, cols]], axis=0)
            for jb in range(nb):
                qh = _half_masks(q_ref[0, r, jb * BLOCK:(jb + 1) * BLOCK, cols] * SCALE)
                kwin = kcat[jb * BLOCK:(jb + 2) * BLOCK]
                vwin = vcat[jb * BLOCK:(jb + 2) * BLOCK]
                variant = first_variant if jb == 0 else 1
                outs, lses = [], []
                for hh in range(2):
                    s = _dot_nt(qh[hh], kwin) + bias_ref[variant, hp, hh * BLOCK:(hh + 1) * BLOCK, :]
                    m = jnp.max(s, axis=-1, keepdims=True)
                    e = jnp.exp(s - m)
                    den = jnp.sum(e, axis=-1, keepdims=True)
                    outs.append(_dot(e.astype(BF16), vwin) / den)
                    lses.append(jnp.broadcast_to(m + jnp.log(den), (BLOCK, LANES)))
                if dil == 1:
                    rows = pl.ds(jb * BLOCK, BLOCK)
                else:
                    rows = pl.ds(jb * BLOCK * dil + r, BLOCK, stride=dil)
                o_ref[0, hp, rows, :] = jnp.where(lane < HEAD_DIM, outs[0], outs[1])
                lse_ref[0, hp, rows, :] = jnp.where(lane < HEAD_DIM, lses[0], lses[1])

    unroll = min(dil, 4)
    if dil == unroll:
        for r in range(dil):
            residue(r)
    else:
        def body(i, carry):
            for j in range(unroll):
                residue(i * unroll + j)
            return carry

        lax.fori_loop(0, dil // unroll, body, 0)


def _attn_b_group(qkv_g, bias_g, g, nb):
    bsz, dil, m_len, _ = qkv_g.shape
    tb = BLOCK * nb
    cur = lambda c: pl.BlockSpec((1, dil, tb, B_W), lambda b, n: (b, 0, n, c))
    prev = lambda c: pl.BlockSpec((1, dil, BLOCK, B_W), lambda b, n: (b, 0, jnp.maximum(n * nb - 1, 0), c))
    pairs = B_HEADS // 2
    out_spec = pl.BlockSpec((1, pairs, tb * dil, LANES), lambda b, n: (b, 0, n, 0))
    out_sds = jax.ShapeDtypeStruct((bsz, pairs, m_len * dil, LANES), F32)
    return pl.pallas_call(
        functools.partial(_attn_b_kernel, dil=dil, nb=nb),
        grid=(bsz, m_len // tb),
        in_specs=[cur(0), prev(1), cur(1), prev(2), cur(2),
                  pl.BlockSpec(bias_g.shape, lambda b, n: (0, 0, 0, 0))],
        out_specs=[out_spec, out_spec],
        out_shape=[out_sds, out_sds],
        compiler_params=_params("parallel", "arbitrary"),
        name=f"attn_dilated_g{g}",
    )(qkv_g, qkv_g, qkv_g, qkv_g, qkv_g, bias_g)


def _merge_kernel(oa_ref, ob0, ob1, ob2, ls0, ls1, ls2, oc_ref, x_ref, mod_ref, g_ref,
                  wg_ref, wa_ref, wb_ref, wc_ref, wo_ref, o_ref):
    x = x_ref[...]
    m = mod_ref[0]
    hb = _rms_mod(x, g_ref[...], m[1:2], m[0:1]).astype(BF16)
    parts = []
    for hp in range(B_HEADS // 2):
        l0, l1, l2 = ls0[0, hp], ls1[0, hp], ls2[0, hp]
        mx = jnp.maximum(jnp.maximum(l0, l1), l2)
        e0, e1, e2 = jnp.exp(l0 - mx), jnp.exp(l1 - mx), jnp.exp(l2 - mx)
        den = e0 + e1 + e2
        parts.append((e0 / den) * ob0[0, hp] + (e1 / den) * ob1[0, hp] + (e2 / den) * ob2[0, hp])
    ob = jnp.concatenate(parts, axis=1)
    d = D_MODEL
    gate = lambda k: jax.nn.sigmoid(_dot(hb, wg_ref[:, k * d:(k + 1) * d]))
    merged = (gate(0) * _dot(oa_ref[...], wa_ref[...])
              + gate(1) * _dot(ob.astype(BF16), wb_ref[...])
              + gate(2) * _dot(oc_ref[...], wc_ref[...]))
    y = _dot(merged.astype(BF16), wo_ref[...])
    o_ref[...] = x + m[2:3] * y


def _merge(oa, obs, lses, oc, x2d, mod3, g, w_gate, wa, wb, wc, wo, *, seq, tm):
    rows, d = x2d.shape
    per_b = seq // tm
    row = lambda w: pl.BlockSpec((tm, w), lambda i: (i, 0))
    paired = pl.BlockSpec((1, B_HEADS // 2, tm, LANES), lambda i: (i // per_b, 0, i % per_b, 0))
    return pl.pallas_call(
        _merge_kernel,
        grid=(rows // tm,),
        in_specs=[row(A_W)] + [paired] * 6 + [row(C_W), row(d),
                  pl.BlockSpec((1, 6, d), lambda i: (i // per_b, 0, 0)),
                  pl.BlockSpec((1, d), lambda i: (0, 0)),
                  _resident(w_gate), _resident(wa), _resident(wb), _resident(wc), _resident(wo)],
        out_specs=row(d),
        out_shape=jax.ShapeDtypeStruct((rows, d), F32),
        compiler_params=_params("parallel"),
        name="merge_outproj",
    )(oa, *obs, *lses, oc, x2d, mod3, g, w_gate, wa, wb, wc, wo)


MXU_TILE = 256


def _ffn_kernel(x_ref, mod_ref, g_ref, wg_ref, wu_ref, wd_ref, o_ref, *, bounds):
    x = x_ref[...]
    m = mod_ref[0]
    h = _rms_mod(x, g_ref[...], m[4:5], m[3:4]).astype(BF16)
    acc = None
    for lo, hi in bounds:
        act = (_silu(_dot(h, wg_ref[:, lo:hi])) * _dot(h, wu_ref[:, lo:hi])).astype(BF16)
        part = _dot(act, wd_ref[lo:hi, :])
        acc = part if acc is None else acc + part
    o_ref[...] = x + m[5:6] * acc


def _ffn(x2d, mod3, g, wg, wu, wd, *, seq, tm):
    rows, d = x2d.shape
    dff = wg.shape[1]
    half = (dff // MXU_TILE + 1) // 2 * MXU_TILE
    per_b = seq // tm
    return pl.pallas_call(
        functools.partial(_ffn_kernel, bounds=((0, half), (half, dff))),
        grid=(rows // tm,),
        in_specs=[
            pl.BlockSpec((tm, d), lambda i: (i, 0)),
            pl.BlockSpec((1, 6, d), lambda i: (i // per_b, 0, 0)),
            pl.BlockSpec((1, d), lambda i: (0, 0)),
            _resident(wg), _resident(wu), _resident(wd),
        ],
        out_specs=pl.BlockSpec((tm, d), lambda i: (i, 0)),
        out_shape=jax.ShapeDtypeStruct((rows, d), F32),
        compiler_params=_params("parallel"),
        name="ffn_dense",
    )(x2d, mod3, g, wg, wu, wd)


def _route_kernel(x_ref, mod_ref, g_ref, wr_ref, tri_ref, h_ref, ridx_ref, rw_ref, cnt_ref, carry_scr):
    i = pl.program_id(0)

    @pl.when(i == 0)
    def _():
        carry_scr[...] = jnp.zeros(carry_scr.shape, F32)

    m = mod_ref[0]
    h = _rms_mod(x_ref[...], g_ref[...], m[4:5], m[3:4])
    _store_row_tiles(h_ref, h)
    logits = jnp.dot(h, wr_ref[...], preferred_element_type=F32, precision=lax.Precision.HIGHEST)
    idx = lax.broadcasted_iota(jnp.int32, logits.shape, 1)
    n = logits.shape[1]
    m1 = jnp.max(logits, axis=-1, keepdims=True)
    i1 = jnp.min(jnp.where(logits == m1, idx, n), axis=-1, keepdims=True)
    first = idx == i1
    rest = jnp.where(first, -jnp.inf, logits)
    m2 = jnp.max(rest, axis=-1, keepdims=True)
    i2 = jnp.min(jnp.where(rest == m2, idx, n), axis=-1, keepdims=True)
    second = idx == i2
    e = jnp.exp(m2 - m1)
    den = 1.0 + e
    onehot = jnp.where(first | second, 1.0, 0.0)
    before = _dot(tri_ref[...], onehot.astype(BF16)) - onehot + carry_scr[...]
    rank1 = jnp.sum(jnp.where(first, before, 0.0), axis=-1, keepdims=True).astype(jnp.int32)
    rank2 = jnp.sum(jnp.where(second, before, 0.0), axis=-1, keepdims=True).astype(jnp.int32)
    carry_scr[...] += jnp.sum(onehot, axis=0, keepdims=True)
    ridx_ref[...] = jnp.where(idx == 0, i1, jnp.where(idx == 1, i2, jnp.where(idx == 2, rank1,
                              jnp.where(idx == 3, rank2, 0))))
    rw_ref[...] = jnp.where(idx == 0, 1.0 / den, jnp.where(idx == 1, e / den, 0.0))
    cnt_ref[...] = carry_scr[...]


def _route(x2d, mod3, g, wr, *, seq, tm):
    rows, d = x2d.shape
    ne = wr.shape[1]
    per_b = seq // tm
    tri = (jnp.arange(tm)[:, None] >= jnp.arange(tm)[None, :]).astype(BF16)
    row = lambda w: pl.BlockSpec((tm, w), lambda i: (i, 0))
    return pl.pallas_call(
        _route_kernel,
        grid=(rows // tm,),
        in_specs=[row(d), pl.BlockSpec((1, 6, d), lambda i: (i // per_b, 0, 0)),
                  pl.BlockSpec((1, d), lambda i: (0, 0)), pl.BlockSpec((d, ne), lambda i: (0, 0)),
                  pl.BlockSpec((tm, tm), lambda i: (0, 0))],
        out_specs=[pl.BlockSpec((tm, d // LANES, LANES), lambda i: (i, 0, 0)), row(ne), row(ne),
                   pl.BlockSpec((1, ne), lambda i: (0, 0))],
        out_shape=[jax.ShapeDtypeStruct((rows, d // LANES, LANES), F32),
                   jax.ShapeDtypeStruct((rows, ne), jnp.int32),
                   jax.ShapeDtypeStruct((rows, ne), F32), jax.ShapeDtypeStruct((1, ne), F32)],
        scratch_shapes=[pltpu.VMEM((1, ne), F32)],
        compiler_params=_params("arbitrary"),
        name="moe_route",
    )(x2d, mod3, g, wr, tri)


def _store_row_tiles(ref, val):
    for c in range(ref.shape[-2]):
        ref[:, c, :] = val[:, c * LANES:(c + 1) * LANES]


ROW_PITCH = D_MODEL // LANES + 1


ROW_TILE = D_MODEL // LANES


def _gather_rows(idx_of, n, src_hbm, dst, sem):
    for j in range(n):
        start = pl.multiple_of(idx_of(j) * ROW_TILE, ROW_TILE)
        pltpu.make_async_copy(src_hbm.at[pl.ds(start, ROW_TILE)], dst.at[pl.ds(j * ROW_PITCH, ROW_TILE)],
                              sem).start()


def _gather_wait(n, src_hbm, dst, sem):
    pltpu.make_async_copy(src_hbm.at[pl.ds(0, n * ROW_TILE)], dst.at[pl.ds(0, n * ROW_TILE)], sem).wait()


def _load_gathered(buf, n):
    return jnp.concatenate([buf[pl.ds(c, n, stride=ROW_PITCH), :] for c in range(ROW_TILE)], axis=1)


def _expert_kernel(te_ref, src_ref, nu_ref, h_hbm, wg_ref, wu_ref, wd_ref, y_ref,
                   xbuf, xb_scr, acc_scr, sem, *, nf, r):
    i = pl.program_id(0)
    f = pl.program_id(1)
    active = i < nu_ref[0]
    slot = i % 2

    def gather(tile, s):
        _gather_rows(lambda j: src_ref[tile * r + j] >> 1, r, h_hbm, xbuf.at[s], sem.at[s])

    @pl.when((i == 0) & (f == 0))
    def _():
        gather(0, 0)

    @pl.when(active & (f == 0))
    def _():
        _gather_wait(r, h_hbm, xbuf.at[slot], sem.at[slot])

        @pl.when(i + 1 < nu_ref[0])
        def _():
            gather(i + 1, 1 - slot)

        xb_scr[...] = _load_gathered(xbuf.at[slot], r).astype(BF16)
        acc_scr[...] = jnp.zeros(acc_scr.shape, F32)

    @pl.when(active)
    def _():
        xb = xb_scr[...]
        act = (_silu(_dot(xb, wg_ref[0])) * _dot(xb, wu_ref[0])).astype(BF16)
        acc_scr[...] += _dot(act, wd_ref[0])

    @pl.when(f == nf - 1)
    def _():
        _store_row_tiles(y_ref, jnp.where(active, acc_scr[...], 0.0))


def _experts(h2, tile_expert, src_pair, n_used, wg, wu, wd, *, r, tf):
    ne, d, dff = wg.shape
    nf = dff // tf
    p_rows = src_pair.shape[0]
    nt = p_rows // r
    fsel = lambda i, f, nu: jnp.where(i < nu[0], f, nf - 1)
    grid_spec = pltpu.PrefetchScalarGridSpec(
        num_scalar_prefetch=3,
        grid=(nt, nf),
        in_specs=[
            pl.BlockSpec(memory_space=pl.ANY),
            pl.BlockSpec((1, d, tf), lambda i, f, te, sp, nu: (te[i], 0, fsel(i, f, nu))),
            pl.BlockSpec((1, d, tf), lambda i, f, te, sp, nu: (te[i], 0, fsel(i, f, nu))),
            pl.BlockSpec((1, tf, d), lambda i, f, te, sp, nu: (te[i], fsel(i, f, nu), 0)),
        ],
        out_specs=pl.BlockSpec((r, d // LANES, LANES), lambda i, f, te, sp, nu: (i, 0, 0)),
        scratch_shapes=[pltpu.VMEM((2, r * ROW_PITCH, LANES), F32), pltpu.VMEM((r, d), BF16),
                        pltpu.VMEM((r, d), F32), pltpu.SemaphoreType.DMA((2,))],
    )
    return pl.pallas_call(
        functools.partial(_expert_kernel, nf=nf, r=r),
        grid_spec=grid_spec,
        out_shape=jax.ShapeDtypeStruct((p_rows, d // LANES, LANES), F32),
        compiler_params=_params("arbitrary", "arbitrary"),
        name="moe_experts",
    )(tile_expert, src_pair, n_used, h2, wg, wu, wd)


def _combine_kernel(pos_ref, y_hbm, x_ref, rw_ref, mod_ref, fg_ref, o_ref, ybuf, sem, *, tm, final):
    i = pl.program_id(0)
    n = pl.num_programs(0)
    slot = i % 2

    def gather(tile, s):
        for k in range(2):
            _gather_rows(lambda j: pos_ref[(tile * tm + j) * 2 + k], tm, y_hbm, ybuf.at[s, k], sem.at[s])

    @pl.when(i == 0)
    def _():
        gather(0, 0)

    for k in range(2):
        _gather_wait(tm, y_hbm, ybuf.at[slot, k], sem.at[slot])

    @pl.when(i + 1 < n)
    def _():
        gather(i + 1, 1 - slot)

    w = rw_ref[...]
    f = w[:, 0:1] * _load_gathered(ybuf.at[slot, 0], tm) + w[:, 1:2] * _load_gathered(ybuf.at[slot, 1], tm)
    x = x_ref[...] + mod_ref[0][5:6] * f
    if final:
        x = (x * lax.rsqrt(jnp.mean(x * x, axis=-1, keepdims=True) + EPS)) * fg_ref[...]
    o_ref[...] = x


def _combine(pos, y, x2d, rw, mod3, final_g, *, seq, tm, final):
    rows, d = x2d.shape
    per_b = seq // tm
    grid_spec = pltpu.PrefetchScalarGridSpec(
        num_scalar_prefetch=1,
        grid=(rows // tm,),
        in_specs=[
            pl.BlockSpec(memory_space=pl.ANY),
            pl.BlockSpec((tm, d), lambda i, p: (i, 0)),
            pl.BlockSpec((tm, rw.shape[1]), lambda i, p: (i, 0)),
            pl.BlockSpec((1, 6, d), lambda i, p: (i // per_b, 0, 0)),
            pl.BlockSpec((1, d), lambda i, p: (0, 0)),
        ],
        out_specs=pl.BlockSpec((tm, d), lambda i, p: (i, 0)),
        scratch_shapes=[pltpu.VMEM((2, 2, tm * ROW_PITCH, LANES), F32), pltpu.SemaphoreType.DMA((2,))],
    )
    return pl.pallas_call(
        functools.partial(_combine_kernel, tm=tm, final=final),
        grid_spec=grid_spec,
        out_shape=jax.ShapeDtypeStruct((rows, d), F32),
        compiler_params=_params("arbitrary"),
        name="moe_combine",
    )(pos, y, x2d, rw, mod3, final_g)


def _moe(x2d, mod3, g, wr, wg, wu, wd, final_g, *, seq, final):
    rows, d = x2d.shape
    ne = wr.shape[1]
    r = 512
    h2, ridx, rw, cnt = _route(x2d, mod3, g, wr, seq=seq, tm=1024)
    counts = cnt[0].astype(jnp.int32)
    tiles_e = (counts + r - 1) // r
    tile_end = jnp.cumsum(tiles_e)
    start = (tile_end - tiles_e) * r
    pos = jnp.take(start, ridx[:, 0:2], axis=0) + ridx[:, 2:4]
    p_rows = 2 * rows + ne * r
    nt = p_rows // r
    pair_id = jnp.arange(2 * rows, dtype=jnp.int32)
    src_pair = jnp.zeros((p_rows,), jnp.int32).at[pos.reshape(-1)].set(pair_id)
    n_used = tile_end[-1:]
    tile_idx = jnp.minimum(jnp.arange(nt), n_used[0] - 1)
    tile_expert = jnp.sum(tile_idx[:, None] >= tile_end[None, :], axis=1)
    y = _experts(h2.reshape(-1, LANES), tile_expert.astype(jnp.int32), src_pair, n_used.astype(jnp.int32),
                 wg, wu, wd, r=r, tf=1792)
    return _combine(pos.reshape(-1).astype(jnp.int32), y.reshape(-1, LANES), x2d, rw, mod3, final_g,
                    seq=seq, tm=256, final=final)


def _final_norm_kernel(x_ref, g_ref, o_ref):
    x = x_ref[...]
    o_ref[...] = (x * lax.rsqrt(jnp.mean(x * x, axis=-1, keepdims=True) + EPS)) * g_ref[...]


def _final_norm(x2d, g, *, tm):
    rows, d = x2d.shape
    return pl.pallas_call(
        _final_norm_kernel,
        grid=(rows // tm,),
        in_specs=[pl.BlockSpec((tm, d), lambda i: (i, 0)), pl.BlockSpec((1, d), lambda i: (0, 0))],
        out_specs=pl.BlockSpec((tm, d), lambda i: (i, 0)),
        out_shape=jax.ShapeDtypeStruct((rows, d), F32),
        compiler_params=_params("parallel"),
        name="final_norm",
    )(x2d, g)


def _t5_bucket(dist):
    n = jnp.maximum(dist, 0)
    max_exact = N_BUCKETS // 2
    nf = jnp.maximum(n, 1).astype(F32)
    large = max_exact + (jnp.log(nf / max_exact) / math.log(REL_MAX_DIST / max_exact)
                         * (N_BUCKETS - max_exact)).astype(jnp.int32)
    large = jnp.minimum(large, N_BUCKETS - 1)
    return jnp.where(n < max_exact, n, large)


def _bias_tiles_a(rel_bias, seq, t):
    nq = seq // t
    ncol = 2 * A_HEADS
    tab = rel_bias[:, :ncol][_t5_bucket(jnp.arange(seq))].astype(F32).T
    vneg = jnp.full((ncol, t), NEG, F32)
    v = jnp.concatenate([vneg, tab], axis=1)
    u = jnp.concatenate([v[:, 1:seq + 1][:, ::-1], vneg[:, :1], v[:, seq + 1:seq + t][:, ::-1]], axis=1)

    def toeplitz_kernel(u_ref, o_ref):
        x = jnp.broadcast_to(u_ref[0], (t, seq + t))
        r = pltpu.roll(x, 0, 1, stride=1, stride_axis=0)
        for delta in range(nq):
            c0 = (nq - 1 - delta) * t
            o_ref[0, delta] = r[:, c0:c0 + t]

    return pl.pallas_call(
        toeplitz_kernel,
        grid=(ncol,),
        in_specs=[pl.BlockSpec((1, 1, seq + t), lambda c: (c, 0, 0))],
        out_specs=pl.BlockSpec((1, nq, t, t), lambda c: (c // 2, 0, c % 2, 0)),
        out_shape=jax.ShapeDtypeStruct((A_HEADS, nq, 2 * t, t), F32),
        compiler_params=_params("parallel"),
        name="bias_tiles_diff",
    )(u.reshape(ncol, 1, seq + t))


def _bias_tiles_b(rel_bias):
    ng = len(B_GROUPS)
    period = 3 * BLOCK
    rows = []
    for g, (win, dil) in enumerate(B_GROUPS):
        n_back = win // dil
        tab = rel_bias[:, 2 * A_HEADS + g * B_HEADS:2 * A_HEADS + (g + 1) * B_HEADS]
        vals = tab[_t5_bucket(jnp.arange(n_back, -1, -1) * dil)].astype(F32).T
        rows.append(jnp.concatenate([vals, jnp.full((B_HEADS, period - n_back - 1), NEG, F32)], axis=1))
    u = jnp.concatenate(rows, axis=0)

    def toeplitz_kernel(u_ref, o_ref):
        x = jnp.broadcast_to(u_ref[0], (BLOCK, period))
        r = pltpu.roll(x, 0, 1, stride=1, stride_axis=0)[:, :2 * BLOCK]
        col = lax.broadcasted_iota(jnp.int32, r.shape, 1)
        o_ref[0, 0, 0] = jnp.where(col >= BLOCK, r, NEG)
        o_ref[0, 1, 0] = r

    return pl.pallas_call(
        toeplitz_kernel,
        grid=(ng * B_HEADS,),
        in_specs=[pl.BlockSpec((1, 1, period), lambda c: (c, 0, 0))],
        out_specs=pl.BlockSpec((1, 2, 1, BLOCK, 2 * BLOCK),
                               lambda c: (c // B_HEADS, 0, (c % B_HEADS) // 2, c % 2, 0)),
        out_shape=jax.ShapeDtypeStruct((ng, 2, B_HEADS // 2, 2 * BLOCK, 2 * BLOCK), F32),
        compiler_params=_params("parallel"),
        name="bias_tiles_dilated",
    )(u.reshape(ng * B_HEADS, 1, period))


def kernel(x, c, norm_mix_g, norm_ffn_g, w_mod, b_mod, w_in, b_forget, lam_q1, lam_k1, lam_q2, lam_k2,
           subln_g, rel_bias, w_br_a, w_br_b, w_br_c, w_out, w_ff_gate, w_ff_up, w_ff_down, w_router,
           w_exp_gate, w_exp_up, w_exp_down, final_norm_g):
    bsz, seq, d = x.shape
    depth = w_mod.shape[0]
    rows = bsz * seq
    t_attn = 256
    x2d = x.reshape(rows, d)

    mod = _modulation(c, w_mod, b_mod)
    bias_a = _bias_tiles_a(rel_bias, seq, t_attn)
    bias_b = _bias_tiles_b(rel_bias)
    w_qkv_all, w_gf_all = _prep_w_in(w_in)

    final_g = final_norm_g.reshape(1, d)
    fused_final = False
    for l in range(depth):
        lam_init = 0.8 - 0.6 * math.exp(-0.3 * l)
        mod3 = mod[l].reshape(bsz, 6, d)
        w_qkv, w_gate, w_f = w_qkv_all[l], w_gf_all[l, :, :GATE_W], w_gf_all[l, :, GATE_W:]

        g_mix = norm_mix_g[l].reshape(1, d)
        qkv_ac, qkv_b0, qkv_b1, qkv_b2, f_logit = _qkv_proj(x2d, mod3, g_mix, w_qkv, w_f,
                                                            bsz=bsz, seq=seq, tm=512)
        qkv_ac = qkv_ac.reshape(bsz, seq, -1)

        b_f8 = jnp.pad(b_forget[l], (0, 8 - C_HEADS)).reshape(8, 1)
        fcum = _forget_cumsum(f_logit.reshape(bsz, seq, LANES), b_f8, col_block=0)
        fcum = fcum[:, :C_HEADS].reshape(bsz, C_HEADS // 2, 2, seq)

        oa = _attn_a(qkv_ac, bias_a, lam_q1[l].reshape(1, -1), lam_k1[l].reshape(1, -1),
                     lam_q2[l].reshape(1, -1), lam_k2[l].reshape(1, -1), subln_g[l].reshape(1, -1),
                     lam_init=lam_init, t=t_attn, nsb=4, nbb=2)
        oc = _attn_c(qkv_ac, fcum, t=t_attn, nsb=4, nbb=2)
        obs, lses = [], []
        groups = (qkv_b0.reshape(bsz, 1, seq, 3 * B_W), qkv_b1, qkv_b2)
        for g, (qkv_g, nb) in enumerate(zip(groups, (4, 1, 1))):
            o_g, lse_g = _attn_b_group(qkv_g, bias_b[g], g, nb)
            obs.append(o_g)
            lses.append(lse_g)

        x2d = _merge(oa.reshape(rows, A_W), obs, lses, oc.reshape(rows, C_W), x2d, mod3, g_mix, w_gate,
                     w_br_a[l].astype(BF16), w_br_b[l].astype(BF16), w_br_c[l].astype(BF16),
                     w_out[l].astype(BF16), seq=seq, tm=512)

        g_ffn = norm_ffn_g[l].reshape(1, d)
        if l % 2 == 0:
            j = l // 2
            x2d = _ffn(x2d, mod3, g_ffn, w_ff_gate[j].astype(BF16), w_ff_up[j].astype(BF16),
                       w_ff_down[j].astype(BF16), seq=seq, tm=512)
        else:
            j = l // 2
            fused_final = l == depth - 1
            x2d = _moe(x2d, mod3, g_ffn, w_router[j], w_exp_gate[j].astype(BF16),
                       w_exp_up[j].astype(BF16), w_exp_down[j].astype(BF16), final_g,
                       seq=seq, final=fused_final)

    if not fused_final:
        x2d = _final_norm(x2d, final_g, tm=1024)
    return x2d.reshape(bsz, seq, d)
```

```python
import functools
import math

import jax
import jax.numpy as jnp
from jax import lax
from jax.experimental import pallas as pl
from jax.experimental.pallas import tpu as pltpu

F32 = jnp.float32
BF16 = jnp.bfloat16

D_MODEL = 1024
HEAD_DIM = 64
LANES = 128
A_HEADS = 4
A_W = A_HEADS * 2 * HEAD_DIM
B_GROUPS = ((128, 1), (512, 4), (2048, 16))
B_HEADS = 6
B_W = B_HEADS * HEAD_DIM
B_QW = len(B_GROUPS) * B_W
C_HEADS = 6
C_W = C_HEADS * HEAD_DIM
N_BRANCH = 3
BLOCK = 128
N_BUCKETS = 32
REL_MAX_DIST = 2048
N_EXPERTS = 8
EPS = 1e-6
QKV_W = 3 * A_W + 3 * B_QW + 3 * C_W
GATE_W = N_BRANCH * D_MODEL
NEG = -1e30
SCALE = HEAD_DIM ** -0.5
VMEM_LIMIT = 56 * 1024 * 1024


def _params(*sem):
    return pltpu.CompilerParams(dimension_semantics=sem, vmem_limit_bytes=VMEM_LIMIT)


def _resident(a):
    return pl.BlockSpec(a.shape, lambda *_: (0, 0), pipeline_mode=pl.Buffered(1))


def _rms_mod(x, g, sc, sh):
    y = x * lax.rsqrt(jnp.mean(x * x, axis=-1, keepdims=True) + EPS)
    return (y * g) * (1.0 + sc) + sh


def _dot(a, b):
    return jnp.dot(a, b, preferred_element_type=F32)


def _dot_nt(a, b):
    return lax.dot_general(a, b, (((1,), (1,)), ((), ())), preferred_element_type=F32)


def _silu(a):
    return a * jax.nn.sigmoid(a)


def _mod_kernel(c_ref, w_ref, b_ref, o_ref):
    a = _silu(c_ref[...]).astype(BF16)
    o_ref[0] = _dot(a, w_ref[0].astype(BF16)) + b_ref[0]


def _modulation(c, w_mod, b_mod):
    depth, d, n = w_mod.shape
    bsz = c.shape[0]
    tn = 1536
    return pl.pallas_call(
        _mod_kernel,
        grid=(depth, n // tn),
        in_specs=[
            pl.BlockSpec((bsz, d), lambda l, j: (0, 0)),
            pl.BlockSpec((1, d, tn), lambda l, j: (l, 0, j)),
            pl.BlockSpec((1, 1, tn), lambda l, j: (l, 0, j)),
        ],
        out_specs=pl.BlockSpec((1, bsz, tn), lambda l, j: (l, 0, j)),
        out_shape=jax.ShapeDtypeStruct((depth, bsz, n), F32),
        compiler_params=_params("parallel", "parallel"),
        name="modulation",
    )(c, w_mod, b_mod.reshape(depth, 1, n))


_A_BLOCKS = 3 * A_W // B_W
_B_BLOCKS = 3 * B_QW // B_W
_QKV_SRC_BLOCKS = (list(range(_A_BLOCKS))
                   + list(range(_A_BLOCKS + _B_BLOCKS, QKV_W // B_W))
                   + [_A_BLOCKS + s * len(B_GROUPS) + g for g in range(len(B_GROUPS)) for s in range(3)])


def _prep_w_in_kernel(*refs):
    n = len(_QKV_SRC_BLOCKS)
    piece_refs, x_ref, y_ref, qkv_ref, gf_ref = refs[:n], refs[n], refs[n + 1], refs[n + 2], refs[n + 3]
    for j, ref in enumerate(piece_refs):
        qkv_ref[0, :, j * B_W:(j + 1) * B_W] = ref[0].astype(BF16)
    x = x_ref[0]
    rolled = pltpu.roll(x, GATE_W - C_HEADS, 1)
    tail = pltpu.roll(y_ref[0], LANES - C_HEADS, 1)
    lane = lax.broadcasted_iota(jnp.int32, tail.shape, 1)
    gf_ref[0, :, :GATE_W - LANES] = rolled[:, :GATE_W - LANES].astype(BF16)
    gf_ref[0, :, GATE_W - LANES:GATE_W] = jnp.where(lane < LANES - C_HEADS, rolled[:, GATE_W - LANES:],
                                                     tail).astype(BF16)
    gf_ref[0, :, GATE_W:] = jnp.where(lane < C_HEADS, x[:, :LANES], 0.0).astype(BF16)


def _prep_w_in(w_in):
    depth, d, _ = w_in.shape
    tr = 256
    piece = lambda c: pl.BlockSpec((1, tr, B_W), lambda l, i: (l, i, c))
    return pl.pallas_call(
        _prep_w_in_kernel,
        grid=(depth, d // tr),
        in_specs=[piece(c) for c in _QKV_SRC_BLOCKS]
        + [pl.BlockSpec((1, tr, GATE_W), lambda l, i: (l, i, QKV_W // GATE_W)),
           pl.BlockSpec((1, tr, LANES), lambda l, i: (l, i, (QKV_W + GATE_W) // LANES))],
        out_specs=[pl.BlockSpec((1, tr, QKV_W), lambda l, i: (l, i, 0)),
                   pl.BlockSpec((1, tr, GATE_W + LANES), lambda l, i: (l, i, 0))],
        out_shape=[jax.ShapeDtypeStruct((depth, d, QKV_W), BF16),
                   jax.ShapeDtypeStruct((depth, d, GATE_W + LANES), BF16)],
        compiler_params=_params("parallel", "parallel"),
        name="prep_w_in",
    )(*([w_in] * (len(_QKV_SRC_BLOCKS) + 2)))


def _qkv_kernel(x_ref, mod_ref, g_ref, w_ref, wf_ref, ac_ref, b0_ref, b1_ref, b2_ref, f_ref, h_scr, *, tm):
    m = mod_ref[0]
    h = _rms_mod(x_ref[...], g_ref[...], m[1:2], m[0:1])
    nc = h_scr.shape[0]
    for c in range(nc):
        h_scr[c] = h[:, c * LANES:(c + 1) * LANES]
    hb = h.astype(BF16)
    n_ac = ac_ref.shape[1]
    n_b = b0_ref.shape[1]
    f_ref[...] = _dot(hb, wf_ref[...])
    ac_ref[...] = _dot(hb, w_ref[:, 0:n_ac]).astype(BF16)
    b0_ref[...] = _dot(hb, w_ref[:, n_ac:n_ac + n_b]).astype(BF16)
    for gi, ref in ((1, b1_ref), (2, b2_ref)):
        dil = B_GROUPS[gi][1]
        per = tm // dil
        hp = jnp.concatenate(
            [jnp.concatenate([h_scr[c, pl.ds(r, per, stride=dil), :] for c in range(nc)], axis=1).astype(BF16)
             for r in range(dil)], axis=0)
        y = _dot(hp, w_ref[:, n_ac + gi * n_b:n_ac + (gi + 1) * n_b]).astype(BF16)
        for r in range(dil):
            ref[0, r] = y[r * per:(r + 1) * per]


def _qkv_proj(x2d, mod3, g, w, w_f, *, bsz, seq, tm):
    rows, d = x2d.shape
    per_b = seq // tm
    n_b = 3 * B_W
    n_ac = w.shape[1] - 3 * n_b
    dil1, dil2 = B_GROUPS[1][1], B_GROUPS[2][1]
    strided = lambda dil: pl.BlockSpec((1, dil, tm // dil, n_b), lambda i: (i // per_b, 0, i % per_b, 0))
    return pl.pallas_call(
        functools.partial(_qkv_kernel, tm=tm),
        grid=(rows // tm,),
        in_specs=[
            pl.BlockSpec((tm, d), lambda i: (i, 0)),
            pl.BlockSpec((1, 6, d), lambda i: (i // per_b, 0, 0)),
            pl.BlockSpec((1, d), lambda i: (0, 0)),
            _resident(w), _resident(w_f),
        ],
        out_specs=[pl.BlockSpec((tm, n_ac), lambda i: (i, 0)), pl.BlockSpec((tm, n_b), lambda i: (i, 0)),
                   strided(dil1), strided(dil2), pl.BlockSpec((tm, LANES), lambda i: (i, 0))],
        out_shape=[jax.ShapeDtypeStruct((rows, n_ac), BF16), jax.ShapeDtypeStruct((rows, n_b), BF16),
                   jax.ShapeDtypeStruct((bsz, dil1, seq // dil1, n_b), BF16),
                   jax.ShapeDtypeStruct((bsz, dil2, seq // dil2, n_b), BF16),
                   jax.ShapeDtypeStruct((rows, LANES), F32)],
        scratch_shapes=[pltpu.VMEM((d // LANES, tm, LANES), F32)],
        compiler_params=_params("parallel"),
        name="qkv_proj",
    )(x2d, mod3, g, w, w_f)


def _fcum_kernel(f_ref, b_ref, o_ref):
    z = f_ref[0].T[:8] + b_ref[...]
    x = jnp.minimum(z, 0.0) - jnp.log1p(jnp.exp(-jnp.abs(z)))
    s = x.shape[1]
    lane = lax.broadcasted_iota(jnp.int32, x.shape, 1)
    k = 1
    while k < s:
        x = x + jnp.where(lane >= k, pltpu.roll(x, k, 1), 0.0)
        k *= 2
    o_ref[0] = x


def _forget_cumsum(gf, b_f8, *, col_block):
    bsz, seq, _ = gf.shape
    return pl.pallas_call(
        _fcum_kernel,
        grid=(bsz,),
        in_specs=[
            pl.BlockSpec((1, seq, LANES), lambda b: (b, 0, col_block)),
            pl.BlockSpec((8, 1), lambda b: (0, 0)),
        ],
        out_specs=pl.BlockSpec((1, 8, seq), lambda b: (b, 0, 0)),
        out_shape=jax.ShapeDtypeStruct((bsz, 8, seq), F32),
        compiler_params=_params("parallel"),
        name="forget_cumsum",
    )(gf, b_f8)


def _half_masks(q):
    lane = lax.broadcasted_iota(jnp.int32, q.shape, 1)
    zero = jnp.zeros_like(q)
    return jnp.where(lane < HEAD_DIM, q, zero), jnp.where(lane >= HEAD_DIM, q, zero)


def _flash_init(first, v_ref, vext_scr, m_scr, acc_scr):
    @pl.when(first)
    def _():
        for bb in range(vext_scr.shape[0]):
            vext_scr[bb, :, :LANES] = v_ref[bb]
            vext_scr[bb, :, LANES:] = jnp.ones((vext_scr.shape[1], LANES), BF16)

    m_scr[...] = jnp.full(m_scr.shape, -jnp.inf, F32)
    acc_scr[...] = jnp.zeros(acc_scr.shape, F32)


def _lane_tile(a, n):
    return a if n == 1 else jnp.concatenate([a] * n, axis=1)


def _flash_update(s, vext, rows, m_scr, acc_scr):
    m_prev = m_scr[rows]
    m_new = jnp.maximum(m_prev, jnp.max(s, axis=-1, keepdims=True))
    alpha = jnp.exp(m_prev - m_new)
    p = jnp.exp(s - _lane_tile(m_new, s.shape[1] // LANES))
    acc_scr[rows] = _lane_tile(alpha, 2) * acc_scr[rows] + _dot(p.astype(BF16), vext)
    m_scr[rows] = m_new


def _flash_result(acc_scr):
    acc = acc_scr[...]
    return acc[:, :LANES] / acc[:, LANES:]


def _attn_a_kernel(q_ref, k_ref, v_ref, bias_ref, lq1, lk1, lq2, lk2, sg_ref, o_ref,
                   vext_scr, m_scr, acc_scr, *, t, nsb, nbb, lam_init):
    qi = pl.program_id(2)
    _flash_init(qi == 0, v_ref, vext_scr, m_scr, acc_scr)
    qh = [[_half_masks(q_ref[bb, sb * t:(sb + 1) * t, :] * SCALE) for sb in range(nsb)] for bb in range(nbb)]
    chain_rows = lambda bb, sb, hh: pl.ds(((bb * nsb + sb) * 2 + hh) * t, t)

    def step(kb, nkb, plan):
        off = pl.multiple_of(kb * t, t)
        for bb in range(nbb):
            kblk = k_ref[bb, pl.ds(off, nkb * t), :]
            vext = vext_scr[bb, pl.ds(off, nkb * t), :]
            for sb, deltas in plan:
                for hh in range(2):
                    bias = [bias_ref[0, d, hh * t:(hh + 1) * t, :] for d in deltas]
                    s = _dot_nt(qh[bb][sb][hh], kblk) + (bias[0] if nkb == 1 else jnp.concatenate(bias, axis=1))
                    _flash_update(s, vext, chain_rows(bb, sb, hh), m_scr, acc_scr)

    def body(kb2, carry):
        first = [qi * nsb + sb - 2 * kb2 for sb in range(nsb)]
        step(2 * kb2, 2, [(sb, (first[sb], first[sb] - 1)) for sb in range(nsb)])
        return carry

    lax.fori_loop(0, qi * (nsb // 2), body, 0)
    for j in range(0, nsb, 2):
        step(qi * nsb + j, 1, [(j, (0,))])
        step(qi * nsb + j, 2, [(sb, (sb - j, sb - j - 1)) for sb in range(j + 1, nsb)])

    o = _flash_result(acc_scr)
    lam = (jnp.exp(jnp.sum(lq1[...] * lk1[...], axis=-1, keepdims=True))
           - jnp.exp(jnp.sum(lq2[...] * lk2[...], axis=-1, keepdims=True)) + lam_init)
    for bb in range(nbb):
        for sb in range(nsb):
            c0 = (bb * nsb + sb) * 2 * t
            d = o[c0:c0 + t] - lam * o[c0 + t:c0 + 2 * t]
            y = d * lax.rsqrt(jnp.mean(d * d, axis=-1, keepdims=True) + EPS)
            o_ref[bb, sb * t:(sb + 1) * t, :] = ((y * sg_ref[...]) * (1.0 - lam_init)).astype(o_ref.dtype)


def _flash_scratch(seq, t, nsb, nbb):
    chains = 2 * nsb * nbb
    return [pltpu.VMEM((nbb, seq, 2 * LANES), BF16), pltpu.VMEM((chains * t, LANES), F32),
            pltpu.VMEM((chains * t, 2 * LANES), F32)]


def _attn_a(qkv, bias_a, lq1, lk1, lq2, lk2, subln_g, *, lam_init, t, nsb, nbb):
    bsz, seq, _ = qkv.shape
    tq = t * nsb
    assert bsz % nbb == 0 and seq % tq == 0 and nsb % 2 == 0
    vec = lambda n: pl.BlockSpec((1, n), lambda h, b, i: (0, 0))
    return pl.pallas_call(
        functools.partial(_attn_a_kernel, t=t, nsb=nsb, nbb=nbb, lam_init=lam_init),
        grid=(A_HEADS, bsz // nbb, seq // tq),
        in_specs=[
            pl.BlockSpec((nbb, tq, LANES), lambda h, b, i: (b, i, h)),
            pl.BlockSpec((nbb, seq, LANES), lambda h, b, i: (b, 0, A_HEADS + h)),
            pl.BlockSpec((nbb, seq, LANES), lambda h, b, i: (b, 0, 2 * A_HEADS + h)),
            pl.BlockSpec((1, seq // t, 2 * t, t), lambda h, b, i: (h, 0, 0, 0)),
            vec(HEAD_DIM), vec(HEAD_DIM), vec(HEAD_DIM), vec(HEAD_DIM), vec(LANES),
        ],
        out_specs=pl.BlockSpec((nbb, tq, LANES), lambda h, b, i: (b, i, h)),
        out_shape=jax.ShapeDtypeStruct((bsz, seq, A_W), BF16),
        scratch_shapes=_flash_scratch(seq, t, nsb, nbb),
        compiler_params=_params("parallel", "parallel", "arbitrary"),
        name="attn_diff",
    )(qkv, qkv, qkv, bias_a, lq1, lk1, lq2, lk2, subln_g)


def _attn_c_kernel(q_ref, k_ref, v_ref, f_ref, o_ref, vext_scr, m_scr, acc_scr, *, t, nsb, nbb):
    qi = pl.program_id(2)
    _flash_init(qi == 0, v_ref, vext_scr, m_scr, acc_scr)
    qh = [[_half_masks(q_ref[bb, sb * t:(sb + 1) * t, :] * SCALE) for sb in range(nsb)] for bb in range(nbb)]
    chain_rows = lambda bb, sb, hh: pl.ds(((bb * nsb + sb) * 2 + hh) * t, t)
    q_off = pl.multiple_of(qi * (t * nsb), t * nsb)
    f_anchor = [f_ref[bb, 0, :, pl.ds(q_off, LANES)][:, :1] for bb in range(nbb)]
    def causal(nkb):
        r = lax.broadcasted_iota(jnp.int32, (t, nkb * t), 0)
        c = lax.broadcasted_iota(jnp.int32, (t, nkb * t), 1)
        return r + (nkb - 1) * t >= c

    def step(kb, nkb, sbs, diag_sb):
        off = pl.multiple_of(kb * t, t)
        for bb in range(nbb):
            kblk = k_ref[bb, pl.ds(off, nkb * t), :]
            vext = vext_scr[bb, pl.ds(off, nkb * t), :]
            dec = [f_anchor[bb][hh:hh + 1] - f_ref[bb, 0, hh:hh + 1, pl.ds(off, nkb * t)] for hh in range(2)]
            for sb in sbs:
                for hh in range(2):
                    s = _dot_nt(qh[bb][sb][hh], kblk) + dec[hh]
                    if sb == diag_sb:
                        s = jnp.where(causal(nkb), s, NEG)
                    _flash_update(s, vext, chain_rows(bb, sb, hh), m_scr, acc_scr)

    def body(kb2, carry):
        step(2 * kb2, 2, range(nsb), None)
        return carry

    lax.fori_loop(0, qi * (nsb // 2), body, 0)
    for j in range(0, nsb, 2):
        step(qi * nsb + j, 1, [j], j)
        step(qi * nsb + j, 2, range(j + 1, nsb), j + 1)

    o = _flash_result(acc_scr)
    lane = lax.broadcasted_iota(jnp.int32, (t, LANES), 1)
    for bb in range(nbb):
        for sb in range(nsb):
            c0 = (bb * nsb + sb) * 2 * t
            pair = jnp.where(lane < HEAD_DIM, o[c0:c0 + t], o[c0 + t:c0 + 2 * t])
            o_ref[bb, sb * t:(sb + 1) * t, :] = pair.astype(o_ref.dtype)


def _attn_c(qkv, fcum, *, t, nsb, nbb):
    bsz, seq, _ = qkv.shape
    tq = t * nsb
    assert bsz % nbb == 0 and seq % tq == 0 and nsb % 2 == 0
    pairs = C_HEADS // 2
    q0 = 3 * A_W // LANES
    return pl.pallas_call(
        functools.partial(_attn_c_kernel, t=t, nsb=nsb, nbb=nbb),
        grid=(pairs, bsz // nbb, seq // tq),
        in_specs=[
            pl.BlockSpec((nbb, tq, LANES), lambda p, b, i: (b, i, q0 + p)),
            pl.BlockSpec((nbb, seq, LANES), lambda p, b, i: (b, 0, q0 + pairs + p)),
            pl.BlockSpec((nbb, seq, LANES), lambda p, b, i: (b, 0, q0 + 2 * pairs + p)),
            pl.BlockSpec((nbb, 1, 2, seq), lambda p, b, i: (b, p, 0, 0)),
        ],
        out_specs=pl.BlockSpec((nbb, tq, LANES), lambda p, b, i: (b, i, p)),
        out_shape=jax.ShapeDtypeStruct((bsz, seq, C_W), BF16),
        scratch_shapes=_flash_scratch(seq, t, nsb, nbb),
        compiler_params=_params("parallel", "parallel", "arbitrary"),
        name="attn_forget",
    )(qkv, qkv, qkv, fcum)


def _attn_b_kernel(q_ref, kp_ref, kc_ref, vp_ref, vc_ref, bias_ref, o_ref, lse_ref, *, dil, nb):
    n = pl.program_id(1)
    lane = lax.broadcasted_iota(jnp.int32, (BLOCK, LANES), 1)
    first_variant = jnp.minimum(n, 1)

    def residue(r):
        for hp in range(B_HEADS // 2):
            cols = slice(hp * LANES, (hp + 1) * LANES)
            kcat = jnp.concatenate([kp_ref[0, r, :, cols], kc_ref[0, r, :, cols]], axis=0)
            vcat = jnp.concatenate([vp_ref[0, r, :, cols], vc_ref[0, r, :, cols]], axis=0)
            for jb in range(nb):
                qh = _half_masks(q_ref[0, r, jb * BLOCK:(jb + 1) * BLOCK, cols] * SCALE)
                kwin = kcat[jb * BLOCK:(jb + 2) * BLOCK]
                vwin = vcat[jb * BLOCK:(jb + 2) * BLOCK]
                variant = first_variant if jb == 0 else 1
                outs, lses = [], []
                for hh in range(2):
                    s = _dot_nt(qh[hh], kwin) + bias_ref[variant, hp, hh * BLOCK:(hh + 1) * BLOCK, :]
                    m = jnp.max(s, axis=-1, keepdims=True)
                    e = jnp.exp(s - m)
                    den = jnp.sum(e, axis=-1, keepdims=True)
                    outs.append(_dot(e.astype(BF16), vwin) / den)
                    lses.append(jnp.broadcast_to(m + jnp.log(den), (BLOCK, LANES)))
                if dil == 1:
                    rows = pl.ds(jb * BLOCK, BLOCK)
                else:
                    rows = pl.ds(jb * BLOCK * dil + r, BLOCK, stride=dil)
                o_ref[0, hp, rows, :] = jnp.where(lane < HEAD_DIM, outs[0], outs[1])
                lse_ref[0, hp, rows, :] = jnp.where(lane < HEAD_DIM, lses[0], lses[1])

    unroll = min(dil, 4)
    if dil == unroll:
        for r in range(dil):
            residue(r)
    else:
        def body(i, carry):
            for j in range(unroll):
                residue(i * unroll + j)
            return carry

        lax.fori_loop(0, dil // unroll, body, 0)


def _attn_b_group(qkv_g, bias_g, g, nb):
    bsz, dil, m_len, _ = qkv_g.shape
    tb = BLOCK * nb
    cur = lambda c: pl.BlockSpec((1, dil, tb, B_W), lambda b, n: (b, 0, n, c))
    prev = lambda c: pl.BlockSpec((1, dil, BLOCK, B_W), lambda b, n: (b, 0, jnp.maximum(n * nb - 1, 0), c))
    pairs = B_HEADS // 2
    out_spec = pl.BlockSpec((1, pairs, tb * dil, LANES), lambda b, n: (b, 0, n, 0))
    out_sds = jax.ShapeDtypeStruct((bsz, pairs, m_len * dil, LANES), F32)
    return pl.pallas_call(
        functools.partial(_attn_b_kernel, dil=dil, nb=nb),
        grid=(bsz, m_len // tb),
        in_specs=[cur(0), prev(1), cur(1), prev(2), cur(2),
                  pl.BlockSpec(bias_g.shape, lambda b, n: (0, 0, 0, 0))],
        out_specs=[out_spec, out_spec],
        out_shape=[out_sds, out_sds],
        compiler_params=_params("parallel", "arbitrary"),
        name=f"attn_dilated_g{g}",
    )(qkv_g, qkv_g, qkv_g, qkv_g, qkv_g, bias_g)


def _merge_kernel(oa_ref, ob0, ob1, ob2, ls0, ls1, ls2, oc_ref, x_ref, mod_ref, g_ref,
                  wg_ref, wa_ref, wb_ref, wc_ref, wo_ref, o_ref):
    x = x_ref[...]
    m = mod_ref[0]
    hb = _rms_mod(x, g_ref[...], m[1:2], m[0:1]).astype(BF16)
    parts = []
    for hp in range(B_HEADS // 2):
        l0, l1, l2 = ls0[0, hp], ls1[0, hp], ls2[0, hp]
        mx = jnp.maximum(jnp.maximum(l0, l1), l2)
        e0, e1, e2 = jnp.exp(l0 - mx), jnp.exp(l1 - mx), jnp.exp(l2 - mx)
        den = e0 + e1 + e2
        parts.append((e0 / den) * ob0[0, hp] + (e1 / den) * ob1[0, hp] + (e2 / den) * ob2[0, hp])
    ob = jnp.concatenate(parts, axis=1)
    d = D_MODEL
    gate = lambda k: jax.nn.sigmoid(_dot(hb, wg_ref[:, k * d:(k + 1) * d]))
    merged = (gate(0) * _dot(oa_ref[...], wa_ref[...])
              + gate(1) * _dot(ob.astype(BF16), wb_ref[...])
              + gate(2) * _dot(oc_ref[...], wc_ref[...]))
    y = _dot(merged.astype(BF16), wo_ref[...])
    o_ref[...] = x + m[2:3] * y


def _merge(oa, obs, lses, oc, x2d, mod3, g, w_gate, wa, wb, wc, wo, *, seq, tm):
    rows, d = x2d.shape
    per_b = seq // tm
    row = lambda w: pl.BlockSpec((tm, w), lambda i: (i, 0))
    paired = pl.BlockSpec((1, B_HEADS // 2, tm, LANES), lambda i: (i // per_b, 0, i % per_b, 0))
    return pl.pallas_call(
        _merge_kernel,
        grid=(rows // tm,),
        in_specs=[row(A_W)] + [paired] * 6 + [row(C_W), row(d),
                  pl.BlockSpec((1, 6, d), lambda i: (i // per_b, 0, 0)),
                  pl.BlockSpec((1, d), lambda i: (0, 0)),
                  _resident(w_gate), _resident(wa), _resident(wb), _resident(wc), _resident(wo)],
        out_specs=row(d),
        out_shape=jax.ShapeDtypeStruct((rows, d), F32),
        compiler_params=_params("parallel"),
        name="merge_outproj",
    )(oa, *obs, *lses, oc, x2d, mod3, g, w_gate, wa, wb, wc, wo)


MXU_TILE = 256


def _ffn_kernel(x_ref, mod_ref, g_ref, wg_ref, wu_ref, wd_ref, o_ref, *, bounds):
    x = x_ref[...]
    m = mod_ref[0]
    h = _rms_mod(x, g_ref[...], m[4:5], m[3:4]).astype(BF16)
    acc = None
    for lo, hi in bounds:
        act = (_silu(_dot(h, wg_ref[:, lo:hi])) * _dot(h, wu_ref[:, lo:hi])).astype(BF16)
        part = _dot(act, wd_ref[lo:hi, :])
        acc = part if acc is None else acc + part
    o_ref[...] = x + m[5:6] * acc


def _ffn(x2d, mod3, g, wg, wu, wd, *, seq, tm):
    rows, d = x2d.shape
    dff = wg.shape[1]
    half = (dff // MXU_TILE + 1) // 2 * MXU_TILE
    per_b = seq // tm
    return pl.pallas_call(
        functools.partial(_ffn_kernel, bounds=((0, half), (half, dff))),
        grid=(rows // tm,),
        in_specs=[
            pl.BlockSpec((tm, d), lambda i: (i, 0)),
            pl.BlockSpec((1, 6, d), lambda i: (i // per_b, 0, 0)),
            pl.BlockSpec((1, d), lambda i: (0, 0)),
            _resident(wg), _resident(wu), _resident(wd),
        ],
        out_specs=pl.BlockSpec((tm, d), lambda i: (i, 0)),
        out_shape=jax.ShapeDtypeStruct((rows, d), F32),
        compiler_params=_params("parallel"),
        name="ffn_dense",
    )(x2d, mod3, g, wg, wu, wd)


def _route_kernel(x_ref, mod_ref, g_ref, wr_ref, tri_ref, h_ref, ridx_ref, rw_ref, cnt_ref, carry_scr):
    i = pl.program_id(0)

    @pl.when(i == 0)
    def _():
        carry_scr[...] = jnp.zeros(carry_scr.shape, F32)

    m = mod_ref[0]
    h = _rms_mod(x_ref[...], g_ref[...], m[4:5], m[3:4])
    _store_row_tiles(h_ref, h)
    w = wr_ref[...]
    h_hi, w_hi = h.astype(BF16), w.astype(BF16)
    h_lo = (h - h_hi.astype(F32)).astype(BF16)
    w_lo = (w - w_hi.astype(F32)).astype(BF16)
    logits = _dot_nt(w_hi, h_hi) + (_dot_nt(w_hi, h_lo) + _dot_nt(w_lo, h_hi))
    idx = lax.broadcasted_iota(jnp.int32, logits.shape, 0)
    n = logits.shape[0]
    m1 = jnp.max(logits, axis=0, keepdims=True)
    i1 = jnp.min(jnp.where(logits == m1, idx, n), axis=0, keepdims=True)
    first = idx == i1
    rest = jnp.where(first, -jnp.inf, logits)
    m2 = jnp.max(rest, axis=0, keepdims=True)
    i2 = jnp.min(jnp.where(rest == m2, idx, n), axis=0, keepdims=True)
    second = idx == i2
    e = jnp.exp(m2 - m1)
    den = 1.0 + e
    onehot = jnp.where(first | second, 1.0, 0.0)
    before = _dot(onehot.astype(BF16), tri_ref[...]) - onehot + carry_scr[...]
    rank1 = jnp.sum(jnp.where(first, before, 0.0), axis=0, keepdims=True)
    rank2 = jnp.sum(jnp.where(second, before, 0.0), axis=0, keepdims=True)
    carry_scr[...] += jnp.sum(onehot, axis=1, keepdims=True)
    picks = jnp.where(idx == 0, i1.astype(F32), jnp.where(idx == 1, i2.astype(F32),
                      jnp.where(idx == 2, rank1, jnp.where(idx == 3, rank2, 0.0))))
    weights = jnp.where(idx == 0, 1.0 / den, jnp.where(idx == 1, e / den, 0.0))
    ridx_ref[...] = picks.T.astype(jnp.int32)
    rw_ref[...] = weights.T
    cnt_ref[...] = carry_scr[...]


def _route(x2d, mod3, g, wr, *, seq, tm):
    rows, d = x2d.shape
    ne = wr.shape[1]
    per_b = seq // tm
    tri = (jnp.arange(tm)[:, None] <= jnp.arange(tm)[None, :]).astype(BF16)
    row = lambda w: pl.BlockSpec((tm, w), lambda i: (i, 0))
    return pl.pallas_call(
        _route_kernel,
        grid=(rows // tm,),
        in_specs=[row(d), pl.BlockSpec((1, 6, d), lambda i: (i // per_b, 0, 0)),
                  pl.BlockSpec((1, d), lambda i: (0, 0)), pl.BlockSpec((ne, d), lambda i: (0, 0)),
                  pl.BlockSpec((tm, tm), lambda i: (0, 0))],
        out_specs=[pl.BlockSpec((tm, d // LANES, LANES), lambda i: (i, 0, 0)), row(ne), row(ne),
                   pl.BlockSpec((ne, 1), lambda i: (0, 0))],
        out_shape=[jax.ShapeDtypeStruct((rows, d // LANES, LANES), F32),
                   jax.ShapeDtypeStruct((rows, ne), jnp.int32),
                   jax.ShapeDtypeStruct((rows, ne), F32), jax.ShapeDtypeStruct((ne, 1), F32)],
        scratch_shapes=[pltpu.VMEM((ne, 1), F32)],
        compiler_params=_params("arbitrary"),
        name="moe_route",
    )(x2d, mod3, g, wr.T, tri)


def _store_row_tiles(ref, val):
    for c in range(ref.shape[-2]):
        ref[:, c, :] = val[:, c * LANES:(c + 1) * LANES]


ROW_PITCH = D_MODEL // LANES + 1


ROW_TILE = D_MODEL // LANES


def _gather_rows(idx_of, n, src_hbm, dst, sem):
    for j in range(n):
        start = pl.multiple_of(idx_of(j) * ROW_TILE, ROW_TILE)
        pltpu.make_async_copy(src_hbm.at[pl.ds(start, ROW_TILE)], dst.at[pl.ds(j * ROW_PITCH, ROW_TILE)],
                              sem).start()


def _gather_wait(n, src_hbm, dst, sem):
    pltpu.make_async_copy(src_hbm.at[pl.ds(0, n * ROW_TILE)], dst.at[pl.ds(0, n * ROW_TILE)], sem).wait()


def _load_gathered(buf, n):
    return jnp.concatenate([buf[pl.ds(c, n, stride=ROW_PITCH), :] for c in range(ROW_TILE)], axis=1)


def _expert_kernel(te_ref, src_ref, nu_ref, h_hbm, wg_ref, wu_ref, wd_ref, y_ref,
                   xbuf, xb_scr, acc_scr, sem, *, nf, r):
    i = pl.program_id(0)
    f = pl.program_id(1)
    active = i < nu_ref[0]
    slot = i % 2

    def gather(tile, s):
        _gather_rows(lambda j: src_ref[tile * r + j] >> 1, r, h_hbm, xbuf.at[s], sem.at[s])

    @pl.when((i == 0) & (f == 0))
    def _():
        gather(0, 0)

    @pl.when(active & (f == 0))
    def _():
        _gather_wait(r, h_hbm, xbuf.at[slot], sem.at[slot])

        @pl.when(i + 1 < nu_ref[0])
        def _():
            gather(i + 1, 1 - slot)

        xb_scr[...] = _load_gathered(xbuf.at[slot], r).astype(BF16)
        acc_scr[...] = jnp.zeros(acc_scr.shape, F32)

    @pl.when(active)
    def _():
        xb = xb_scr[...]
        act = (_silu(_dot(xb, wg_ref[0])) * _dot(xb, wu_ref[0])).astype(BF16)
        acc_scr[...] += _dot(act, wd_ref[0])

    @pl.when(f == nf - 1)
    def _():
        _store_row_tiles(y_ref, jnp.where(active, acc_scr[...], 0.0))


def _experts(h2, tile_expert, src_pair, n_used, wg, wu, wd, *, r, tf):
    ne, d, dff = wg.shape
    nf = dff // tf
    p_rows = src_pair.shape[0]
    nt = p_rows // r
    fsel = lambda i, f, nu: jnp.where(i < nu[0], f, nf - 1)
    grid_spec = pltpu.PrefetchScalarGridSpec(
        num_scalar_prefetch=3,
        grid=(nt, nf),
        in_specs=[
            pl.BlockSpec(memory_space=pl.ANY),
            pl.BlockSpec((1, d, tf), lambda i, f, te, sp, nu: (te[i], 0, fsel(i, f, nu))),
            pl.BlockSpec((1, d, tf), lambda i, f, te, sp, nu: (te[i], 0, fsel(i, f, nu))),
            pl.BlockSpec((1, tf, d), lambda i, f, te, sp, nu: (te[i], fsel(i, f, nu), 0)),
        ],
        out_specs=pl.BlockSpec((r, d // LANES, LANES), lambda i, f, te, sp, nu: (i, 0, 0)),
        scratch_shapes=[pltpu.VMEM((2, r * ROW_PITCH, LANES), F32), pltpu.VMEM((r, d), BF16),
                        pltpu.VMEM((r, d), F32), pltpu.SemaphoreType.DMA((2,))],
    )
    return pl.pallas_call(
        functools.partial(_expert_kernel, nf=nf, r=r),
        grid_spec=grid_spec,
        out_shape=jax.ShapeDtypeStruct((p_rows, d // LANES, LANES), F32),
        compiler_params=_params("arbitrary", "arbitrary"),
        name="moe_experts",
    )(tile_expert, src_pair, n_used, h2, wg, wu, wd)


COMBINE_SLOTS = 3


def _combine_kernel(pos_ref, y_hbm, x_ref, rw_ref, mod_ref, fg_ref, o_ref, ybuf, sem, *, tm, final):
    i = pl.program_id(0)
    n = pl.num_programs(0)
    ahead = COMBINE_SLOTS - 1
    slot = i % COMBINE_SLOTS

    def gather(tile, s):
        for k in range(2):
            _gather_rows(lambda j: pos_ref[(tile * tm + j) * 2 + k], tm, y_hbm, ybuf.at[s, k], sem.at[s])

    @pl.when(i == 0)
    def _():
        for t0 in range(ahead):
            @pl.when(t0 < n)
            def _():
                gather(t0, t0)

    for k in range(2):
        _gather_wait(tm, y_hbm, ybuf.at[slot, k], sem.at[slot])

    @pl.when(i + ahead < n)
    def _():
        gather(i + ahead, (i + ahead) % COMBINE_SLOTS)

    w = rw_ref[...]
    f = w[:, 0:1] * _load_gathered(ybuf.at[slot, 0], tm) + w[:, 1:2] * _load_gathered(ybuf.at[slot, 1], tm)
    x = x_ref[...] + mod_ref[0][5:6] * f
    if final:
        x = (x * lax.rsqrt(jnp.mean(x * x, axis=-1, keepdims=True) + EPS)) * fg_ref[...]
    o_ref[...] = x


def _combine(pos, y, x2d, rw, mod3, final_g, *, seq, tm, final):
    rows, d = x2d.shape
    per_b = seq // tm
    grid_spec = pltpu.PrefetchScalarGridSpec(
        num_scalar_prefetch=1,
        grid=(rows // tm,),
        in_specs=[
            pl.BlockSpec(memory_space=pl.ANY),
            pl.BlockSpec((tm, d), lambda i, p: (i, 0)),
            pl.BlockSpec((tm, rw.shape[1]), lambda i, p: (i, 0)),
            pl.BlockSpec((1, 6, d), lambda i, p: (i // per_b, 0, 0)),
            pl.BlockSpec((1, d), lambda i, p: (0, 0)),
        ],
        out_specs=pl.BlockSpec((tm, d), lambda i, p: (i, 0)),
        scratch_shapes=[pltpu.VMEM((COMBINE_SLOTS, 2, tm * ROW_PITCH, LANES), F32),
                        pltpu.SemaphoreType.DMA((COMBINE_SLOTS,))],
    )
    return pl.pallas_call(
        functools.partial(_combine_kernel, tm=tm, final=final),
        grid_spec=grid_spec,
        out_shape=jax.ShapeDtypeStruct((rows, d), F32),
        compiler_params=_params("arbitrary"),
        name="moe_combine",
    )(pos, y, x2d, rw, mod3, final_g)


def _moe(x2d, mod3, g, wr, wg, wu, wd, final_g, *, seq, final):
    rows, d = x2d.shape
    ne = wr.shape[1]
    r = 512
    h2, ridx, rw, cnt = _route(x2d, mod3, g, wr, seq=seq, tm=1024)
    counts = cnt[:, 0].astype(jnp.int32)
    tiles_e = (counts + r - 1) // r
    tile_end = jnp.cumsum(tiles_e)
    start = (tile_end - tiles_e) * r
    pos = jnp.take(start, ridx[:, 0:2], axis=0) + ridx[:, 2:4]
    p_rows = 2 * rows + ne * r
    nt = p_rows // r
    pair_id = jnp.arange(2 * rows, dtype=jnp.int32)
    src_pair = jnp.zeros((p_rows,), jnp.int32).at[pos.reshape(-1)].set(pair_id)
    n_used = tile_end[-1:]
    tile_idx = jnp.minimum(jnp.arange(nt), n_used[0] - 1)
    tile_expert = jnp.sum(tile_idx[:, None] >= tile_end[None, :], axis=1)
    y = _experts(h2.reshape(-1, LANES), tile_expert.astype(jnp.int32), src_pair, n_used.astype(jnp.int32),
                 wg, wu, wd, r=r, tf=1792)
    return _combine(pos.reshape(-1).astype(jnp.int32), y.reshape(-1, LANES), x2d, rw, mod3, final_g,
                    seq=seq, tm=256, final=final)


def _final_norm_kernel(x_ref, g_ref, o_ref):
    x = x_ref[...]
    o_ref[...] = (x * lax.rsqrt(jnp.mean(x * x, axis=-1, keepdims=True) + EPS)) * g_ref[...]


def _final_norm(x2d, g, *, tm):
    rows, d = x2d.shape
    return pl.pallas_call(
        _final_norm_kernel,
        grid=(rows // tm,),
        in_specs=[pl.BlockSpec((tm, d), lambda i: (i, 0)), pl.BlockSpec((1, d), lambda i: (0, 0))],
        out_specs=pl.BlockSpec((tm, d), lambda i: (i, 0)),
        out_shape=jax.ShapeDtypeStruct((rows, d), F32),
        compiler_params=_params("parallel"),
        name="final_norm",
    )(x2d, g)


def _t5_bucket(dist):
    n = jnp.maximum(dist, 0)
    max_exact = N_BUCKETS // 2
    nf = jnp.maximum(n, 1).astype(F32)
    large = max_exact + (jnp.log(nf / max_exact) / math.log(REL_MAX_DIST / max_exact)
                         * (N_BUCKETS - max_exact)).astype(jnp.int32)
    large = jnp.minimum(large, N_BUCKETS - 1)
    return jnp.where(n < max_exact, n, large)


def _bias_tiles_a(rel_bias, seq, t):
    nq = seq // t
    ncol = 2 * A_HEADS
    tab = rel_bias[:, :ncol][_t5_bucket(jnp.arange(seq))].astype(F32).T
    vneg = jnp.full((ncol, t), NEG, F32)
    v = jnp.concatenate([vneg, tab], axis=1)
    u = jnp.concatenate([v[:, 1:seq + 1][:, ::-1], vneg[:, :1], v[:, seq + 1:seq + t][:, ::-1]], axis=1)

    def toeplitz_kernel(u_ref, o_ref):
        x = jnp.broadcast_to(u_ref[0], (t, seq + t))
        r = pltpu.roll(x, 0, 1, stride=1, stride_axis=0)
        for delta in range(nq):
            c0 = (nq - 1 - delta) * t
            o_ref[0, delta] = r[:, c0:c0 + t]

    return pl.pallas_call(
        toeplitz_kernel,
        grid=(ncol,),
        in_specs=[pl.BlockSpec((1, 1, seq + t), lambda c: (c, 0, 0))],
        out_specs=pl.BlockSpec((1, nq, t, t), lambda c: (c // 2, 0, c % 2, 0)),
        out_shape=jax.ShapeDtypeStruct((A_HEADS, nq, 2 * t, t), F32),
        compiler_params=_params("parallel"),
        name="bias_tiles_diff",
    )(u.reshape(ncol, 1, seq + t))


def _bias_tiles_b(rel_bias):
    ng = len(B_GROUPS)
    period = 3 * BLOCK
    rows = []
    for g, (win, dil) in enumerate(B_GROUPS):
        n_back = win // dil
        tab = rel_bias[:, 2 * A_HEADS + g * B_HEADS:2 * A_HEADS + (g + 1) * B_HEADS]
        vals = tab[_t5_bucket(jnp.arange(n_back, -1, -1) * dil)].astype(F32).T
        rows.append(jnp.concatenate([vals, jnp.full((B_HEADS, period - n_back - 1), NEG, F32)], axis=1))
    u = jnp.concatenate(rows, axis=0)

    def toeplitz_kernel(u_ref, o_ref):
        x = jnp.broadcast_to(u_ref[0], (BLOCK, period))
        r = pltpu.roll(x, 0, 1, stride=1, stride_axis=0)[:, :2 * BLOCK]
        col = lax.broadcasted_iota(jnp.int32, r.shape, 1)
        o_ref[0, 0, 0] = jnp.where(col >= BLOCK, r, NEG)
        o_ref[0, 1, 0] = r

    return pl.pallas_call(
        toeplitz_kernel,
        grid=(ng * B_HEADS,),
        in_specs=[pl.BlockSpec((1, 1, period), lambda c: (c, 0, 0))],
        out_specs=pl.BlockSpec((1, 2, 1, BLOCK, 2 * BLOCK),
                               lambda c: (c // B_HEADS, 0, (c % B_HEADS) // 2, c % 2, 0)),
        out_shape=jax.ShapeDtypeStruct((ng, 2, B_HEADS // 2, 2 * BLOCK, 2 * BLOCK), F32),
        compiler_params=_params("parallel"),
        name="bias_tiles_dilated",
    )(u.reshape(ng * B_HEADS, 1, period))


def kernel(x, c, norm_mix_g, norm_ffn_g, w_mod, b_mod, w_in, b_forget, lam_q1, lam_k1, lam_q2, lam_k2,
           subln_g, rel_bias, w_br_a, w_br_b, w_br_c, w_out, w_ff_gate, w_ff_up, w_ff_down, w_router,
           w_exp_gate, w_exp_up, w_exp_down, final_norm_g):
    bsz, seq, d = x.shape
    depth = w_mod.shape[0]
    rows = bsz * seq
    t_attn = 256
    x2d = x.reshape(rows, d)

    mod = _modulation(c, w_mod, b_mod)
    bias_a = _bias_tiles_a(rel_bias, seq, t_attn)
    bias_b = _bias_tiles_b(rel_bias)
    w_qkv_all, w_gf_all = _prep_w_in(w_in)

    final_g = final_norm_g.reshape(1, d)
    fused_final = False
    for l in range(depth):
        lam_init = 0.8 - 0.6 * math.exp(-0.3 * l)
        mod3 = mod[l].reshape(bsz, 6, d)
        w_qkv, w_gate, w_f = w_qkv_all[l], w_gf_all[l, :, :GATE_W], w_gf_all[l, :, GATE_W:]

        g_mix = norm_mix_g[l].reshape(1, d)
        qkv_ac, qkv_b0, qkv_b1, qkv_b2, f_logit = _qkv_proj(x2d, mod3, g_mix, w_qkv, w_f,
                                                            bsz=bsz, seq=seq, tm=512)
        qkv_ac = qkv_ac.reshape(bsz, seq, -1)

        b_f8 = jnp.pad(b_forget[l], (0, 8 - C_HEADS)).reshape(8, 1)
        fcum = _forget_cumsum(f_logit.reshape(bsz, seq, LANES), b_f8, col_block=0)
        fcum = fcum[:, :C_HEADS].reshape(bsz, C_HEADS // 2, 2, seq)

        oa = _attn_a(qkv_ac, bias_a, lam_q1[l].reshape(1, -1), lam_k1[l].reshape(1, -1),
                     lam_q2[l].reshape(1, -1), lam_k2[l].reshape(1, -1), subln_g[l].reshape(1, -1),
                     lam_init=lam_init, t=t_attn, nsb=4, nbb=2)
        oc = _attn_c(qkv_ac, fcum, t=t_attn, nsb=4, nbb=2)
        obs, lses = [], []
        groups = (qkv_b0.reshape(bsz, 1, seq, 3 * B_W), qkv_b1, qkv_b2)
        for g, (qkv_g, nb) in enumerate(zip(groups, (4, 1, 1))):
            o_g, lse_g = _attn_b_group(qkv_g, bias_b[g], g, nb)
            obs.append(o_g)
            lses.append(lse_g)

        x2d = _merge(oa.reshape(rows, A_W), obs, lses, oc.reshape(rows, C_W), x2d, mod3, g_mix, w_gate,
                     w_br_a[l].astype(BF16), w_br_b[l].astype(BF16), w_br_c[l].astype(BF16),
                     w_out[l].astype(BF16), seq=seq, tm=512)

        g_ffn = norm_ffn_g[l].reshape(1, d)
        if l % 2 == 0:
            j = l // 2
            x2d = _ffn(x2d, mod3, g_ffn, w_ff_gate[j].astype(BF16), w_ff_up[j].astype(BF16),
                       w_ff_down[j].astype(BF16), seq=seq, tm=512)
        else:
            j = l // 2
            fused_final = l == depth - 1
            x2d = _moe(x2d, mod3, g_ffn, w_router[j], w_exp_gate[j].astype(BF16),
                       w_exp_up[j].astype(BF16), w_exp_down[j].astype(BF16), final_g,
                       seq=seq, final=fused_final)

    if not fused_final:
        x2d = _final_norm(x2d, final_g, tm=1024)
    return x2d.reshape(bsz, seq, d)
```

```python
import functools
import math

import jax
import jax.numpy as jnp
from jax import lax
from jax.experimental import pallas as pl
from jax.experimental.pallas import tpu as pltpu

F32 = jnp.float32
BF16 = jnp.bfloat16

D_MODEL = 1024
HEAD_DIM = 64
LANES = 128
A_HEADS = 4
A_W = A_HEADS * 2 * HEAD_DIM
B_GROUPS = ((128, 1), (512, 4), (2048, 16))
B_HEADS = 6
B_W = B_HEADS * HEAD_DIM
B_QW = len(B_GROUPS) * B_W
C_HEADS = 6
C_W = C_HEADS * HEAD_DIM
N_BRANCH = 3
BLOCK = 128
N_BUCKETS = 32
REL_MAX_DIST = 2048
N_EXPERTS = 8
EPS = 1e-6
QKV_W = 3 * A_W + 3 * B_QW + 3 * C_W
GATE_W = N_BRANCH * D_MODEL
NEG = -1e30
SCALE = HEAD_DIM ** -0.5
VMEM_LIMIT = 56 * 1024 * 1024


def _params(*sem):
    return pltpu.CompilerParams(dimension_semantics=sem, vmem_limit_bytes=VMEM_LIMIT)


def _resident(a):
    return pl.BlockSpec(a.shape, lambda *_: (0, 0), pipeline_mode=pl.Buffered(1))


def _rms_mod(x, g, sc, sh):
    y = x * lax.rsqrt(jnp.mean(x * x, axis=-1, keepdims=True) + EPS)
    return (y * g) * (1.0 + sc) + sh


def _dot(a, b):
    return jnp.dot(a, b, preferred_element_type=F32)


def _dot_nt(a, b):
    return lax.dot_general(a, b, (((1,), (1,)), ((), ())), preferred_element_type=F32)


def _silu(a):
    return a * jax.nn.sigmoid(a)


def _mod_kernel(c_ref, w_ref, b_ref, o_ref):
    a = _silu(c_ref[...]).astype(BF16)
    o_ref[0] = _dot(a, w_ref[0].astype(BF16)) + b_ref[0]


def _modulation(c, w_mod, b_mod):
    depth, d, n = w_mod.shape
    bsz = c.shape[0]
    tn = 1536
    return pl.pallas_call(
        _mod_kernel,
        grid=(depth, n // tn),
        in_specs=[
            pl.BlockSpec((bsz, d), lambda l, j: (0, 0)),
            pl.BlockSpec((1, d, tn), lambda l, j: (l, 0, j)),
            pl.BlockSpec((1, 1, tn), lambda l, j: (l, 0, j)),
        ],
        out_specs=pl.BlockSpec((1, bsz, tn), lambda l, j: (l, 0, j)),
        out_shape=jax.ShapeDtypeStruct((depth, bsz, n), F32),
        compiler_params=_params("parallel", "parallel"),
        name="modulation",
    )(c, w_mod, b_mod.reshape(depth, 1, n))


_A_BLOCKS = 3 * A_W // B_W
_B_BLOCKS = 3 * B_QW // B_W
_QKV_SRC_BLOCKS = (list(range(_A_BLOCKS))
                   + list(range(_A_BLOCKS + _B_BLOCKS, QKV_W // B_W))
                   + [_A_BLOCKS + s * len(B_GROUPS) + g for g in range(len(B_GROUPS)) for s in range(3)])


def _prep_w_in_kernel(*refs):
    n = len(_QKV_SRC_BLOCKS)
    piece_refs, x_ref, y_ref, qkv_ref, gf_ref = refs[:n], refs[n], refs[n + 1], refs[n + 2], refs[n + 3]
    for j, ref in enumerate(piece_refs):
        qkv_ref[0, :, j * B_W:(j + 1) * B_W] = ref[0].astype(BF16)
    x = x_ref[0]
    rolled = pltpu.roll(x, GATE_W - C_HEADS, 1)
    tail = pltpu.roll(y_ref[0], LANES - C_HEADS, 1)
    lane = lax.broadcasted_iota(jnp.int32, tail.shape, 1)
    gf_ref[0, :, :GATE_W - LANES] = rolled[:, :GATE_W - LANES].astype(BF16)
    gf_ref[0, :, GATE_W - LANES:GATE_W] = jnp.where(lane < LANES - C_HEADS, rolled[:, GATE_W - LANES:],
                                                     tail).astype(BF16)
    gf_ref[0, :, GATE_W:] = jnp.where(lane < C_HEADS, x[:, :LANES], 0.0).astype(BF16)


def _prep_w_in(w_in):
    depth, d, _ = w_in.shape
    tr = 256
    piece = lambda c: pl.BlockSpec((1, tr, B_W), lambda l, i: (l, i, c))
    return pl.pallas_call(
        _prep_w_in_kernel,
        grid=(depth, d // tr),
        in_specs=[piece(c) for c in _QKV_SRC_BLOCKS]
        + [pl.BlockSpec((1, tr, GATE_W), lambda l, i: (l, i, QKV_W // GATE_W)),
           pl.BlockSpec((1, tr, LANES), lambda l, i: (l, i, (QKV_W + GATE_W) // LANES))],
        out_specs=[pl.BlockSpec((1, tr, QKV_W), lambda l, i: (l, i, 0)),
                   pl.BlockSpec((1, tr, GATE_W + LANES), lambda l, i: (l, i, 0))],
        out_shape=[jax.ShapeDtypeStruct((depth, d, QKV_W), BF16),
                   jax.ShapeDtypeStruct((depth, d, GATE_W + LANES), BF16)],
        compiler_params=_params("parallel", "parallel"),
        name="prep_w_in",
    )(*([w_in] * (len(_QKV_SRC_BLOCKS) + 2)))


def _qkv_kernel(x_ref, mod_ref, g_ref, w_ref, wf_ref, ac_ref, b0_ref, b1_ref, b2_ref, f_ref, h_scr, *, tm):
    m = mod_ref[0]
    h = _rms_mod(x_ref[...], g_ref[...], m[1:2], m[0:1])
    nc = h_scr.shape[0]
    for c in range(nc):
        h_scr[c] = h[:, c * LANES:(c + 1) * LANES]
    hb = h.astype(BF16)
    n_ac = ac_ref.shape[1]
    n_b = b0_ref.shape[1]
    f_ref[...] = _dot(hb, wf_ref[...])
    ac_ref[...] = _dot(hb, w_ref[:, 0:n_ac]).astype(BF16)
    b0_ref[...] = _dot(hb, w_ref[:, n_ac:n_ac + n_b]).astype(BF16)
    for gi, ref in ((1, b1_ref), (2, b2_ref)):
        dil = B_GROUPS[gi][1]
        per = tm // dil
        hp = jnp.concatenate(
            [jnp.concatenate([h_scr[c, pl.ds(r, per, stride=dil), :] for c in range(nc)], axis=1).astype(BF16)
             for r in range(dil)], axis=0)
        y = _dot(hp, w_ref[:, n_ac + gi * n_b:n_ac + (gi + 1) * n_b]).astype(BF16)
        for r in range(dil):
            ref[0, r] = y[r * per:(r + 1) * per]


def _qkv_proj(x2d, mod3, g, w, w_f, *, bsz, seq, tm):
    rows, d = x2d.shape
    per_b = seq // tm
    n_b = 3 * B_W
    n_ac = w.shape[1] - 3 * n_b
    dil1, dil2 = B_GROUPS[1][1], B_GROUPS[2][1]
    strided = lambda dil: pl.BlockSpec((1, dil, tm // dil, n_b), lambda i: (i // per_b, 0, i % per_b, 0))
    return pl.pallas_call(
        functools.partial(_qkv_kernel, tm=tm),
        grid=(rows // tm,),
        in_specs=[
            pl.BlockSpec((tm, d), lambda i: (i, 0)),
            pl.BlockSpec((1, 6, d), lambda i: (i // per_b, 0, 0)),
            pl.BlockSpec((1, d), lambda i: (0, 0)),
            _resident(w), _resident(w_f),
        ],
        out_specs=[pl.BlockSpec((tm, n_ac), lambda i: (i, 0)), pl.BlockSpec((tm, n_b), lambda i: (i, 0)),
                   strided(dil1), strided(dil2), pl.BlockSpec((tm, LANES), lambda i: (i, 0))],
        out_shape=[jax.ShapeDtypeStruct((rows, n_ac), BF16), jax.ShapeDtypeStruct((rows, n_b), BF16),
                   jax.ShapeDtypeStruct((bsz, dil1, seq // dil1, n_b), BF16),
                   jax.ShapeDtypeStruct((bsz, dil2, seq // dil2, n_b), BF16),
                   jax.ShapeDtypeStruct((rows, LANES), F32)],
        scratch_shapes=[pltpu.VMEM((d // LANES, tm, LANES), F32)],
        compiler_params=_params("parallel"),
        name="qkv_proj",
    )(x2d, mod3, g, w, w_f)


def _fcum_kernel(f_ref, b_ref, o_ref):
    z = f_ref[0].T[:8] + b_ref[...]
    x = jnp.minimum(z, 0.0) - jnp.log1p(jnp.exp(-jnp.abs(z)))
    s = x.shape[1]
    lane = lax.broadcasted_iota(jnp.int32, x.shape, 1)
    k = 1
    while k < s:
        x = x + jnp.where(lane >= k, pltpu.roll(x, k, 1), 0.0)
        k *= 2
    o_ref[0] = x


def _forget_cumsum(gf, b_f8, *, col_block):
    bsz, seq, _ = gf.shape
    return pl.pallas_call(
        _fcum_kernel,
        grid=(bsz,),
        in_specs=[
            pl.BlockSpec((1, seq, LANES), lambda b: (b, 0, col_block)),
            pl.BlockSpec((8, 1), lambda b: (0, 0)),
        ],
        out_specs=pl.BlockSpec((1, 8, seq), lambda b: (b, 0, 0)),
        out_shape=jax.ShapeDtypeStruct((bsz, 8, seq), F32),
        compiler_params=_params("parallel"),
        name="forget_cumsum",
    )(gf, b_f8)


def _half_masks(q):
    lane = lax.broadcasted_iota(jnp.int32, q.shape, 1)
    zero = jnp.zeros_like(q)
    return jnp.where(lane < HEAD_DIM, q, zero), jnp.where(lane >= HEAD_DIM, q, zero)


def _flash_init(first, v_ref, vext_scr, m_scr, acc_scr):
    @pl.when(first)
    def _():
        for bb in range(vext_scr.shape[0]):
            vext_scr[bb, :, :LANES] = v_ref[bb]
            vext_scr[bb, :, LANES:] = jnp.ones((vext_scr.shape[1], LANES), BF16)

    m_scr[...] = jnp.full(m_scr.shape, -jnp.inf, F32)
    acc_scr[...] = jnp.zeros(acc_scr.shape, F32)


def _lane_tile(a, n):
    return a if n == 1 else jnp.concatenate([a] * n, axis=1)


def _flash_update(s, vext, rows, m_scr, acc_scr):
    m_prev = m_scr[rows]
    m_new = jnp.maximum(m_prev, jnp.max(s, axis=-1, keepdims=True))
    alpha = jnp.exp(m_prev - m_new)
    p = jnp.exp(s - _lane_tile(m_new, s.shape[1] // LANES))
    acc_scr[rows] = _lane_tile(alpha, 2) * acc_scr[rows] + _dot(p.astype(BF16), vext)
    m_scr[rows] = m_new


def _flash_result(acc_scr):
    acc = acc_scr[...]
    return acc[:, :LANES] / acc[:, LANES:]


def _attn_a_kernel(q_ref, k_ref, v_ref, bias_ref, lq1, lk1, lq2, lk2, sg_ref, o_ref,
                   vext_scr, m_scr, acc_scr, *, t, nsb, nbb, lam_init):
    qi = pl.program_id(2)
    _flash_init(qi == 0, v_ref, vext_scr, m_scr, acc_scr)
    qh = [[_half_masks(q_ref[bb, sb * t:(sb + 1) * t, :] * SCALE) for sb in range(nsb)] for bb in range(nbb)]
    chain_rows = lambda bb, sb, hh: pl.ds(((bb * nsb + sb) * 2 + hh) * t, t)

    def step(kb, nkb, plan):
        off = pl.multiple_of(kb * t, t)
        for bb in range(nbb):
            kblk = k_ref[bb, pl.ds(off, nkb * t), :]
            vext = vext_scr[bb, pl.ds(off, nkb * t), :]
            for sb, deltas in plan:
                for hh in range(2):
                    bias = [bias_ref[0, d, hh * t:(hh + 1) * t, :] for d in deltas]
                    s = _dot_nt(qh[bb][sb][hh], kblk) + (bias[0] if nkb == 1 else jnp.concatenate(bias, axis=1))
                    _flash_update(s, vext, chain_rows(bb, sb, hh), m_scr, acc_scr)

    def body(kb2, carry):
        first = [qi * nsb + sb - 2 * kb2 for sb in range(nsb)]
        step(2 * kb2, 2, [(sb, (first[sb], first[sb] - 1)) for sb in range(nsb)])
        return carry

    if q_ref.shape[1] < k_ref.shape[1]:
        lax.fori_loop(0, qi * (nsb // 2), body, 0)
    for j in range(0, nsb, 2):
        step(qi * nsb + j, 1, [(j, (0,))])
        step(qi * nsb + j, 2, [(sb, (sb - j, sb - j - 1)) for sb in range(j + 1, nsb)])

    o = _flash_result(acc_scr)
    lam = (jnp.exp(jnp.sum(lq1[...] * lk1[...], axis=-1, keepdims=True))
           - jnp.exp(jnp.sum(lq2[...] * lk2[...], axis=-1, keepdims=True)) + lam_init)
    for bb in range(nbb):
        for sb in range(nsb):
            c0 = (bb * nsb + sb) * 2 * t
            d = o[c0:c0 + t] - lam * o[c0 + t:c0 + 2 * t]
            y = d * lax.rsqrt(jnp.mean(d * d, axis=-1, keepdims=True) + EPS)
            o_ref[bb, sb * t:(sb + 1) * t, :] = ((y * sg_ref[...]) * (1.0 - lam_init)).astype(o_ref.dtype)


def _flash_scratch(seq, t, nsb, nbb):
    chains = 2 * nsb * nbb
    return [pltpu.VMEM((nbb, seq, 2 * LANES), BF16), pltpu.VMEM((chains * t, LANES), F32),
            pltpu.VMEM((chains * t, 2 * LANES), F32)]


def _attn_a(qkv, bias_a, lq1, lk1, lq2, lk2, subln_g, *, lam_init, t, nsb, nbb):
    bsz, seq, _ = qkv.shape
    tq = t * nsb
    assert bsz % nbb == 0 and seq % tq == 0 and nsb % 2 == 0
    vec = lambda n: pl.BlockSpec((1, n), lambda h, b, i: (0, 0))
    return pl.pallas_call(
        functools.partial(_attn_a_kernel, t=t, nsb=nsb, nbb=nbb, lam_init=lam_init),
        grid=(A_HEADS, bsz // nbb, seq // tq),
        in_specs=[
            pl.BlockSpec((nbb, tq, LANES), lambda h, b, i: (b, i, h)),
            pl.BlockSpec((nbb, seq, LANES), lambda h, b, i: (b, 0, A_HEADS + h)),
            pl.BlockSpec((nbb, seq, LANES), lambda h, b, i: (b, 0, 2 * A_HEADS + h)),
            pl.BlockSpec((1, seq // t, 2 * t, t), lambda h, b, i: (h, 0, 0, 0)),
            vec(HEAD_DIM), vec(HEAD_DIM), vec(HEAD_DIM), vec(HEAD_DIM), vec(LANES),
        ],
        out_specs=pl.BlockSpec((nbb, tq, LANES), lambda h, b, i: (b, i, h)),
        out_shape=jax.ShapeDtypeStruct((bsz, seq, A_W), BF16),
        scratch_shapes=_flash_scratch(seq, t, nsb, nbb),
        compiler_params=_params("parallel", "parallel", "arbitrary"),
        name="attn_diff",
    )(qkv, qkv, qkv, bias_a, lq1, lk1, lq2, lk2, subln_g)


def _attn_c_kernel(q_ref, k_ref, v_ref, f_ref, o_ref, vext_scr, m_scr, acc_scr, *, t, nsb, nbb):
    qi = pl.program_id(2)
    _flash_init(qi == 0, v_ref, vext_scr, m_scr, acc_scr)
    qh = [[_half_masks(q_ref[bb, sb * t:(sb + 1) * t, :] * SCALE) for sb in range(nsb)] for bb in range(nbb)]
    chain_rows = lambda bb, sb, hh: pl.ds(((bb * nsb + sb) * 2 + hh) * t, t)
    q_off = pl.multiple_of(qi * (t * nsb), t * nsb)
    f_anchor = [f_ref[bb, 0, :, pl.ds(q_off, LANES)][:, :1] for bb in range(nbb)]
    def causal(nkb):
        r = lax.broadcasted_iota(jnp.int32, (t, nkb * t), 0)
        c = lax.broadcasted_iota(jnp.int32, (t, nkb * t), 1)
        return r + (nkb - 1) * t >= c

    def step(kb, nkb, sbs, diag_sb):
        off = pl.multiple_of(kb * t, t)
        for bb in range(nbb):
            kblk = k_ref[bb, pl.ds(off, nkb * t), :]
            vext = vext_scr[bb, pl.ds(off, nkb * t), :]
            dec = [f_anchor[bb][hh:hh + 1] - f_ref[bb, 0, hh:hh + 1, pl.ds(off, nkb * t)] for hh in range(2)]
            for sb in sbs:
                for hh in range(2):
                    s = _dot_nt(qh[bb][sb][hh], kblk) + dec[hh]
                    if sb == diag_sb:
                        s = jnp.where(causal(nkb), s, NEG)
                    _flash_update(s, vext, chain_rows(bb, sb, hh), m_scr, acc_scr)

    def body(kb2, carry):
        step(2 * kb2, 2, range(nsb), None)
        return carry

    if q_ref.shape[1] < k_ref.shape[1]:
        lax.fori_loop(0, qi * (nsb // 2), body, 0)
    for j in range(0, nsb, 2):
        step(qi * nsb + j, 1, [j], j)
        step(qi * nsb + j, 2, range(j + 1, nsb), j + 1)

    o = _flash_result(acc_scr)
    lane = lax.broadcasted_iota(jnp.int32, (t, LANES), 1)
    for bb in range(nbb):
        for sb in range(nsb):
            c0 = (bb * nsb + sb) * 2 * t
            pair = jnp.where(lane < HEAD_DIM, o[c0:c0 + t], o[c0 + t:c0 + 2 * t])
            o_ref[bb, sb * t:(sb + 1) * t, :] = pair.astype(o_ref.dtype)


def _attn_c(qkv, fcum, *, t, nsb, nbb):
    bsz, seq, _ = qkv.shape
    tq = t * nsb
    assert bsz % nbb == 0 and seq % tq == 0 and nsb % 2 == 0
    pairs = C_HEADS // 2
    q0 = 3 * A_W // LANES
    return pl.pallas_call(
        functools.partial(_attn_c_kernel, t=t, nsb=nsb, nbb=nbb),
        grid=(pairs, bsz // nbb, seq // tq),
        in_specs=[
            pl.BlockSpec((nbb, tq, LANES), lambda p, b, i: (b, i, q0 + p)),
            pl.BlockSpec((nbb, seq, LANES), lambda p, b, i: (b, 0, q0 + pairs + p)),
            pl.BlockSpec((nbb, seq, LANES), lambda p, b, i: (b, 0, q0 + 2 * pairs + p)),
            pl.BlockSpec((nbb, 1, 2, seq), lambda p, b, i: (b, p, 0, 0)),
        ],
        out_specs=pl.BlockSpec((nbb, tq, LANES), lambda p, b, i: (b, i, p)),
        out_shape=jax.ShapeDtypeStruct((bsz, seq, C_W), BF16),
        scratch_shapes=_flash_scratch(seq, t, nsb, nbb),
        compiler_params=_params("parallel", "parallel", "arbitrary"),
        name="attn_forget",
    )(qkv, qkv, qkv, fcum)


def _attn_b_kernel(q_ref, kp_ref, kc_ref, vp_ref, vc_ref, bias_ref, o_ref, lse_ref, *, dil, nb):
    n = pl.program_id(1)
    lane = lax.broadcasted_iota(jnp.int32, (BLOCK, LANES), 1)
    first_variant = jnp.minimum(n, 1)

    def residue(r):
        for hp in range(B_HEADS // 2):
            cols = slice(hp * LANES, (hp + 1) * LANES)
            kcat = jnp.concatenate([kp_ref[0, r, :, cols], kc_ref[0, r, :, cols]], axis=0)
            vcat = jnp.concatenate([vp_ref[0, r, :, cols], vc_ref[0, r, :, cols]], axis=0)
            for jb in range(nb):
                qh = _half_masks(q_ref[0, r, jb * BLOCK:(jb + 1) * BLOCK, cols] * SCALE)
                kwin = kcat[jb * BLOCK:(jb + 2) * BLOCK]
                vwin = vcat[jb * BLOCK:(jb + 2) * BLOCK]
                variant = first_variant if jb == 0 else 1
                outs, lses = [], []
                for hh in range(2):
                    s = _dot_nt(qh[hh], kwin) + bias_ref[variant, hp, hh * BLOCK:(hh + 1) * BLOCK, :]
                    m = jnp.max(s, axis=-1, keepdims=True)
                    e = jnp.exp(s - m)
                    den = jnp.sum(e, axis=-1, keepdims=True)
                    outs.append(_dot(e.astype(BF16), vwin) / den)
                    lses.append(jnp.broadcast_to(m + jnp.log(den), (BLOCK, LANES)))
                if dil == 1:
                    rows = pl.ds(jb * BLOCK, BLOCK)
                else:
                    rows = pl.ds(jb * BLOCK * dil + r, BLOCK, stride=dil)
                o_ref[0, hp, rows, :] = jnp.where(lane < HEAD_DIM, outs[0], outs[1])
                lse_ref[0, hp, rows, :] = jnp.where(lane < HEAD_DIM, lses[0], lses[1])

    unroll = min(dil, 4)
    if dil == unroll:
        for r in range(dil):
            residue(r)
    else:
        def body(i, carry):
            for j in range(unroll):
                residue(i * unroll + j)
            return carry

        lax.fori_loop(0, dil // unroll, body, 0)


def _attn_b_group(qkv_g, bias_g, g, nb):
    bsz, dil, m_len, _ = qkv_g.shape
    tb = BLOCK * nb
    cur = lambda c: pl.BlockSpec((1, dil, tb, B_W), lambda b, n: (b, 0, n, c))
    prev = lambda c: pl.BlockSpec((1, dil, BLOCK, B_W), lambda b, n: (b, 0, jnp.maximum(n * nb - 1, 0), c))
    pairs = B_HEADS // 2
    out_spec = pl.BlockSpec((1, pairs, tb * dil, LANES), lambda b, n: (b, 0, n, 0))
    out_sds = jax.ShapeDtypeStruct((bsz, pairs, m_len * dil, LANES), F32)
    return pl.pallas_call(
        functools.partial(_attn_b_kernel, dil=dil, nb=nb),
        grid=(bsz, m_len // tb),
        in_specs=[cur(0), prev(1), cur(1), prev(2), cur(2),
                  pl.BlockSpec(bias_g.shape, lambda b, n: (0, 0, 0, 0))],
        out_specs=[out_spec, out_spec],
        out_shape=[out_sds, out_sds],
        compiler_params=_params("parallel", "arbitrary"),
        name=f"attn_dilated_g{g}",
    )(qkv_g, qkv_g, qkv_g, qkv_g, qkv_g, bias_g)


def _merge_kernel(oa_ref, ob0, ob1, ob2, ls0, ls1, ls2, oc_ref, x_ref, mod_ref, g_ref,
                  wg_ref, wa_ref, wb_ref, wc_ref, wo_ref, o_ref):
    x = x_ref[...]
    m = mod_ref[0]
    hb = _rms_mod(x, g_ref[...], m[1:2], m[0:1]).astype(BF16)
    parts = []
    for hp in range(B_HEADS // 2):
        l0, l1, l2 = ls0[0, hp], ls1[0, hp], ls2[0, hp]
        mx = jnp.maximum(jnp.maximum(l0, l1), l2)
        e0, e1, e2 = jnp.exp(l0 - mx), jnp.exp(l1 - mx), jnp.exp(l2 - mx)
        den = e0 + e1 + e2
        parts.append((e0 / den) * ob0[0, hp] + (e1 / den) * ob1[0, hp] + (e2 / den) * ob2[0, hp])
    ob = jnp.concatenate(parts, axis=1)
    d = D_MODEL
    gate = lambda k: jax.nn.sigmoid(_dot(hb, wg_ref[:, k * d:(k + 1) * d]))
    merged = (gate(0) * _dot(oa_ref[...], wa_ref[...])
              + gate(1) * _dot(ob.astype(BF16), wb_ref[...])
              + gate(2) * _dot(oc_ref[...], wc_ref[...]))
    y = _dot(merged.astype(BF16), wo_ref[...])
    o_ref[...] = x + m[2:3] * y


def _merge(oa, obs, lses, oc, x2d, mod3, g, w_gate, wa, wb, wc, wo, *, seq, tm):
    rows, d = x2d.shape
    per_b = seq // tm
    row = lambda w: pl.BlockSpec((tm, w), lambda i: (i, 0))
    paired = pl.BlockSpec((1, B_HEADS // 2, tm, LANES), lambda i: (i // per_b, 0, i % per_b, 0))
    return pl.pallas_call(
        _merge_kernel,
        grid=(rows // tm,),
        in_specs=[row(A_W)] + [paired] * 6 + [row(C_W), row(d),
                  pl.BlockSpec((1, 6, d), lambda i: (i // per_b, 0, 0)),
                  pl.BlockSpec((1, d), lambda i: (0, 0)),
                  _resident(w_gate), _resident(wa), _resident(wb), _resident(wc), _resident(wo)],
        out_specs=row(d),
        out_shape=jax.ShapeDtypeStruct((rows, d), F32),
        compiler_params=_params("parallel"),
        name="merge_outproj",
    )(oa, *obs, *lses, oc, x2d, mod3, g, w_gate, wa, wb, wc, wo)


MXU_TILE = 256


def _ffn_kernel(x_ref, mod_ref, g_ref, wg_ref, wu_ref, wd_ref, o_ref, *, bounds):
    x = x_ref[...]
    m = mod_ref[0]
    h = _rms_mod(x, g_ref[...], m[4:5], m[3:4]).astype(BF16)
    acc = None
    for lo, hi in bounds:
        act = (_silu(_dot(h, wg_ref[:, lo:hi])) * _dot(h, wu_ref[:, lo:hi])).astype(BF16)
        part = _dot(act, wd_ref[lo:hi, :])
        acc = part if acc is None else acc + part
    o_ref[...] = x + m[5:6] * acc


def _ffn(x2d, mod3, g, wg, wu, wd, *, seq, tm):
    rows, d = x2d.shape
    dff = wg.shape[1]
    half = (dff // MXU_TILE + 1) // 2 * MXU_TILE
    per_b = seq // tm
    return pl.pallas_call(
        functools.partial(_ffn_kernel, bounds=((0, half), (half, dff))),
        grid=(rows // tm,),
        in_specs=[
            pl.BlockSpec((tm, d), lambda i: (i, 0)),
            pl.BlockSpec((1, 6, d), lambda i: (i // per_b, 0, 0)),
            pl.BlockSpec((1, d), lambda i: (0, 0)),
            _resident(wg), _resident(wu), _resident(wd),
        ],
        out_specs=pl.BlockSpec((tm, d), lambda i: (i, 0)),
        out_shape=jax.ShapeDtypeStruct((rows, d), F32),
        compiler_params=_params("parallel"),
        name="ffn_dense",
    )(x2d, mod3, g, wg, wu, wd)


def _route_kernel(x_ref, mod_ref, g_ref, wr_ref, tri_ref, h_ref, ridx_ref, rw_ref, cnt_ref, carry_scr):
    i = pl.program_id(0)

    @pl.when(i == 0)
    def _():
        carry_scr[...] = jnp.zeros(carry_scr.shape, F32)

    m = mod_ref[0]
    h = _rms_mod(x_ref[...], g_ref[...], m[4:5], m[3:4])
    _store_row_tiles(h_ref, h)
    w = wr_ref[...]
    h_hi, w_hi = h.astype(BF16), w.astype(BF16)
    h_lo = (h - h_hi.astype(F32)).astype(BF16)
    w_lo = (w - w_hi.astype(F32)).astype(BF16)
    logits = _dot_nt(w_hi, h_hi) + (_dot_nt(w_hi, h_lo) + _dot_nt(w_lo, h_hi))
    idx = lax.broadcasted_iota(jnp.int32, logits.shape, 0)
    n = logits.shape[0]
    m1 = jnp.max(logits, axis=0, keepdims=True)
    i1 = jnp.min(jnp.where(logits == m1, idx, n), axis=0, keepdims=True)
    first = idx == i1
    rest = jnp.where(first, -jnp.inf, logits)
    m2 = jnp.max(rest, axis=0, keepdims=True)
    i2 = jnp.min(jnp.where(rest == m2, idx, n), axis=0, keepdims=True)
    second = idx == i2
    e = jnp.exp(m2 - m1)
    den = 1.0 + e
    onehot = jnp.where(first | second, 1.0, 0.0)
    before = _dot(onehot.astype(BF16), tri_ref[...]) - onehot + carry_scr[...]
    rank1 = jnp.sum(jnp.where(first, before, 0.0), axis=0, keepdims=True)
    rank2 = jnp.sum(jnp.where(second, before, 0.0), axis=0, keepdims=True)
    carry_scr[...] += jnp.sum(onehot, axis=1, keepdims=True)
    picks = jnp.where(idx == 0, i1.astype(F32), jnp.where(idx == 1, i2.astype(F32),
                      jnp.where(idx == 2, rank1, jnp.where(idx == 3, rank2, 0.0))))
    weights = jnp.where(idx == 0, 1.0 / den, jnp.where(idx == 1, e / den, 0.0))
    ridx_ref[...] = picks.T.astype(jnp.int32)
    rw_ref[...] = weights.T
    cnt_ref[...] = carry_scr[...]


def _route(x2d, mod3, g, wr, *, seq, tm):
    rows, d = x2d.shape
    ne = wr.shape[1]
    per_b = seq // tm
    tri = (jnp.arange(tm)[:, None] <= jnp.arange(tm)[None, :]).astype(BF16)
    row = lambda w: pl.BlockSpec((tm, w), lambda i: (i, 0))
    return pl.pallas_call(
        _route_kernel,
        grid=(rows // tm,),
        in_specs=[row(d), pl.BlockSpec((1, 6, d), lambda i: (i // per_b, 0, 0)),
                  pl.BlockSpec((1, d), lambda i: (0, 0)), pl.BlockSpec((ne, d), lambda i: (0, 0)),
                  pl.BlockSpec((tm, tm), lambda i: (0, 0))],
        out_specs=[pl.BlockSpec((tm, d // LANES, LANES), lambda i: (i, 0, 0)), row(ne), row(ne),
                   pl.BlockSpec((ne, 1), lambda i: (0, 0))],
        out_shape=[jax.ShapeDtypeStruct((rows, d // LANES, LANES), F32),
                   jax.ShapeDtypeStruct((rows, ne), jnp.int32),
                   jax.ShapeDtypeStruct((rows, ne), F32), jax.ShapeDtypeStruct((ne, 1), F32)],
        scratch_shapes=[pltpu.VMEM((ne, 1), F32)],
        compiler_params=_params("arbitrary"),
        name="moe_route",
    )(x2d, mod3, g, wr.T, tri)


def _store_row_tiles(ref, val):
    for c in range(ref.shape[-2]):
        ref[:, c, :] = val[:, c * LANES:(c + 1) * LANES]


ROW_PITCH = D_MODEL // LANES + 1


ROW_TILE = D_MODEL // LANES


def _gather_rows(idx_of, n, src_hbm, dst, sem):
    for j in range(n):
        start = pl.multiple_of(idx_of(j) * ROW_TILE, ROW_TILE)
        pltpu.make_async_copy(src_hbm.at[pl.ds(start, ROW_TILE)], dst.at[pl.ds(j * ROW_PITCH, ROW_TILE)],
                              sem).start()


def _gather_wait(n, src_hbm, dst, sem):
    pltpu.make_async_copy(src_hbm.at[pl.ds(0, n * ROW_TILE)], dst.at[pl.ds(0, n * ROW_TILE)], sem).wait()


def _load_gathered(buf, n):
    return jnp.concatenate([buf[pl.ds(c, n, stride=ROW_PITCH), :] for c in range(ROW_TILE)], axis=1)


def _expert_kernel(te_ref, src_ref, nu_ref, h_hbm, wg_ref, wu_ref, wd_ref, y_ref,
                   xbuf, xb_scr, acc_scr, sem, *, nf, r):
    i = pl.program_id(0)
    f = pl.program_id(1)
    active = i < nu_ref[0]
    slot = i % 2

    def gather(tile, s):
        _gather_rows(lambda j: src_ref[tile * r + j] >> 1, r, h_hbm, xbuf.at[s], sem.at[s])

    @pl.when((i == 0) & (f == 0))
    def _():
        gather(0, 0)

    @pl.when(active & (f == 0))
    def _():
        _gather_wait(r, h_hbm, xbuf.at[slot], sem.at[slot])

        @pl.when(i + 1 < nu_ref[0])
        def _():
            gather(i + 1, 1 - slot)

        xb_scr[...] = _load_gathered(xbuf.at[slot], r).astype(BF16)
        acc_scr[...] = jnp.zeros(acc_scr.shape, F32)

    @pl.when(active)
    def _():
        xb = xb_scr[...]
        act = (_silu(_dot(xb, wg_ref[0])) * _dot(xb, wu_ref[0])).astype(BF16)
        acc_scr[...] += _dot(act, wd_ref[0])

    @pl.when(f == nf - 1)
    def _():
        _store_row_tiles(y_ref, jnp.where(active, acc_scr[...], 0.0))


def _experts(h2, tile_expert, src_pair, n_used, wg, wu, wd, *, r, tf):
    ne, d, dff = wg.shape
    nf = dff // tf
    p_rows = src_pair.shape[0]
    nt = p_rows // r
    fsel = lambda i, f, nu: jnp.where(i < nu[0], f, nf - 1)
    grid_spec = pltpu.PrefetchScalarGridSpec(
        num_scalar_prefetch=3,
        grid=(nt, nf),
        in_specs=[
            pl.BlockSpec(memory_space=pl.ANY),
            pl.BlockSpec((1, d, tf), lambda i, f, te, sp, nu: (te[i], 0, fsel(i, f, nu))),
            pl.BlockSpec((1, d, tf), lambda i, f, te, sp, nu: (te[i], 0, fsel(i, f, nu))),
            pl.BlockSpec((1, tf, d), lambda i, f, te, sp, nu: (te[i], fsel(i, f, nu), 0)),
        ],
        out_specs=pl.BlockSpec((r, d // LANES, LANES), lambda i, f, te, sp, nu: (i, 0, 0)),
        scratch_shapes=[pltpu.VMEM((2, r * ROW_PITCH, LANES), F32), pltpu.VMEM((r, d), BF16),
                        pltpu.VMEM((r, d), F32), pltpu.SemaphoreType.DMA((2,))],
    )
    return pl.pallas_call(
        functools.partial(_expert_kernel, nf=nf, r=r),
        grid_spec=grid_spec,
        out_shape=jax.ShapeDtypeStruct((p_rows, d // LANES, LANES), F32),
        compiler_params=_params("arbitrary", "arbitrary"),
        name="moe_experts",
    )(tile_expert, src_pair, n_used, h2, wg, wu, wd)


COMBINE_SLOTS = 3


def _combine_kernel(pos_ref, y_hbm, x_ref, rw_ref, mod_ref, fg_ref, o_ref, ybuf, sem, *, tm, final):
    i = pl.program_id(0)
    n = pl.num_programs(0)
    ahead = COMBINE_SLOTS - 1
    slot = i % COMBINE_SLOTS

    def gather(tile, s):
        for k in range(2):
            _gather_rows(lambda j: pos_ref[(tile * tm + j) * 2 + k], tm, y_hbm, ybuf.at[s, k], sem.at[s])

    @pl.when(i == 0)
    def _():
        for t0 in range(ahead):
            @pl.when(t0 < n)
            def _():
                gather(t0, t0)

    for k in range(2):
        _gather_wait(tm, y_hbm, ybuf.at[slot, k], sem.at[slot])

    @pl.when(i + ahead < n)
    def _():
        gather(i + ahead, (i + ahead) % COMBINE_SLOTS)

    w = rw_ref[...]
    f = w[:, 0:1] * _load_gathered(ybuf.at[slot, 0], tm) + w[:, 1:2] * _load_gathered(ybuf.at[slot, 1], tm)
    x = x_ref[...] + mod_ref[0][5:6] * f
    if final:
        x = (x * lax.rsqrt(jnp.mean(x * x, axis=-1, keepdims=True) + EPS)) * fg_ref[...]
    o_ref[...] = x


def _combine(pos, y, x2d, rw, mod3, final_g, *, seq, tm, final):
    rows, d = x2d.shape
    per_b = seq // tm
    grid_spec = pltpu.PrefetchScalarGridSpec(
        num_scalar_prefetch=1,
        grid=(rows // tm,),
        in_specs=[
            pl.BlockSpec(memory_space=pl.ANY),
            pl.BlockSpec((tm, d), lambda i, p: (i, 0)),
            pl.BlockSpec((tm, rw.shape[1]), lambda i, p: (i, 0)),
            pl.BlockSpec((1, 6, d), lambda i, p: (i // per_b, 0, 0)),
            pl.BlockSpec((1, d), lambda i, p: (0, 0)),
        ],
        out_specs=pl.BlockSpec((tm, d), lambda i, p: (i, 0)),
        scratch_shapes=[pltpu.VMEM((COMBINE_SLOTS, 2, tm * ROW_PITCH, LANES), F32),
                        pltpu.SemaphoreType.DMA((COMBINE_SLOTS,))],
    )
    return pl.pallas_call(
        functools.partial(_combine_kernel, tm=tm, final=final),
        grid_spec=grid_spec,
        out_shape=jax.ShapeDtypeStruct((rows, d), F32),
        compiler_params=_params("arbitrary"),
        name="moe_combine",
    )(pos, y, x2d, rw, mod3, final_g)


def _moe(x2d, mod3, g, wr, wg, wu, wd, final_g, *, seq, final):
    rows, d = x2d.shape
    ne = wr.shape[1]
    r = 512
    h2, ridx, rw, cnt = _route(x2d, mod3, g, wr, seq=seq, tm=1024)
    counts = cnt[:, 0].astype(jnp.int32)
    tiles_e = (counts + r - 1) // r
    tile_end = jnp.cumsum(tiles_e)
    start = (tile_end - tiles_e) * r
    pos = jnp.take(start, ridx[:, 0:2], axis=0) + ridx[:, 2:4]
    p_rows = 2 * rows + ne * r
    nt = p_rows // r
    pair_id = jnp.arange(2 * rows, dtype=jnp.int32)
    src_pair = jnp.zeros((p_rows,), jnp.int32).at[pos.reshape(-1)].set(pair_id)
    n_used = tile_end[-1:]
    tile_idx = jnp.minimum(jnp.arange(nt), n_used[0] - 1)
    tile_expert = jnp.sum(tile_idx[:, None] >= tile_end[None, :], axis=1)
    y = _experts(h2.reshape(-1, LANES), tile_expert.astype(jnp.int32), src_pair, n_used.astype(jnp.int32),
                 wg, wu, wd, r=r, tf=1792)
    return _combine(pos.reshape(-1).astype(jnp.int32), y.reshape(-1, LANES), x2d, rw, mod3, final_g,
                    seq=seq, tm=256, final=final)


def _final_norm_kernel(x_ref, g_ref, o_ref):
    x = x_ref[...]
    o_ref[...] = (x * lax.rsqrt(jnp.mean(x * x, axis=-1, keepdims=True) + EPS)) * g_ref[...]


def _final_norm(x2d, g, *, tm):
    rows, d = x2d.shape
    return pl.pallas_call(
        _final_norm_kernel,
        grid=(rows // tm,),
        in_specs=[pl.BlockSpec((tm, d), lambda i: (i, 0)), pl.BlockSpec((1, d), lambda i: (0, 0))],
        out_specs=pl.BlockSpec((tm, d), lambda i: (i, 0)),
        out_shape=jax.ShapeDtypeStruct((rows, d), F32),
        compiler_params=_params("parallel"),
        name="final_norm",
    )(x2d, g)


def _t5_bucket(dist):
    n = jnp.maximum(dist, 0)
    max_exact = N_BUCKETS // 2
    nf = jnp.maximum(n, 1).astype(F32)
    large = max_exact + (jnp.log(nf / max_exact) / math.log(REL_MAX_DIST / max_exact)
                         * (N_BUCKETS - max_exact)).astype(jnp.int32)
    large = jnp.minimum(large, N_BUCKETS - 1)
    return jnp.where(n < max_exact, n, large)


def _bias_tiles_a(rel_bias, seq, t):
    nq = seq // t
    ncol = 2 * A_HEADS
    tab = rel_bias[:, :ncol][_t5_bucket(jnp.arange(seq))].astype(F32).T
    vneg = jnp.full((ncol, t), NEG, F32)
    v = jnp.concatenate([vneg, tab], axis=1)
    u = jnp.concatenate([v[:, 1:seq + 1][:, ::-1], vneg[:, :1], v[:, seq + 1:seq + t][:, ::-1]], axis=1)

    def toeplitz_kernel(u_ref, o_ref):
        x = jnp.broadcast_to(u_ref[0], (t, seq + t))
        r = pltpu.roll(x, 0, 1, stride=1, stride_axis=0)
        for delta in range(nq):
            c0 = (nq - 1 - delta) * t
            o_ref[0, delta] = r[:, c0:c0 + t]

    return pl.pallas_call(
        toeplitz_kernel,
        grid=(ncol,),
        in_specs=[pl.BlockSpec((1, 1, seq + t), lambda c: (c, 0, 0))],
        out_specs=pl.BlockSpec((1, nq, t, t), lambda c: (c // 2, 0, c % 2, 0)),
        out_shape=jax.ShapeDtypeStruct((A_HEADS, nq, 2 * t, t), F32),
        compiler_params=_params("parallel"),
        name="bias_tiles_diff",
    )(u.reshape(ncol, 1, seq + t))


def _bias_tiles_b(rel_bias):
    ng = len(B_GROUPS)
    period = 3 * BLOCK
    rows = []
    for g, (win, dil) in enumerate(B_GROUPS):
        n_back = win // dil
        tab = rel_bias[:, 2 * A_HEADS + g * B_HEADS:2 * A_HEADS + (g + 1) * B_HEADS]
        vals = tab[_t5_bucket(jnp.arange(n_back, -1, -1) * dil)].astype(F32).T
        rows.append(jnp.concatenate([vals, jnp.full((B_HEADS, period - n_back - 1), NEG, F32)], axis=1))
    u = jnp.concatenate(rows, axis=0)

    def toeplitz_kernel(u_ref, o_ref):
        x = jnp.broadcast_to(u_ref[0], (BLOCK, period))
        r = pltpu.roll(x, 0, 1, stride=1, stride_axis=0)[:, :2 * BLOCK]
        col = lax.broadcasted_iota(jnp.int32, r.shape, 1)
        o_ref[0, 0, 0] = jnp.where(col >= BLOCK, r, NEG)
        o_ref[0, 1, 0] = r

    return pl.pallas_call(
        toeplitz_kernel,
        grid=(ng * B_HEADS,),
        in_specs=[pl.BlockSpec((1, 1, period), lambda c: (c, 0, 0))],
        out_specs=pl.BlockSpec((1, 2, 1, BLOCK, 2 * BLOCK),
                               lambda c: (c // B_HEADS, 0, (c % B_HEADS) // 2, c % 2, 0)),
        out_shape=jax.ShapeDtypeStruct((ng, 2, B_HEADS // 2, 2 * BLOCK, 2 * BLOCK), F32),
        compiler_params=_params("parallel"),
        name="bias_tiles_dilated",
    )(u.reshape(ng * B_HEADS, 1, period))


def kernel(x, c, norm_mix_g, norm_ffn_g, w_mod, b_mod, w_in, b_forget, lam_q1, lam_k1, lam_q2, lam_k2,
           subln_g, rel_bias, w_br_a, w_br_b, w_br_c, w_out, w_ff_gate, w_ff_up, w_ff_down, w_router,
           w_exp_gate, w_exp_up, w_exp_down, final_norm_g):
    bsz, seq, d = x.shape
    depth = w_mod.shape[0]
    rows = bsz * seq
    t_attn = 256
    x2d = x.reshape(rows, d)

    mod = _modulation(c, w_mod, b_mod)
    bias_a = _bias_tiles_a(rel_bias, seq, t_attn)
    bias_b = _bias_tiles_b(rel_bias)
    w_qkv_all, w_gf_all = _prep_w_in(w_in)

    final_g = final_norm_g.reshape(1, d)
    fused_final = False
    for l in range(depth):
        lam_init = 0.8 - 0.6 * math.exp(-0.3 * l)
        mod3 = mod[l].reshape(bsz, 6, d)
        w_qkv, w_gate, w_f = w_qkv_all[l], w_gf_all[l, :, :GATE_W], w_gf_all[l, :, GATE_W:]

        g_mix = norm_mix_g[l].reshape(1, d)
        qkv_ac, qkv_b0, qkv_b1, qkv_b2, f_logit = _qkv_proj(x2d, mod3, g_mix, w_qkv, w_f,
                                                            bsz=bsz, seq=seq, tm=512)
        qkv_ac = qkv_ac.reshape(bsz, seq, -1)

        b_f8 = jnp.pad(b_forget[l], (0, 8 - C_HEADS)).reshape(8, 1)
        fcum = _forget_cumsum(f_logit.reshape(bsz, seq, LANES), b_f8, col_block=0)
        fcum = fcum[:, :C_HEADS].reshape(bsz, C_HEADS // 2, 2, seq)

        oa = _attn_a(qkv_ac, bias_a, lam_q1[l].reshape(1, -1), lam_k1[l].reshape(1, -1),
                     lam_q2[l].reshape(1, -1), lam_k2[l].reshape(1, -1), subln_g[l].reshape(1, -1),
                     lam_init=lam_init, t=t_attn, nsb=8, nbb=2)
        oc = _attn_c(qkv_ac, fcum, t=t_attn, nsb=8, nbb=2)
        obs, lses = [], []
        groups = (qkv_b0.reshape(bsz, 1, seq, 3 * B_W), qkv_b1, qkv_b2)
        for g, (qkv_g, nb) in enumerate(zip(groups, (4, 1, 1))):
            o_g, lse_g = _attn_b_group(qkv_g, bias_b[g], g, nb)
            obs.append(o_g)
            lses.append(lse_g)

        x2d = _merge(oa.reshape(rows, A_W), obs, lses, oc.reshape(rows, C_W), x2d, mod3, g_mix, w_gate,
                     w_br_a[l].astype(BF16), w_br_b[l].astype(BF16), w_br_c[l].astype(BF16),
                     w_out[l].astype(BF16), seq=seq, tm=512)

        g_ffn = norm_ffn_g[l].reshape(1, d)
        if l % 2 == 0:
            j = l // 2
            x2d = _ffn(x2d, mod3, g_ffn, w_ff_gate[j].astype(BF16), w_ff_up[j].astype(BF16),
                       w_ff_down[j].astype(BF16), seq=seq, tm=512)
        else:
            j = l // 2
            fused_final = l == depth - 1
            x2d = _moe(x2d, mod3, g_ffn, w_router[j], w_exp_gate[j].astype(BF16),
                       w_exp_up[j].astype(BF16), w_exp_down[j].astype(BF16), final_g,
                       seq=seq, final=fused_final)

    if not fused_final:
        x2d = _final_norm(x2d, final_g, tm=1024)
    return x2d.reshape(bsz, seq, d)
```

```python
import functools
import math

import jax
import jax.numpy as jnp
from jax import lax
from jax.experimental import pallas as pl
from jax.experimental.pallas import tpu as pltpu

F32 = jnp.float32
BF16 = jnp.bfloat16

D_MODEL = 1024
HEAD_DIM = 64
LANES = 128
A_HEADS = 4
A_W = A_HEADS * 2 * HEAD_DIM
B_GROUPS = ((128, 1), (512, 4), (2048, 16))
B_HEADS = 6
B_W = B_HEADS * HEAD_DIM
B_QW = len(B_GROUPS) * B_W
C_HEADS = 6
C_W = C_HEADS * HEAD_DIM
N_BRANCH = 3
BLOCK = 128
N_BUCKETS = 32
REL_MAX_DIST = 2048
N_EXPERTS = 8
EPS = 1e-6
QKV_W = 3 * A_W + 3 * B_QW + 3 * C_W
GATE_W = N_BRANCH * D_MODEL
NEG = -1e30
SCALE = HEAD_DIM ** -0.5
VMEM_LIMIT = 56 * 1024 * 1024


def _params(*sem):
    return pltpu.CompilerParams(dimension_semantics=sem, vmem_limit_bytes=VMEM_LIMIT)


def _resident(a):
    return pl.BlockSpec(a.shape, lambda *_: (0, 0), pipeline_mode=pl.Buffered(1))


def _rms_mod(x, g, sc, sh):
    y = x * lax.rsqrt(jnp.mean(x * x, axis=-1, keepdims=True) + EPS)
    return (y * g) * (1.0 + sc) + sh


def _dot(a, b):
    return jnp.dot(a, b, preferred_element_type=F32)


def _dot_nt(a, b):
    return lax.dot_general(a, b, (((1,), (1,)), ((), ())), preferred_element_type=F32)


def _silu(a):
    return a * jax.nn.sigmoid(a)


def _mod_kernel(c_ref, w_ref, b_ref, o_ref):
    a = _silu(c_ref[...]).astype(BF16)
    o_ref[0] = _dot(a, w_ref[0].astype(BF16)) + b_ref[0]


def _modulation(c, w_mod, b_mod):
    depth, d, n = w_mod.shape
    bsz = c.shape[0]
    tn = 1536
    return pl.pallas_call(
        _mod_kernel,
        grid=(depth, n // tn),
        in_specs=[
            pl.BlockSpec((bsz, d), lambda l, j: (0, 0)),
            pl.BlockSpec((1, d, tn), lambda l, j: (l, 0, j)),
            pl.BlockSpec((1, 1, tn), lambda l, j: (l, 0, j)),
        ],
        out_specs=pl.BlockSpec((1, bsz, tn), lambda l, j: (l, 0, j)),
        out_shape=jax.ShapeDtypeStruct((depth, bsz, n), F32),
        compiler_params=_params("parallel", "parallel"),
        name="modulation",
    )(c, w_mod, b_mod.reshape(depth, 1, n))


_A_BLOCKS = 3 * A_W // B_W
_B_BLOCKS = 3 * B_QW // B_W
_QKV_SRC_BLOCKS = (list(range(_A_BLOCKS))
                   + list(range(_A_BLOCKS + _B_BLOCKS, QKV_W // B_W))
                   + [_A_BLOCKS + s * len(B_GROUPS) + g for g in range(len(B_GROUPS)) for s in range(3)])


def _prep_w_in_kernel(*refs):
    n = len(_QKV_SRC_BLOCKS)
    piece_refs, x_ref, y_ref, qkv_ref, gf_ref = refs[:n], refs[n], refs[n + 1], refs[n + 2], refs[n + 3]
    for j, ref in enumerate(piece_refs):
        qkv_ref[0, :, j * B_W:(j + 1) * B_W] = ref[0].astype(BF16)
    x = x_ref[0]
    rolled = pltpu.roll(x, GATE_W - C_HEADS, 1)
    tail = pltpu.roll(y_ref[0], LANES - C_HEADS, 1)
    lane = lax.broadcasted_iota(jnp.int32, tail.shape, 1)
    gf_ref[0, :, :GATE_W - LANES] = rolled[:, :GATE_W - LANES].astype(BF16)
    gf_ref[0, :, GATE_W - LANES:GATE_W] = jnp.where(lane < LANES - C_HEADS, rolled[:, GATE_W - LANES:],
                                                     tail).astype(BF16)
    gf_ref[0, :, GATE_W:] = jnp.where(lane < C_HEADS, x[:, :LANES], 0.0).astype(BF16)


def _prep_w_in(w_in):
    depth, d, _ = w_in.shape
    tr = 256
    piece = lambda c: pl.BlockSpec((1, tr, B_W), lambda l, i: (l, i, c))
    return pl.pallas_call(
        _prep_w_in_kernel,
        grid=(depth, d // tr),
        in_specs=[piece(c) for c in _QKV_SRC_BLOCKS]
        + [pl.BlockSpec((1, tr, GATE_W), lambda l, i: (l, i, QKV_W // GATE_W)),
           pl.BlockSpec((1, tr, LANES), lambda l, i: (l, i, (QKV_W + GATE_W) // LANES))],
        out_specs=[pl.BlockSpec((1, tr, QKV_W), lambda l, i: (l, i, 0)),
                   pl.BlockSpec((1, tr, GATE_W + LANES), lambda l, i: (l, i, 0))],
        out_shape=[jax.ShapeDtypeStruct((depth, d, QKV_W), BF16),
                   jax.ShapeDtypeStruct((depth, d, GATE_W + LANES), BF16)],
        compiler_params=_params("parallel", "parallel"),
        name="prep_w_in",
    )(*([w_in] * (len(_QKV_SRC_BLOCKS) + 2)))


def _qkv_kernel(x_ref, mod_ref, g_ref, w_ref, wf_ref, ac_ref, b0_ref, b1_ref, b2_ref, f_ref, h_scr, *, tm):
    m = mod_ref[0]
    h = _rms_mod(x_ref[...], g_ref[...], m[1:2], m[0:1])
    nc = h_scr.shape[0]
    for c in range(nc):
        h_scr[c] = h[:, c * LANES:(c + 1) * LANES]
    hb = h.astype(BF16)
    n_ac = ac_ref.shape[1]
    n_b = b0_ref.shape[1]
    f_ref[...] = _dot(hb, wf_ref[...])
    ac_ref[...] = _dot(hb, w_ref[:, 0:n_ac]).astype(BF16)
    b0_ref[...] = _dot(hb, w_ref[:, n_ac:n_ac + n_b]).astype(BF16)
    for gi, ref in ((1, b1_ref), (2, b2_ref)):
        dil = B_GROUPS[gi][1]
        per = tm // dil
        hp = jnp.concatenate(
            [jnp.concatenate([h_scr[c, pl.ds(r, per, stride=dil), :] for c in range(nc)], axis=1).astype(BF16)
             for r in range(dil)], axis=0)
        y = _dot(hp, w_ref[:, n_ac + gi * n_b:n_ac + (gi + 1) * n_b]).astype(BF16)
        for r in range(dil):
            ref[0, r] = y[r * per:(r + 1) * per]


def _qkv_proj(x2d, mod3, g, w, w_f, *, bsz, seq, tm):
    rows, d = x2d.shape
    per_b = seq // tm
    n_b = 3 * B_W
    n_ac = w.shape[1] - 3 * n_b
    dil1, dil2 = B_GROUPS[1][1], B_GROUPS[2][1]
    strided = lambda dil: pl.BlockSpec((1, dil, tm // dil, n_b), lambda i: (i // per_b, 0, i % per_b, 0))
    return pl.pallas_call(
        functools.partial(_qkv_kernel, tm=tm),
        grid=(rows // tm,),
        in_specs=[
            pl.BlockSpec((tm, d), lambda i: (i, 0)),
            pl.BlockSpec((1, 6, d), lambda i: (i // per_b, 0, 0)),
            pl.BlockSpec((1, d), lambda i: (0, 0)),
            _resident(w), _resident(w_f),
        ],
        out_specs=[pl.BlockSpec((tm, n_ac), lambda i: (i, 0)), pl.BlockSpec((tm, n_b), lambda i: (i, 0)),
                   strided(dil1), strided(dil2), pl.BlockSpec((tm, LANES), lambda i: (i, 0))],
        out_shape=[jax.ShapeDtypeStruct((rows, n_ac), BF16), jax.ShapeDtypeStruct((rows, n_b), BF16),
                   jax.ShapeDtypeStruct((bsz, dil1, seq // dil1, n_b), BF16),
                   jax.ShapeDtypeStruct((bsz, dil2, seq // dil2, n_b), BF16),
                   jax.ShapeDtypeStruct((rows, LANES), F32)],
        scratch_shapes=[pltpu.VMEM((d // LANES, tm, LANES), F32)],
        compiler_params=_params("parallel"),
        name="qkv_proj",
    )(x2d, mod3, g, w, w_f)


def _fcum_kernel(f_ref, b_ref, o_ref):
    z = f_ref[0].T[:8] + b_ref[...]
    x = jnp.minimum(z, 0.0) - jnp.log1p(jnp.exp(-jnp.abs(z)))
    s = x.shape[1]
    lane = lax.broadcasted_iota(jnp.int32, x.shape, 1)
    k = 1
    while k < s:
        x = x + jnp.where(lane >= k, pltpu.roll(x, k, 1), 0.0)
        k *= 2
    o_ref[0] = x


def _forget_cumsum(gf, b_f8, *, col_block):
    bsz, seq, _ = gf.shape
    return pl.pallas_call(
        _fcum_kernel,
        grid=(bsz,),
        in_specs=[
            pl.BlockSpec((1, seq, LANES), lambda b: (b, 0, col_block)),
            pl.BlockSpec((8, 1), lambda b: (0, 0)),
        ],
        out_specs=pl.BlockSpec((1, 8, seq), lambda b: (b, 0, 0)),
        out_shape=jax.ShapeDtypeStruct((bsz, 8, seq), F32),
        compiler_params=_params("parallel"),
        name="forget_cumsum",
    )(gf, b_f8)


def _half_masks(q):
    lane = lax.broadcasted_iota(jnp.int32, q.shape, 1)
    zero = jnp.zeros_like(q)
    return jnp.where(lane < HEAD_DIM, q, zero), jnp.where(lane >= HEAD_DIM, q, zero)


def _flash_init(first, v_ref, vext_scr, m_scr, acc_scr):
    @pl.when(first)
    def _():
        for bb in range(vext_scr.shape[0]):
            vext_scr[bb, :, :LANES] = v_ref[bb]
            vext_scr[bb, :, LANES:] = jnp.ones((vext_scr.shape[1], LANES), BF16)

    m_scr[...] = jnp.full(m_scr.shape, -jnp.inf, F32)
    acc_scr[...] = jnp.zeros(acc_scr.shape, F32)


def _lane_tile(a, n):
    return a if n == 1 else jnp.concatenate([a] * n, axis=1)


def _flash_update(s, vext, rows, m_scr, acc_scr):
    m_prev = m_scr[rows]
    m_new = jnp.maximum(m_prev, jnp.max(s, axis=-1, keepdims=True))
    alpha = jnp.exp(m_prev - m_new)
    p = jnp.exp(s - _lane_tile(m_new, s.shape[1] // LANES))
    acc_scr[rows] = _lane_tile(alpha, 2) * acc_scr[rows] + _dot(p.astype(BF16), vext)
    m_scr[rows] = m_new


def _flash_result(acc_scr):
    acc = acc_scr[...]
    return acc[:, :LANES] / acc[:, LANES:]


def _attn_a_kernel(q_ref, k_ref, v_ref, bias_ref, lq1, lk1, lq2, lk2, sg_ref, o_ref,
                   vext_scr, m_scr, acc_scr, *, t, nsb, nbb, lam_init):
    qi = pl.program_id(2)
    _flash_init(qi == 0, v_ref, vext_scr, m_scr, acc_scr)
    qh = [[_half_masks(q_ref[bb, sb * t:(sb + 1) * t, :] * SCALE) for sb in range(nsb)] for bb in range(nbb)]
    chain_rows = lambda bb, sb, hh: pl.ds(((bb * nsb + sb) * 2 + hh) * t, t)

    def step(kb, nkb, plan):
        off = pl.multiple_of(kb * t, t)
        for bb in range(nbb):
            kblk = k_ref[bb, pl.ds(off, nkb * t), :]
            vext = vext_scr[bb, pl.ds(off, nkb * t), :]
            for sb, deltas in plan:
                for hh in range(2):
                    bias = [bias_ref[0, d, hh * t:(hh + 1) * t, :] for d in deltas]
                    s = _dot_nt(qh[bb][sb][hh], kblk) + (bias[0] if nkb == 1 else jnp.concatenate(bias, axis=1))
                    _flash_update(s, vext, chain_rows(bb, sb, hh), m_scr, acc_scr)

    def body(kb2, carry):
        first = [qi * nsb + sb - 2 * kb2 for sb in range(nsb)]
        step(2 * kb2, 2, [(sb, (first[sb], first[sb] - 1)) for sb in range(nsb)])
        return carry

    if q_ref.shape[1] < k_ref.shape[1]:
        lax.fori_loop(0, qi * (nsb // 2), body, 0)
    for j in range(0, nsb, 2):
        step(qi * nsb + j, 1, [(j, (0,))])
        step(qi * nsb + j, 2, [(sb, (sb - j, sb - j - 1)) for sb in range(j + 1, nsb)])

    o = _flash_result(acc_scr)
    lam = (jnp.exp(jnp.sum(lq1[...] * lk1[...], axis=-1, keepdims=True))
           - jnp.exp(jnp.sum(lq2[...] * lk2[...], axis=-1, keepdims=True)) + lam_init)
    for bb in range(nbb):
        for sb in range(nsb):
            c0 = (bb * nsb + sb) * 2 * t
            d = o[c0:c0 + t] - lam * o[c0 + t:c0 + 2 * t]
            y = d * lax.rsqrt(jnp.mean(d * d, axis=-1, keepdims=True) + EPS)
            o_ref[bb, sb * t:(sb + 1) * t, :] = ((y * sg_ref[...]) * (1.0 - lam_init)).astype(o_ref.dtype)


def _flash_scratch(seq, t, nsb, nbb):
    chains = 2 * nsb * nbb
    return [pltpu.VMEM((nbb, seq, 2 * LANES), BF16), pltpu.VMEM((chains * t, LANES), F32),
            pltpu.VMEM((chains * t, 2 * LANES), F32)]


def _attn_a(qkv, bias_a, lq1, lk1, lq2, lk2, subln_g, *, lam_init, t, nsb, nbb):
    bsz, seq, _ = qkv.shape
    tq = t * nsb
    assert bsz % nbb == 0 and seq % tq == 0 and nsb % 2 == 0
    vec = lambda n: pl.BlockSpec((1, n), lambda h, b, i: (0, 0))
    return pl.pallas_call(
        functools.partial(_attn_a_kernel, t=t, nsb=nsb, nbb=nbb, lam_init=lam_init),
        grid=(A_HEADS, bsz // nbb, seq // tq),
        in_specs=[
            pl.BlockSpec((nbb, tq, LANES), lambda h, b, i: (b, i, h)),
            pl.BlockSpec((nbb, seq, LANES), lambda h, b, i: (b, 0, A_HEADS + h)),
            pl.BlockSpec((nbb, seq, LANES), lambda h, b, i: (b, 0, 2 * A_HEADS + h)),
            pl.BlockSpec((1, seq // t, 2 * t, t), lambda h, b, i: (h, 0, 0, 0)),
            vec(HEAD_DIM), vec(HEAD_DIM), vec(HEAD_DIM), vec(HEAD_DIM), vec(LANES),
        ],
        out_specs=pl.BlockSpec((nbb, tq, LANES), lambda h, b, i: (b, i, h)),
        out_shape=jax.ShapeDtypeStruct((bsz, seq, A_W), BF16),
        scratch_shapes=_flash_scratch(seq, t, nsb, nbb),
        compiler_params=_params("parallel", "parallel", "arbitrary"),
        name="attn_diff",
    )(qkv, qkv, qkv, bias_a, lq1, lk1, lq2, lk2, subln_g)


def _attn_c_kernel(q_ref, k_ref, v_ref, f_ref, o_ref, vext_scr, m_scr, acc_scr, *, t, nsb, nbb):
    qi = pl.program_id(2)
    _flash_init(qi == 0, v_ref, vext_scr, m_scr, acc_scr)
    qh = [[_half_masks(q_ref[bb, sb * t:(sb + 1) * t, :] * SCALE) for sb in range(nsb)] for bb in range(nbb)]
    chain_rows = lambda bb, sb, hh: pl.ds(((bb * nsb + sb) * 2 + hh) * t, t)
    q_off = pl.multiple_of(qi * (t * nsb), t * nsb)
    f_anchor = [[f_ref[bb, 0, :, pl.ds(pl.multiple_of(q_off + sb * t, t), LANES)][:, :1] for sb in range(nsb)]
                for bb in range(nbb)]

    def causal(nkb):
        r = lax.broadcasted_iota(jnp.int32, (t, nkb * t), 0)
        c = lax.broadcasted_iota(jnp.int32, (t, nkb * t), 1)
        return r + (nkb - 1) * t >= c

    def step(kb, nkb, sbs, diag_sb):
        off = pl.multiple_of(kb * t, t)
        for bb in range(nbb):
            kblk = k_ref[bb, pl.ds(off, nkb * t), :]
            vext = vext_scr[bb, pl.ds(off, nkb * t), :]
            f_keys = [f_ref[bb, 0, hh:hh + 1, pl.ds(off, nkb * t)] for hh in range(2)]
            for sb in sbs:
                for hh in range(2):
                    s = _dot_nt(qh[bb][sb][hh], kblk) + (f_anchor[bb][sb][hh:hh + 1] - f_keys[hh])
                    if sb == diag_sb:
                        s = jnp.where(causal(nkb), s, NEG)
                    _flash_update(s, vext, chain_rows(bb, sb, hh), m_scr, acc_scr)

    def body(kb2, carry):
        step(2 * kb2, 2, range(nsb), None)
        return carry

    if q_ref.shape[1] < k_ref.shape[1]:
        lax.fori_loop(0, qi * (nsb // 2), body, 0)
    for j in range(0, nsb, 2):
        step(qi * nsb + j, 1, [j], j)
        step(qi * nsb + j, 2, range(j + 1, nsb), j + 1)

    o = _flash_result(acc_scr)
    lane = lax.broadcasted_iota(jnp.int32, (t, LANES), 1)
    for bb in range(nbb):
        for sb in range(nsb):
            c0 = (bb * nsb + sb) * 2 * t
            pair = jnp.where(lane < HEAD_DIM, o[c0:c0 + t], o[c0 + t:c0 + 2 * t])
            o_ref[bb, sb * t:(sb + 1) * t, :] = pair.astype(o_ref.dtype)


def _attn_c(qkv, fcum, *, t, nsb, nbb):
    bsz, seq, _ = qkv.shape
    tq = t * nsb
    assert bsz % nbb == 0 and seq % tq == 0 and nsb % 2 == 0
    pairs = C_HEADS // 2
    q0 = 3 * A_W // LANES
    return pl.pallas_call(
        functools.partial(_attn_c_kernel, t=t, nsb=nsb, nbb=nbb),
        grid=(pairs, bsz // nbb, seq // tq),
        in_specs=[
            pl.BlockSpec((nbb, tq, LANES), lambda p, b, i: (b, i, q0 + p)),
            pl.BlockSpec((nbb, seq, LANES), lambda p, b, i: (b, 0, q0 + pairs + p)),
            pl.BlockSpec((nbb, seq, LANES), lambda p, b, i: (b, 0, q0 + 2 * pairs + p)),
            pl.BlockSpec((nbb, 1, 2, seq), lambda p, b, i: (b, p, 0, 0)),
        ],
        out_specs=pl.BlockSpec((nbb, tq, LANES), lambda p, b, i: (b, i, p)),
        out_shape=jax.ShapeDtypeStruct((bsz, seq, C_W), BF16),
        scratch_shapes=_flash_scratch(seq, t, nsb, nbb),
        compiler_params=_params("parallel", "parallel", "arbitrary"),
        name="attn_forget",
    )(qkv, qkv, qkv, fcum)


def _attn_b_kernel(q_ref, kp_ref, kc_ref, vp_ref, vc_ref, bias_ref, o_ref, lse_ref, *, dil, nb):
    n = pl.program_id(1)
    lane = lax.broadcasted_iota(jnp.int32, (BLOCK, LANES), 1)
    first_variant = jnp.minimum(n, 1)

    def residue(r):
        for hp in range(B_HEADS // 2):
            cols = slice(hp * LANES, (hp + 1) * LANES)
            kcat = jnp.concatenate([kp_ref[0, r, :, cols], kc_ref[0, r, :, cols]], axis=0)
            vcat = jnp.concatenate([vp_ref[0, r, :, cols], vc_ref[0, r, :, cols]], axis=0)
            for jb in range(nb):
                qh = _half_masks(q_ref[0, r, jb * BLOCK:(jb + 1) * BLOCK, cols] * SCALE)
                kwin = kcat[jb * BLOCK:(jb + 2) * BLOCK]
                vwin = vcat[jb * BLOCK:(jb + 2) * BLOCK]
                variant = first_variant if jb == 0 else 1
                outs, lses = [], []
                for hh in range(2):
                    s = _dot_nt(qh[hh], kwin) + bias_ref[variant, hp, hh * BLOCK:(hh + 1) * BLOCK, :]
                    m = jnp.max(s, axis=-1, keepdims=True)
                    e = jnp.exp(s - m)
                    den = jnp.sum(e, axis=-1, keepdims=True)
                    outs.append(_dot(e.astype(BF16), vwin) / den)
                    lses.append(jnp.broadcast_to(m + jnp.log(den), (BLOCK, LANES)))
                if dil == 1:
                    rows = pl.ds(jb * BLOCK, BLOCK)
                else:
                    rows = pl.ds(jb * BLOCK * dil + r, BLOCK, stride=dil)
                o_ref[0, hp, rows, :] = jnp.where(lane < HEAD_DIM, outs[0], outs[1])
                lse_ref[0, hp, rows, :] = jnp.where(lane < HEAD_DIM, lses[0], lses[1])

    unroll = min(dil, 4)
    if dil == unroll:
        for r in range(dil):
            residue(r)
    else:
        def body(i, carry):
            for j in range(unroll):
                residue(i * unroll + j)
            return carry

        lax.fori_loop(0, dil // unroll, body, 0)


def _attn_b_group(qkv_g, bias_g, g, nb):
    bsz, dil, m_len, _ = qkv_g.shape
    tb = BLOCK * nb
    cur = lambda c: pl.BlockSpec((1, dil, tb, B_W), lambda b, n: (b, 0, n, c))
    prev = lambda c: pl.BlockSpec((1, dil, BLOCK, B_W), lambda b, n: (b, 0, jnp.maximum(n * nb - 1, 0), c))
    pairs = B_HEADS // 2
    out_spec = pl.BlockSpec((1, pairs, tb * dil, LANES), lambda b, n: (b, 0, n, 0))
    out_sds = jax.ShapeDtypeStruct((bsz, pairs, m_len * dil, LANES), F32)
    return pl.pallas_call(
        functools.partial(_attn_b_kernel, dil=dil, nb=nb),
        grid=(bsz, m_len // tb),
        in_specs=[cur(0), prev(1), cur(1), prev(2), cur(2),
                  pl.BlockSpec(bias_g.shape, lambda b, n: (0, 0, 0, 0))],
        out_specs=[out_spec, out_spec],
        out_shape=[out_sds, out_sds],
        compiler_params=_params("parallel", "arbitrary"),
        name=f"attn_dilated_g{g}",
    )(qkv_g, qkv_g, qkv_g, qkv_g, qkv_g, bias_g)


def _merge_kernel(oa_ref, ob0, ob1, ob2, ls0, ls1, ls2, oc_ref, x_ref, mod_ref, g_ref,
                  wg_ref, wa_ref, wb_ref, wc_ref, wo_ref, o_ref):
    x = x_ref[...]
    m = mod_ref[0]
    hb = _rms_mod(x, g_ref[...], m[1:2], m[0:1]).astype(BF16)
    parts = []
    for hp in range(B_HEADS // 2):
        l0, l1, l2 = ls0[0, hp], ls1[0, hp], ls2[0, hp]
        mx = jnp.maximum(jnp.maximum(l0, l1), l2)
        e0, e1, e2 = jnp.exp(l0 - mx), jnp.exp(l1 - mx), jnp.exp(l2 - mx)
        den = e0 + e1 + e2
        parts.append((e0 / den) * ob0[0, hp] + (e1 / den) * ob1[0, hp] + (e2 / den) * ob2[0, hp])
    ob = jnp.concatenate(parts, axis=1)
    d = D_MODEL
    gate = lambda k: jax.nn.sigmoid(_dot(hb, wg_ref[:, k * d:(k + 1) * d]))
    merged = (gate(0) * _dot(oa_ref[...], wa_ref[...])
              + gate(1) * _dot(ob.astype(BF16), wb_ref[...])
              + gate(2) * _dot(oc_ref[...], wc_ref[...]))
    y = _dot(merged.astype(BF16), wo_ref[...])
    o_ref[...] = x + m[2:3] * y


def _merge(oa, obs, lses, oc, x2d, mod3, g, w_gate, wa, wb, wc, wo, *, seq, tm):
    rows, d = x2d.shape
    per_b = seq // tm
    row = lambda w: pl.BlockSpec((tm, w), lambda i: (i, 0))
    paired = pl.BlockSpec((1, B_HEADS // 2, tm, LANES), lambda i: (i // per_b, 0, i % per_b, 0))
    return pl.pallas_call(
        _merge_kernel,
        grid=(rows // tm,),
        in_specs=[row(A_W)] + [paired] * 6 + [row(C_W), row(d),
                  pl.BlockSpec((1, 6, d), lambda i: (i // per_b, 0, 0)),
                  pl.BlockSpec((1, d), lambda i: (0, 0)),
                  _resident(w_gate), _resident(wa), _resident(wb), _resident(wc), _resident(wo)],
        out_specs=row(d),
        out_shape=jax.ShapeDtypeStruct((rows, d), F32),
        compiler_params=_params("parallel"),
        name="merge_outproj",
    )(oa, *obs, *lses, oc, x2d, mod3, g, w_gate, wa, wb, wc, wo)


MXU_TILE = 256


def _ffn_kernel(x_ref, mod_ref, g_ref, wg_ref, wu_ref, wd_ref, o_ref, *, bounds):
    x = x_ref[...]
    m = mod_ref[0]
    h = _rms_mod(x, g_ref[...], m[4:5], m[3:4]).astype(BF16)
    acc = None
    for lo, hi in bounds:
        act = (_silu(_dot(h, wg_ref[:, lo:hi])) * _dot(h, wu_ref[:, lo:hi])).astype(BF16)
        part = _dot(act, wd_ref[lo:hi, :])
        acc = part if acc is None else acc + part
    o_ref[...] = x + m[5:6] * acc


def _ffn(x2d, mod3, g, wg, wu, wd, *, seq, tm):
    rows, d = x2d.shape
    dff = wg.shape[1]
    half = (dff // MXU_TILE + 1) // 2 * MXU_TILE
    per_b = seq // tm
    return pl.pallas_call(
        functools.partial(_ffn_kernel, bounds=((0, half), (half, dff))),
        grid=(rows // tm,),
        in_specs=[
            pl.BlockSpec((tm, d), lambda i: (i, 0)),
            pl.BlockSpec((1, 6, d), lambda i: (i // per_b, 0, 0)),
            pl.BlockSpec((1, d), lambda i: (0, 0)),
            _resident(wg), _resident(wu), _resident(wd),
        ],
        out_specs=pl.BlockSpec((tm, d), lambda i: (i, 0)),
        out_shape=jax.ShapeDtypeStruct((rows, d), F32),
        compiler_params=_params("parallel"),
        name="ffn_dense",
    )(x2d, mod3, g, wg, wu, wd)


def _route_kernel(x_ref, mod_ref, g_ref, wr_ref, tri_ref, h_ref, ridx_ref, rw_ref, cnt_ref, carry_scr):
    i = pl.program_id(0)

    @pl.when(i == 0)
    def _():
        carry_scr[...] = jnp.zeros(carry_scr.shape, F32)

    m = mod_ref[0]
    h = _rms_mod(x_ref[...], g_ref[...], m[4:5], m[3:4])
    _store_row_tiles(h_ref, h)
    w = wr_ref[...]
    h_hi, w_hi = h.astype(BF16), w.astype(BF16)
    h_lo = (h - h_hi.astype(F32)).astype(BF16)
    w_lo = (w - w_hi.astype(F32)).astype(BF16)
    logits = _dot_nt(w_hi, h_hi) + (_dot_nt(w_hi, h_lo) + _dot_nt(w_lo, h_hi))
    idx = lax.broadcasted_iota(jnp.int32, logits.shape, 0)
    n = logits.shape[0]
    m1 = jnp.max(logits, axis=0, keepdims=True)
    i1 = jnp.min(jnp.where(logits == m1, idx, n), axis=0, keepdims=True)
    first = idx == i1
    rest = jnp.where(first, -jnp.inf, logits)
    m2 = jnp.max(rest, axis=0, keepdims=True)
    i2 = jnp.min(jnp.where(rest == m2, idx, n), axis=0, keepdims=True)
    second = idx == i2
    e = jnp.exp(m2 - m1)
    den = 1.0 + e
    onehot = jnp.where(first | second, 1.0, 0.0)
    before = _dot(onehot.astype(BF16), tri_ref[...]) - onehot + carry_scr[...]
    rank1 = jnp.sum(jnp.where(first, before, 0.0), axis=0, keepdims=True)
    rank2 = jnp.sum(jnp.where(second, before, 0.0), axis=0, keepdims=True)
    carry_scr[...] += jnp.sum(onehot, axis=1, keepdims=True)
    picks = jnp.where(idx == 0, i1.astype(F32), jnp.where(idx == 1, i2.astype(F32),
                      jnp.where(idx == 2, rank1, jnp.where(idx == 3, rank2, 0.0))))
    weights = jnp.where(idx == 0, 1.0 / den, jnp.where(idx == 1, e / den, 0.0))
    ridx_ref[...] = picks.T.astype(jnp.int32)
    rw_ref[...] = weights.T
    cnt_ref[...] = carry_scr[...]


def _route(x2d, mod3, g, wr, *, seq, tm):
    rows, d = x2d.shape
    ne = wr.shape[1]
    per_b = seq // tm
    tri = (jnp.arange(tm)[:, None] <= jnp.arange(tm)[None, :]).astype(BF16)
    row = lambda w: pl.BlockSpec((tm, w), lambda i: (i, 0))
    return pl.pallas_call(
        _route_kernel,
        grid=(rows // tm,),
        in_specs=[row(d), pl.BlockSpec((1, 6, d), lambda i: (i // per_b, 0, 0)),
                  pl.BlockSpec((1, d), lambda i: (0, 0)), pl.BlockSpec((ne, d), lambda i: (0, 0)),
                  pl.BlockSpec((tm, tm), lambda i: (0, 0))],
        out_specs=[pl.BlockSpec((tm, d // LANES, LANES), lambda i: (i, 0, 0)), row(ne), row(ne),
                   pl.BlockSpec((ne, 1), lambda i: (0, 0))],
        out_shape=[jax.ShapeDtypeStruct((rows, d // LANES, LANES), F32),
                   jax.ShapeDtypeStruct((rows, ne), jnp.int32),
                   jax.ShapeDtypeStruct((rows, ne), F32), jax.ShapeDtypeStruct((ne, 1), F32)],
        scratch_shapes=[pltpu.VMEM((ne, 1), F32)],
        compiler_params=_params("arbitrary"),
        name="moe_route",
    )(x2d, mod3, g, wr.T, tri)


def _store_row_tiles(ref, val):
    for c in range(ref.shape[-2]):
        ref[:, c, :] = val[:, c * LANES:(c + 1) * LANES]


ROW_PITCH = D_MODEL // LANES + 1


ROW_TILE = D_MODEL // LANES


def _gather_rows(idx_of, n, src_hbm, dst, sem):
    for j in range(n):
        start = pl.multiple_of(idx_of(j) * ROW_TILE, ROW_TILE)
        pltpu.make_async_copy(src_hbm.at[pl.ds(start, ROW_TILE)], dst.at[pl.ds(j * ROW_PITCH, ROW_TILE)],
                              sem).start()


def _gather_wait(n, src_hbm, dst, sem):
    pltpu.make_async_copy(src_hbm.at[pl.ds(0, n * ROW_TILE)], dst.at[pl.ds(0, n * ROW_TILE)], sem).wait()


def _load_gathered(buf, n):
    return jnp.concatenate([buf[pl.ds(c, n, stride=ROW_PITCH), :] for c in range(ROW_TILE)], axis=1)


def _expert_kernel(te_ref, src_ref, nu_ref, h_hbm, wg_ref, wu_ref, wd_ref, y_ref,
                   xbuf, xb_scr, acc_scr, sem, *, nf, r):
    i = pl.program_id(0)
    f = pl.program_id(1)
    active = i < nu_ref[0]
    slot = i % 2

    def gather(tile, s):
        _gather_rows(lambda j: src_ref[tile * r + j] >> 1, r, h_hbm, xbuf.at[s], sem.at[s])

    @pl.when((i == 0) & (f == 0))
    def _():
        gather(0, 0)

    @pl.when(active & (f == 0))
    def _():
        _gather_wait(r, h_hbm, xbuf.at[slot], sem.at[slot])

        @pl.when(i + 1 < nu_ref[0])
        def _():
            gather(i + 1, 1 - slot)

        xb_scr[...] = _load_gathered(xbuf.at[slot], r).astype(BF16)
        acc_scr[...] = jnp.zeros(acc_scr.shape, F32)

    @pl.when(active)
    def _():
        xb = xb_scr[...]
        act = (_silu(_dot(xb, wg_ref[0])) * _dot(xb, wu_ref[0])).astype(BF16)
        acc_scr[...] += _dot(act, wd_ref[0])

    @pl.when(f == nf - 1)
    def _():
        _store_row_tiles(y_ref, jnp.where(active, acc_scr[...], 0.0))


def _experts(h2, tile_expert, src_pair, n_used, wg, wu, wd, *, r, tf):
    ne, d, dff = wg.shape
    nf = dff // tf
    p_rows = src_pair.shape[0]
    nt = p_rows // r
    fsel = lambda i, f, nu: jnp.where(i < nu[0], f, nf - 1)
    grid_spec = pltpu.PrefetchScalarGridSpec(
        num_scalar_prefetch=3,
        grid=(nt, nf),
        in_specs=[
            pl.BlockSpec(memory_space=pl.ANY),
            pl.BlockSpec((1, d, tf), lambda i, f, te, sp, nu: (te[i], 0, fsel(i, f, nu))),
            pl.BlockSpec((1, d, tf), lambda i, f, te, sp, nu: (te[i], 0, fsel(i, f, nu))),
            pl.BlockSpec((1, tf, d), lambda i, f, te, sp, nu: (te[i], fsel(i, f, nu), 0)),
        ],
        out_specs=pl.BlockSpec((r, d // LANES, LANES), lambda i, f, te, sp, nu: (i, 0, 0)),
        scratch_shapes=[pltpu.VMEM((2, r * ROW_PITCH, LANES), F32), pltpu.VMEM((r, d), BF16),
                        pltpu.VMEM((r, d), F32), pltpu.SemaphoreType.DMA((2,))],
    )
    return pl.pallas_call(
        functools.partial(_expert_kernel, nf=nf, r=r),
        grid_spec=grid_spec,
        out_shape=jax.ShapeDtypeStruct((p_rows, d // LANES, LANES), F32),
        compiler_params=_params("arbitrary", "arbitrary"),
        name="moe_experts",
    )(tile_expert, src_pair, n_used, h2, wg, wu, wd)


COMBINE_SLOTS = 3


def _combine_kernel(pos_ref, y_hbm, x_ref, rw_ref, mod_ref, fg_ref, o_ref, ybuf, sem, *, tm, final):
    i = pl.program_id(0)
    n = pl.num_programs(0)
    ahead = COMBINE_SLOTS - 1
    slot = i % COMBINE_SLOTS

    def gather(tile, s):
        for k in range(2):
            _gather_rows(lambda j: pos_ref[(tile * tm + j) * 2 + k], tm, y_hbm, ybuf.at[s, k], sem.at[s])

    @pl.when(i == 0)
    def _():
        for t0 in range(ahead):
            @pl.when(t0 < n)
            def _():
                gather(t0, t0)

    for k in range(2):
        _gather_wait(tm, y_hbm, ybuf.at[slot, k], sem.at[slot])

    @pl.when(i + ahead < n)
    def _():
        gather(i + ahead, (i + ahead) % COMBINE_SLOTS)

    w = rw_ref[...]
    f = w[:, 0:1] * _load_gathered(ybuf.at[slot, 0], tm) + w[:, 1:2] * _load_gathered(ybuf.at[slot, 1], tm)
    x = x_ref[...] + mod_ref[0][5:6] * f
    if final:
        x = (x * lax.rsqrt(jnp.mean(x * x, axis=-1, keepdims=True) + EPS)) * fg_ref[...]
    o_ref[...] = x


def _combine(pos, y, x2d, rw, mod3, final_g, *, seq, tm, final):
    rows, d = x2d.shape
    per_b = seq // tm
    grid_spec = pltpu.PrefetchScalarGridSpec(
        num_scalar_prefetch=1,
        grid=(rows // tm,),
        in_specs=[
            pl.BlockSpec(memory_space=pl.ANY),
            pl.BlockSpec((tm, d), lambda i, p: (i, 0)),
            pl.BlockSpec((tm, rw.shape[1]), lambda i, p: (i, 0)),
            pl.BlockSpec((1, 6, d), lambda i, p: (i // per_b, 0, 0)),
            pl.BlockSpec((1, d), lambda i, p: (0, 0)),
        ],
        out_specs=pl.BlockSpec((tm, d), lambda i, p: (i, 0)),
        scratch_shapes=[pltpu.VMEM((COMBINE_SLOTS, 2, tm * ROW_PITCH, LANES), F32),
                        pltpu.SemaphoreType.DMA((COMBINE_SLOTS,))],
    )
    return pl.pallas_call(
        functools.partial(_combine_kernel, tm=tm, final=final),
        grid_spec=grid_spec,
        out_shape=jax.ShapeDtypeStruct((rows, d), F32),
        compiler_params=_params("arbitrary"),
        name="moe_combine",
    )(pos, y, x2d, rw, mod3, final_g)


def _moe(x2d, mod3, g, wr, wg, wu, wd, final_g, *, seq, final):
    rows, d = x2d.shape
    ne = wr.shape[1]
    r = 512
    h2, ridx, rw, cnt = _route(x2d, mod3, g, wr, seq=seq, tm=1024)
    counts = cnt[:, 0].astype(jnp.int32)
    tiles_e = (counts + r - 1) // r
    tile_end = jnp.cumsum(tiles_e)
    start = (tile_end - tiles_e) * r
    pos = jnp.take(start, ridx[:, 0:2], axis=0) + ridx[:, 2:4]
    p_rows = 2 * rows + ne * r
    nt = p_rows // r
    pair_id = jnp.arange(2 * rows, dtype=jnp.int32)
    src_pair = jnp.zeros((p_rows,), jnp.int32).at[pos.reshape(-1)].set(pair_id)
    n_used = tile_end[-1:]
    tile_idx = jnp.minimum(jnp.arange(nt), n_used[0] - 1)
    tile_expert = jnp.sum(tile_idx[:, None] >= tile_end[None, :], axis=1)
    y = _experts(h2.reshape(-1, LANES), tile_expert.astype(jnp.int32), src_pair, n_used.astype(jnp.int32),
                 wg, wu, wd, r=r, tf=1792)
    return _combine(pos.reshape(-1).astype(jnp.int32), y.reshape(-1, LANES), x2d, rw, mod3, final_g,
                    seq=seq, tm=256, final=final)


def _final_norm_kernel(x_ref, g_ref, o_ref):
    x = x_ref[...]
    o_ref[...] = (x * lax.rsqrt(jnp.mean(x * x, axis=-1, keepdims=True) + EPS)) * g_ref[...]


def _final_norm(x2d, g, *, tm):
    rows, d = x2d.shape
    return pl.pallas_call(
        _final_norm_kernel,
        grid=(rows // tm,),
        in_specs=[pl.BlockSpec((tm, d), lambda i: (i, 0)), pl.BlockSpec((1, d), lambda i: (0, 0))],
        out_specs=pl.BlockSpec((tm, d), lambda i: (i, 0)),
        out_shape=jax.ShapeDtypeStruct((rows, d), F32),
        compiler_params=_params("parallel"),
        name="final_norm",
    )(x2d, g)


def _t5_bucket(dist):
    n = jnp.maximum(dist, 0)
    max_exact = N_BUCKETS // 2
    nf = jnp.maximum(n, 1).astype(F32)
    large = max_exact + (jnp.log(nf / max_exact) / math.log(REL_MAX_DIST / max_exact)
                         * (N_BUCKETS - max_exact)).astype(jnp.int32)
    large = jnp.minimum(large, N_BUCKETS - 1)
    return jnp.where(n < max_exact, n, large)


def _bias_tiles_a(rel_bias, seq, t):
    nq = seq // t
    ncol = 2 * A_HEADS
    tab = rel_bias[:, :ncol][_t5_bucket(jnp.arange(seq))].astype(F32).T
    vneg = jnp.full((ncol, t), NEG, F32)
    v = jnp.concatenate([vneg, tab], axis=1)
    u = jnp.concatenate([v[:, 1:seq + 1][:, ::-1], vneg[:, :1], v[:, seq + 1:seq + t][:, ::-1]], axis=1)

    def toeplitz_kernel(u_ref, o_ref):
        x = jnp.broadcast_to(u_ref[0], (t, seq + t))
        r = pltpu.roll(x, 0, 1, stride=1, stride_axis=0)
        for delta in range(nq):
            c0 = (nq - 1 - delta) * t
            o_ref[0, delta] = r[:, c0:c0 + t]

    return pl.pallas_call(
        toeplitz_kernel,
        grid=(ncol,),
        in_specs=[pl.BlockSpec((1, 1, seq + t), lambda c: (c, 0, 0))],
        out_specs=pl.BlockSpec((1, nq, t, t), lambda c: (c // 2, 0, c % 2, 0)),
        out_shape=jax.ShapeDtypeStruct((A_HEADS, nq, 2 * t, t), F32),
        compiler_params=_params("parallel"),
        name="bias_tiles_diff",
    )(u.reshape(ncol, 1, seq + t))


def _bias_tiles_b(rel_bias):
    ng = len(B_GROUPS)
    period = 3 * BLOCK
    rows = []
    for g, (win, dil) in enumerate(B_GROUPS):
        n_back = win // dil
        tab = rel_bias[:, 2 * A_HEADS + g * B_HEADS:2 * A_HEADS + (g + 1) * B_HEADS]
        vals = tab[_t5_bucket(jnp.arange(n_back, -1, -1) * dil)].astype(F32).T
        rows.append(jnp.concatenate([vals, jnp.full((B_HEADS, period - n_back - 1), NEG, F32)], axis=1))
    u = jnp.concatenate(rows, axis=0)

    def toeplitz_kernel(u_ref, o_ref):
        x = jnp.broadcast_to(u_ref[0], (BLOCK, period))
        r = pltpu.roll(x, 0, 1, stride=1, stride_axis=0)[:, :2 * BLOCK]
        col = lax.broadcasted_iota(jnp.int32, r.shape, 1)
        o_ref[0, 0, 0] = jnp.where(col >= BLOCK, r, NEG)
        o_ref[0, 1, 0] = r

    return pl.pallas_call(
        toeplitz_kernel,
        grid=(ng * B_HEADS,),
        in_specs=[pl.BlockSpec((1, 1, period), lambda c: (c, 0, 0))],
        out_specs=pl.BlockSpec((1, 2, 1, BLOCK, 2 * BLOCK),
                               lambda c: (c // B_HEADS, 0, (c % B_HEADS) // 2, c % 2, 0)),
        out_shape=jax.ShapeDtypeStruct((ng, 2, B_HEADS // 2, 2 * BLOCK, 2 * BLOCK), F32),
        compiler_params=_params("parallel"),
        name="bias_tiles_dilated",
    )(u.reshape(ng * B_HEADS, 1, period))


def kernel(x, c, norm_mix_g, norm_ffn_g, w_mod, b_mod, w_in, b_forget, lam_q1, lam_k1, lam_q2, lam_k2,
           subln_g, rel_bias, w_br_a, w_br_b, w_br_c, w_out, w_ff_gate, w_ff_up, w_ff_down, w_router,
           w_exp_gate, w_exp_up, w_exp_down, final_norm_g):
    bsz, seq, d = x.shape
    depth = w_mod.shape[0]
    rows = bsz * seq
    t_attn = 256
    x2d = x.reshape(rows, d)

    mod = _modulation(c, w_mod, b_mod)
    bias_a = _bias_tiles_a(rel_bias, seq, t_attn)
    bias_b = _bias_tiles_b(rel_bias)
    w_qkv_all, w_gf_all = _prep_w_in(w_in)

    final_g = final_norm_g.reshape(1, d)
    fused_final = False
    for l in range(depth):
        lam_init = 0.8 - 0.6 * math.exp(-0.3 * l)
        mod3 = mod[l].reshape(bsz, 6, d)
        w_qkv, w_gate, w_f = w_qkv_all[l], w_gf_all[l, :, :GATE_W], w_gf_all[l, :, GATE_W:]

        g_mix = norm_mix_g[l].reshape(1, d)
        qkv_ac, qkv_b0, qkv_b1, qkv_b2, f_logit = _qkv_proj(x2d, mod3, g_mix, w_qkv, w_f,
                                                            bsz=bsz, seq=seq, tm=512)
        qkv_ac = qkv_ac.reshape(bsz, seq, -1)

        b_f8 = jnp.pad(b_forget[l], (0, 8 - C_HEADS)).reshape(8, 1)
        fcum = _forget_cumsum(f_logit.reshape(bsz, seq, LANES), b_f8, col_block=0)
        fcum = fcum[:, :C_HEADS].reshape(bsz, C_HEADS // 2, 2, seq)

        oa = _attn_a(qkv_ac, bias_a, lam_q1[l].reshape(1, -1), lam_k1[l].reshape(1, -1),
                     lam_q2[l].reshape(1, -1), lam_k2[l].reshape(1, -1), subln_g[l].reshape(1, -1),
                     lam_init=lam_init, t=t_attn, nsb=8, nbb=2)
        oc = _attn_c(qkv_ac, fcum, t=t_attn, nsb=8, nbb=2)
        obs, lses = [], []
        groups = (qkv_b0.reshape(bsz, 1, seq, 3 * B_W), qkv_b1, qkv_b2)
        for g, (qkv_g, nb) in enumerate(zip(groups, (4, 1, 1))):
            o_g, lse_g = _attn_b_group(qkv_g, bias_b[g], g, nb)
            obs.append(o_g)
            lses.append(lse_g)

        x2d = _merge(oa.reshape(rows, A_W), obs, lses, oc.reshape(rows, C_W), x2d, mod3, g_mix, w_gate,
                     w_br_a[l].astype(BF16), w_br_b[l].astype(BF16), w_br_c[l].astype(BF16),
                     w_out[l].astype(BF16), seq=seq, tm=512)

        g_ffn = norm_ffn_g[l].reshape(1, d)
        if l % 2 == 0:
            j = l // 2
            x2d = _ffn(x2d, mod3, g_ffn, w_ff_gate[j].astype(BF16), w_ff_up[j].astype(BF16),
                       w_ff_down[j].astype(BF16), seq=seq, tm=512)
        else:
            j = l // 2
            fused_final = l == depth - 1
            x2d = _moe(x2d, mod3, g_ffn, w_router[j], w_exp_gate[j].astype(BF16),
                       w_exp_up[j].astype(BF16), w_exp_down[j].astype(BF16), final_g,
                       seq=seq, final=fused_final)

    if not fused_final:
        x2d = _final_norm(x2d, final_g, tm=1024)
    return x2d.reshape(bsz, seq, d)
```

```python
import functools
import math

import jax
import jax.numpy as jnp
from jax import lax
from jax.experimental import pallas as pl
from jax.experimental.pallas import tpu as pltpu

F32 = jnp.float32
BF16 = jnp.bfloat16

D_MODEL = 1024
HEAD_DIM = 64
LANES = 128
A_HEADS = 4
A_W = A_HEADS * 2 * HEAD_DIM
B_GROUPS = ((128, 1), (512, 4), (2048, 16))
B_HEADS = 6
B_W = B_HEADS * HEAD_DIM
B_QW = len(B_GROUPS) * B_W
C_HEADS = 6
C_W = C_HEADS * HEAD_DIM
N_BRANCH = 3
BLOCK = 128
N_BUCKETS = 32
REL_MAX_DIST = 2048
N_EXPERTS = 8
EPS = 1e-6
QKV_W = 3 * A_W + 3 * B_QW + 3 * C_W
GATE_W = N_BRANCH * D_MODEL
NEG = -1e30
SCALE = HEAD_DIM ** -0.5
VMEM_LIMIT = 56 * 1024 * 1024


MXU_TILE = 256

ROWS_PROJ = 512
ROWS_ROUTE = 1024
ROWS_EXPERT = 512
ROWS_COMBINE = 256
EXPERT_FF_CHUNK = 7 * MXU_TILE
ATTN_BLOCK = 256
ATTN_SUB_TILES = 8
ATTN_BATCH_ROWS = 2
DILATED_BLOCKS = (4, 1, 1)


def _params(*sem):
    return pltpu.CompilerParams(dimension_semantics=sem, vmem_limit_bytes=VMEM_LIMIT)


def _resident(a):
    return pl.BlockSpec(a.shape, lambda *_: (0, 0), pipeline_mode=pl.Buffered(1))


def _rms_mod(x, g, sc, sh):
    y = x * lax.rsqrt(jnp.mean(x * x, axis=-1, keepdims=True) + EPS)
    return (y * g) * (1.0 + sc) + sh


def _dot(a, b):
    return jnp.dot(a, b, preferred_element_type=F32)


def _dot_nt(a, b):
    return lax.dot_general(a, b, (((1,), (1,)), ((), ())), preferred_element_type=F32)


def _silu(a):
    return a * jax.nn.sigmoid(a)


def _mod_kernel(c_ref, w_ref, b_ref, o_ref):
    a = _silu(c_ref[...]).astype(BF16)
    o_ref[0] = _dot(a, w_ref[0].astype(BF16)) + b_ref[0]


def _modulation(c, w_mod, b_mod):
    depth, d, n = w_mod.shape
    bsz = c.shape[0]
    tn = 1536
    return pl.pallas_call(
        _mod_kernel,
        grid=(depth, n // tn),
        in_specs=[
            pl.BlockSpec((bsz, d), lambda l, j: (0, 0)),
            pl.BlockSpec((1, d, tn), lambda l, j: (l, 0, j)),
            pl.BlockSpec((1, 1, tn), lambda l, j: (l, 0, j)),
        ],
        out_specs=pl.BlockSpec((1, bsz, tn), lambda l, j: (l, 0, j)),
        out_shape=jax.ShapeDtypeStruct((depth, bsz, n), F32),
        compiler_params=_params("parallel", "parallel"),
        name="modulation",
    )(c, w_mod, b_mod.reshape(depth, 1, n))


_A_BLOCKS = 3 * A_W // B_W
_B_BLOCKS = 3 * B_QW // B_W
_QKV_SRC_BLOCKS = (list(range(_A_BLOCKS))
                   + list(range(_A_BLOCKS + _B_BLOCKS, QKV_W // B_W))
                   + [_A_BLOCKS + s * len(B_GROUPS) + g for g in range(len(B_GROUPS)) for s in range(3)])


def _prep_w_in_kernel(*refs):
    n = len(_QKV_SRC_BLOCKS)
    piece_refs, x_ref, y_ref, qkv_ref, gf_ref = refs[:n], refs[n], refs[n + 1], refs[n + 2], refs[n + 3]
    for j, ref in enumerate(piece_refs):
        qkv_ref[0, :, j * B_W:(j + 1) * B_W] = ref[0].astype(BF16)
    x = x_ref[0]
    rolled = pltpu.roll(x, GATE_W - C_HEADS, 1)
    tail = pltpu.roll(y_ref[0], LANES - C_HEADS, 1)
    lane = lax.broadcasted_iota(jnp.int32, tail.shape, 1)
    gf_ref[0, :, :GATE_W - LANES] = rolled[:, :GATE_W - LANES].astype(BF16)
    gf_ref[0, :, GATE_W - LANES:GATE_W] = jnp.where(lane < LANES - C_HEADS, rolled[:, GATE_W - LANES:],
                                                     tail).astype(BF16)
    gf_ref[0, :, GATE_W:] = jnp.where(lane < C_HEADS, x[:, :LANES], 0.0).astype(BF16)


def _prep_w_in(w_in):
    depth, d, _ = w_in.shape
    tr = 256
    piece = lambda c: pl.BlockSpec((1, tr, B_W), lambda l, i: (l, i, c))
    return pl.pallas_call(
        _prep_w_in_kernel,
        grid=(depth, d // tr),
        in_specs=[piece(c) for c in _QKV_SRC_BLOCKS]
        + [pl.BlockSpec((1, tr, GATE_W), lambda l, i: (l, i, QKV_W // GATE_W)),
           pl.BlockSpec((1, tr, LANES), lambda l, i: (l, i, (QKV_W + GATE_W) // LANES))],
        out_specs=[pl.BlockSpec((1, tr, QKV_W), lambda l, i: (l, i, 0)),
                   pl.BlockSpec((1, tr, GATE_W + LANES), lambda l, i: (l, i, 0))],
        out_shape=[jax.ShapeDtypeStruct((depth, d, QKV_W), BF16),
                   jax.ShapeDtypeStruct((depth, d, GATE_W + LANES), BF16)],
        compiler_params=_params("parallel", "parallel"),
        name="prep_w_in",
    )(*([w_in] * (len(_QKV_SRC_BLOCKS) + 2)))


def _qkv_kernel(x_ref, mod_ref, g_ref, w_ref, wf_ref, ac_ref, b0_ref, b1_ref, b2_ref, f_ref, h_scr, *, tm):
    m = mod_ref[0]
    h = _rms_mod(x_ref[...], g_ref[...], m[1:2], m[0:1])
    nc = h_scr.shape[0]
    for c in range(nc):
        h_scr[c] = h[:, c * LANES:(c + 1) * LANES]
    hb = h.astype(BF16)
    n_ac = ac_ref.shape[1]
    n_b = b0_ref.shape[1]
    f_ref[...] = _dot(hb, wf_ref[...])
    ac_ref[...] = _dot(hb, w_ref[:, 0:n_ac]).astype(BF16)
    b0_ref[...] = _dot(hb, w_ref[:, n_ac:n_ac + n_b]).astype(BF16)
    for gi, ref in ((1, b1_ref), (2, b2_ref)):
        dil = B_GROUPS[gi][1]
        per = tm // dil
        hp = jnp.concatenate(
            [jnp.concatenate([h_scr[c, pl.ds(r, per, stride=dil), :] for c in range(nc)], axis=1).astype(BF16)
             for r in range(dil)], axis=0)
        y = _dot(hp, w_ref[:, n_ac + gi * n_b:n_ac + (gi + 1) * n_b]).astype(BF16)
        for r in range(dil):
            ref[0, r] = y[r * per:(r + 1) * per]


def _qkv_proj(x2d, mod3, g, w, w_f, *, bsz, seq, tm):
    rows, d = x2d.shape
    per_b = seq // tm
    n_b = 3 * B_W
    n_ac = w.shape[1] - 3 * n_b
    dil1, dil2 = B_GROUPS[1][1], B_GROUPS[2][1]
    strided = lambda dil: pl.BlockSpec((1, dil, tm // dil, n_b), lambda i: (i // per_b, 0, i % per_b, 0))
    return pl.pallas_call(
        functools.partial(_qkv_kernel, tm=tm),
        grid=(rows // tm,),
        in_specs=[
            pl.BlockSpec((tm, d), lambda i: (i, 0)),
            pl.BlockSpec((1, 6, d), lambda i: (i // per_b, 0, 0)),
            pl.BlockSpec((1, d), lambda i: (0, 0)),
            _resident(w), _resident(w_f),
        ],
        out_specs=[pl.BlockSpec((tm, n_ac), lambda i: (i, 0)), pl.BlockSpec((tm, n_b), lambda i: (i, 0)),
                   strided(dil1), strided(dil2), pl.BlockSpec((tm, LANES), lambda i: (i, 0))],
        out_shape=[jax.ShapeDtypeStruct((rows, n_ac), BF16), jax.ShapeDtypeStruct((rows, n_b), BF16),
                   jax.ShapeDtypeStruct((bsz, dil1, seq // dil1, n_b), BF16),
                   jax.ShapeDtypeStruct((bsz, dil2, seq // dil2, n_b), BF16),
                   jax.ShapeDtypeStruct((rows, LANES), F32)],
        scratch_shapes=[pltpu.VMEM((d // LANES, tm, LANES), F32)],
        compiler_params=_params("parallel"),
        name="qkv_proj",
    )(x2d, mod3, g, w, w_f)


def _fcum_kernel(f_ref, b_ref, o_ref):
    z = f_ref[0].T[:8] + b_ref[...]
    x = jnp.minimum(z, 0.0) - jnp.log1p(jnp.exp(-jnp.abs(z)))
    s = x.shape[1]
    lane = lax.broadcasted_iota(jnp.int32, x.shape, 1)
    k = 1
    while k < s:
        x = x + jnp.where(lane >= k, pltpu.roll(x, k, 1), 0.0)
        k *= 2
    o_ref[0] = x


def _forget_cumsum(gf, b_f8, *, col_block):
    bsz, seq, _ = gf.shape
    return pl.pallas_call(
        _fcum_kernel,
        grid=(bsz,),
        in_specs=[
            pl.BlockSpec((1, seq, LANES), lambda b: (b, 0, col_block)),
            pl.BlockSpec((8, 1), lambda b: (0, 0)),
        ],
        out_specs=pl.BlockSpec((1, 8, seq), lambda b: (b, 0, 0)),
        out_shape=jax.ShapeDtypeStruct((bsz, 8, seq), F32),
        compiler_params=_params("parallel"),
        name="forget_cumsum",
    )(gf, b_f8)


def _half_masks(q):
    lane = lax.broadcasted_iota(jnp.int32, q.shape, 1)
    zero = jnp.zeros_like(q)
    return jnp.where(lane < HEAD_DIM, q, zero), jnp.where(lane >= HEAD_DIM, q, zero)


def _flash_init(first, v_ref, vext_scr, m_scr, acc_scr):
    @pl.when(first)
    def _():
        for bb in range(vext_scr.shape[0]):
            vext_scr[bb, :, :LANES] = v_ref[bb]
            vext_scr[bb, :, LANES:] = jnp.ones((vext_scr.shape[1], LANES), BF16)

    m_scr[...] = jnp.full(m_scr.shape, -jnp.inf, F32)
    acc_scr[...] = jnp.zeros(acc_scr.shape, F32)


def _lane_tile(a, n):
    return a if n == 1 else jnp.concatenate([a] * n, axis=1)


def _flash_update(s, vext, rows, m_scr, acc_scr):
    m_prev = m_scr[rows]
    m_new = jnp.maximum(m_prev, jnp.max(s, axis=-1, keepdims=True))
    alpha = jnp.exp(m_prev - m_new)
    p = jnp.exp(s - _lane_tile(m_new, s.shape[1] // LANES))
    acc_scr[rows] = _lane_tile(alpha, 2) * acc_scr[rows] + _dot(p.astype(BF16), vext)
    m_scr[rows] = m_new


def _flash_result(acc_scr):
    acc = acc_scr[...]
    return acc[:, :LANES] / acc[:, LANES:]


def _attn_a_kernel(q_ref, k_ref, v_ref, bias_ref, lq1, lk1, lq2, lk2, sg_ref, o_ref,
                   vext_scr, m_scr, acc_scr, *, t, nsb, nbb, lam_init):
    qi = pl.program_id(2)
    _flash_init(qi == 0, v_ref, vext_scr, m_scr, acc_scr)
    qh = [[_half_masks(q_ref[bb, sb * t:(sb + 1) * t, :] * SCALE) for sb in range(nsb)] for bb in range(nbb)]
    chain_rows = lambda bb, sb, hh: pl.ds(((bb * nsb + sb) * 2 + hh) * t, t)

    def step(kb, nkb, plan):
        off = pl.multiple_of(kb * t, t)
        for bb in range(nbb):
            kblk = k_ref[bb, pl.ds(off, nkb * t), :]
            vext = vext_scr[bb, pl.ds(off, nkb * t), :]
            for sb, deltas in plan:
                for hh in range(2):
                    bias = [bias_ref[0, d, hh * t:(hh + 1) * t, :] for d in deltas]
                    s = _dot_nt(qh[bb][sb][hh], kblk) + (bias[0] if nkb == 1 else jnp.concatenate(bias, axis=1))
                    _flash_update(s, vext, chain_rows(bb, sb, hh), m_scr, acc_scr)

    def body(kb2, carry):
        first = [qi * nsb + sb - 2 * kb2 for sb in range(nsb)]
        step(2 * kb2, 2, [(sb, (first[sb], first[sb] - 1)) for sb in range(nsb)])
        return carry

    if q_ref.shape[1] < k_ref.shape[1]:
        lax.fori_loop(0, qi * (nsb // 2), body, 0)
    for j in range(0, nsb, 2):
        step(qi * nsb + j, 1, [(j, (0,))])
        step(qi * nsb + j, 2, [(sb, (sb - j, sb - j - 1)) for sb in range(j + 1, nsb)])

    o = _flash_result(acc_scr)
    lam = (jnp.exp(jnp.sum(lq1[...] * lk1[...], axis=-1, keepdims=True))
           - jnp.exp(jnp.sum(lq2[...] * lk2[...], axis=-1, keepdims=True)) + lam_init)
    for bb in range(nbb):
        for sb in range(nsb):
            c0 = (bb * nsb + sb) * 2 * t
            d = o[c0:c0 + t] - lam * o[c0 + t:c0 + 2 * t]
            y = d * lax.rsqrt(jnp.mean(d * d, axis=-1, keepdims=True) + EPS)
            o_ref[bb, sb * t:(sb + 1) * t, :] = ((y * sg_ref[...]) * (1.0 - lam_init)).astype(o_ref.dtype)


def _flash_scratch(seq, t, nsb, nbb):
    chains = 2 * nsb * nbb
    return [pltpu.VMEM((nbb, seq, 2 * LANES), BF16), pltpu.VMEM((chains * t, LANES), F32),
            pltpu.VMEM((chains * t, 2 * LANES), F32)]


def _attn_a(qkv, bias_a, lq1, lk1, lq2, lk2, subln_g, *, lam_init, t, nsb, nbb):
    bsz, seq, _ = qkv.shape
    tq = t * nsb
    assert bsz % nbb == 0 and seq % tq == 0 and nsb % 2 == 0
    vec = lambda n: pl.BlockSpec((1, n), lambda h, b, i: (0, 0))
    return pl.pallas_call(
        functools.partial(_attn_a_kernel, t=t, nsb=nsb, nbb=nbb, lam_init=lam_init),
        grid=(A_HEADS, bsz // nbb, seq // tq),
        in_specs=[
            pl.BlockSpec((nbb, tq, LANES), lambda h, b, i: (b, i, h)),
            pl.BlockSpec((nbb, seq, LANES), lambda h, b, i: (b, 0, A_HEADS + h)),
            pl.BlockSpec((nbb, seq, LANES), lambda h, b, i: (b, 0, 2 * A_HEADS + h)),
            pl.BlockSpec((1, seq // t, 2 * t, t), lambda h, b, i: (h, 0, 0, 0)),
            vec(HEAD_DIM), vec(HEAD_DIM), vec(HEAD_DIM), vec(HEAD_DIM), vec(LANES),
        ],
        out_specs=pl.BlockSpec((nbb, tq, LANES), lambda h, b, i: (b, i, h)),
        out_shape=jax.ShapeDtypeStruct((bsz, seq, A_W), BF16),
        scratch_shapes=_flash_scratch(seq, t, nsb, nbb),
        compiler_params=_params("parallel", "parallel", "arbitrary"),
        name="attn_diff",
    )(qkv, qkv, qkv, bias_a, lq1, lk1, lq2, lk2, subln_g)


def _attn_c_kernel(q_ref, k_ref, v_ref, f_ref, o_ref, vext_scr, m_scr, acc_scr, *, t, nsb, nbb):
    qi = pl.program_id(2)
    _flash_init(qi == 0, v_ref, vext_scr, m_scr, acc_scr)
    qh = [[_half_masks(q_ref[bb, sb * t:(sb + 1) * t, :] * SCALE) for sb in range(nsb)] for bb in range(nbb)]
    chain_rows = lambda bb, sb, hh: pl.ds(((bb * nsb + sb) * 2 + hh) * t, t)
    q_off = pl.multiple_of(qi * (t * nsb), t * nsb)
    f_anchor = [[f_ref[bb, 0, :, pl.ds(pl.multiple_of(q_off + sb * t, t), LANES)][:, :1] for sb in range(nsb)]
                for bb in range(nbb)]

    def causal(nkb):
        r = lax.broadcasted_iota(jnp.int32, (t, nkb * t), 0)
        c = lax.broadcasted_iota(jnp.int32, (t, nkb * t), 1)
        return r + (nkb - 1) * t >= c

    def step(kb, nkb, sbs, diag_sb):
        off = pl.multiple_of(kb * t, t)
        for bb in range(nbb):
            kblk = k_ref[bb, pl.ds(off, nkb * t), :]
            vext = vext_scr[bb, pl.ds(off, nkb * t), :]
            f_keys = [f_ref[bb, 0, hh:hh + 1, pl.ds(off, nkb * t)] for hh in range(2)]
            for sb in sbs:
                for hh in range(2):
                    s = _dot_nt(qh[bb][sb][hh], kblk) + (f_anchor[bb][sb][hh:hh + 1] - f_keys[hh])
                    if sb == diag_sb:
                        s = jnp.where(causal(nkb), s, NEG)
                    _flash_update(s, vext, chain_rows(bb, sb, hh), m_scr, acc_scr)

    def body(kb2, carry):
        step(2 * kb2, 2, range(nsb), None)
        return carry

    if q_ref.shape[1] < k_ref.shape[1]:
        lax.fori_loop(0, qi * (nsb // 2), body, 0)
    for j in range(0, nsb, 2):
        step(qi * nsb + j, 1, [j], j)
        step(qi * nsb + j, 2, range(j + 1, nsb), j + 1)

    o = _flash_result(acc_scr)
    lane = lax.broadcasted_iota(jnp.int32, (t, LANES), 1)
    for bb in range(nbb):
        for sb in range(nsb):
            c0 = (bb * nsb + sb) * 2 * t
            pair = jnp.where(lane < HEAD_DIM, o[c0:c0 + t], o[c0 + t:c0 + 2 * t])
            o_ref[bb, sb * t:(sb + 1) * t, :] = pair.astype(o_ref.dtype)


def _attn_c(qkv, fcum, *, t, nsb, nbb):
    bsz, seq, _ = qkv.shape
    tq = t * nsb
    assert bsz % nbb == 0 and seq % tq == 0 and nsb % 2 == 0
    pairs = C_HEADS // 2
    q0 = 3 * A_W // LANES
    return pl.pallas_call(
        functools.partial(_attn_c_kernel, t=t, nsb=nsb, nbb=nbb),
        grid=(pairs, bsz // nbb, seq // tq),
        in_specs=[
            pl.BlockSpec((nbb, tq, LANES), lambda p, b, i: (b, i, q0 + p)),
            pl.BlockSpec((nbb, seq, LANES), lambda p, b, i: (b, 0, q0 + pairs + p)),
            pl.BlockSpec((nbb, seq, LANES), lambda p, b, i: (b, 0, q0 + 2 * pairs + p)),
            pl.BlockSpec((nbb, 1, 2, seq), lambda p, b, i: (b, p, 0, 0)),
        ],
        out_specs=pl.BlockSpec((nbb, tq, LANES), lambda p, b, i: (b, i, p)),
        out_shape=jax.ShapeDtypeStruct((bsz, seq, C_W), BF16),
        scratch_shapes=_flash_scratch(seq, t, nsb, nbb),
        compiler_params=_params("parallel", "parallel", "arbitrary"),
        name="attn_forget",
    )(qkv, qkv, qkv, fcum)


def _attn_b_kernel(q_ref, kp_ref, kc_ref, vp_ref, vc_ref, bias_ref, o_ref, lse_ref, *, dil, nb):
    n = pl.program_id(1)
    lane = lax.broadcasted_iota(jnp.int32, (BLOCK, LANES), 1)
    first_variant = jnp.minimum(n, 1)

    def residue(r):
        for hp in range(B_HEADS // 2):
            cols = slice(hp * LANES, (hp + 1) * LANES)
            kcat = jnp.concatenate([kp_ref[0, r, :, cols], kc_ref[0, r, :, cols]], axis=0)
            vcat = jnp.concatenate([vp_ref[0, r, :, cols], vc_ref[0, r, :, cols]], axis=0)
            for jb in range(nb):
                qh = _half_masks(q_ref[0, r, jb * BLOCK:(jb + 1) * BLOCK, cols] * SCALE)
                kwin = kcat[jb * BLOCK:(jb + 2) * BLOCK]
                vwin = vcat[jb * BLOCK:(jb + 2) * BLOCK]
                variant = first_variant if jb == 0 else 1
                outs, lses = [], []
                for hh in range(2):
                    s = _dot_nt(qh[hh], kwin) + bias_ref[variant, hp, hh * BLOCK:(hh + 1) * BLOCK, :]
                    m = jnp.max(s, axis=-1, keepdims=True)
                    e = jnp.exp(s - m)
                    den = jnp.sum(e, axis=-1, keepdims=True)
                    outs.append(_dot(e.astype(BF16), vwin) / den)
                    lses.append(jnp.broadcast_to(m + jnp.log(den), (BLOCK, LANES)))
                if dil == 1:
                    rows = pl.ds(jb * BLOCK, BLOCK)
                else:
                    rows = pl.ds(jb * BLOCK * dil + r, BLOCK, stride=dil)
                o_ref[0, hp, rows, :] = jnp.where(lane < HEAD_DIM, outs[0], outs[1])
                lse_ref[0, hp, rows, :] = jnp.where(lane < HEAD_DIM, lses[0], lses[1])

    unroll = min(dil, 4)
    if dil == unroll:
        for r in range(dil):
            residue(r)
    else:
        def body(i, carry):
            for j in range(unroll):
                residue(i * unroll + j)
            return carry

        lax.fori_loop(0, dil // unroll, body, 0)


def _attn_b_group(qkv_g, bias_g, g, nb):
    bsz, dil, m_len, _ = qkv_g.shape
    tb = BLOCK * nb
    cur = lambda c: pl.BlockSpec((1, dil, tb, B_W), lambda b, n: (b, 0, n, c))
    prev = lambda c: pl.BlockSpec((1, dil, BLOCK, B_W), lambda b, n: (b, 0, jnp.maximum(n * nb - 1, 0), c))
    pairs = B_HEADS // 2
    out_spec = pl.BlockSpec((1, pairs, tb * dil, LANES), lambda b, n: (b, 0, n, 0))
    out_sds = jax.ShapeDtypeStruct((bsz, pairs, m_len * dil, LANES), F32)
    return pl.pallas_call(
        functools.partial(_attn_b_kernel, dil=dil, nb=nb),
        grid=(bsz, m_len // tb),
        in_specs=[cur(0), prev(1), cur(1), prev(2), cur(2),
                  pl.BlockSpec(bias_g.shape, lambda b, n: (0, 0, 0, 0))],
        out_specs=[out_spec, out_spec],
        out_shape=[out_sds, out_sds],
        compiler_params=_params("parallel", "arbitrary"),
        name=f"attn_dilated_g{g}",
    )(qkv_g, qkv_g, qkv_g, qkv_g, qkv_g, bias_g)


def _merge_kernel(oa_ref, ob0, ob1, ob2, ls0, ls1, ls2, oc_ref, x_ref, mod_ref, g_ref,
                  wg_ref, wa_ref, wb_ref, wc_ref, wo_ref, o_ref):
    x = x_ref[...]
    m = mod_ref[0]
    hb = _rms_mod(x, g_ref[...], m[1:2], m[0:1]).astype(BF16)
    parts = []
    for hp in range(B_HEADS // 2):
        l0, l1, l2 = ls0[0, hp], ls1[0, hp], ls2[0, hp]
        mx = jnp.maximum(jnp.maximum(l0, l1), l2)
        e0, e1, e2 = jnp.exp(l0 - mx), jnp.exp(l1 - mx), jnp.exp(l2 - mx)
        den = e0 + e1 + e2
        parts.append((e0 / den) * ob0[0, hp] + (e1 / den) * ob1[0, hp] + (e2 / den) * ob2[0, hp])
    ob = jnp.concatenate(parts, axis=1)
    d = D_MODEL
    gate = lambda k: jax.nn.sigmoid(_dot(hb, wg_ref[:, k * d:(k + 1) * d]))
    merged = (gate(0) * _dot(oa_ref[...], wa_ref[...])
              + gate(1) * _dot(ob.astype(BF16), wb_ref[...])
              + gate(2) * _dot(oc_ref[...], wc_ref[...]))
    y = _dot(merged.astype(BF16), wo_ref[...])
    o_ref[...] = x + m[2:3] * y


def _merge(oa, obs, lses, oc, x2d, mod3, g, w_gate, wa, wb, wc, wo, *, seq, tm):
    rows, d = x2d.shape
    per_b = seq // tm
    row = lambda w: pl.BlockSpec((tm, w), lambda i: (i, 0))
    paired = pl.BlockSpec((1, B_HEADS // 2, tm, LANES), lambda i: (i // per_b, 0, i % per_b, 0))
    return pl.pallas_call(
        _merge_kernel,
        grid=(rows // tm,),
        in_specs=[row(A_W)] + [paired] * 6 + [row(C_W), row(d),
                  pl.BlockSpec((1, 6, d), lambda i: (i // per_b, 0, 0)),
                  pl.BlockSpec((1, d), lambda i: (0, 0)),
                  _resident(w_gate), _resident(wa), _resident(wb), _resident(wc), _resident(wo)],
        out_specs=row(d),
        out_shape=jax.ShapeDtypeStruct((rows, d), F32),
        compiler_params=_params("parallel"),
        name="merge_outproj",
    )(oa, *obs, *lses, oc, x2d, mod3, g, w_gate, wa, wb, wc, wo)


def _ffn_kernel(x_ref, mod_ref, g_ref, wg_ref, wu_ref, wd_ref, o_ref, *, bounds):
    x = x_ref[...]
    m = mod_ref[0]
    h = _rms_mod(x, g_ref[...], m[4:5], m[3:4]).astype(BF16)
    acc = None
    for lo, hi in bounds:
        act = (_silu(_dot(h, wg_ref[:, lo:hi])) * _dot(h, wu_ref[:, lo:hi])).astype(BF16)
        part = _dot(act, wd_ref[lo:hi, :])
        acc = part if acc is None else acc + part
    o_ref[...] = x + m[5:6] * acc


def _ffn(x2d, mod3, g, wg, wu, wd, *, seq, tm):
    rows, d = x2d.shape
    dff = wg.shape[1]
    half = (dff // MXU_TILE + 1) // 2 * MXU_TILE
    per_b = seq // tm
    return pl.pallas_call(
        functools.partial(_ffn_kernel, bounds=((0, half), (half, dff))),
        grid=(rows // tm,),
        in_specs=[
            pl.BlockSpec((tm, d), lambda i: (i, 0)),
            pl.BlockSpec((1, 6, d), lambda i: (i // per_b, 0, 0)),
            pl.BlockSpec((1, d), lambda i: (0, 0)),
            _resident(wg), _resident(wu), _resident(wd),
        ],
        out_specs=pl.BlockSpec((tm, d), lambda i: (i, 0)),
        out_shape=jax.ShapeDtypeStruct((rows, d), F32),
        compiler_params=_params("parallel"),
        name="ffn_dense",
    )(x2d, mod3, g, wg, wu, wd)


def _route_kernel(x_ref, mod_ref, g_ref, wr_ref, tri_ref, h_ref, ridx_ref, rw_ref, cnt_ref, carry_scr):
    i = pl.program_id(0)

    @pl.when(i == 0)
    def _():
        carry_scr[...] = jnp.zeros(carry_scr.shape, F32)

    m = mod_ref[0]
    h = _rms_mod(x_ref[...], g_ref[...], m[4:5], m[3:4])
    _store_row_tiles(h_ref, h)
    w = wr_ref[...]
    h_hi, w_hi = h.astype(BF16), w.astype(BF16)
    h_lo = (h - h_hi.astype(F32)).astype(BF16)
    w_lo = (w - w_hi.astype(F32)).astype(BF16)
    logits = _dot_nt(w_hi, h_hi) + (_dot_nt(w_hi, h_lo) + _dot_nt(w_lo, h_hi))
    idx = lax.broadcasted_iota(jnp.int32, logits.shape, 0)
    n = logits.shape[0]
    m1 = jnp.max(logits, axis=0, keepdims=True)
    i1 = jnp.min(jnp.where(logits == m1, idx, n), axis=0, keepdims=True)
    first = idx == i1
    rest = jnp.where(first, -jnp.inf, logits)
    m2 = jnp.max(rest, axis=0, keepdims=True)
    i2 = jnp.min(jnp.where(rest == m2, idx, n), axis=0, keepdims=True)
    second = idx == i2
    e = jnp.exp(m2 - m1)
    den = 1.0 + e
    onehot = jnp.where(first | second, 1.0, 0.0)
    before = _dot(onehot.astype(BF16), tri_ref[...]) - onehot + carry_scr[...]
    rank1 = jnp.sum(jnp.where(first, before, 0.0), axis=0, keepdims=True)
    rank2 = jnp.sum(jnp.where(second, before, 0.0), axis=0, keepdims=True)
    carry_scr[...] += jnp.sum(onehot, axis=1, keepdims=True)
    picks = jnp.where(idx == 0, i1.astype(F32), jnp.where(idx == 1, i2.astype(F32),
                      jnp.where(idx == 2, rank1, jnp.where(idx == 3, rank2, 0.0))))
    weights = jnp.where(idx == 0, 1.0 / den, jnp.where(idx == 1, e / den, 0.0))
    ridx_ref[...] = picks.T.astype(jnp.int32)
    rw_ref[...] = weights.T
    cnt_ref[...] = carry_scr[...]


def _route(x2d, mod3, g, wr, *, seq, tm):
    rows, d = x2d.shape
    ne = wr.shape[1]
    per_b = seq // tm
    tri = (jnp.arange(tm)[:, None] <= jnp.arange(tm)[None, :]).astype(BF16)
    row = lambda w: pl.BlockSpec((tm, w), lambda i: (i, 0))
    return pl.pallas_call(
        _route_kernel,
        grid=(rows // tm,),
        in_specs=[row(d), pl.BlockSpec((1, 6, d), lambda i: (i // per_b, 0, 0)),
                  pl.BlockSpec((1, d), lambda i: (0, 0)), pl.BlockSpec((ne, d), lambda i: (0, 0)),
                  pl.BlockSpec((tm, tm), lambda i: (0, 0))],
        out_specs=[pl.BlockSpec((tm, d // LANES, LANES), lambda i: (i, 0, 0)), row(ne), row(ne),
                   pl.BlockSpec((ne, 1), lambda i: (0, 0))],
        out_shape=[jax.ShapeDtypeStruct((rows, d // LANES, LANES), F32),
                   jax.ShapeDtypeStruct((rows, ne), jnp.int32),
                   jax.ShapeDtypeStruct((rows, ne), F32), jax.ShapeDtypeStruct((ne, 1), F32)],
        scratch_shapes=[pltpu.VMEM((ne, 1), F32)],
        compiler_params=_params("arbitrary"),
        name="moe_route",
    )(x2d, mod3, g, wr.T, tri)


def _store_row_tiles(ref, val):
    for c in range(ref.shape[-2]):
        ref[:, c, :] = val[:, c * LANES:(c + 1) * LANES]


ROW_PITCH = D_MODEL // LANES + 1


ROW_TILE = D_MODEL // LANES


def _gather_rows(idx_of, n, src_hbm, dst, sem):
    for j in range(n):
        start = pl.multiple_of(idx_of(j) * ROW_TILE, ROW_TILE)
        pltpu.make_async_copy(src_hbm.at[pl.ds(start, ROW_TILE)], dst.at[pl.ds(j * ROW_PITCH, ROW_TILE)],
                              sem).start()


def _gather_wait(n, src_hbm, dst, sem):
    pltpu.make_async_copy(src_hbm.at[pl.ds(0, n * ROW_TILE)], dst.at[pl.ds(0, n * ROW_TILE)], sem).wait()


def _load_gathered(buf, n):
    return jnp.concatenate([buf[pl.ds(c, n, stride=ROW_PITCH), :] for c in range(ROW_TILE)], axis=1)


def _expert_kernel(te_ref, src_ref, nu_ref, h_hbm, wg_ref, wu_ref, wd_ref, y_ref,
                   xbuf, xb_scr, acc_scr, sem, *, nf, r):
    i = pl.program_id(0)
    f = pl.program_id(1)
    active = i < nu_ref[0]
    slot = i % 2

    def gather(tile, s):
        _gather_rows(lambda j: src_ref[tile * r + j] >> 1, r, h_hbm, xbuf.at[s], sem.at[s])

    @pl.when((i == 0) & (f == 0))
    def _():
        gather(0, 0)

    @pl.when(active & (f == 0))
    def _():
        _gather_wait(r, h_hbm, xbuf.at[slot], sem.at[slot])

        @pl.when(i + 1 < nu_ref[0])
        def _():
            gather(i + 1, 1 - slot)

        xb_scr[...] = _load_gathered(xbuf.at[slot], r).astype(BF16)
        acc_scr[...] = jnp.zeros(acc_scr.shape, F32)

    @pl.when(active)
    def _():
        xb = xb_scr[...]
        act = (_silu(_dot(xb, wg_ref[0])) * _dot(xb, wu_ref[0])).astype(BF16)
        acc_scr[...] += _dot(act, wd_ref[0])

    @pl.when(f == nf - 1)
    def _():
        _store_row_tiles(y_ref, jnp.where(active, acc_scr[...], 0.0))


def _experts(h2, tile_expert, src_pair, n_used, wg, wu, wd, *, r, tf):
    ne, d, dff = wg.shape
    nf = dff // tf
    p_rows = src_pair.shape[0]
    nt = p_rows // r
    fsel = lambda i, f, nu: jnp.where(i < nu[0], f, nf - 1)
    grid_spec = pltpu.PrefetchScalarGridSpec(
        num_scalar_prefetch=3,
        grid=(nt, nf),
        in_specs=[
            pl.BlockSpec(memory_space=pl.ANY),
            pl.BlockSpec((1, d, tf), lambda i, f, te, sp, nu: (te[i], 0, fsel(i, f, nu))),
            pl.BlockSpec((1, d, tf), lambda i, f, te, sp, nu: (te[i], 0, fsel(i, f, nu))),
            pl.BlockSpec((1, tf, d), lambda i, f, te, sp, nu: (te[i], fsel(i, f, nu), 0)),
        ],
        out_specs=pl.BlockSpec((r, d // LANES, LANES), lambda i, f, te, sp, nu: (i, 0, 0)),
        scratch_shapes=[pltpu.VMEM((2, r * ROW_PITCH, LANES), F32), pltpu.VMEM((r, d), BF16),
                        pltpu.VMEM((r, d), F32), pltpu.SemaphoreType.DMA((2,))],
    )
    return pl.pallas_call(
        functools.partial(_expert_kernel, nf=nf, r=r),
        grid_spec=grid_spec,
        out_shape=jax.ShapeDtypeStruct((p_rows, d // LANES, LANES), F32),
        compiler_params=_params("arbitrary", "arbitrary"),
        name="moe_experts",
    )(tile_expert, src_pair, n_used, h2, wg, wu, wd)


COMBINE_SLOTS = 3


def _combine_kernel(pos_ref, y_hbm, x_ref, rw_ref, mod_ref, fg_ref, o_ref, ybuf, sem, *, tm, final):
    i = pl.program_id(0)
    n = pl.num_programs(0)
    ahead = COMBINE_SLOTS - 1
    slot = i % COMBINE_SLOTS

    def gather(tile, s):
        for k in range(2):
            _gather_rows(lambda j: pos_ref[(tile * tm + j) * 2 + k], tm, y_hbm, ybuf.at[s, k], sem.at[s])

    @pl.when(i == 0)
    def _():
        for t0 in range(ahead):
            @pl.when(t0 < n)
            def _():
                gather(t0, t0)

    for k in range(2):
        _gather_wait(tm, y_hbm, ybuf.at[slot, k], sem.at[slot])

    @pl.when(i + ahead < n)
    def _():
        gather(i + ahead, (i + ahead) % COMBINE_SLOTS)

    w = rw_ref[...]
    f = w[:, 0:1] * _load_gathered(ybuf.at[slot, 0], tm) + w[:, 1:2] * _load_gathered(ybuf.at[slot, 1], tm)
    x = x_ref[...] + mod_ref[0][5:6] * f
    if final:
        x = (x * lax.rsqrt(jnp.mean(x * x, axis=-1, keepdims=True) + EPS)) * fg_ref[...]
    o_ref[...] = x


def _combine(pos, y, x2d, rw, mod3, final_g, *, seq, tm, final):
    rows, d = x2d.shape
    per_b = seq // tm
    grid_spec = pltpu.PrefetchScalarGridSpec(
        num_scalar_prefetch=1,
        grid=(rows // tm,),
        in_specs=[
            pl.BlockSpec(memory_space=pl.ANY),
            pl.BlockSpec((tm, d), lambda i, p: (i, 0)),
            pl.BlockSpec((tm, rw.shape[1]), lambda i, p: (i, 0)),
            pl.BlockSpec((1, 6, d), lambda i, p: (i // per_b, 0, 0)),
            pl.BlockSpec((1, d), lambda i, p: (0, 0)),
        ],
        out_specs=pl.BlockSpec((tm, d), lambda i, p: (i, 0)),
        scratch_shapes=[pltpu.VMEM((COMBINE_SLOTS, 2, tm * ROW_PITCH, LANES), F32),
                        pltpu.SemaphoreType.DMA((COMBINE_SLOTS,))],
    )
    return pl.pallas_call(
        functools.partial(_combine_kernel, tm=tm, final=final),
        grid_spec=grid_spec,
        out_shape=jax.ShapeDtypeStruct((rows, d), F32),
        compiler_params=_params("arbitrary"),
        name="moe_combine",
    )(pos, y, x2d, rw, mod3, final_g)


def _moe(x2d, mod3, g, wr, wg, wu, wd, final_g, *, seq, final):
    rows, d = x2d.shape
    ne = wr.shape[1]
    r = ROWS_EXPERT
    h2, ridx, rw, cnt = _route(x2d, mod3, g, wr, seq=seq, tm=ROWS_ROUTE)
    counts = cnt[:, 0].astype(jnp.int32)
    tiles_e = (counts + r - 1) // r
    tile_end = jnp.cumsum(tiles_e)
    start = (tile_end - tiles_e) * r
    pos = jnp.take(start, ridx[:, 0:2], axis=0) + ridx[:, 2:4]
    p_rows = 2 * rows + ne * r
    nt = p_rows // r
    pair_id = jnp.arange(2 * rows, dtype=jnp.int32)
    src_pair = jnp.zeros((p_rows,), jnp.int32).at[pos.reshape(-1)].set(pair_id)
    n_used = tile_end[-1:]
    tile_idx = jnp.minimum(jnp.arange(nt), n_used[0] - 1)
    tile_expert = jnp.sum(tile_idx[:, None] >= tile_end[None, :], axis=1)
    y = _experts(h2.reshape(-1, LANES), tile_expert.astype(jnp.int32), src_pair, n_used.astype(jnp.int32),
                 wg, wu, wd, r=r, tf=EXPERT_FF_CHUNK)
    return _combine(pos.reshape(-1).astype(jnp.int32), y.reshape(-1, LANES), x2d, rw, mod3, final_g,
                    seq=seq, tm=ROWS_COMBINE, final=final)


def _final_norm_kernel(x_ref, g_ref, o_ref):
    x = x_ref[...]
    o_ref[...] = (x * lax.rsqrt(jnp.mean(x * x, axis=-1, keepdims=True) + EPS)) * g_ref[...]


def _final_norm(x2d, g, *, tm):
    rows, d = x2d.shape
    return pl.pallas_call(
        _final_norm_kernel,
        grid=(rows // tm,),
        in_specs=[pl.BlockSpec((tm, d), lambda i: (i, 0)), pl.BlockSpec((1, d), lambda i: (0, 0))],
        out_specs=pl.BlockSpec((tm, d), lambda i: (i, 0)),
        out_shape=jax.ShapeDtypeStruct((rows, d), F32),
        compiler_params=_params("parallel"),
        name="final_norm",
    )(x2d, g)


def _t5_bucket(dist):
    n = jnp.maximum(dist, 0)
    max_exact = N_BUCKETS // 2
    nf = jnp.maximum(n, 1).astype(F32)
    large = max_exact + (jnp.log(nf / max_exact) / math.log(REL_MAX_DIST / max_exact)
                         * (N_BUCKETS - max_exact)).astype(jnp.int32)
    large = jnp.minimum(large, N_BUCKETS - 1)
    return jnp.where(n < max_exact, n, large)


def _bias_tiles_a(rel_bias, seq, t):
    nq = seq // t
    ncol = 2 * A_HEADS
    tab = rel_bias[:, :ncol][_t5_bucket(jnp.arange(seq))].astype(F32).T
    vneg = jnp.full((ncol, t), NEG, F32)
    v = jnp.concatenate([vneg, tab], axis=1)
    u = jnp.concatenate([v[:, 1:seq + 1][:, ::-1], vneg[:, :1], v[:, seq + 1:seq + t][:, ::-1]], axis=1)

    def toeplitz_kernel(u_ref, o_ref):
        x = jnp.broadcast_to(u_ref[0], (t, seq + t))
        r = pltpu.roll(x, 0, 1, stride=1, stride_axis=0)
        for delta in range(nq):
            c0 = (nq - 1 - delta) * t
            o_ref[0, delta] = r[:, c0:c0 + t]

    return pl.pallas_call(
        toeplitz_kernel,
        grid=(ncol,),
        in_specs=[pl.BlockSpec((1, 1, seq + t), lambda c: (c, 0, 0))],
        out_specs=pl.BlockSpec((1, nq, t, t), lambda c: (c // 2, 0, c % 2, 0)),
        out_shape=jax.ShapeDtypeStruct((A_HEADS, nq, 2 * t, t), F32),
        compiler_params=_params("parallel"),
        name="bias_tiles_diff",
    )(u.reshape(ncol, 1, seq + t))


def _bias_tiles_b(rel_bias):
    ng = len(B_GROUPS)
    period = 3 * BLOCK
    rows = []
    for g, (win, dil) in enumerate(B_GROUPS):
        n_back = win // dil
        tab = rel_bias[:, 2 * A_HEADS + g * B_HEADS:2 * A_HEADS + (g + 1) * B_HEADS]
        vals = tab[_t5_bucket(jnp.arange(n_back, -1, -1) * dil)].astype(F32).T
        rows.append(jnp.concatenate([vals, jnp.full((B_HEADS, period - n_back - 1), NEG, F32)], axis=1))
    u = jnp.concatenate(rows, axis=0)

    def toeplitz_kernel(u_ref, o_ref):
        x = jnp.broadcast_to(u_ref[0], (BLOCK, period))
        r = pltpu.roll(x, 0, 1, stride=1, stride_axis=0)[:, :2 * BLOCK]
        col = lax.broadcasted_iota(jnp.int32, r.shape, 1)
        o_ref[0, 0, 0] = jnp.where(col >= BLOCK, r, NEG)
        o_ref[0, 1, 0] = r

    return pl.pallas_call(
        toeplitz_kernel,
        grid=(ng * B_HEADS,),
        in_specs=[pl.BlockSpec((1, 1, period), lambda c: (c, 0, 0))],
        out_specs=pl.BlockSpec((1, 2, 1, BLOCK, 2 * BLOCK),
                               lambda c: (c // B_HEADS, 0, (c % B_HEADS) // 2, c % 2, 0)),
        out_shape=jax.ShapeDtypeStruct((ng, 2, B_HEADS // 2, 2 * BLOCK, 2 * BLOCK), F32),
        compiler_params=_params("parallel"),
        name="bias_tiles_dilated",
    )(u.reshape(ng * B_HEADS, 1, period))


def kernel(x, c, norm_mix_g, norm_ffn_g, w_mod, b_mod, w_in, b_forget, lam_q1, lam_k1, lam_q2, lam_k2,
           subln_g, rel_bias, w_br_a, w_br_b, w_br_c, w_out, w_ff_gate, w_ff_up, w_ff_down, w_router,
           w_exp_gate, w_exp_up, w_exp_down, final_norm_g):
    bsz, seq, d = x.shape
    depth = w_mod.shape[0]
    rows = bsz * seq
    x2d = x.reshape(rows, d)
    attn_tiles = dict(t=ATTN_BLOCK, nsb=ATTN_SUB_TILES, nbb=ATTN_BATCH_ROWS)

    mod = _modulation(c, w_mod, b_mod)
    bias_a = _bias_tiles_a(rel_bias, seq, ATTN_BLOCK)
    bias_b = _bias_tiles_b(rel_bias)
    w_qkv_all, w_gf_all = _prep_w_in(w_in)

    final_g = final_norm_g.reshape(1, d)
    fused_final = False
    for l in range(depth):
        lam_init = 0.8 - 0.6 * math.exp(-0.3 * l)
        mod3 = mod[l].reshape(bsz, 6, d)
        w_qkv, w_gate, w_f = w_qkv_all[l], w_gf_all[l, :, :GATE_W], w_gf_all[l, :, GATE_W:]

        g_mix = norm_mix_g[l].reshape(1, d)
        qkv_ac, qkv_b0, qkv_b1, qkv_b2, f_logit = _qkv_proj(x2d, mod3, g_mix, w_qkv, w_f,
                                                            bsz=bsz, seq=seq, tm=ROWS_PROJ)
        qkv_ac = qkv_ac.reshape(bsz, seq, -1)

        b_f8 = jnp.pad(b_forget[l], (0, 8 - C_HEADS)).reshape(8, 1)
        fcum = _forget_cumsum(f_logit.reshape(bsz, seq, LANES), b_f8, col_block=0)
        fcum = fcum[:, :C_HEADS].reshape(bsz, C_HEADS // 2, 2, seq)

        oa = _attn_a(qkv_ac, bias_a, lam_q1[l].reshape(1, -1), lam_k1[l].reshape(1, -1),
                     lam_q2[l].reshape(1, -1), lam_k2[l].reshape(1, -1), subln_g[l].reshape(1, -1),
                     lam_init=lam_init, **attn_tiles)
        oc = _attn_c(qkv_ac, fcum, **attn_tiles)
        obs, lses = [], []
        groups = (qkv_b0.reshape(bsz, 1, seq, 3 * B_W), qkv_b1, qkv_b2)
        for g, (qkv_g, nb) in enumerate(zip(groups, DILATED_BLOCKS)):
            o_g, lse_g = _attn_b_group(qkv_g, bias_b[g], g, nb)
            obs.append(o_g)
            lses.append(lse_g)

        x2d = _merge(oa.reshape(rows, A_W), obs, lses, oc.reshape(rows, C_W), x2d, mod3, g_mix, w_gate,
                     w_br_a[l].astype(BF16), w_br_b[l].astype(BF16), w_br_c[l].astype(BF16),
                     w_out[l].astype(BF16), seq=seq, tm=ROWS_PROJ)

        g_ffn = norm_ffn_g[l].reshape(1, d)
        if l % 2 == 0:
            j = l // 2
            x2d = _ffn(x2d, mod3, g_ffn, w_ff_gate[j].astype(BF16), w_ff_up[j].astype(BF16),
                       w_ff_down[j].astype(BF16), seq=seq, tm=ROWS_PROJ)
        else:
            j = l // 2
            fused_final = l == depth - 1
            x2d = _moe(x2d, mod3, g_ffn, w_router[j], w_exp_gate[j].astype(BF16),
                       w_exp_up[j].astype(BF16), w_exp_down[j].astype(BF16), final_g,
                       seq=seq, final=fused_final)

    if not fused_final:
        x2d = _final_norm(x2d, final_g, tm=ROWS_ROUTE)
    return x2d.reshape(bsz, seq, d)
```

```python
import functools
import math

import jax
import jax.numpy as jnp
from jax import lax
from jax.experimental import pallas as pl
from jax.experimental.pallas import tpu as pltpu

F32 = jnp.float32
BF16 = jnp.bfloat16

D_MODEL = 1024
HEAD_DIM = 64
LANES = 128
A_HEADS = 4
A_W = A_HEADS * 2 * HEAD_DIM
B_GROUPS = ((128, 1), (512, 4), (2048, 16))
B_HEADS = 6
B_W = B_HEADS * HEAD_DIM
B_QW = len(B_GROUPS) * B_W
C_HEADS = 6
C_W = C_HEADS * HEAD_DIM
N_BRANCH = 3
BLOCK = 128
N_BUCKETS = 32
REL_MAX_DIST = 2048
N_EXPERTS = 8
EPS = 1e-6
QKV_W = 3 * A_W + 3 * B_QW + 3 * C_W
GATE_W = N_BRANCH * D_MODEL
NEG = -1e30
SCALE = HEAD_DIM ** -0.5
VMEM_LIMIT = 56 * 1024 * 1024


MXU_TILE = 256

ROWS_PROJ = 512
ROWS_ROUTE = 1024
ROWS_EXPERT = 512
ROWS_COMBINE = 256
EXPERT_FF_CHUNK = 7 * MXU_TILE
ATTN_BLOCK = 256
ATTN_SUB_TILES = 8
ATTN_BATCH_ROWS = 2
DILATED_BLOCKS = (4, 1, 1)


def _params(*sem):
    return pltpu.CompilerParams(dimension_semantics=sem, vmem_limit_bytes=VMEM_LIMIT)


def _resident(a):
    return pl.BlockSpec(a.shape, lambda *_: (0, 0), pipeline_mode=pl.Buffered(1))


def _rms_mod(x, g, sc, sh):
    y = x * lax.rsqrt(jnp.mean(x * x, axis=-1, keepdims=True) + EPS)
    return (y * g) * (1.0 + sc) + sh


def _dot(a, b):
    return jnp.dot(a, b, preferred_element_type=F32)


def _dot_nt(a, b):
    return lax.dot_general(a, b, (((1,), (1,)), ((), ())), preferred_element_type=F32)


def _silu(a):
    return a * jax.nn.sigmoid(a)


def _mod_kernel(c_ref, w_ref, b_ref, o_ref):
    a = _silu(c_ref[...]).astype(BF16)
    o_ref[0] = _dot(a, w_ref[0].astype(BF16)) + b_ref[0]


def _modulation(c, w_mod, b_mod):
    depth, d, n = w_mod.shape
    bsz = c.shape[0]
    tn = 1536
    return pl.pallas_call(
        _mod_kernel,
        grid=(depth, n // tn),
        in_specs=[
            pl.BlockSpec((bsz, d), lambda l, j: (0, 0)),
            pl.BlockSpec((1, d, tn), lambda l, j: (l, 0, j)),
            pl.BlockSpec((1, 1, tn), lambda l, j: (l, 0, j)),
        ],
        out_specs=pl.BlockSpec((1, bsz, tn), lambda l, j: (l, 0, j)),
        out_shape=jax.ShapeDtypeStruct((depth, bsz, n), F32),
        compiler_params=_params("parallel", "parallel"),
        name="modulation",
    )(c, w_mod, b_mod.reshape(depth, 1, n))


_A_BLOCKS = 3 * A_W // B_W
_B_BLOCKS = 3 * B_QW // B_W
_QKV_SRC_BLOCKS = (list(range(_A_BLOCKS))
                   + list(range(_A_BLOCKS + _B_BLOCKS, QKV_W // B_W))
                   + [_A_BLOCKS + s * len(B_GROUPS) + g for g in range(len(B_GROUPS)) for s in range(3)])


def _prep_w_in_kernel(*refs):
    n = len(_QKV_SRC_BLOCKS)
    piece_refs, x_ref, y_ref, qkv_ref, gf_ref = refs[:n], refs[n], refs[n + 1], refs[n + 2], refs[n + 3]
    for j, ref in enumerate(piece_refs):
        qkv_ref[0, :, j * B_W:(j + 1) * B_W] = ref[0].astype(BF16)
    x = x_ref[0]
    rolled = pltpu.roll(x, GATE_W - C_HEADS, 1)
    tail = pltpu.roll(y_ref[0], LANES - C_HEADS, 1)
    lane = lax.broadcasted_iota(jnp.int32, tail.shape, 1)
    gf_ref[0, :, :GATE_W - LANES] = rolled[:, :GATE_W - LANES].astype(BF16)
    gf_ref[0, :, GATE_W - LANES:GATE_W] = jnp.where(lane < LANES - C_HEADS, rolled[:, GATE_W - LANES:],
                                                     tail).astype(BF16)
    gf_ref[0, :, GATE_W:] = jnp.where(lane < C_HEADS, x[:, :LANES], 0.0).astype(BF16)


def _prep_w_in(w_in):
    depth, d, _ = w_in.shape
    tr = 256
    piece = lambda c: pl.BlockSpec((1, tr, B_W), lambda l, i: (l, i, c))
    return pl.pallas_call(
        _prep_w_in_kernel,
        grid=(depth, d // tr),
        in_specs=[piece(c) for c in _QKV_SRC_BLOCKS]
        + [pl.BlockSpec((1, tr, GATE_W), lambda l, i: (l, i, QKV_W // GATE_W)),
           pl.BlockSpec((1, tr, LANES), lambda l, i: (l, i, (QKV_W + GATE_W) // LANES))],
        out_specs=[pl.BlockSpec((1, tr, QKV_W), lambda l, i: (l, i, 0)),
                   pl.BlockSpec((1, tr, GATE_W + LANES), lambda l, i: (l, i, 0))],
        out_shape=[jax.ShapeDtypeStruct((depth, d, QKV_W), BF16),
                   jax.ShapeDtypeStruct((depth, d, GATE_W + LANES), BF16)],
        compiler_params=_params("parallel", "parallel"),
        name="prep_w_in",
    )(*([w_in] * (len(_QKV_SRC_BLOCKS) + 2)))


def _qkv_kernel(x_ref, mod_ref, g_ref, w_ref, wf_ref, ac_ref, b0_ref, b1_ref, b2_ref, f_ref, h_scr, *, tm):
    m = mod_ref[0]
    h = _rms_mod(x_ref[...], g_ref[...], m[1:2], m[0:1])
    nc = h_scr.shape[0]
    for c in range(nc):
        h_scr[c] = h[:, c * LANES:(c + 1) * LANES]
    hb = h.astype(BF16)
    n_ac = ac_ref.shape[1]
    n_b = b0_ref.shape[1]
    f_ref[...] = _dot(hb, wf_ref[...])
    ac_ref[...] = _dot(hb, w_ref[:, 0:n_ac]).astype(BF16)
    b0_ref[...] = _dot(hb, w_ref[:, n_ac:n_ac + n_b]).astype(BF16)
    for gi, ref in ((1, b1_ref), (2, b2_ref)):
        dil = B_GROUPS[gi][1]
        per = tm // dil
        hp = jnp.concatenate(
            [jnp.concatenate([h_scr[c, pl.ds(r, per, stride=dil), :] for c in range(nc)], axis=1).astype(BF16)
             for r in range(dil)], axis=0)
        y = _dot(hp, w_ref[:, n_ac + gi * n_b:n_ac + (gi + 1) * n_b]).astype(BF16)
        for r in range(dil):
            ref[0, r] = y[r * per:(r + 1) * per]


def _qkv_proj(x2d, mod3, g, w, w_f, *, bsz, seq, tm):
    rows, d = x2d.shape
    per_b = seq // tm
    n_b = 3 * B_W
    n_ac = w.shape[1] - 3 * n_b
    dil1, dil2 = B_GROUPS[1][1], B_GROUPS[2][1]
    strided = lambda dil: pl.BlockSpec((1, dil, tm // dil, n_b), lambda i: (i // per_b, 0, i % per_b, 0))
    return pl.pallas_call(
        functools.partial(_qkv_kernel, tm=tm),
        grid=(rows // tm,),
        in_specs=[
            pl.BlockSpec((tm, d), lambda i: (i, 0)),
            pl.BlockSpec((1, 6, d), lambda i: (i // per_b, 0, 0)),
            pl.BlockSpec((1, d), lambda i: (0, 0)),
            _resident(w), _resident(w_f),
        ],
        out_specs=[pl.BlockSpec((tm, n_ac), lambda i: (i, 0)), pl.BlockSpec((tm, n_b), lambda i: (i, 0)),
                   strided(dil1), strided(dil2), pl.BlockSpec((tm, LANES), lambda i: (i, 0))],
        out_shape=[jax.ShapeDtypeStruct((rows, n_ac), BF16), jax.ShapeDtypeStruct((rows, n_b), BF16),
                   jax.ShapeDtypeStruct((bsz, dil1, seq // dil1, n_b), BF16),
                   jax.ShapeDtypeStruct((bsz, dil2, seq // dil2, n_b), BF16),
                   jax.ShapeDtypeStruct((rows, LANES), F32)],
        scratch_shapes=[pltpu.VMEM((d // LANES, tm, LANES), F32)],
        compiler_params=_params("parallel"),
        name="qkv_proj",
    )(x2d, mod3, g, w, w_f)


def _fcum_kernel(f_ref, b_ref, o_ref):
    z = f_ref[0].T[:8] + b_ref[...]
    x = jnp.minimum(z, 0.0) - jnp.log1p(jnp.exp(-jnp.abs(z)))
    s = x.shape[1]
    lane = lax.broadcasted_iota(jnp.int32, x.shape, 1)
    k = 1
    while k < s:
        x = x + jnp.where(lane >= k, pltpu.roll(x, k, 1), 0.0)
        k *= 2
    o_ref[0] = x


def _forget_cumsum(gf, b_f8, *, col_block):
    bsz, seq, _ = gf.shape
    return pl.pallas_call(
        _fcum_kernel,
        grid=(bsz,),
        in_specs=[
            pl.BlockSpec((1, seq, LANES), lambda b: (b, 0, col_block)),
            pl.BlockSpec((8, 1), lambda b: (0, 0)),
        ],
        out_specs=pl.BlockSpec((1, 8, seq), lambda b: (b, 0, 0)),
        out_shape=jax.ShapeDtypeStruct((bsz, 8, seq), F32),
        compiler_params=_params("parallel"),
        name="forget_cumsum",
    )(gf, b_f8)


def _half_masks(q):
    lane = lax.broadcasted_iota(jnp.int32, q.shape, 1)
    zero = jnp.zeros_like(q)
    return jnp.where(lane < HEAD_DIM, q, zero), jnp.where(lane >= HEAD_DIM, q, zero)


def _flash_init(first, v_ref, vext_scr, m_scr, acc_scr):
    @pl.when(first)
    def _():
        for bb in range(vext_scr.shape[0]):
            vext_scr[bb, :, :LANES] = v_ref[bb]
            vext_scr[bb, :, LANES:] = jnp.ones((vext_scr.shape[1], LANES), BF16)

    m_scr[...] = jnp.full(m_scr.shape, -jnp.inf, F32)
    acc_scr[...] = jnp.zeros(acc_scr.shape, F32)


def _lane_tile(a, n):
    return a if n == 1 else jnp.concatenate([a] * n, axis=1)


def _flash_update(s, vext, rows, m_scr, acc_scr):
    m_prev = m_scr[rows]
    m_new = jnp.maximum(m_prev, jnp.max(s, axis=-1, keepdims=True))
    alpha = jnp.exp(m_prev - m_new)
    p = jnp.exp(s - _lane_tile(m_new, s.shape[1] // LANES))
    acc_scr[rows] = _lane_tile(alpha, 2) * acc_scr[rows] + _dot(p.astype(BF16), vext)
    m_scr[rows] = m_new


def _flash_result(acc_scr):
    acc = acc_scr[...]
    return acc[:, :LANES] / acc[:, LANES:]


def _attn_a_kernel(q_ref, k_ref, v_ref, bias_ref, lq1, lk1, lq2, lk2, sg_ref, o_ref,
                   vext_scr, m_scr, acc_scr, *, t, nsb, nbb, lam_init):
    qi = pl.program_id(2)
    _flash_init(qi == 0, v_ref, vext_scr, m_scr, acc_scr)
    qh = [[_half_masks(q_ref[bb, sb * t:(sb + 1) * t, :] * SCALE) for sb in range(nsb)] for bb in range(nbb)]
    chain_rows = lambda bb, sb, hh: pl.ds(((bb * nsb + sb) * 2 + hh) * t, t)

    def step(kb, nkb, plan):
        off = pl.multiple_of(kb * t, t)
        for bb in range(nbb):
            kblk = k_ref[bb, pl.ds(off, nkb * t), :]
            vext = vext_scr[bb, pl.ds(off, nkb * t), :]
            for sb, deltas in plan:
                for hh in range(2):
                    bias = [bias_ref[0, d, hh * t:(hh + 1) * t, :] for d in deltas]
                    s = _dot_nt(qh[bb][sb][hh], kblk) + (bias[0] if nkb == 1 else jnp.concatenate(bias, axis=1))
                    _flash_update(s, vext, chain_rows(bb, sb, hh), m_scr, acc_scr)

    def body(kb2, carry):
        first = [qi * nsb + sb - 2 * kb2 for sb in range(nsb)]
        step(2 * kb2, 2, [(sb, (first[sb], first[sb] - 1)) for sb in range(nsb)])
        return carry

    if q_ref.shape[1] < k_ref.shape[1]:
        lax.fori_loop(0, qi * (nsb // 2), body, 0)
    for j in range(0, nsb, 2):
        step(qi * nsb + j, 1, [(j, (0,))])
        step(qi * nsb + j, 2, [(sb, (sb - j, sb - j - 1)) for sb in range(j + 1, nsb)])

    o = _flash_result(acc_scr)
    lam = (jnp.exp(jnp.sum(lq1[...] * lk1[...], axis=-1, keepdims=True))
           - jnp.exp(jnp.sum(lq2[...] * lk2[...], axis=-1, keepdims=True)) + lam_init)
    for bb in range(nbb):
        for sb in range(nsb):
            c0 = (bb * nsb + sb) * 2 * t
            d = o[c0:c0 + t] - lam * o[c0 + t:c0 + 2 * t]
            y = d * lax.rsqrt(jnp.mean(d * d, axis=-1, keepdims=True) + EPS)
            o_ref[bb, sb * t:(sb + 1) * t, :] = ((y * sg_ref[...]) * (1.0 - lam_init)).astype(o_ref.dtype)


def _flash_scratch(seq, t, nsb, nbb):
    chains = 2 * nsb * nbb
    return [pltpu.VMEM((nbb, seq, 2 * LANES), BF16), pltpu.VMEM((chains * t, LANES), F32),
            pltpu.VMEM((chains * t, 2 * LANES), F32)]


def _attn_a(qkv, bias_a, lq1, lk1, lq2, lk2, subln_g, *, lam_init, t, nsb, nbb):
    bsz, seq, _ = qkv.shape
    tq = t * nsb
    assert bsz % nbb == 0 and seq % tq == 0 and nsb % 2 == 0
    vec = lambda n: pl.BlockSpec((1, n), lambda h, b, i: (0, 0))
    return pl.pallas_call(
        functools.partial(_attn_a_kernel, t=t, nsb=nsb, nbb=nbb, lam_init=lam_init),
        grid=(A_HEADS, bsz // nbb, seq // tq),
        in_specs=[
            pl.BlockSpec((nbb, tq, LANES), lambda h, b, i: (b, i, h)),
            pl.BlockSpec((nbb, seq, LANES), lambda h, b, i: (b, 0, A_HEADS + h)),
            pl.BlockSpec((nbb, seq, LANES), lambda h, b, i: (b, 0, 2 * A_HEADS + h)),
            pl.BlockSpec((1, seq // t, 2 * t, t), lambda h, b, i: (h, 0, 0, 0)),
            vec(HEAD_DIM), vec(HEAD_DIM), vec(HEAD_DIM), vec(HEAD_DIM), vec(LANES),
        ],
        out_specs=pl.BlockSpec((nbb, tq, LANES), lambda h, b, i: (b, i, h)),
        out_shape=jax.ShapeDtypeStruct((bsz, seq, A_W), BF16),
        scratch_shapes=_flash_scratch(seq, t, nsb, nbb),
        compiler_params=_params("parallel", "parallel", "arbitrary"),
        name="attn_diff",
    )(qkv, qkv, qkv, bias_a, lq1, lk1, lq2, lk2, subln_g)


def _attn_c_kernel(q_ref, k_ref, v_ref, f_ref, o_ref, vext_scr, m_scr, acc_scr, *, t, nsb, nbb):
    qi = pl.program_id(2)
    _flash_init(qi == 0, v_ref, vext_scr, m_scr, acc_scr)
    qh = [[_half_masks(q_ref[bb, sb * t:(sb + 1) * t, :] * SCALE) for sb in range(nsb)] for bb in range(nbb)]
    chain_rows = lambda bb, sb, hh: pl.ds(((bb * nsb + sb) * 2 + hh) * t, t)
    q_off = pl.multiple_of(qi * (t * nsb), t * nsb)
    f_anchor = [[f_ref[bb, 0, :, pl.ds(pl.multiple_of(q_off + sb * t, t), LANES)][:, :1] for sb in range(nsb)]
                for bb in range(nbb)]

    def causal(nkb):
        r = lax.broadcasted_iota(jnp.int32, (t, nkb * t), 0)
        c = lax.broadcasted_iota(jnp.int32, (t, nkb * t), 1)
        return r + (nkb - 1) * t >= c

    def step(kb, nkb, sbs, diag_sb):
        off = pl.multiple_of(kb * t, t)
        for bb in range(nbb):
            kblk = k_ref[bb, pl.ds(off, nkb * t), :]
            vext = vext_scr[bb, pl.ds(off, nkb * t), :]
            f_keys = [f_ref[bb, 0, hh:hh + 1, pl.ds(off, nkb * t)] for hh in range(2)]
            for sb in sbs:
                for hh in range(2):
                    s = _dot_nt(qh[bb][sb][hh], kblk) + (f_anchor[bb][sb][hh:hh + 1] - f_keys[hh])
                    if sb == diag_sb:
                        s = jnp.where(causal(nkb), s, NEG)
                    _flash_update(s, vext, chain_rows(bb, sb, hh), m_scr, acc_scr)

    def body(kb2, carry):
        step(2 * kb2, 2, range(nsb), None)
        return carry

    if q_ref.shape[1] < k_ref.shape[1]:
        lax.fori_loop(0, qi * (nsb // 2), body, 0)
    for j in range(0, nsb, 2):
        step(qi * nsb + j, 1, [j], j)
        step(qi * nsb + j, 2, range(j + 1, nsb), j + 1)

    o = _flash_result(acc_scr)
    lane = lax.broadcasted_iota(jnp.int32, (t, LANES), 1)
    for bb in range(nbb):
        for sb in range(nsb):
            c0 = (bb * nsb + sb) * 2 * t
            pair = jnp.where(lane < HEAD_DIM, o[c0:c0 + t], o[c0 + t:c0 + 2 * t])
            o_ref[bb, sb * t:(sb + 1) * t, :] = pair.astype(o_ref.dtype)


def _attn_c(qkv, fcum, *, t, nsb, nbb):
    bsz, seq, _ = qkv.shape
    tq = t * nsb
    assert bsz % nbb == 0 and seq % tq == 0 and nsb % 2 == 0
    pairs = C_HEADS // 2
    q0 = 3 * A_W // LANES
    return pl.pallas_call(
        functools.partial(_attn_c_kernel, t=t, nsb=nsb, nbb=nbb),
        grid=(pairs, bsz // nbb, seq // tq),
        in_specs=[
            pl.BlockSpec((nbb, tq, LANES), lambda p, b, i: (b, i, q0 + p)),
            pl.BlockSpec((nbb, seq, LANES), lambda p, b, i: (b, 0, q0 + pairs + p)),
            pl.BlockSpec((nbb, seq, LANES), lambda p, b, i: (b, 0, q0 + 2 * pairs + p)),
            pl.BlockSpec((nbb, 1, 2, seq), lambda p, b, i: (b, p, 0, 0)),
        ],
        out_specs=pl.BlockSpec((nbb, tq, LANES), lambda p, b, i: (b, i, p)),
        out_shape=jax.ShapeDtypeStruct((bsz, seq, C_W), BF16),
        scratch_shapes=_flash_scratch(seq, t, nsb, nbb),
        compiler_params=_params("parallel", "parallel", "arbitrary"),
        name="attn_forget",
    )(qkv, qkv, qkv, fcum)


def _attn_b_kernel(q_ref, kp_ref, kc_ref, vp_ref, vc_ref, bias_ref, o_ref, lse_ref, *, dil, nb):
    n = pl.program_id(1)
    lane = lax.broadcasted_iota(jnp.int32, (BLOCK, LANES), 1)
    first_variant = jnp.minimum(n, 1)

    def residue(r):
        for hp in range(B_HEADS // 2):
            cols = slice(hp * LANES, (hp + 1) * LANES)
            kcat = jnp.concatenate([kp_ref[0, r, :, cols], kc_ref[0, r, :, cols]], axis=0)
            vcat = jnp.concatenate([vp_ref[0, r, :, cols], vc_ref[0, r, :, cols]], axis=0)
            for jb in range(nb):
                qh = _half_masks(q_ref[0, r, jb * BLOCK:(jb + 1) * BLOCK, cols] * SCALE)
                kwin = kcat[jb * BLOCK:(jb + 2) * BLOCK]
                vwin = vcat[jb * BLOCK:(jb + 2) * BLOCK]
                variant = first_variant if jb == 0 else 1
                outs, lses = [], []
                for hh in range(2):
                    s = _dot_nt(qh[hh], kwin) + bias_ref[variant, hp, hh * BLOCK:(hh + 1) * BLOCK, :]
                    m = jnp.max(s, axis=-1, keepdims=True)
                    e = jnp.exp(s - m)
                    den = jnp.sum(e, axis=-1, keepdims=True)
                    outs.append(_dot(e.astype(BF16), vwin) / den)
                    lses.append(jnp.broadcast_to(m + jnp.log(den), (BLOCK, LANES)))
                if dil == 1:
                    rows = pl.ds(jb * BLOCK, BLOCK)
                else:
                    rows = pl.ds(jb * BLOCK * dil + r, BLOCK, stride=dil)
                o_ref[0, hp, rows, :] = jnp.where(lane < HEAD_DIM, outs[0], outs[1])
                lse_ref[0, hp, rows, :] = jnp.where(lane < HEAD_DIM, lses[0], lses[1])

    unroll = min(dil, 4)
    if dil == unroll:
        for r in range(dil):
            residue(r)
    else:
        def body(i, carry):
            for j in range(unroll):
                residue(i * unroll + j)
            return carry

        lax.fori_loop(0, dil // unroll, body, 0)


def _attn_b_group(qkv_g, bias_g, g, nb):
    bsz, dil, m_len, _ = qkv_g.shape
    tb = BLOCK * nb
    cur = lambda c: pl.BlockSpec((1, dil, tb, B_W), lambda b, n: (b, 0, n, c))
    prev = lambda c: pl.BlockSpec((1, dil, BLOCK, B_W), lambda b, n: (b, 0, jnp.maximum(n * nb - 1, 0), c))
    pairs = B_HEADS // 2
    out_spec = pl.BlockSpec((1, pairs, tb * dil, LANES), lambda b, n: (b, 0, n, 0))
    out_sds = jax.ShapeDtypeStruct((bsz, pairs, m_len * dil, LANES), F32)
    return pl.pallas_call(
        functools.partial(_attn_b_kernel, dil=dil, nb=nb),
        grid=(bsz, m_len // tb),
        in_specs=[cur(0), prev(1), cur(1), prev(2), cur(2),
                  pl.BlockSpec(bias_g.shape, lambda b, n: (0, 0, 0, 0))],
        out_specs=[out_spec, out_spec],
        out_shape=[out_sds, out_sds],
        compiler_params=_params("parallel", "arbitrary"),
        name=f"attn_dilated_g{g}",
    )(qkv_g, qkv_g, qkv_g, qkv_g, qkv_g, bias_g)


def _merge_kernel(oa_ref, ob0, ob1, ob2, ls0, ls1, ls2, oc_ref, x_ref, mod_ref, g_ref,
                  wg_ref, wa_ref, wb_ref, wc_ref, wo_ref, o_ref):
    x = x_ref[...]
    m = mod_ref[0]
    hb = _rms_mod(x, g_ref[...], m[1:2], m[0:1]).astype(BF16)
    parts = []
    for hp in range(B_HEADS // 2):
        l0, l1, l2 = ls0[0, hp], ls1[0, hp], ls2[0, hp]
        mx = jnp.maximum(jnp.maximum(l0, l1), l2)
        e0, e1, e2 = jnp.exp(l0 - mx), jnp.exp(l1 - mx), jnp.exp(l2 - mx)
        den = e0 + e1 + e2
        parts.append((e0 / den) * ob0[0, hp] + (e1 / den) * ob1[0, hp] + (e2 / den) * ob2[0, hp])
    ob = jnp.concatenate(parts, axis=1)
    d = D_MODEL
    gate = lambda k: jax.nn.sigmoid(_dot(hb, wg_ref[:, k * d:(k + 1) * d]))
    merged = (gate(0) * _dot(oa_ref[...], wa_ref[...])
              + gate(1) * _dot(ob.astype(BF16), wb_ref[...])
              + gate(2) * _dot(oc_ref[...], wc_ref[...]))
    y = _dot(merged.astype(BF16), wo_ref[...])
    o_ref[...] = x + m[2:3] * y


def _merge(oa, obs, lses, oc, x2d, mod3, g, w_gate, wa, wb, wc, wo, *, seq, tm):
    rows, d = x2d.shape
    per_b = seq // tm
    row = lambda w: pl.BlockSpec((tm, w), lambda i: (i, 0))
    paired = pl.BlockSpec((1, B_HEADS // 2, tm, LANES), lambda i: (i // per_b, 0, i % per_b, 0))
    return pl.pallas_call(
        _merge_kernel,
        grid=(rows // tm,),
        in_specs=[row(A_W)] + [paired] * 6 + [row(C_W), row(d),
                  pl.BlockSpec((1, 6, d), lambda i: (i // per_b, 0, 0)),
                  pl.BlockSpec((1, d), lambda i: (0, 0)),
                  _resident(w_gate), _resident(wa), _resident(wb), _resident(wc), _resident(wo)],
        out_specs=row(d),
        out_shape=jax.ShapeDtypeStruct((rows, d), F32),
        compiler_params=_params("parallel"),
        name="merge_outproj",
    )(oa, *obs, *lses, oc, x2d, mod3, g, w_gate, wa, wb, wc, wo)


def _ffn_kernel(x_ref, mod_ref, g_ref, wg_ref, wu_ref, wd_ref, o_ref, *, bounds):
    x = x_ref[...]
    m = mod_ref[0]
    h = _rms_mod(x, g_ref[...], m[4:5], m[3:4]).astype(BF16)
    acc = None
    for lo, hi in bounds:
        act = (_silu(_dot(h, wg_ref[:, lo:hi])) * _dot(h, wu_ref[:, lo:hi])).astype(BF16)
        part = _dot(act, wd_ref[lo:hi, :])
        acc = part if acc is None else acc + part
    o_ref[...] = x + m[5:6] * acc


def _ffn(x2d, mod3, g, wg, wu, wd, *, seq, tm):
    rows, d = x2d.shape
    dff = wg.shape[1]
    half = (dff // MXU_TILE + 1) // 2 * MXU_TILE
    per_b = seq // tm
    return pl.pallas_call(
        functools.partial(_ffn_kernel, bounds=((0, half), (half, dff))),
        grid=(rows // tm,),
        in_specs=[
            pl.BlockSpec((tm, d), lambda i: (i, 0)),
            pl.BlockSpec((1, 6, d), lambda i: (i // per_b, 0, 0)),
            pl.BlockSpec((1, d), lambda i: (0, 0)),
            _resident(wg), _resident(wu), _resident(wd),
        ],
        out_specs=pl.BlockSpec((tm, d), lambda i: (i, 0)),
        out_shape=jax.ShapeDtypeStruct((rows, d), F32),
        compiler_params=_params("parallel"),
        name="ffn_dense",
    )(x2d, mod3, g, wg, wu, wd)


def _route_kernel(x_ref, mod_ref, g_ref, wr_ref, tri_ref, h_ref, ridx_ref, rw_ref, cnt_ref, carry_scr):
    i = pl.program_id(0)

    @pl.when(i == 0)
    def _():
        carry_scr[...] = jnp.zeros(carry_scr.shape, F32)

    m = mod_ref[0]
    h = _rms_mod(x_ref[...], g_ref[...], m[4:5], m[3:4])
    _store_row_tiles(h_ref, h)
    w = wr_ref[...]
    h_hi, w_hi = h.astype(BF16), w.astype(BF16)
    h_lo = (h - h_hi.astype(F32)).astype(BF16)
    w_lo = (w - w_hi.astype(F32)).astype(BF16)
    logits = _dot_nt(w_hi, h_hi) + (_dot_nt(w_hi, h_lo) + _dot_nt(w_lo, h_hi))
    idx = lax.broadcasted_iota(jnp.int32, logits.shape, 0)
    n = logits.shape[0]
    m1 = jnp.max(logits, axis=0, keepdims=True)
    i1 = jnp.min(jnp.where(logits == m1, idx, n), axis=0, keepdims=True)
    first = idx == i1
    rest = jnp.where(first, -jnp.inf, logits)
    m2 = jnp.max(rest, axis=0, keepdims=True)
    i2 = jnp.min(jnp.where(rest == m2, idx, n), axis=0, keepdims=True)
    second = idx == i2
    e = jnp.exp(m2 - m1)
    den = 1.0 + e
    onehot = jnp.where(first | second, 1.0, 0.0)
    before = _dot(onehot.astype(BF16), tri_ref[...]) - onehot + carry_scr[...]
    rank1 = jnp.sum(jnp.where(first, before, 0.0), axis=0, keepdims=True)
    rank2 = jnp.sum(jnp.where(second, before, 0.0), axis=0, keepdims=True)
    carry_scr[...] += jnp.sum(onehot, axis=1, keepdims=True)
    picks = jnp.where(idx == 0, i1.astype(F32), jnp.where(idx == 1, i2.astype(F32),
                      jnp.where(idx == 2, rank1, jnp.where(idx == 3, rank2, 0.0))))
    weights = jnp.where(idx == 0, 1.0 / den, jnp.where(idx == 1, e / den, 0.0))
    ridx_ref[...] = picks.T.astype(jnp.int32)
    rw_ref[...] = weights.T
    cnt_ref[...] = carry_scr[...]


def _route(x2d, mod3, g, wr, *, seq, tm):
    rows, d = x2d.shape
    ne = wr.shape[1]
    per_b = seq // tm
    tri = (jnp.arange(tm)[:, None] <= jnp.arange(tm)[None, :]).astype(BF16)
    row = lambda w: pl.BlockSpec((tm, w), lambda i: (i, 0))
    return pl.pallas_call(
        _route_kernel,
        grid=(rows // tm,),
        in_specs=[row(d), pl.BlockSpec((1, 6, d), lambda i: (i // per_b, 0, 0)),
                  pl.BlockSpec((1, d), lambda i: (0, 0)), pl.BlockSpec((ne, d), lambda i: (0, 0)),
                  pl.BlockSpec((tm, tm), lambda i: (0, 0))],
        out_specs=[pl.BlockSpec((tm, d // LANES, LANES), lambda i: (i, 0, 0)), row(ne), row(ne),
                   pl.BlockSpec((ne, 1), lambda i: (0, 0))],
        out_shape=[jax.ShapeDtypeStruct((rows, d // LANES, LANES), F32),
                   jax.ShapeDtypeStruct((rows, ne), jnp.int32),
                   jax.ShapeDtypeStruct((rows, ne), F32), jax.ShapeDtypeStruct((ne, 1), F32)],
        scratch_shapes=[pltpu.VMEM((ne, 1), F32)],
        compiler_params=_params("arbitrary"),
        name="moe_route",
    )(x2d, mod3, g, wr.T, tri)


def _store_row_tiles(ref, val):
    for c in range(ref.shape[-2]):
        ref[:, c, :] = val[:, c * LANES:(c + 1) * LANES]


ROW_PITCH = D_MODEL // LANES + 1


ROW_TILE = D_MODEL // LANES


def _gather_rows(idx_of, n, src_hbm, dst, sem):
    for j in range(n):
        start = pl.multiple_of(idx_of(j) * ROW_TILE, ROW_TILE)
        pltpu.make_async_copy(src_hbm.at[pl.ds(start, ROW_TILE)], dst.at[pl.ds(j * ROW_PITCH, ROW_TILE)],
                              sem).start()


def _gather_wait(n, src_hbm, dst, sem):
    pltpu.make_async_copy(src_hbm.at[pl.ds(0, n * ROW_TILE)], dst.at[pl.ds(0, n * ROW_TILE)], sem).wait()


def _load_gathered(buf, n):
    return jnp.concatenate([buf[pl.ds(c, n, stride=ROW_PITCH), :] for c in range(ROW_TILE)], axis=1)


def _expert_kernel(te_ref, pos_ref, nu_ref, h_hbm, zeros_hbm, wg_ref, wu_ref, wd_ref, y_ref,
                   xbuf, xb_scr, acc_scr, src_smem, sem, *, nf, r):
    i = pl.program_id(0)
    f = pl.program_id(1)
    active = i < nu_ref[0]
    slot = i % 2

    def gather(tile, s):
        _gather_rows(lambda j: src_smem[tile * r + j], r, h_hbm, xbuf.at[s], sem.at[s])

    @pl.when((i == 0) & (f == 0))
    def _():
        clear = pltpu.make_async_copy(zeros_hbm, src_smem, sem.at[0])
        clear.start()
        clear.wait()

        def place(p, carry):
            src_smem[pos_ref[p]] = p >> 1
            return carry

        lax.fori_loop(0, pos_ref.shape[0], place, 0, unroll=8)
        gather(0, 0)

    @pl.when(active & (f == 0))
    def _():
        _gather_wait(r, h_hbm, xbuf.at[slot], sem.at[slot])

        @pl.when(i + 1 < nu_ref[0])
        def _():
            gather(i + 1, 1 - slot)

        xb_scr[...] = _load_gathered(xbuf.at[slot], r).astype(BF16)
        acc_scr[...] = jnp.zeros(acc_scr.shape, F32)

    @pl.when(active)
    def _():
        xb = xb_scr[...]
        act = (_silu(_dot(xb, wg_ref[0])) * _dot(xb, wu_ref[0])).astype(BF16)
        acc_scr[...] += _dot(act, wd_ref[0])

    @pl.when(f == nf - 1)
    def _():
        _store_row_tiles(y_ref, jnp.where(active, acc_scr[...], 0.0))


def _experts(h2, tile_expert, pos, n_used, wg, wu, wd, *, r, tf):
    ne, d, dff = wg.shape
    nf = dff // tf
    nt = tile_expert.shape[0]
    p_rows = nt * r
    fsel = lambda i, f, nu: jnp.where(i < nu[0], f, nf - 1)
    grid_spec = pltpu.PrefetchScalarGridSpec(
        num_scalar_prefetch=3,
        grid=(nt, nf),
        in_specs=[
            pl.BlockSpec(memory_space=pl.ANY),
            pl.BlockSpec(memory_space=pl.ANY),
            pl.BlockSpec((1, d, tf), lambda i, f, te, sp, nu: (te[i], 0, fsel(i, f, nu))),
            pl.BlockSpec((1, d, tf), lambda i, f, te, sp, nu: (te[i], 0, fsel(i, f, nu))),
            pl.BlockSpec((1, tf, d), lambda i, f, te, sp, nu: (te[i], fsel(i, f, nu), 0)),
        ],
        out_specs=pl.BlockSpec((r, d // LANES, LANES), lambda i, f, te, sp, nu: (i, 0, 0)),
        scratch_shapes=[pltpu.VMEM((2, r * ROW_PITCH, LANES), F32), pltpu.VMEM((r, d), BF16),
                        pltpu.VMEM((r, d), F32), pltpu.SMEM((p_rows,), jnp.int32),
                        pltpu.SemaphoreType.DMA((2,))],
    )
    return pl.pallas_call(
        functools.partial(_expert_kernel, nf=nf, r=r),
        grid_spec=grid_spec,
        out_shape=jax.ShapeDtypeStruct((p_rows, d // LANES, LANES), F32),
        compiler_params=_params("arbitrary", "arbitrary"),
        name="moe_experts",
    )(tile_expert, pos, n_used, h2, jnp.zeros((p_rows,), jnp.int32), wg, wu, wd)


COMBINE_SLOTS = 3


def _combine_kernel(pos_ref, y_hbm, x_ref, rw_ref, mod_ref, fg_ref, o_ref, ybuf, sem, *, tm, final):
    i = pl.program_id(0)
    n = pl.num_programs(0)
    ahead = COMBINE_SLOTS - 1
    slot = i % COMBINE_SLOTS

    def gather(tile, s):
        for k in range(2):
            _gather_rows(lambda j: pos_ref[(tile * tm + j) * 2 + k], tm, y_hbm, ybuf.at[s, k], sem.at[s])

    @pl.when(i == 0)
    def _():
        for t0 in range(ahead):
            @pl.when(t0 < n)
            def _():
                gather(t0, t0)

    for k in range(2):
        _gather_wait(tm, y_hbm, ybuf.at[slot, k], sem.at[slot])

    @pl.when(i + ahead < n)
    def _():
        gather(i + ahead, (i + ahead) % COMBINE_SLOTS)

    w = rw_ref[...]
    f = w[:, 0:1] * _load_gathered(ybuf.at[slot, 0], tm) + w[:, 1:2] * _load_gathered(ybuf.at[slot, 1], tm)
    x = x_ref[...] + mod_ref[0][5:6] * f
    if final:
        x = (x * lax.rsqrt(jnp.mean(x * x, axis=-1, keepdims=True) + EPS)) * fg_ref[...]
    o_ref[...] = x


def _combine(pos, y, x2d, rw, mod3, final_g, *, seq, tm, final):
    rows, d = x2d.shape
    per_b = seq // tm
    grid_spec = pltpu.PrefetchScalarGridSpec(
        num_scalar_prefetch=1,
        grid=(rows // tm,),
        in_specs=[
            pl.BlockSpec(memory_space=pl.ANY),
            pl.BlockSpec((tm, d), lambda i, p: (i, 0)),
            pl.BlockSpec((tm, rw.shape[1]), lambda i, p: (i, 0)),
            pl.BlockSpec((1, 6, d), lambda i, p: (i // per_b, 0, 0)),
            pl.BlockSpec((1, d), lambda i, p: (0, 0)),
        ],
        out_specs=pl.BlockSpec((tm, d), lambda i, p: (i, 0)),
        scratch_shapes=[pltpu.VMEM((COMBINE_SLOTS, 2, tm * ROW_PITCH, LANES), F32),
                        pltpu.SemaphoreType.DMA((COMBINE_SLOTS,))],
    )
    return pl.pallas_call(
        functools.partial(_combine_kernel, tm=tm, final=final),
        grid_spec=grid_spec,
        out_shape=jax.ShapeDtypeStruct((rows, d), F32),
        compiler_params=_params("arbitrary"),
        name="moe_combine",
    )(pos, y, x2d, rw, mod3, final_g)


def _moe(x2d, mod3, g, wr, wg, wu, wd, final_g, *, seq, final):
    rows, d = x2d.shape
    ne = wr.shape[1]
    r = ROWS_EXPERT
    h2, ridx, rw, cnt = _route(x2d, mod3, g, wr, seq=seq, tm=ROWS_ROUTE)
    counts = cnt[:, 0].astype(jnp.int32)
    tiles_e = (counts + r - 1) // r
    tile_end = jnp.cumsum(tiles_e)
    start = (tile_end - tiles_e) * r
    pos = (jnp.take(start, ridx[:, 0:2], axis=0) + ridx[:, 2:4]).reshape(-1).astype(jnp.int32)
    nt = (2 * rows) // r + ne
    n_used = tile_end[-1:]
    tile_idx = jnp.minimum(jnp.arange(nt), n_used[0] - 1)
    tile_expert = jnp.sum(tile_idx[:, None] >= tile_end[None, :], axis=1)
    y = _experts(h2.reshape(-1, LANES), tile_expert.astype(jnp.int32), pos, n_used.astype(jnp.int32),
                 wg, wu, wd, r=r, tf=EXPERT_FF_CHUNK)
    return _combine(pos, y.reshape(-1, LANES), x2d, rw, mod3, final_g, seq=seq, tm=ROWS_COMBINE, final=final)


def _final_norm_kernel(x_ref, g_ref, o_ref):
    x = x_ref[...]
    o_ref[...] = (x * lax.rsqrt(jnp.mean(x * x, axis=-1, keepdims=True) + EPS)) * g_ref[...]


def _final_norm(x2d, g, *, tm):
    rows, d = x2d.shape
    return pl.pallas_call(
        _final_norm_kernel,
        grid=(rows // tm,),
        in_specs=[pl.BlockSpec((tm, d), lambda i: (i, 0)), pl.BlockSpec((1, d), lambda i: (0, 0))],
        out_specs=pl.BlockSpec((tm, d), lambda i: (i, 0)),
        out_shape=jax.ShapeDtypeStruct((rows, d), F32),
        compiler_params=_params("parallel"),
        name="final_norm",
    )(x2d, g)


def _t5_bucket(dist):
    n = jnp.maximum(dist, 0)
    max_exact = N_BUCKETS // 2
    nf = jnp.maximum(n, 1).astype(F32)
    large = max_exact + (jnp.log(nf / max_exact) / math.log(REL_MAX_DIST / max_exact)
                         * (N_BUCKETS - max_exact)).astype(jnp.int32)
    large = jnp.minimum(large, N_BUCKETS - 1)
    return jnp.where(n < max_exact, n, large)


def _bias_tiles_a(rel_bias, seq, t):
    nq = seq // t
    ncol = 2 * A_HEADS
    tab = rel_bias[:, :ncol][_t5_bucket(jnp.arange(seq))].astype(F32).T
    vneg = jnp.full((ncol, t), NEG, F32)
    v = jnp.concatenate([vneg, tab], axis=1)
    u = jnp.concatenate([v[:, 1:seq + 1][:, ::-1], vneg[:, :1], v[:, seq + 1:seq + t][:, ::-1]], axis=1)

    def toeplitz_kernel(u_ref, o_ref):
        x = jnp.broadcast_to(u_ref[0], (t, seq + t))
        r = pltpu.roll(x, 0, 1, stride=1, stride_axis=0)
        for delta in range(nq):
            c0 = (nq - 1 - delta) * t
            o_ref[0, delta] = r[:, c0:c0 + t]

    return pl.pallas_call(
        toeplitz_kernel,
        grid=(ncol,),
        in_specs=[pl.BlockSpec((1, 1, seq + t), lambda c: (c, 0, 0))],
        out_specs=pl.BlockSpec((1, nq, t, t), lambda c: (c // 2, 0, c % 2, 0)),
        out_shape=jax.ShapeDtypeStruct((A_HEADS, nq, 2 * t, t), F32),
        compiler_params=_params("parallel"),
        name="bias_tiles_diff",
    )(u.reshape(ncol, 1, seq + t))


def _bias_tiles_b(rel_bias):
    ng = len(B_GROUPS)
    period = 3 * BLOCK
    rows = []
    for g, (win, dil) in enumerate(B_GROUPS):
        n_back = win // dil
        tab = rel_bias[:, 2 * A_HEADS + g * B_HEADS:2 * A_HEADS + (g + 1) * B_HEADS]
        vals = tab[_t5_bucket(jnp.arange(n_back, -1, -1) * dil)].astype(F32).T
        rows.append(jnp.concatenate([vals, jnp.full((B_HEADS, period - n_back - 1), NEG, F32)], axis=1))
    u = jnp.concatenate(rows, axis=0)

    def toeplitz_kernel(u_ref, o_ref):
        x = jnp.broadcast_to(u_ref[0], (BLOCK, period))
        r = pltpu.roll(x, 0, 1, stride=1, stride_axis=0)[:, :2 * BLOCK]
        col = lax.broadcasted_iota(jnp.int32, r.shape, 1)
        o_ref[0, 0, 0] = jnp.where(col >= BLOCK, r, NEG)
        o_ref[0, 1, 0] = r

    return pl.pallas_call(
        toeplitz_kernel,
        grid=(ng * B_HEADS,),
        in_specs=[pl.BlockSpec((1, 1, period), lambda c: (c, 0, 0))],
        out_specs=pl.BlockSpec((1, 2, 1, BLOCK, 2 * BLOCK),
                               lambda c: (c // B_HEADS, 0, (c % B_HEADS) // 2, c % 2, 0)),
        out_shape=jax.ShapeDtypeStruct((ng, 2, B_HEADS // 2, 2 * BLOCK, 2 * BLOCK), F32),
        compiler_params=_params("parallel"),
        name="bias_tiles_dilated",
    )(u.reshape(ng * B_HEADS, 1, period))


def kernel(x, c, norm_mix_g, norm_ffn_g, w_mod, b_mod, w_in, b_forget, lam_q1, lam_k1, lam_q2, lam_k2,
           subln_g, rel_bias, w_br_a, w_br_b, w_br_c, w_out, w_ff_gate, w_ff_up, w_ff_down, w_router,
           w_exp_gate, w_exp_up, w_exp_down, final_norm_g):
    bsz, seq, d = x.shape
    depth = w_mod.shape[0]
    rows = bsz * seq
    x2d = x.reshape(rows, d)
    attn_tiles = dict(t=ATTN_BLOCK, nsb=ATTN_SUB_TILES, nbb=ATTN_BATCH_ROWS)

    mod = _modulation(c, w_mod, b_mod)
    bias_a = _bias_tiles_a(rel_bias, seq, ATTN_BLOCK)
    bias_b = _bias_tiles_b(rel_bias)
    w_qkv_all, w_gf_all = _prep_w_in(w_in)

    final_g = final_norm_g.reshape(1, d)
    fused_final = False
    for l in range(depth):
        lam_init = 0.8 - 0.6 * math.exp(-0.3 * l)
        mod3 = mod[l].reshape(bsz, 6, d)
        w_qkv, w_gate, w_f = w_qkv_all[l], w_gf_all[l, :, :GATE_W], w_gf_all[l, :, GATE_W:]

        g_mix = norm_mix_g[l].reshape(1, d)
        qkv_ac, qkv_b0, qkv_b1, qkv_b2, f_logit = _qkv_proj(x2d, mod3, g_mix, w_qkv, w_f,
                                                            bsz=bsz, seq=seq, tm=ROWS_PROJ)
        qkv_ac = qkv_ac.reshape(bsz, seq, -1)

        b_f8 = jnp.pad(b_forget[l], (0, 8 - C_HEADS)).reshape(8, 1)
        fcum = _forget_cumsum(f_logit.reshape(bsz, seq, LANES), b_f8, col_block=0)
        fcum = fcum[:, :C_HEADS].reshape(bsz, C_HEADS // 2, 2, seq)

        oa = _attn_a(qkv_ac, bias_a, lam_q1[l].reshape(1, -1), lam_k1[l].reshape(1, -1),
                     lam_q2[l].reshape(1, -1), lam_k2[l].reshape(1, -1), subln_g[l].reshape(1, -1),
                     lam_init=lam_init, **attn_tiles)
        oc = _attn_c(qkv_ac, fcum, **attn_tiles)
        obs, lses = [], []
        groups = (qkv_b0.reshape(bsz, 1, seq, 3 * B_W), qkv_b1, qkv_b2)
        for g, (qkv_g, nb) in enumerate(zip(groups, DILATED_BLOCKS)):
            o_g, lse_g = _attn_b_group(qkv_g, bias_b[g], g, nb)
            obs.append(o_g)
            lses.append(lse_g)

        x2d = _merge(oa.reshape(rows, A_W), obs, lses, oc.reshape(rows, C_W), x2d, mod3, g_mix, w_gate,
                     w_br_a[l].astype(BF16), w_br_b[l].astype(BF16), w_br_c[l].astype(BF16),
                     w_out[l].astype(BF16), seq=seq, tm=ROWS_PROJ)

        g_ffn = norm_ffn_g[l].reshape(1, d)
        if l % 2 == 0:
            j = l // 2
            x2d = _ffn(x2d, mod3, g_ffn, w_ff_gate[j].astype(BF16), w_ff_up[j].astype(BF16),
                       w_ff_down[j].astype(BF16), seq=seq, tm=ROWS_PROJ)
        else:
            j = l // 2
            fused_final = l == depth - 1
            x2d = _moe(x2d, mod3, g_ffn, w_router[j], w_exp_gate[j].astype(BF16),
                       w_exp_up[j].astype(BF16), w_exp_down[j].astype(BF16), final_g,
                       seq=seq, final=fused_final)

    if not fused_final:
        x2d = _final_norm(x2d, final_g, tm=ROWS_ROUTE)
    return x2d.reshape(bsz, seq, d)
```

```python
import functools
import math

import jax
import jax.numpy as jnp
from jax import lax
from jax.experimental import pallas as pl
from jax.experimental.pallas import tpu as pltpu

F32 = jnp.float32
BF16 = jnp.bfloat16

D_MODEL = 1024
HEAD_DIM = 64
LANES = 128
A_HEADS = 4
A_W = A_HEADS * 2 * HEAD_DIM
B_GROUPS = ((128, 1), (512, 4), (2048, 16))
B_HEADS = 6
B_W = B_HEADS * HEAD_DIM
B_QW = len(B_GROUPS) * B_W
C_HEADS = 6
C_W = C_HEADS * HEAD_DIM
N_BRANCH = 3
BLOCK = 128
N_BUCKETS = 32
REL_MAX_DIST = 2048
N_EXPERTS = 8
EPS = 1e-6
QKV_W = 3 * A_W + 3 * B_QW + 3 * C_W
GATE_W = N_BRANCH * D_MODEL
NEG = -1e30
SCALE = HEAD_DIM ** -0.5
VMEM_LIMIT = 56 * 1024 * 1024


MXU_TILE = 256

ROWS_PROJ = 512
ROWS_ROUTE = 1024
ROWS_EXPERT = 512
ROWS_COMBINE = 256
EXPERT_FF_CHUNK = 7 * MXU_TILE
ATTN_BLOCK = 256
ATTN_SUB_TILES = 8
ATTN_BATCH_ROWS = 2
DILATED_BLOCKS = (4, 1, 1)


def _params(*sem):
    return pltpu.CompilerParams(dimension_semantics=sem, vmem_limit_bytes=VMEM_LIMIT)


def _resident(a):
    return pl.BlockSpec(a.shape, lambda *_: (0, 0), pipeline_mode=pl.Buffered(1))


def _rms_mod(x, g, sc, sh):
    y = x * lax.rsqrt(jnp.mean(x * x, axis=-1, keepdims=True) + EPS)
    return (y * g) * (1.0 + sc) + sh


def _dot(a, b):
    return jnp.dot(a, b, preferred_element_type=F32)


def _dot_nt(a, b):
    return lax.dot_general(a, b, (((1,), (1,)), ((), ())), preferred_element_type=F32)


def _silu(a):
    return a * jax.nn.sigmoid(a)


def _mod_kernel(c_ref, w_ref, b_ref, o_ref):
    a = _silu(c_ref[...]).astype(BF16)
    o_ref[0] = _dot(a, w_ref[0].astype(BF16)) + b_ref[0]


def _modulation(c, w_mod, b_mod):
    depth, d, n = w_mod.shape
    bsz = c.shape[0]
    tn = 1536
    return pl.pallas_call(
        _mod_kernel,
        grid=(depth, n // tn),
        in_specs=[
            pl.BlockSpec((bsz, d), lambda l, j: (0, 0)),
            pl.BlockSpec((1, d, tn), lambda l, j: (l, 0, j)),
            pl.BlockSpec((1, 1, tn), lambda l, j: (l, 0, j)),
        ],
        out_specs=pl.BlockSpec((1, bsz, tn), lambda l, j: (l, 0, j)),
        out_shape=jax.ShapeDtypeStruct((depth, bsz, n), F32),
        compiler_params=_params("parallel", "parallel"),
        name="modulation",
    )(c, w_mod, b_mod.reshape(depth, 1, n))


_A_BLOCKS = 3 * A_W // B_W
_B_BLOCKS = 3 * B_QW // B_W
_QKV_SRC_BLOCKS = (list(range(_A_BLOCKS))
                   + list(range(_A_BLOCKS + _B_BLOCKS, QKV_W // B_W))
                   + [_A_BLOCKS + s * len(B_GROUPS) + g for g in range(len(B_GROUPS)) for s in range(3)])


def _prep_w_in_kernel(*refs):
    n = len(_QKV_SRC_BLOCKS)
    piece_refs, x_ref, y_ref, qkv_ref, gf_ref = refs[:n], refs[n], refs[n + 1], refs[n + 2], refs[n + 3]
    for j, ref in enumerate(piece_refs):
        qkv_ref[0, :, j * B_W:(j + 1) * B_W] = ref[0].astype(BF16)
    x = x_ref[0]
    rolled = pltpu.roll(x, GATE_W - C_HEADS, 1)
    tail = pltpu.roll(y_ref[0], LANES - C_HEADS, 1)
    lane = lax.broadcasted_iota(jnp.int32, tail.shape, 1)
    gf_ref[0, :, :GATE_W - LANES] = rolled[:, :GATE_W - LANES].astype(BF16)
    gf_ref[0, :, GATE_W - LANES:GATE_W] = jnp.where(lane < LANES - C_HEADS, rolled[:, GATE_W - LANES:],
                                                     tail).astype(BF16)
    gf_ref[0, :, GATE_W:] = jnp.where(lane < C_HEADS, x[:, :LANES], 0.0).astype(BF16)


def _prep_w_in(w_in):
    depth, d, _ = w_in.shape
    tr = 256
    piece = lambda c: pl.BlockSpec((1, tr, B_W), lambda l, i: (l, i, c))
    return pl.pallas_call(
        _prep_w_in_kernel,
        grid=(depth, d // tr),
        in_specs=[piece(c) for c in _QKV_SRC_BLOCKS]
        + [pl.BlockSpec((1, tr, GATE_W), lambda l, i: (l, i, QKV_W // GATE_W)),
           pl.BlockSpec((1, tr, LANES), lambda l, i: (l, i, (QKV_W + GATE_W) // LANES))],
        out_specs=[pl.BlockSpec((1, tr, QKV_W), lambda l, i: (l, i, 0)),
                   pl.BlockSpec((1, tr, GATE_W + LANES), lambda l, i: (l, i, 0))],
        out_shape=[jax.ShapeDtypeStruct((depth, d, QKV_W), BF16),
                   jax.ShapeDtypeStruct((depth, d, GATE_W + LANES), BF16)],
        compiler_params=_params("parallel", "parallel"),
        name="prep_w_in",
    )(*([w_in] * (len(_QKV_SRC_BLOCKS) + 2)))


def _qkv_kernel(x_ref, mod_ref, g_ref, w_ref, wf_ref, ac_ref, b0_ref, b1_ref, b2_ref, f_ref, h_scr, *, tm):
    m = mod_ref[0]
    h = _rms_mod(x_ref[...], g_ref[...], m[1:2], m[0:1])
    nc = h_scr.shape[0]
    for c in range(nc):
        h_scr[c] = h[:, c * LANES:(c + 1) * LANES]
    hb = h.astype(BF16)
    n_ac = ac_ref.shape[1]
    n_b = b0_ref.shape[1]
    f_ref[...] = _dot(hb, wf_ref[...])
    ac_ref[...] = _dot(hb, w_ref[:, 0:n_ac]).astype(BF16)
    b0_ref[...] = _dot(hb, w_ref[:, n_ac:n_ac + n_b]).astype(BF16)
    for gi, ref in ((1, b1_ref), (2, b2_ref)):
        dil = B_GROUPS[gi][1]
        per = tm // dil
        hp = jnp.concatenate(
            [jnp.concatenate([h_scr[c, pl.ds(r, per, stride=dil), :] for c in range(nc)], axis=1).astype(BF16)
             for r in range(dil)], axis=0)
        y = _dot(hp, w_ref[:, n_ac + gi * n_b:n_ac + (gi + 1) * n_b]).astype(BF16)
        for r in range(dil):
            ref[0, r] = y[r * per:(r + 1) * per]


def _qkv_proj(x2d, mod3, g, w, w_f, *, bsz, seq, tm):
    rows, d = x2d.shape
    per_b = seq // tm
    n_b = 3 * B_W
    n_ac = w.shape[1] - 3 * n_b
    dil1, dil2 = B_GROUPS[1][1], B_GROUPS[2][1]
    strided = lambda dil: pl.BlockSpec((1, dil, tm // dil, n_b), lambda i: (i // per_b, 0, i % per_b, 0))
    return pl.pallas_call(
        functools.partial(_qkv_kernel, tm=tm),
        grid=(rows // tm,),
        in_specs=[
            pl.BlockSpec((tm, d), lambda i: (i, 0)),
            pl.BlockSpec((1, 6, d), lambda i: (i // per_b, 0, 0)),
            pl.BlockSpec((1, d), lambda i: (0, 0)),
            _resident(w), _resident(w_f),
        ],
        out_specs=[pl.BlockSpec((tm, n_ac), lambda i: (i, 0)), pl.BlockSpec((tm, n_b), lambda i: (i, 0)),
                   strided(dil1), strided(dil2), pl.BlockSpec((tm, LANES), lambda i: (i, 0))],
        out_shape=[jax.ShapeDtypeStruct((rows, n_ac), BF16), jax.ShapeDtypeStruct((rows, n_b), BF16),
                   jax.ShapeDtypeStruct((bsz, dil1, seq // dil1, n_b), BF16),
                   jax.ShapeDtypeStruct((bsz, dil2, seq // dil2, n_b), BF16),
                   jax.ShapeDtypeStruct((rows, LANES), F32)],
        scratch_shapes=[pltpu.VMEM((d // LANES, tm, LANES), F32)],
        compiler_params=_params("parallel"),
        name="qkv_proj",
    )(x2d, mod3, g, w, w_f)


def _fcum_kernel(f_ref, b_ref, o_ref):
    z = f_ref[0].T[:8] + b_ref[...]
    x = jnp.minimum(z, 0.0) - jnp.log1p(jnp.exp(-jnp.abs(z)))
    s = x.shape[1]
    lane = lax.broadcasted_iota(jnp.int32, x.shape, 1)
    k = 1
    while k < s:
        x = x + jnp.where(lane >= k, pltpu.roll(x, k, 1), 0.0)
        k *= 2
    o_ref[0] = x


def _forget_cumsum(gf, b_f8, *, col_block):
    bsz, seq, _ = gf.shape
    return pl.pallas_call(
        _fcum_kernel,
        grid=(bsz,),
        in_specs=[
            pl.BlockSpec((1, seq, LANES), lambda b: (b, 0, col_block)),
            pl.BlockSpec((8, 1), lambda b: (0, 0)),
        ],
        out_specs=pl.BlockSpec((1, 8, seq), lambda b: (b, 0, 0)),
        out_shape=jax.ShapeDtypeStruct((bsz, 8, seq), F32),
        compiler_params=_params("parallel"),
        name="forget_cumsum",
    )(gf, b_f8)


def _half_masks(q):
    lane = lax.broadcasted_iota(jnp.int32, q.shape, 1)
    zero = jnp.zeros_like(q)
    return jnp.where(lane < HEAD_DIM, q, zero), jnp.where(lane >= HEAD_DIM, q, zero)


def _flash_init(first, v_ref, vext_scr, m_scr, acc_scr):
    @pl.when(first)
    def _():
        for bb in range(vext_scr.shape[0]):
            vext_scr[bb, :, :LANES] = v_ref[bb]
            vext_scr[bb, :, LANES:] = jnp.ones((vext_scr.shape[1], LANES), BF16)

    m_scr[...] = jnp.full(m_scr.shape, -jnp.inf, F32)
    acc_scr[...] = jnp.zeros(acc_scr.shape, F32)


def _lane_tile(a, n):
    return a if n == 1 else jnp.concatenate([a] * n, axis=1)


def _flash_update(s, vext, rows, m_scr, acc_scr):
    m_prev = m_scr[rows]
    m_new = jnp.maximum(m_prev, jnp.max(s, axis=-1, keepdims=True))
    alpha = jnp.exp(m_prev - m_new)
    p = jnp.exp(s - _lane_tile(m_new, s.shape[1] // LANES))
    acc_scr[rows] = _lane_tile(alpha, 2) * acc_scr[rows] + _dot(p.astype(BF16), vext)
    m_scr[rows] = m_new


def _flash_result(acc_scr):
    acc = acc_scr[...]
    return acc[:, :LANES] / acc[:, LANES:]


def _attn_a_kernel(q_ref, k_ref, v_ref, bias_ref, lq1, lk1, lq2, lk2, sg_ref, o_ref,
                   vext_scr, m_scr, acc_scr, *, t, nsb, nbb, lam_init):
    qi = pl.program_id(2)
    _flash_init(qi == 0, v_ref, vext_scr, m_scr, acc_scr)
    qh = [[_half_masks(q_ref[bb, sb * t:(sb + 1) * t, :] * SCALE) for sb in range(nsb)] for bb in range(nbb)]
    chain_rows = lambda bb, sb, hh: pl.ds(((bb * nsb + sb) * 2 + hh) * t, t)

    def step(kb, nkb, plan):
        off = pl.multiple_of(kb * t, t)
        for bb in range(nbb):
            kblk = k_ref[bb, pl.ds(off, nkb * t), :]
            vext = vext_scr[bb, pl.ds(off, nkb * t), :]
            for sb, deltas in plan:
                for hh in range(2):
                    bias = [bias_ref[0, d, hh * t:(hh + 1) * t, :] for d in deltas]
                    s = _dot_nt(qh[bb][sb][hh], kblk) + (bias[0] if nkb == 1 else jnp.concatenate(bias, axis=1))
                    _flash_update(s, vext, chain_rows(bb, sb, hh), m_scr, acc_scr)

    def body(kb2, carry):
        first = [qi * nsb + sb - 2 * kb2 for sb in range(nsb)]
        step(2 * kb2, 2, [(sb, (first[sb], first[sb] - 1)) for sb in range(nsb)])
        return carry

    if q_ref.shape[1] < k_ref.shape[1]:
        lax.fori_loop(0, qi * (nsb // 2), body, 0)
    for j in range(0, nsb, 2):
        step(qi * nsb + j, 1, [(j, (0,))])
        step(qi * nsb + j, 2, [(sb, (sb - j, sb - j - 1)) for sb in range(j + 1, nsb)])

    o = _flash_result(acc_scr)
    lam = (jnp.exp(jnp.sum(lq1[...] * lk1[...], axis=-1, keepdims=True))
           - jnp.exp(jnp.sum(lq2[...] * lk2[...], axis=-1, keepdims=True)) + lam_init)
    for bb in range(nbb):
        for sb in range(nsb):
            c0 = (bb * nsb + sb) * 2 * t
            d = o[c0:c0 + t] - lam * o[c0 + t:c0 + 2 * t]
            y = d * lax.rsqrt(jnp.mean(d * d, axis=-1, keepdims=True) + EPS)
            o_ref[bb, sb * t:(sb + 1) * t, :] = ((y * sg_ref[...]) * (1.0 - lam_init)).astype(o_ref.dtype)


def _flash_scratch(seq, t, nsb, nbb):
    chains = 2 * nsb * nbb
    return [pltpu.VMEM((nbb, seq, 2 * LANES), BF16), pltpu.VMEM((chains * t, LANES), F32),
            pltpu.VMEM((chains * t, 2 * LANES), F32)]


def _attn_a(qkv, bias_a, lq1, lk1, lq2, lk2, subln_g, *, lam_init, t, nsb, nbb):
    bsz, seq, _ = qkv.shape
    tq = t * nsb
    assert bsz % nbb == 0 and seq % tq == 0 and nsb % 2 == 0
    vec = lambda n: pl.BlockSpec((1, n), lambda h, b, i: (0, 0))
    return pl.pallas_call(
        functools.partial(_attn_a_kernel, t=t, nsb=nsb, nbb=nbb, lam_init=lam_init),
        grid=(A_HEADS, bsz // nbb, seq // tq),
        in_specs=[
            pl.BlockSpec((nbb, tq, LANES), lambda h, b, i: (b, i, h)),
            pl.BlockSpec((nbb, seq, LANES), lambda h, b, i: (b, 0, A_HEADS + h)),
            pl.BlockSpec((nbb, seq, LANES), lambda h, b, i: (b, 0, 2 * A_HEADS + h)),
            pl.BlockSpec((1, seq // t, 2 * t, t), lambda h, b, i: (h, 0, 0, 0)),
            vec(HEAD_DIM), vec(HEAD_DIM), vec(HEAD_DIM), vec(HEAD_DIM), vec(LANES),
        ],
        out_specs=pl.BlockSpec((nbb, tq, LANES), lambda h, b, i: (b, i, h)),
        out_shape=jax.ShapeDtypeStruct((bsz, seq, A_W), BF16),
        scratch_shapes=_flash_scratch(seq, t, nsb, nbb),
        compiler_params=_params("parallel", "parallel", "arbitrary"),
        name="attn_diff",
    )(qkv, qkv, qkv, bias_a, lq1, lk1, lq2, lk2, subln_g)


def _attn_c_kernel(q_ref, k_ref, v_ref, f_ref, o_ref, vext_scr, m_scr, acc_scr, *, t, nsb, nbb):
    qi = pl.program_id(2)
    _flash_init(qi == 0, v_ref, vext_scr, m_scr, acc_scr)
    qh = [[_half_masks(q_ref[bb, sb * t:(sb + 1) * t, :] * SCALE) for sb in range(nsb)] for bb in range(nbb)]
    chain_rows = lambda bb, sb, hh: pl.ds(((bb * nsb + sb) * 2 + hh) * t, t)
    q_off = pl.multiple_of(qi * (t * nsb), t * nsb)
    f_anchor = [[f_ref[bb, 0, :, pl.ds(pl.multiple_of(q_off + sb * t, t), LANES)][:, :1] for sb in range(nsb)]
                for bb in range(nbb)]

    def causal(nkb):
        r = lax.broadcasted_iota(jnp.int32, (t, nkb * t), 0)
        c = lax.broadcasted_iota(jnp.int32, (t, nkb * t), 1)
        return r + (nkb - 1) * t >= c

    def step(kb, nkb, sbs, diag_sb):
        off = pl.multiple_of(kb * t, t)
        for bb in range(nbb):
            kblk = k_ref[bb, pl.ds(off, nkb * t), :]
            vext = vext_scr[bb, pl.ds(off, nkb * t), :]
            f_keys = [f_ref[bb, 0, hh:hh + 1, pl.ds(off, nkb * t)] for hh in range(2)]
            for sb in sbs:
                for hh in range(2):
                    s = _dot_nt(qh[bb][sb][hh], kblk) + (f_anchor[bb][sb][hh:hh + 1] - f_keys[hh])
                    if sb == diag_sb:
                        s = jnp.where(causal(nkb), s, NEG)
                    _flash_update(s, vext, chain_rows(bb, sb, hh), m_scr, acc_scr)

    def body(kb2, carry):
        step(2 * kb2, 2, range(nsb), None)
        return carry

    if q_ref.shape[1] < k_ref.shape[1]:
        lax.fori_loop(0, qi * (nsb // 2), body, 0)
    for j in range(0, nsb, 2):
        step(qi * nsb + j, 1, [j], j)
        step(qi * nsb + j, 2, range(j + 1, nsb), j + 1)

    o = _flash_result(acc_scr)
    lane = lax.broadcasted_iota(jnp.int32, (t, LANES), 1)
    for bb in range(nbb):
        for sb in range(nsb):
            c0 = (bb * nsb + sb) * 2 * t
            pair = jnp.where(lane < HEAD_DIM, o[c0:c0 + t], o[c0 + t:c0 + 2 * t])
            o_ref[bb, sb * t:(sb + 1) * t, :] = pair.astype(o_ref.dtype)


def _attn_c(qkv, fcum, *, t, nsb, nbb):
    bsz, seq, _ = qkv.shape
    tq = t * nsb
    assert bsz % nbb == 0 and seq % tq == 0 and nsb % 2 == 0
    pairs = C_HEADS // 2
    q0 = 3 * A_W // LANES
    return pl.pallas_call(
        functools.partial(_attn_c_kernel, t=t, nsb=nsb, nbb=nbb),
        grid=(pairs, bsz // nbb, seq // tq),
        in_specs=[
            pl.BlockSpec((nbb, tq, LANES), lambda p, b, i: (b, i, q0 + p)),
            pl.BlockSpec((nbb, seq, LANES), lambda p, b, i: (b, 0, q0 + pairs + p)),
            pl.BlockSpec((nbb, seq, LANES), lambda p, b, i: (b, 0, q0 + 2 * pairs + p)),
            pl.BlockSpec((nbb, 1, 2, seq), lambda p, b, i: (b, p, 0, 0)),
        ],
        out_specs=pl.BlockSpec((nbb, tq, LANES), lambda p, b, i: (b, i, p)),
        out_shape=jax.ShapeDtypeStruct((bsz, seq, C_W), BF16),
        scratch_shapes=_flash_scratch(seq, t, nsb, nbb),
        compiler_params=_params("parallel", "parallel", "arbitrary"),
        name="attn_forget",
    )(qkv, qkv, qkv, fcum)


def _attn_b_kernel(q_ref, kp_ref, kc_ref, vp_ref, vc_ref, bias_ref, o_ref, lse_ref, *, dil, nb):
    n = pl.program_id(1)
    lane = lax.broadcasted_iota(jnp.int32, (BLOCK, LANES), 1)
    first_variant = jnp.minimum(n, 1)

    def residue(r):
        for hp in range(B_HEADS // 2):
            cols = slice(hp * LANES, (hp + 1) * LANES)
            kcat = jnp.concatenate([kp_ref[0, r, :, cols], kc_ref[0, r, :, cols]], axis=0)
            vcat = jnp.concatenate([vp_ref[0, r, :, cols], vc_ref[0, r, :, cols]], axis=0)
            for jb in range(nb):
                qh = _half_masks(q_ref[0, r, jb * BLOCK:(jb + 1) * BLOCK, cols] * SCALE)
                kwin = kcat[jb * BLOCK:(jb + 2) * BLOCK]
                vwin = vcat[jb * BLOCK:(jb + 2) * BLOCK]
                variant = first_variant if jb == 0 else 1
                outs, lses = [], []
                for hh in range(2):
                    s = _dot_nt(qh[hh], kwin) + bias_ref[variant, hp, hh * BLOCK:(hh + 1) * BLOCK, :]
                    m = jnp.max(s, axis=-1, keepdims=True)
                    e = jnp.exp(s - m)
                    den = jnp.sum(e, axis=-1, keepdims=True)
                    outs.append(_dot(e.astype(BF16), vwin) / den)
                    lses.append(jnp.broadcast_to(m + jnp.log(den), (BLOCK, LANES)))
                if dil == 1:
                    rows = pl.ds(jb * BLOCK, BLOCK)
                else:
                    rows = pl.ds(jb * BLOCK * dil + r, BLOCK, stride=dil)
                o_ref[0, hp, rows, :] = jnp.where(lane < HEAD_DIM, outs[0], outs[1])
                lse_ref[0, hp, rows, :] = jnp.where(lane < HEAD_DIM, lses[0], lses[1])

    unroll = min(dil, 4)
    if dil == unroll:
        for r in range(dil):
            residue(r)
    else:
        def body(i, carry):
            for j in range(unroll):
                residue(i * unroll + j)
            return carry

        lax.fori_loop(0, dil // unroll, body, 0)


def _attn_b_group(qkv_g, bias_g, g, nb):
    bsz, dil, m_len, _ = qkv_g.shape
    tb = BLOCK * nb
    cur = lambda c: pl.BlockSpec((1, dil, tb, B_W), lambda b, n: (b, 0, n, c))
    prev = lambda c: pl.BlockSpec((1, dil, BLOCK, B_W), lambda b, n: (b, 0, jnp.maximum(n * nb - 1, 0), c))
    pairs = B_HEADS // 2
    out_spec = pl.BlockSpec((1, pairs, tb * dil, LANES), lambda b, n: (b, 0, n, 0))
    out_sds = jax.ShapeDtypeStruct((bsz, pairs, m_len * dil, LANES), F32)
    return pl.pallas_call(
        functools.partial(_attn_b_kernel, dil=dil, nb=nb),
        grid=(bsz, m_len // tb),
        in_specs=[cur(0), prev(1), cur(1), prev(2), cur(2),
                  pl.BlockSpec(bias_g.shape, lambda b, n: (0, 0, 0, 0))],
        out_specs=[out_spec, out_spec],
        out_shape=[out_sds, out_sds],
        compiler_params=_params("parallel", "arbitrary"),
        name=f"attn_dilated_g{g}",
    )(qkv_g, qkv_g, qkv_g, qkv_g, qkv_g, bias_g)


def _merge_kernel(oa_ref, ob0, ob1, ob2, ls0, ls1, ls2, oc_ref, x_ref, mod_ref, g_ref,
                  wg_ref, wa_ref, wb_ref, wc_ref, wo_ref, o_ref):
    x = x_ref[...]
    m = mod_ref[0]
    hb = _rms_mod(x, g_ref[...], m[1:2], m[0:1]).astype(BF16)
    parts = []
    for hp in range(B_HEADS // 2):
        l0, l1, l2 = ls0[0, hp], ls1[0, hp], ls2[0, hp]
        mx = jnp.maximum(jnp.maximum(l0, l1), l2)
        e0, e1, e2 = jnp.exp(l0 - mx), jnp.exp(l1 - mx), jnp.exp(l2 - mx)
        den = e0 + e1 + e2
        parts.append((e0 / den) * ob0[0, hp] + (e1 / den) * ob1[0, hp] + (e2 / den) * ob2[0, hp])
    ob = jnp.concatenate(parts, axis=1)
    d = D_MODEL
    gate = lambda k: jax.nn.sigmoid(_dot(hb, wg_ref[:, k * d:(k + 1) * d]))
    merged = (gate(0) * _dot(oa_ref[...], wa_ref[...])
              + gate(1) * _dot(ob.astype(BF16), wb_ref[...])
              + gate(2) * _dot(oc_ref[...], wc_ref[...]))
    y = _dot(merged.astype(BF16), wo_ref[...])
    o_ref[...] = x + m[2:3] * y


def _merge(oa, obs, lses, oc, x2d, mod3, g, w_gate, wa, wb, wc, wo, *, seq, tm):
    rows, d = x2d.shape
    per_b = seq // tm
    row = lambda w: pl.BlockSpec((tm, w), lambda i: (i, 0))
    paired = pl.BlockSpec((1, B_HEADS // 2, tm, LANES), lambda i: (i // per_b, 0, i % per_b, 0))
    return pl.pallas_call(
        _merge_kernel,
        grid=(rows // tm,),
        in_specs=[row(A_W)] + [paired] * 6 + [row(C_W), row(d),
                  pl.BlockSpec((1, 6, d), lambda i: (i // per_b, 0, 0)),
                  pl.BlockSpec((1, d), lambda i: (0, 0)),
                  _resident(w_gate), _resident(wa), _resident(wb), _resident(wc), _resident(wo)],
        out_specs=row(d),
        out_shape=jax.ShapeDtypeStruct((rows, d), F32),
        compiler_params=_params("parallel"),
        name="merge_outproj",
    )(oa, *obs, *lses, oc, x2d, mod3, g, w_gate, wa, wb, wc, wo)


def _ffn_kernel(x_ref, mod_ref, g_ref, wg_ref, wu_ref, wd_ref, o_ref, *, bounds):
    x = x_ref[...]
    m = mod_ref[0]
    h = _rms_mod(x, g_ref[...], m[4:5], m[3:4]).astype(BF16)
    acc = None
    for lo, hi in bounds:
        act = (_silu(_dot(h, wg_ref[:, lo:hi])) * _dot(h, wu_ref[:, lo:hi])).astype(BF16)
        part = _dot(act, wd_ref[lo:hi, :])
        acc = part if acc is None else acc + part
    o_ref[...] = x + m[5:6] * acc


def _ffn(x2d, mod3, g, wg, wu, wd, *, seq, tm):
    rows, d = x2d.shape
    dff = wg.shape[1]
    half = (dff // MXU_TILE + 1) // 2 * MXU_TILE
    per_b = seq // tm
    return pl.pallas_call(
        functools.partial(_ffn_kernel, bounds=((0, half), (half, dff))),
        grid=(rows // tm,),
        in_specs=[
            pl.BlockSpec((tm, d), lambda i: (i, 0)),
            pl.BlockSpec((1, 6, d), lambda i: (i // per_b, 0, 0)),
            pl.BlockSpec((1, d), lambda i: (0, 0)),
            _resident(wg), _resident(wu), _resident(wd),
        ],
        out_specs=pl.BlockSpec((tm, d), lambda i: (i, 0)),
        out_shape=jax.ShapeDtypeStruct((rows, d), F32),
        compiler_params=_params("parallel"),
        name="ffn_dense",
    )(x2d, mod3, g, wg, wu, wd)


def _route_kernel(x_ref, mod_ref, g_ref, wr_ref, tri_ref, h_ref, ridx_ref, rw_ref, cnt_ref, carry_scr):
    i = pl.program_id(0)

    @pl.when(i == 0)
    def _():
        carry_scr[...] = jnp.zeros(carry_scr.shape, F32)

    m = mod_ref[0]
    h = _rms_mod(x_ref[...], g_ref[...], m[4:5], m[3:4])
    _store_row_tiles(h_ref, h)
    w = wr_ref[...]
    h_hi, w_hi = h.astype(BF16), w.astype(BF16)
    h_lo = (h - h_hi.astype(F32)).astype(BF16)
    w_lo = (w - w_hi.astype(F32)).astype(BF16)
    logits = _dot_nt(w_hi, h_hi) + (_dot_nt(w_hi, h_lo) + _dot_nt(w_lo, h_hi))
    idx = lax.broadcasted_iota(jnp.int32, logits.shape, 0)
    n = logits.shape[0]
    m1 = jnp.max(logits, axis=0, keepdims=True)
    i1 = jnp.min(jnp.where(logits == m1, idx, n), axis=0, keepdims=True)
    first = idx == i1
    rest = jnp.where(first, -jnp.inf, logits)
    m2 = jnp.max(rest, axis=0, keepdims=True)
    i2 = jnp.min(jnp.where(rest == m2, idx, n), axis=0, keepdims=True)
    second = idx == i2
    e = jnp.exp(m2 - m1)
    den = 1.0 + e
    onehot = jnp.where(first | second, 1.0, 0.0)
    before = _dot(onehot.astype(BF16), tri_ref[...]) - onehot + carry_scr[...]
    rank1 = jnp.sum(jnp.where(first, before, 0.0), axis=0, keepdims=True)
    rank2 = jnp.sum(jnp.where(second, before, 0.0), axis=0, keepdims=True)
    carry_scr[...] += jnp.sum(onehot, axis=1, keepdims=True)
    picks = jnp.where(idx == 0, i1.astype(F32), jnp.where(idx == 1, i2.astype(F32),
                      jnp.where(idx == 2, rank1, jnp.where(idx == 3, rank2, 0.0))))
    weights = jnp.where(idx == 0, 1.0 / den, jnp.where(idx == 1, e / den, 0.0))
    ridx_ref[...] = picks.T.astype(jnp.int32)
    rw_ref[...] = weights.T
    cnt_ref[...] = carry_scr[...]


def _route(x2d, mod3, g, wr, *, seq, tm):
    rows, d = x2d.shape
    ne = wr.shape[1]
    per_b = seq // tm
    tri = (jnp.arange(tm)[:, None] <= jnp.arange(tm)[None, :]).astype(BF16)
    row = lambda w: pl.BlockSpec((tm, w), lambda i: (i, 0))
    return pl.pallas_call(
        _route_kernel,
        grid=(rows // tm,),
        in_specs=[row(d), pl.BlockSpec((1, 6, d), lambda i: (i // per_b, 0, 0)),
                  pl.BlockSpec((1, d), lambda i: (0, 0)), pl.BlockSpec((ne, d), lambda i: (0, 0)),
                  pl.BlockSpec((tm, tm), lambda i: (0, 0))],
        out_specs=[pl.BlockSpec((tm, d // LANES, LANES), lambda i: (i, 0, 0)), row(ne), row(ne),
                   pl.BlockSpec((ne, 1), lambda i: (0, 0))],
        out_shape=[jax.ShapeDtypeStruct((rows, d // LANES, LANES), F32),
                   jax.ShapeDtypeStruct((rows, ne), jnp.int32),
                   jax.ShapeDtypeStruct((rows, ne), F32), jax.ShapeDtypeStruct((ne, 1), F32)],
        scratch_shapes=[pltpu.VMEM((ne, 1), F32)],
        compiler_params=_params("arbitrary"),
        name="moe_route",
    )(x2d, mod3, g, wr.T, tri)


def _store_row_tiles(ref, val):
    for c in range(ref.shape[-2]):
        ref[:, c, :] = val[:, c * LANES:(c + 1) * LANES]


ROW_PITCH = D_MODEL // LANES + 1


ROW_TILE = D_MODEL // LANES


def _gather_rows(idx_of, n, src_hbm, dst, sem):
    for j in range(n):
        start = pl.multiple_of(idx_of(j) * ROW_TILE, ROW_TILE)
        pltpu.make_async_copy(src_hbm.at[pl.ds(start, ROW_TILE)], dst.at[pl.ds(j * ROW_PITCH, ROW_TILE)],
                              sem).start()


def _gather_wait(n, src_hbm, dst, sem):
    pltpu.make_async_copy(src_hbm.at[pl.ds(0, n * ROW_TILE)], dst.at[pl.ds(0, n * ROW_TILE)], sem).wait()


def _load_gathered(buf, n):
    return jnp.concatenate([buf[pl.ds(c, n, stride=ROW_PITCH), :] for c in range(ROW_TILE)], axis=1)


def _expert_kernel(te_ref, pos_ref, nu_ref, h_hbm, zeros_hbm, wg_ref, wu_ref, wd_ref, y_ref,
                   xbuf, xb_scr, acc_scr, src_smem, sem, *, nf, r):
    i = pl.program_id(0)
    f = pl.program_id(1)
    active = i < nu_ref[0]
    slot = i % 2

    def gather(tile, s):
        _gather_rows(lambda j: src_smem[tile * r + j], r, h_hbm, xbuf.at[s], sem.at[s])

    @pl.when((i == 0) & (f == 0))
    def _():
        clear = pltpu.make_async_copy(zeros_hbm, src_smem, sem.at[0])
        clear.start()
        clear.wait()

        def place(p, carry):
            src_smem[pos_ref[p]] = p >> 1
            return carry

        lax.fori_loop(0, pos_ref.shape[0], place, 0, unroll=8)
        gather(0, 0)

    @pl.when(active & (f == 0))
    def _():
        _gather_wait(r, h_hbm, xbuf.at[slot], sem.at[slot])

        @pl.when(i + 1 < nu_ref[0])
        def _():
            gather(i + 1, 1 - slot)

        xb_scr[...] = _load_gathered(xbuf.at[slot], r).astype(BF16)
        acc_scr[...] = jnp.zeros(acc_scr.shape, F32)

    @pl.when(active)
    def _():
        xb = xb_scr[...]
        act = (_silu(_dot(xb, wg_ref[0])) * _dot(xb, wu_ref[0])).astype(BF16)
        acc_scr[...] += _dot(act, wd_ref[0])

    @pl.when(f == nf - 1)
    def _():
        y_ref[...] = jnp.where(active, acc_scr[...], 0.0)


def _experts(h2, tile_expert, pos, n_used, wg, wu, wd, *, r, tf):
    ne, d, dff = wg.shape
    nf = dff // tf
    nt = tile_expert.shape[0]
    p_rows = nt * r
    fsel = lambda i, f, nu: jnp.where(i < nu[0], f, nf - 1)
    grid_spec = pltpu.PrefetchScalarGridSpec(
        num_scalar_prefetch=3,
        grid=(nt, nf),
        in_specs=[
            pl.BlockSpec(memory_space=pl.ANY),
            pl.BlockSpec(memory_space=pl.ANY),
            pl.BlockSpec((1, d, tf), lambda i, f, te, sp, nu: (te[i], 0, fsel(i, f, nu))),
            pl.BlockSpec((1, d, tf), lambda i, f, te, sp, nu: (te[i], 0, fsel(i, f, nu))),
            pl.BlockSpec((1, tf, d), lambda i, f, te, sp, nu: (te[i], fsel(i, f, nu), 0)),
        ],
        out_specs=pl.BlockSpec((r, d), lambda i, f, te, sp, nu: (i, 0)),
        scratch_shapes=[pltpu.VMEM((2, r * ROW_PITCH, LANES), F32), pltpu.VMEM((r, d), BF16),
                        pltpu.VMEM((r, d), F32), pltpu.SMEM((p_rows,), jnp.int32),
                        pltpu.SemaphoreType.DMA((2,))],
    )
    return pl.pallas_call(
        functools.partial(_expert_kernel, nf=nf, r=r),
        grid_spec=grid_spec,
        out_shape=jax.ShapeDtypeStruct((p_rows, d), F32),
        compiler_params=_params("arbitrary", "arbitrary"),
        name="moe_experts",
    )(tile_expert, pos, n_used, h2, jnp.zeros((p_rows,), jnp.int32), wg, wu, wd)


COMBINE_SLOTS = 3


def _combine_kernel(pos_ref, y_hbm, x_ref, rw_ref, mod_ref, fg_ref, o_ref, ybuf, sem, *, tm, final):
    i = pl.program_id(0)
    n = pl.num_programs(0)
    ahead = COMBINE_SLOTS - 1
    slot = i % COMBINE_SLOTS

    def gather(tile, s):
        for k in range(2):
            for j in range(tm):
                row = pos_ref[(tile * tm + j) * 2 + k]
                pltpu.make_async_copy(y_hbm.at[pl.ds(row, 1)], ybuf.at[s, k, pl.ds(j, 1)], sem.at[s]).start()

    @pl.when(i == 0)
    def _():
        for t0 in range(ahead):
            @pl.when(t0 < n)
            def _():
                gather(t0, t0)

    for k in range(2):
        pltpu.make_async_copy(y_hbm.at[pl.ds(0, tm)], ybuf.at[slot, k], sem.at[slot]).wait()

    @pl.when(i + ahead < n)
    def _():
        gather(i + ahead, (i + ahead) % COMBINE_SLOTS)

    w = rw_ref[...]
    f = w[:, 0:1] * ybuf[slot, 0] + w[:, 1:2] * ybuf[slot, 1]
    x = x_ref[...] + mod_ref[0][5:6] * f
    if final:
        x = (x * lax.rsqrt(jnp.mean(x * x, axis=-1, keepdims=True) + EPS)) * fg_ref[...]
    o_ref[...] = x


def _combine(pos, y, x2d, rw, mod3, final_g, *, seq, tm, final):
    rows, d = x2d.shape
    per_b = seq // tm
    grid_spec = pltpu.PrefetchScalarGridSpec(
        num_scalar_prefetch=1,
        grid=(rows // tm,),
        in_specs=[
            pl.BlockSpec(memory_space=pl.ANY),
            pl.BlockSpec((tm, d), lambda i, p: (i, 0)),
            pl.BlockSpec((tm, rw.shape[1]), lambda i, p: (i, 0)),
            pl.BlockSpec((1, 6, d), lambda i, p: (i // per_b, 0, 0)),
            pl.BlockSpec((1, d), lambda i, p: (0, 0)),
        ],
        out_specs=pl.BlockSpec((tm, d), lambda i, p: (i, 0)),
        scratch_shapes=[pltpu.VMEM((COMBINE_SLOTS, 2, tm, d), F32),
                        pltpu.SemaphoreType.DMA((COMBINE_SLOTS,))],
    )
    return pl.pallas_call(
        functools.partial(_combine_kernel, tm=tm, final=final),
        grid_spec=grid_spec,
        out_shape=jax.ShapeDtypeStruct((rows, d), F32),
        compiler_params=_params("arbitrary"),
        name="moe_combine",
    )(pos, y, x2d, rw, mod3, final_g)


def _moe(x2d, mod3, g, wr, wg, wu, wd, final_g, *, seq, final):
    rows, d = x2d.shape
    ne = wr.shape[1]
    r = ROWS_EXPERT
    h2, ridx, rw, cnt = _route(x2d, mod3, g, wr, seq=seq, tm=ROWS_ROUTE)
    counts = cnt[:, 0].astype(jnp.int32)
    tiles_e = (counts + r - 1) // r
    tile_end = jnp.cumsum(tiles_e)
    start = (tile_end - tiles_e) * r
    pos = (jnp.take(start, ridx[:, 0:2], axis=0) + ridx[:, 2:4]).reshape(-1).astype(jnp.int32)
    nt = (2 * rows) // r + ne
    n_used = tile_end[-1:]
    tile_idx = jnp.minimum(jnp.arange(nt), n_used[0] - 1)
    tile_expert = jnp.sum(tile_idx[:, None] >= tile_end[None, :], axis=1)
    y = _experts(h2.reshape(-1, LANES), tile_expert.astype(jnp.int32), pos, n_used.astype(jnp.int32),
                 wg, wu, wd, r=r, tf=EXPERT_FF_CHUNK)
    return _combine(pos, y, x2d, rw, mod3, final_g, seq=seq, tm=ROWS_COMBINE, final=final)


def _final_norm_kernel(x_ref, g_ref, o_ref):
    x = x_ref[...]
    o_ref[...] = (x * lax.rsqrt(jnp.mean(x * x, axis=-1, keepdims=True) + EPS)) * g_ref[...]


def _final_norm(x2d, g, *, tm):
    rows, d = x2d.shape
    return pl.pallas_call(
        _final_norm_kernel,
        grid=(rows // tm,),
        in_specs=[pl.BlockSpec((tm, d), lambda i: (i, 0)), pl.BlockSpec((1, d), lambda i: (0, 0))],
        out_specs=pl.BlockSpec((tm, d), lambda i: (i, 0)),
        out_shape=jax.ShapeDtypeStruct((rows, d), F32),
        compiler_params=_params("parallel"),
        name="final_norm",
    )(x2d, g)


def _t5_bucket(dist):
    n = jnp.maximum(dist, 0)
    max_exact = N_BUCKETS // 2
    nf = jnp.maximum(n, 1).astype(F32)
    large = max_exact + (jnp.log(nf / max_exact) / math.log(REL_MAX_DIST / max_exact)
                         * (N_BUCKETS - max_exact)).astype(jnp.int32)
    large = jnp.minimum(large, N_BUCKETS - 1)
    return jnp.where(n < max_exact, n, large)


def _bias_tiles_a(rel_bias, seq, t):
    nq = seq // t
    ncol = 2 * A_HEADS
    tab = rel_bias[:, :ncol][_t5_bucket(jnp.arange(seq))].astype(F32).T
    vneg = jnp.full((ncol, t), NEG, F32)
    v = jnp.concatenate([vneg, tab], axis=1)
    u = jnp.concatenate([v[:, 1:seq + 1][:, ::-1], vneg[:, :1], v[:, seq + 1:seq + t][:, ::-1]], axis=1)

    def toeplitz_kernel(u_ref, o_ref):
        x = jnp.broadcast_to(u_ref[0], (t, seq + t))
        r = pltpu.roll(x, 0, 1, stride=1, stride_axis=0)
        for delta in range(nq):
            c0 = (nq - 1 - delta) * t
            o_ref[0, delta] = r[:, c0:c0 + t]

    return pl.pallas_call(
        toeplitz_kernel,
        grid=(ncol,),
        in_specs=[pl.BlockSpec((1, 1, seq + t), lambda c: (c, 0, 0))],
        out_specs=pl.BlockSpec((1, nq, t, t), lambda c: (c // 2, 0, c % 2, 0)),
        out_shape=jax.ShapeDtypeStruct((A_HEADS, nq, 2 * t, t), F32),
        compiler_params=_params("parallel"),
        name="bias_tiles_diff",
    )(u.reshape(ncol, 1, seq + t))


def _bias_tiles_b(rel_bias):
    ng = len(B_GROUPS)
    period = 3 * BLOCK
    rows = []
    for g, (win, dil) in enumerate(B_GROUPS):
        n_back = win // dil
        tab = rel_bias[:, 2 * A_HEADS + g * B_HEADS:2 * A_HEADS + (g + 1) * B_HEADS]
        vals = tab[_t5_bucket(jnp.arange(n_back, -1, -1) * dil)].astype(F32).T
        rows.append(jnp.concatenate([vals, jnp.full((B_HEADS, period - n_back - 1), NEG, F32)], axis=1))
    u = jnp.concatenate(rows, axis=0)

    def toeplitz_kernel(u_ref, o_ref):
        x = jnp.broadcast_to(u_ref[0], (BLOCK, period))
        r = pltpu.roll(x, 0, 1, stride=1, stride_axis=0)[:, :2 * BLOCK]
        col = lax.broadcasted_iota(jnp.int32, r.shape, 1)
        o_ref[0, 0, 0] = jnp.where(col >= BLOCK, r, NEG)
        o_ref[0, 1, 0] = r

    return pl.pallas_call(
        toeplitz_kernel,
        grid=(ng * B_HEADS,),
        in_specs=[pl.BlockSpec((1, 1, period), lambda c: (c, 0, 0))],
        out_specs=pl.BlockSpec((1, 2, 1, BLOCK, 2 * BLOCK),
                               lambda c: (c // B_HEADS, 0, (c % B_HEADS) // 2, c % 2, 0)),
        out_shape=jax.ShapeDtypeStruct((ng, 2, B_HEADS // 2, 2 * BLOCK, 2 * BLOCK), F32),
        compiler_params=_params("parallel"),
        name="bias_tiles_dilated",
    )(u.reshape(ng * B_HEADS, 1, period))


def kernel(x, c, norm_mix_g, norm_ffn_g, w_mod, b_mod, w_in, b_forget, lam_q1, lam_k1, lam_q2, lam_k2,
           subln_g, rel_bias, w_br_a, w_br_b, w_br_c, w_out, w_ff_gate, w_ff_up, w_ff_down, w_router,
           w_exp_gate, w_exp_up, w_exp_down, final_norm_g):
    bsz, seq, d = x.shape
    depth = w_mod.shape[0]
    rows = bsz * seq
    x2d = x.reshape(rows, d)
    attn_tiles = dict(t=ATTN_BLOCK, nsb=ATTN_SUB_TILES, nbb=ATTN_BATCH_ROWS)

    mod = _modulation(c, w_mod, b_mod)
    bias_a = _bias_tiles_a(rel_bias, seq, ATTN_BLOCK)
    bias_b = _bias_tiles_b(rel_bias)
    w_qkv_all, w_gf_all = _prep_w_in(w_in)

    final_g = final_norm_g.reshape(1, d)
    fused_final = False
    for l in range(depth):
        lam_init = 0.8 - 0.6 * math.exp(-0.3 * l)
        mod3 = mod[l].reshape(bsz, 6, d)
        w_qkv, w_gate, w_f = w_qkv_all[l], w_gf_all[l, :, :GATE_W], w_gf_all[l, :, GATE_W:]

        g_mix = norm_mix_g[l].reshape(1, d)
        qkv_ac, qkv_b0, qkv_b1, qkv_b2, f_logit = _qkv_proj(x2d, mod3, g_mix, w_qkv, w_f,
                                                            bsz=bsz, seq=seq, tm=ROWS_PROJ)
        qkv_ac = qkv_ac.reshape(bsz, seq, -1)

        b_f8 = jnp.pad(b_forget[l], (0, 8 - C_HEADS)).reshape(8, 1)
        fcum = _forget_cumsum(f_logit.reshape(bsz, seq, LANES), b_f8, col_block=0)
        fcum = fcum[:, :C_HEADS].reshape(bsz, C_HEADS // 2, 2, seq)

        oa = _attn_a(qkv_ac, bias_a, lam_q1[l].reshape(1, -1), lam_k1[l].reshape(1, -1),
                     lam_q2[l].reshape(1, -1), lam_k2[l].reshape(1, -1), subln_g[l].reshape(1, -1),
                     lam_init=lam_init, **attn_tiles)
        oc = _attn_c(qkv_ac, fcum, **attn_tiles)
        obs, lses = [], []
        groups = (qkv_b0.reshape(bsz, 1, seq, 3 * B_W), qkv_b1, qkv_b2)
        for g, (qkv_g, nb) in enumerate(zip(groups, DILATED_BLOCKS)):
            o_g, lse_g = _attn_b_group(qkv_g, bias_b[g], g, nb)
            obs.append(o_g)
            lses.append(lse_g)

        x2d = _merge(oa.reshape(rows, A_W), obs, lses, oc.reshape(rows, C_W), x2d, mod3, g_mix, w_gate,
                     w_br_a[l].astype(BF16), w_br_b[l].astype(BF16), w_br_c[l].astype(BF16),
                     w_out[l].astype(BF16), seq=seq, tm=ROWS_PROJ)

        g_ffn = norm_ffn_g[l].reshape(1, d)
        if l % 2 == 0:
            j = l // 2
            x2d = _ffn(x2d, mod3, g_ffn, w_ff_gate[j].astype(BF16), w_ff_up[j].astype(BF16),
                       w_ff_down[j].astype(BF16), seq=seq, tm=ROWS_PROJ)
        else:
            j = l // 2
            fused_final = l == depth - 1
            x2d = _moe(x2d, mod3, g_ffn, w_router[j], w_exp_gate[j].astype(BF16),
                       w_exp_up[j].astype(BF16), w_exp_down[j].astype(BF16), final_g,
                       seq=seq, final=fused_final)

    if not fused_final:
        x2d = _final_norm(x2d, final_g, tm=ROWS_ROUTE)
    return x2d.reshape(bsz, seq, d)
```

```python
import functools
import math

import jax
import jax.numpy as jnp
from jax import lax
from jax.experimental import pallas as pl
from jax.experimental.pallas import tpu as pltpu

F32 = jnp.float32
BF16 = jnp.bfloat16

D_MODEL = 1024
HEAD_DIM = 64
LANES = 128
A_HEADS = 4
A_W = A_HEADS * 2 * HEAD_DIM
B_GROUPS = ((128, 1), (512, 4), (2048, 16))
B_HEADS = 6
B_W = B_HEADS * HEAD_DIM
B_QW = len(B_GROUPS) * B_W
C_HEADS = 6
C_W = C_HEADS * HEAD_DIM
N_BRANCH = 3
BLOCK = 128
N_BUCKETS = 32
REL_MAX_DIST = 2048
EPS = 1e-6
QKV_W = 3 * A_W + 3 * B_QW + 3 * C_W
GATE_W = N_BRANCH * D_MODEL
NEG = -1e30
SCALE = HEAD_DIM ** -0.5
VMEM_LIMIT = 56 * 1024 * 1024


MXU_TILE = 256

ROWS_PROJ = 512
ROWS_ROUTE = 1024
ROWS_EXPERT = 512
ROWS_COMBINE = 256
EXPERT_FF_CHUNK = 7 * MXU_TILE
ATTN_BLOCK = 256
ATTN_SUB_TILES = 8
ATTN_BATCH_ROWS = 2
DILATED_BLOCKS = (4, 1, 1)


def _params(*sem):
    return pltpu.CompilerParams(dimension_semantics=sem, vmem_limit_bytes=VMEM_LIMIT)


def _resident(a):
    return pl.BlockSpec(a.shape, lambda *_: (0, 0), pipeline_mode=pl.Buffered(1))


def _rms_mod(x, g, sc, sh):
    y = x * lax.rsqrt(jnp.mean(x * x, axis=-1, keepdims=True) + EPS)
    return (y * g) * (1.0 + sc) + sh


def _dot(a, b):
    return jnp.dot(a, b, preferred_element_type=F32)


def _dot_nt(a, b):
    return lax.dot_general(a, b, (((1,), (1,)), ((), ())), preferred_element_type=F32)


def _silu(a):
    return a * jax.nn.sigmoid(a)


def _mod_kernel(c_ref, w_ref, b_ref, o_ref):
    a = _silu(c_ref[...]).astype(BF16)
    o_ref[0] = _dot(a, w_ref[0].astype(BF16)) + b_ref[0]


def _modulation(c, w_mod, b_mod):
    depth, d, n = w_mod.shape
    bsz = c.shape[0]
    tn = 1536
    return pl.pallas_call(
        _mod_kernel,
        grid=(depth, n // tn),
        in_specs=[
            pl.BlockSpec((bsz, d), lambda l, j: (0, 0)),
            pl.BlockSpec((1, d, tn), lambda l, j: (l, 0, j)),
            pl.BlockSpec((1, 1, tn), lambda l, j: (l, 0, j)),
        ],
        out_specs=pl.BlockSpec((1, bsz, tn), lambda l, j: (l, 0, j)),
        out_shape=jax.ShapeDtypeStruct((depth, bsz, n), F32),
        compiler_params=_params("parallel", "parallel"),
        name="modulation",
    )(c, w_mod, b_mod.reshape(depth, 1, n))


_A_BLOCKS = 3 * A_W // B_W
_B_BLOCKS = 3 * B_QW // B_W
_QKV_SRC_BLOCKS = (list(range(_A_BLOCKS))
                   + list(range(_A_BLOCKS + _B_BLOCKS, QKV_W // B_W))
                   + [_A_BLOCKS + s * len(B_GROUPS) + g for g in range(len(B_GROUPS)) for s in range(3)])


def _prep_w_in_kernel(*refs):
    n = len(_QKV_SRC_BLOCKS)
    piece_refs, x_ref, y_ref, qkv_ref, gf_ref = refs[:n], refs[n], refs[n + 1], refs[n + 2], refs[n + 3]
    for j, ref in enumerate(piece_refs):
        qkv_ref[0, :, j * B_W:(j + 1) * B_W] = ref[0].astype(BF16)
    x = x_ref[0]
    rolled = pltpu.roll(x, GATE_W - C_HEADS, 1)
    tail = pltpu.roll(y_ref[0], LANES - C_HEADS, 1)
    lane = lax.broadcasted_iota(jnp.int32, tail.shape, 1)
    gf_ref[0, :, :GATE_W - LANES] = rolled[:, :GATE_W - LANES].astype(BF16)
    gf_ref[0, :, GATE_W - LANES:GATE_W] = jnp.where(lane < LANES - C_HEADS, rolled[:, GATE_W - LANES:],
                                                     tail).astype(BF16)
    gf_ref[0, :, GATE_W:] = jnp.where(lane < C_HEADS, x[:, :LANES], 0.0).astype(BF16)


def _prep_w_in(w_in):
    depth, d, _ = w_in.shape
    tr = 256
    piece = lambda c: pl.BlockSpec((1, tr, B_W), lambda l, i: (l, i, c))
    return pl.pallas_call(
        _prep_w_in_kernel,
        grid=(depth, d // tr),
        in_specs=[piece(c) for c in _QKV_SRC_BLOCKS]
        + [pl.BlockSpec((1, tr, GATE_W), lambda l, i: (l, i, QKV_W // GATE_W)),
           pl.BlockSpec((1, tr, LANES), lambda l, i: (l, i, (QKV_W + GATE_W) // LANES))],
        out_specs=[pl.BlockSpec((1, tr, QKV_W), lambda l, i: (l, i, 0)),
                   pl.BlockSpec((1, tr, GATE_W + LANES), lambda l, i: (l, i, 0))],
        out_shape=[jax.ShapeDtypeStruct((depth, d, QKV_W), BF16),
                   jax.ShapeDtypeStruct((depth, d, GATE_W + LANES), BF16)],
        compiler_params=_params("parallel", "parallel"),
        name="prep_w_in",
    )(*([w_in] * (len(_QKV_SRC_BLOCKS) + 2)))


def _qkv_kernel(x_ref, mod_ref, g_ref, w_ref, wf_ref, ac_ref, b0_ref, b1_ref, b2_ref, f_ref, h_scr, *, tm):
    m = mod_ref[0]
    h = _rms_mod(x_ref[...], g_ref[...], m[1:2], m[0:1])
    nc = h_scr.shape[0]
    for c in range(nc):
        h_scr[c] = h[:, c * LANES:(c + 1) * LANES]
    hb = h.astype(BF16)
    n_ac = ac_ref.shape[1]
    n_b = b0_ref.shape[1]
    f_ref[...] = _dot(hb, wf_ref[...])
    ac_ref[...] = _dot(hb, w_ref[:, 0:n_ac]).astype(BF16)
    b0_ref[...] = _dot(hb, w_ref[:, n_ac:n_ac + n_b]).astype(BF16)
    for gi, ref in ((1, b1_ref), (2, b2_ref)):
        dil = B_GROUPS[gi][1]
        per = tm // dil
        hp = jnp.concatenate(
            [jnp.concatenate([h_scr[c, pl.ds(r, per, stride=dil), :] for c in range(nc)], axis=1).astype(BF16)
             for r in range(dil)], axis=0)
        y = _dot(hp, w_ref[:, n_ac + gi * n_b:n_ac + (gi + 1) * n_b]).astype(BF16)
        for r in range(dil):
            ref[0, r] = y[r * per:(r + 1) * per]


def _qkv_proj(x2d, mod3, g, w, w_f, *, bsz, seq, tm):
    rows, d = x2d.shape
    per_b = seq // tm
    n_b = 3 * B_W
    n_ac = w.shape[1] - 3 * n_b
    dil1, dil2 = B_GROUPS[1][1], B_GROUPS[2][1]
    strided = lambda dil: pl.BlockSpec((1, dil, tm // dil, n_b), lambda i: (i // per_b, 0, i % per_b, 0))
    return pl.pallas_call(
        functools.partial(_qkv_kernel, tm=tm),
        grid=(rows // tm,),
        in_specs=[
            pl.BlockSpec((tm, d), lambda i: (i, 0)),
            pl.BlockSpec((1, 6, d), lambda i: (i // per_b, 0, 0)),
            pl.BlockSpec((1, d), lambda i: (0, 0)),
            _resident(w), _resident(w_f),
        ],
        out_specs=[pl.BlockSpec((tm, n_ac), lambda i: (i, 0)), pl.BlockSpec((tm, n_b), lambda i: (i, 0)),
                   strided(dil1), strided(dil2), pl.BlockSpec((tm, LANES), lambda i: (i, 0))],
        out_shape=[jax.ShapeDtypeStruct((rows, n_ac), BF16), jax.ShapeDtypeStruct((rows, n_b), BF16),
                   jax.ShapeDtypeStruct((bsz, dil1, seq // dil1, n_b), BF16),
                   jax.ShapeDtypeStruct((bsz, dil2, seq // dil2, n_b), BF16),
                   jax.ShapeDtypeStruct((rows, LANES), F32)],
        scratch_shapes=[pltpu.VMEM((d // LANES, tm, LANES), F32)],
        compiler_params=_params("parallel"),
        name="qkv_proj",
    )(x2d, mod3, g, w, w_f)


def _fcum_kernel(f_ref, b_ref, o_ref):
    z = f_ref[0].T[:8] + b_ref[...]
    x = jnp.minimum(z, 0.0) - jnp.log1p(jnp.exp(-jnp.abs(z)))
    s = x.shape[1]
    lane = lax.broadcasted_iota(jnp.int32, x.shape, 1)
    k = 1
    while k < s:
        x = x + jnp.where(lane >= k, pltpu.roll(x, k, 1), 0.0)
        k *= 2
    o_ref[0] = x


def _forget_cumsum(gf, b_f8, *, col_block):
    bsz, seq, _ = gf.shape
    return pl.pallas_call(
        _fcum_kernel,
        grid=(bsz,),
        in_specs=[
            pl.BlockSpec((1, seq, LANES), lambda b: (b, 0, col_block)),
            pl.BlockSpec((8, 1), lambda b: (0, 0)),
        ],
        out_specs=pl.BlockSpec((1, 8, seq), lambda b: (b, 0, 0)),
        out_shape=jax.ShapeDtypeStruct((bsz, 8, seq), F32),
        compiler_params=_params("parallel"),
        name="forget_cumsum",
    )(gf, b_f8)


def _half_masks(q):
    lane = lax.broadcasted_iota(jnp.int32, q.shape, 1)
    zero = jnp.zeros_like(q)
    return jnp.where(lane < HEAD_DIM, q, zero), jnp.where(lane >= HEAD_DIM, q, zero)


def _flash_init(first, v_ref, vext_scr, m_scr, acc_scr):
    @pl.when(first)
    def _():
        for bb in range(vext_scr.shape[0]):
            vext_scr[bb, :, :LANES] = v_ref[bb]
            vext_scr[bb, :, LANES:] = jnp.ones((vext_scr.shape[1], LANES), BF16)

    m_scr[...] = jnp.full(m_scr.shape, -jnp.inf, F32)
    acc_scr[...] = jnp.zeros(acc_scr.shape, F32)


def _lane_tile(a, n):
    return a if n == 1 else jnp.concatenate([a] * n, axis=1)


def _flash_update(s, vext, rows, m_scr, acc_scr):
    m_prev = m_scr[rows]
    m_new = jnp.maximum(m_prev, jnp.max(s, axis=-1, keepdims=True))
    alpha = jnp.exp(m_prev - m_new)
    p = jnp.exp(s - _lane_tile(m_new, s.shape[1] // LANES))
    acc_scr[rows] = _lane_tile(alpha, 2) * acc_scr[rows] + _dot(p.astype(BF16), vext)
    m_scr[rows] = m_new


def _flash_result(acc_scr):
    acc = acc_scr[...]
    return acc[:, :LANES] / acc[:, LANES:]


def _attn_a_kernel(q_ref, k_ref, v_ref, bias_ref, lq1, lk1, lq2, lk2, sg_ref, o_ref,
                   vext_scr, m_scr, acc_scr, *, t, nsb, nbb, lam_init):
    qi = pl.program_id(2)
    _flash_init(qi == 0, v_ref, vext_scr, m_scr, acc_scr)
    qh = [[_half_masks(q_ref[bb, sb * t:(sb + 1) * t, :] * SCALE) for sb in range(nsb)] for bb in range(nbb)]
    chain_rows = lambda bb, sb, hh: pl.ds(((bb * nsb + sb) * 2 + hh) * t, t)

    def step(kb, nkb, plan):
        off = pl.multiple_of(kb * t, t)
        for bb in range(nbb):
            kblk = k_ref[bb, pl.ds(off, nkb * t), :]
            vext = vext_scr[bb, pl.ds(off, nkb * t), :]
            for sb, deltas in plan:
                for hh in range(2):
                    bias = [bias_ref[0, d, hh * t:(hh + 1) * t, :] for d in deltas]
                    s = _dot_nt(qh[bb][sb][hh], kblk) + (bias[0] if nkb == 1 else jnp.concatenate(bias, axis=1))
                    _flash_update(s, vext, chain_rows(bb, sb, hh), m_scr, acc_scr)

    def body(kb2, carry):
        first = [qi * nsb + sb - 2 * kb2 for sb in range(nsb)]
        step(2 * kb2, 2, [(sb, (first[sb], first[sb] - 1)) for sb in range(nsb)])
        return carry

    if q_ref.shape[1] < k_ref.shape[1]:
        lax.fori_loop(0, qi * (nsb // 2), body, 0)
    for j in range(0, nsb, 2):
        step(qi * nsb + j, 1, [(j, (0,))])
        step(qi * nsb + j, 2, [(sb, (sb - j, sb - j - 1)) for sb in range(j + 1, nsb)])

    o = _flash_result(acc_scr)
    lam = (jnp.exp(jnp.sum(lq1[...] * lk1[...], axis=-1, keepdims=True))
           - jnp.exp(jnp.sum(lq2[...] * lk2[...], axis=-1, keepdims=True)) + lam_init)
    for bb in range(nbb):
        for sb in range(nsb):
            c0 = (bb * nsb + sb) * 2 * t
            d = o[c0:c0 + t] - lam * o[c0 + t:c0 + 2 * t]
            y = d * lax.rsqrt(jnp.mean(d * d, axis=-1, keepdims=True) + EPS)
            o_ref[bb, sb * t:(sb + 1) * t, :] = ((y * sg_ref[...]) * (1.0 - lam_init)).astype(o_ref.dtype)


def _flash_scratch(seq, t, nsb, nbb):
    chains = 2 * nsb * nbb
    return [pltpu.VMEM((nbb, seq, 2 * LANES), BF16), pltpu.VMEM((chains * t, LANES), F32),
            pltpu.VMEM((chains * t, 2 * LANES), F32)]


def _attn_a(qkv, bias_a, lq1, lk1, lq2, lk2, subln_g, *, lam_init, t, nsb, nbb):
    bsz, seq, _ = qkv.shape
    tq = t * nsb
    assert bsz % nbb == 0 and seq % tq == 0 and nsb % 2 == 0
    vec = lambda n: pl.BlockSpec((1, n), lambda h, b, i: (0, 0))
    return pl.pallas_call(
        functools.partial(_attn_a_kernel, t=t, nsb=nsb, nbb=nbb, lam_init=lam_init),
        grid=(A_HEADS, bsz // nbb, seq // tq),
        in_specs=[
            pl.BlockSpec((nbb, tq, LANES), lambda h, b, i: (b, i, h)),
            pl.BlockSpec((nbb, seq, LANES), lambda h, b, i: (b, 0, A_HEADS + h)),
            pl.BlockSpec((nbb, seq, LANES), lambda h, b, i: (b, 0, 2 * A_HEADS + h)),
            pl.BlockSpec((1, seq // t, 2 * t, t), lambda h, b, i: (h, 0, 0, 0)),
            vec(HEAD_DIM), vec(HEAD_DIM), vec(HEAD_DIM), vec(HEAD_DIM), vec(LANES),
        ],
        out_specs=pl.BlockSpec((nbb, tq, LANES), lambda h, b, i: (b, i, h)),
        out_shape=jax.ShapeDtypeStruct((bsz, seq, A_W), BF16),
        scratch_shapes=_flash_scratch(seq, t, nsb, nbb),
        compiler_params=_params("parallel", "parallel", "arbitrary"),
        name="attn_diff",
    )(qkv, qkv, qkv, bias_a, lq1, lk1, lq2, lk2, subln_g)


def _attn_c_kernel(q_ref, k_ref, v_ref, f_ref, o_ref, vext_scr, m_scr, acc_scr, *, t, nsb, nbb):
    qi = pl.program_id(2)
    _flash_init(qi == 0, v_ref, vext_scr, m_scr, acc_scr)
    qh = [[_half_masks(q_ref[bb, sb * t:(sb + 1) * t, :] * SCALE) for sb in range(nsb)] for bb in range(nbb)]
    chain_rows = lambda bb, sb, hh: pl.ds(((bb * nsb + sb) * 2 + hh) * t, t)
    q_off = pl.multiple_of(qi * (t * nsb), t * nsb)
    f_anchor = [[f_ref[bb, 0, :, pl.ds(pl.multiple_of(q_off + sb * t, t), LANES)][:, :1] for sb in range(nsb)]
                for bb in range(nbb)]

    def causal(nkb):
        r = lax.broadcasted_iota(jnp.int32, (t, nkb * t), 0)
        c = lax.broadcasted_iota(jnp.int32, (t, nkb * t), 1)
        return r + (nkb - 1) * t >= c

    def step(kb, nkb, sbs, diag_sb):
        off = pl.multiple_of(kb * t, t)
        for bb in range(nbb):
            kblk = k_ref[bb, pl.ds(off, nkb * t), :]
            vext = vext_scr[bb, pl.ds(off, nkb * t), :]
            f_keys = [f_ref[bb, 0, hh:hh + 1, pl.ds(off, nkb * t)] for hh in range(2)]
            for sb in sbs:
                for hh in range(2):
                    s = _dot_nt(qh[bb][sb][hh], kblk) + (f_anchor[bb][sb][hh:hh + 1] - f_keys[hh])
                    if sb == diag_sb:
                        s = jnp.where(causal(nkb), s, NEG)
                    _flash_update(s, vext, chain_rows(bb, sb, hh), m_scr, acc_scr)

    def body(kb2, carry):
        step(2 * kb2, 2, range(nsb), None)
        return carry

    if q_ref.shape[1] < k_ref.shape[1]:
        lax.fori_loop(0, qi * (nsb // 2), body, 0)
    for j in range(0, nsb, 2):
        step(qi * nsb + j, 1, [j], j)
        step(qi * nsb + j, 2, range(j + 1, nsb), j + 1)

    o = _flash_result(acc_scr)
    lane = lax.broadcasted_iota(jnp.int32, (t, LANES), 1)
    for bb in range(nbb):
        for sb in range(nsb):
            c0 = (bb * nsb + sb) * 2 * t
            pair = jnp.where(lane < HEAD_DIM, o[c0:c0 + t], o[c0 + t:c0 + 2 * t])
            o_ref[bb, sb * t:(sb + 1) * t, :] = pair.astype(o_ref.dtype)


def _attn_c(qkv, fcum, *, t, nsb, nbb):
    bsz, seq, _ = qkv.shape
    tq = t * nsb
    assert bsz % nbb == 0 and seq % tq == 0 and nsb % 2 == 0
    pairs = C_HEADS // 2
    q0 = 3 * A_W // LANES
    return pl.pallas_call(
        functools.partial(_attn_c_kernel, t=t, nsb=nsb, nbb=nbb),
        grid=(pairs, bsz // nbb, seq // tq),
        in_specs=[
            pl.BlockSpec((nbb, tq, LANES), lambda p, b, i: (b, i, q0 + p)),
            pl.BlockSpec((nbb, seq, LANES), lambda p, b, i: (b, 0, q0 + pairs + p)),
            pl.BlockSpec((nbb, seq, LANES), lambda p, b, i: (b, 0, q0 + 2 * pairs + p)),
            pl.BlockSpec((nbb, 1, 2, seq), lambda p, b, i: (b, p, 0, 0)),
        ],
        out_specs=pl.BlockSpec((nbb, tq, LANES), lambda p, b, i: (b, i, p)),
        out_shape=jax.ShapeDtypeStruct((bsz, seq, C_W), BF16),
        scratch_shapes=_flash_scratch(seq, t, nsb, nbb),
        compiler_params=_params("parallel", "parallel", "arbitrary"),
        name="attn_forget",
    )(qkv, qkv, qkv, fcum)


def _attn_b_kernel(q_ref, kp_ref, kc_ref, vp_ref, vc_ref, bias_ref, o_ref, lse_ref, *, dil, nb):
    n = pl.program_id(1)
    lane = lax.broadcasted_iota(jnp.int32, (BLOCK, LANES), 1)
    first_variant = jnp.minimum(n, 1)

    def residue(r):
        for hp in range(B_HEADS // 2):
            cols = slice(hp * LANES, (hp + 1) * LANES)
            kcat = jnp.concatenate([kp_ref[0, r, :, cols], kc_ref[0, r, :, cols]], axis=0)
            vcat = jnp.concatenate([vp_ref[0, r, :, cols], vc_ref[0, r, :, cols]], axis=0)
            for jb in range(nb):
                qh = _half_masks(q_ref[0, r, jb * BLOCK:(jb + 1) * BLOCK, cols] * SCALE)
                kwin = kcat[jb * BLOCK:(jb + 2) * BLOCK]
                vwin = vcat[jb * BLOCK:(jb + 2) * BLOCK]
                variant = first_variant if jb == 0 else 1
                outs, lses = [], []
                for hh in range(2):
                    s = _dot_nt(qh[hh], kwin) + bias_ref[variant, hp, hh * BLOCK:(hh + 1) * BLOCK, :]
                    m = jnp.max(s, axis=-1, keepdims=True)
                    e = jnp.exp(s - m)
                    den = jnp.sum(e, axis=-1, keepdims=True)
                    outs.append(_dot(e.astype(BF16), vwin) / den)
                    lses.append(jnp.broadcast_to(m + jnp.log(den), (BLOCK, LANES)))
                if dil == 1:
                    rows = pl.ds(jb * BLOCK, BLOCK)
                else:
                    rows = pl.ds(jb * BLOCK * dil + r, BLOCK, stride=dil)
                o_ref[0, hp, rows, :] = jnp.where(lane < HEAD_DIM, outs[0], outs[1])
                lse_ref[0, hp, rows, :] = jnp.where(lane < HEAD_DIM, lses[0], lses[1])

    unroll = min(dil, 4)
    if dil == unroll:
        for r in range(dil):
            residue(r)
    else:
        def body(i, carry):
            for j in range(unroll):
                residue(i * unroll + j)
            return carry

        lax.fori_loop(0, dil // unroll, body, 0)


def _attn_b_group(qkv_g, bias_g, g, nb):
    bsz, dil, m_len, _ = qkv_g.shape
    tb = BLOCK * nb
    cur = lambda c: pl.BlockSpec((1, dil, tb, B_W), lambda b, n: (b, 0, n, c))
    prev = lambda c: pl.BlockSpec((1, dil, BLOCK, B_W), lambda b, n: (b, 0, jnp.maximum(n * nb - 1, 0), c))
    pairs = B_HEADS // 2
    out_spec = pl.BlockSpec((1, pairs, tb * dil, LANES), lambda b, n: (b, 0, n, 0))
    out_sds = jax.ShapeDtypeStruct((bsz, pairs, m_len * dil, LANES), F32)
    return pl.pallas_call(
        functools.partial(_attn_b_kernel, dil=dil, nb=nb),
        grid=(bsz, m_len // tb),
        in_specs=[cur(0), prev(1), cur(1), prev(2), cur(2),
                  pl.BlockSpec(bias_g.shape, lambda b, n: (0, 0, 0, 0))],
        out_specs=[out_spec, out_spec],
        out_shape=[out_sds, out_sds],
        compiler_params=_params("parallel", "arbitrary"),
        name=f"attn_dilated_g{g}",
    )(qkv_g, qkv_g, qkv_g, qkv_g, qkv_g, bias_g)


def _merge_kernel(oa_ref, ob0, ob1, ob2, ls0, ls1, ls2, oc_ref, x_ref, mod_ref, g_ref,
                  wg_ref, wa_ref, wb_ref, wc_ref, wo_ref, o_ref):
    x = x_ref[...]
    m = mod_ref[0]
    hb = _rms_mod(x, g_ref[...], m[1:2], m[0:1]).astype(BF16)
    parts = []
    for hp in range(B_HEADS // 2):
        l0, l1, l2 = ls0[0, hp], ls1[0, hp], ls2[0, hp]
        mx = jnp.maximum(jnp.maximum(l0, l1), l2)
        e0, e1, e2 = jnp.exp(l0 - mx), jnp.exp(l1 - mx), jnp.exp(l2 - mx)
        den = e0 + e1 + e2
        parts.append((e0 / den) * ob0[0, hp] + (e1 / den) * ob1[0, hp] + (e2 / den) * ob2[0, hp])
    ob = jnp.concatenate(parts, axis=1)
    d = D_MODEL
    gate = lambda k: jax.nn.sigmoid(_dot(hb, wg_ref[:, k * d:(k + 1) * d]))
    merged = (gate(0) * _dot(oa_ref[...], wa_ref[...])
              + gate(1) * _dot(ob.astype(BF16), wb_ref[...])
              + gate(2) * _dot(oc_ref[...], wc_ref[...]))
    y = _dot(merged.astype(BF16), wo_ref[...])
    o_ref[...] = x + m[2:3] * y


def _merge(oa, obs, lses, oc, x2d, mod3, g, w_gate, wa, wb, wc, wo, *, seq, tm):
    rows, d = x2d.shape
    per_b = seq // tm
    row = lambda w: pl.BlockSpec((tm, w), lambda i: (i, 0))
    paired = pl.BlockSpec((1, B_HEADS // 2, tm, LANES), lambda i: (i // per_b, 0, i % per_b, 0))
    return pl.pallas_call(
        _merge_kernel,
        grid=(rows // tm,),
        in_specs=[row(A_W)] + [paired] * 6 + [row(C_W), row(d),
                  pl.BlockSpec((1, 6, d), lambda i: (i // per_b, 0, 0)),
                  pl.BlockSpec((1, d), lambda i: (0, 0)),
                  _resident(w_gate), _resident(wa), _resident(wb), _resident(wc), _resident(wo)],
        out_specs=row(d),
        out_shape=jax.ShapeDtypeStruct((rows, d), F32),
        compiler_params=_params("parallel"),
        name="merge_outproj",
    )(oa, *obs, *lses, oc, x2d, mod3, g, w_gate, wa, wb, wc, wo)


def _ffn_kernel(x_ref, mod_ref, g_ref, wg_ref, wu_ref, wd_ref, o_ref, *, bounds):
    x = x_ref[...]
    m = mod_ref[0]
    h = _rms_mod(x, g_ref[...], m[4:5], m[3:4]).astype(BF16)
    acc = None
    for lo, hi in bounds:
        act = (_silu(_dot(h, wg_ref[:, lo:hi])) * _dot(h, wu_ref[:, lo:hi])).astype(BF16)
        part = _dot(act, wd_ref[lo:hi, :])
        acc = part if acc is None else acc + part
    o_ref[...] = x + m[5:6] * acc


def _ffn(x2d, mod3, g, wg, wu, wd, *, seq, tm):
    rows, d = x2d.shape
    dff = wg.shape[1]
    half = (dff // MXU_TILE + 1) // 2 * MXU_TILE
    per_b = seq // tm
    return pl.pallas_call(
        functools.partial(_ffn_kernel, bounds=((0, half), (half, dff))),
        grid=(rows // tm,),
        in_specs=[
            pl.BlockSpec((tm, d), lambda i: (i, 0)),
            pl.BlockSpec((1, 6, d), lambda i: (i // per_b, 0, 0)),
            pl.BlockSpec((1, d), lambda i: (0, 0)),
            _resident(wg), _resident(wu), _resident(wd),
        ],
        out_specs=pl.BlockSpec((tm, d), lambda i: (i, 0)),
        out_shape=jax.ShapeDtypeStruct((rows, d), F32),
        compiler_params=_params("parallel"),
        name="ffn_dense",
    )(x2d, mod3, g, wg, wu, wd)


def _route_kernel(x_ref, mod_ref, g_ref, wr_ref, tri_ref, h_ref, ridx_ref, rw_ref, cnt_ref, carry_scr):
    i = pl.program_id(0)

    @pl.when(i == 0)
    def _():
        carry_scr[...] = jnp.zeros(carry_scr.shape, F32)

    m = mod_ref[0]
    h = _rms_mod(x_ref[...], g_ref[...], m[4:5], m[3:4])
    _store_row_tiles(h_ref, h)
    w = wr_ref[...]
    h_hi, w_hi = h.astype(BF16), w.astype(BF16)
    h_lo = (h - h_hi.astype(F32)).astype(BF16)
    w_lo = (w - w_hi.astype(F32)).astype(BF16)
    logits = _dot_nt(w_hi, h_hi) + (_dot_nt(w_hi, h_lo) + _dot_nt(w_lo, h_hi))
    idx = lax.broadcasted_iota(jnp.int32, logits.shape, 0)
    n = logits.shape[0]
    m1 = jnp.max(logits, axis=0, keepdims=True)
    i1 = jnp.min(jnp.where(logits == m1, idx, n), axis=0, keepdims=True)
    first = idx == i1
    rest = jnp.where(first, -jnp.inf, logits)
    m2 = jnp.max(rest, axis=0, keepdims=True)
    i2 = jnp.min(jnp.where(rest == m2, idx, n), axis=0, keepdims=True)
    second = idx == i2
    e = jnp.exp(m2 - m1)
    den = 1.0 + e
    onehot = jnp.where(first | second, 1.0, 0.0)
    before = _dot(onehot.astype(BF16), tri_ref[...]) - onehot + carry_scr[...]
    rank1 = jnp.sum(jnp.where(first, before, 0.0), axis=0, keepdims=True)
    rank2 = jnp.sum(jnp.where(second, before, 0.0), axis=0, keepdims=True)
    carry_scr[...] += jnp.sum(onehot, axis=1, keepdims=True)
    picks = jnp.where(idx == 0, i1.astype(F32), jnp.where(idx == 1, i2.astype(F32),
                      jnp.where(idx == 2, rank1, jnp.where(idx == 3, rank2, 0.0))))
    weights = jnp.where(idx == 0, 1.0 / den, jnp.where(idx == 1, e / den, 0.0))
    ridx_ref[...] = picks.T.astype(jnp.int32)
    rw_ref[...] = weights.T
    cnt_ref[...] = carry_scr[...]


def _route(x2d, mod3, g, wr, *, seq, tm):
    rows, d = x2d.shape
    ne = wr.shape[1]
    per_b = seq // tm
    tri = (jnp.arange(tm)[:, None] <= jnp.arange(tm)[None, :]).astype(BF16)
    row = lambda w: pl.BlockSpec((tm, w), lambda i: (i, 0))
    return pl.pallas_call(
        _route_kernel,
        grid=(rows // tm,),
        in_specs=[row(d), pl.BlockSpec((1, 6, d), lambda i: (i // per_b, 0, 0)),
                  pl.BlockSpec((1, d), lambda i: (0, 0)), pl.BlockSpec((ne, d), lambda i: (0, 0)),
                  pl.BlockSpec((tm, tm), lambda i: (0, 0))],
        out_specs=[pl.BlockSpec((tm, d // LANES, LANES), lambda i: (i, 0, 0)), row(ne), row(ne),
                   pl.BlockSpec((ne, 1), lambda i: (0, 0))],
        out_shape=[jax.ShapeDtypeStruct((rows, d // LANES, LANES), F32),
                   jax.ShapeDtypeStruct((rows, ne), jnp.int32),
                   jax.ShapeDtypeStruct((rows, ne), F32), jax.ShapeDtypeStruct((ne, 1), F32)],
        scratch_shapes=[pltpu.VMEM((ne, 1), F32)],
        compiler_params=_params("arbitrary"),
        name="moe_route",
    )(x2d, mod3, g, wr.T, tri)


def _store_row_tiles(ref, val):
    for c in range(ref.shape[-2]):
        ref[:, c, :] = val[:, c * LANES:(c + 1) * LANES]


ROW_PITCH = D_MODEL // LANES + 1


ROW_TILE = D_MODEL // LANES


def _gather_rows(idx_of, n, src_hbm, dst, sem):
    for j in range(n):
        start = pl.multiple_of(idx_of(j) * ROW_TILE, ROW_TILE)
        pltpu.make_async_copy(src_hbm.at[pl.ds(start, ROW_TILE)], dst.at[pl.ds(j * ROW_PITCH, ROW_TILE)],
                              sem).start()


def _gather_wait(n, src_hbm, dst, sem):
    pltpu.make_async_copy(src_hbm.at[pl.ds(0, n * ROW_TILE)], dst.at[pl.ds(0, n * ROW_TILE)], sem).wait()


def _load_gathered(buf, n):
    return jnp.concatenate([buf[pl.ds(c, n, stride=ROW_PITCH), :] for c in range(ROW_TILE)], axis=1)


def _expert_kernel(te_ref, pos_ref, nu_ref, h_hbm, zeros_hbm, wg_ref, wu_ref, wd_ref, y_ref,
                   xbuf, xb_scr, acc_scr, src_smem, sem, *, nf, r):
    i = pl.program_id(0)
    f = pl.program_id(1)
    active = i < nu_ref[0]
    slot = i % 2

    def gather(tile, s):
        _gather_rows(lambda j: src_smem[tile * r + j], r, h_hbm, xbuf.at[s], sem.at[s])

    @pl.when((i == 0) & (f == 0))
    def _():
        clear = pltpu.make_async_copy(zeros_hbm, src_smem, sem.at[0])
        clear.start()
        clear.wait()

        def place(p, carry):
            src_smem[pos_ref[p]] = p >> 1
            return carry

        lax.fori_loop(0, pos_ref.shape[0], place, 0, unroll=8)
        gather(0, 0)

    @pl.when(active & (f == 0))
    def _():
        _gather_wait(r, h_hbm, xbuf.at[slot], sem.at[slot])

        @pl.when(i + 1 < nu_ref[0])
        def _():
            gather(i + 1, 1 - slot)

        xb_scr[...] = _load_gathered(xbuf.at[slot], r).astype(BF16)
        acc_scr[...] = jnp.zeros(acc_scr.shape, F32)

    @pl.when(active)
    def _():
        xb = xb_scr[...]
        act = (_silu(_dot(xb, wg_ref[0])) * _dot(xb, wu_ref[0])).astype(BF16)
        acc_scr[...] += _dot(act, wd_ref[0])

    @pl.when(f == nf - 1)
    def _():
        y_ref[...] = jnp.where(active, acc_scr[...], 0.0)


def _experts(h2, tile_expert, pos, n_used, wg, wu, wd, *, r, tf):
    ne, d, dff = wg.shape
    nf = dff // tf
    nt = tile_expert.shape[0]
    p_rows = nt * r
    fsel = lambda i, f, nu: jnp.where(i < nu[0], f, nf - 1)
    grid_spec = pltpu.PrefetchScalarGridSpec(
        num_scalar_prefetch=3,
        grid=(nt, nf),
        in_specs=[
            pl.BlockSpec(memory_space=pl.ANY),
            pl.BlockSpec(memory_space=pl.ANY),
            pl.BlockSpec((1, d, tf), lambda i, f, te, sp, nu: (te[i], 0, fsel(i, f, nu))),
            pl.BlockSpec((1, d, tf), lambda i, f, te, sp, nu: (te[i], 0, fsel(i, f, nu))),
            pl.BlockSpec((1, tf, d), lambda i, f, te, sp, nu: (te[i], fsel(i, f, nu), 0)),
        ],
        out_specs=pl.BlockSpec((r, d), lambda i, f, te, sp, nu: (i, 0)),
        scratch_shapes=[pltpu.VMEM((2, r * ROW_PITCH, LANES), F32), pltpu.VMEM((r, d), BF16),
                        pltpu.VMEM((r, d), F32), pltpu.SMEM((p_rows,), jnp.int32),
                        pltpu.SemaphoreType.DMA((2,))],
    )
    return pl.pallas_call(
        functools.partial(_expert_kernel, nf=nf, r=r),
        grid_spec=grid_spec,
        out_shape=jax.ShapeDtypeStruct((p_rows, d), F32),
        compiler_params=_params("arbitrary", "arbitrary"),
        name="moe_experts",
    )(tile_expert, pos, n_used, h2, jnp.zeros((p_rows,), jnp.int32), wg, wu, wd)


COMBINE_SLOTS = 3


def _combine_kernel(pos_ref, y_hbm, x_ref, rw_ref, mod_ref, fg_ref, o_ref, ybuf, sem, *, tm, final):
    i = pl.program_id(0)
    n = pl.num_programs(0)
    ahead = COMBINE_SLOTS - 1
    slot = i % COMBINE_SLOTS

    def gather(tile, s):
        for k in range(2):
            for j in range(tm):
                row = pos_ref[(tile * tm + j) * 2 + k]
                pltpu.make_async_copy(y_hbm.at[pl.ds(row, 1)], ybuf.at[s, k, pl.ds(j, 1)], sem.at[s]).start()

    @pl.when(i == 0)
    def _():
        for t0 in range(ahead):
            @pl.when(t0 < n)
            def _():
                gather(t0, t0)

    for k in range(2):
        pltpu.make_async_copy(y_hbm.at[pl.ds(0, tm)], ybuf.at[slot, k], sem.at[slot]).wait()

    @pl.when(i + ahead < n)
    def _():
        gather(i + ahead, (i + ahead) % COMBINE_SLOTS)

    w = rw_ref[...]
    f = w[:, 0:1] * ybuf[slot, 0] + w[:, 1:2] * ybuf[slot, 1]
    x = x_ref[...] + mod_ref[0][5:6] * f
    if final:
        x = (x * lax.rsqrt(jnp.mean(x * x, axis=-1, keepdims=True) + EPS)) * fg_ref[...]
    o_ref[...] = x


def _combine(pos, y, x2d, rw, mod3, final_g, *, seq, tm, final):
    rows, d = x2d.shape
    per_b = seq // tm
    grid_spec = pltpu.PrefetchScalarGridSpec(
        num_scalar_prefetch=1,
        grid=(rows // tm,),
        in_specs=[
            pl.BlockSpec(memory_space=pl.ANY),
            pl.BlockSpec((tm, d), lambda i, p: (i, 0)),
            pl.BlockSpec((tm, rw.shape[1]), lambda i, p: (i, 0)),
            pl.BlockSpec((1, 6, d), lambda i, p: (i // per_b, 0, 0)),
            pl.BlockSpec((1, d), lambda i, p: (0, 0)),
        ],
        out_specs=pl.BlockSpec((tm, d), lambda i, p: (i, 0)),
        scratch_shapes=[pltpu.VMEM((COMBINE_SLOTS, 2, tm, d), F32),
                        pltpu.SemaphoreType.DMA((COMBINE_SLOTS,))],
    )
    return pl.pallas_call(
        functools.partial(_combine_kernel, tm=tm, final=final),
        grid_spec=grid_spec,
        out_shape=jax.ShapeDtypeStruct((rows, d), F32),
        compiler_params=_params("arbitrary"),
        name="moe_combine",
    )(pos, y, x2d, rw, mod3, final_g)


def _moe(x2d, mod3, g, wr, wg, wu, wd, final_g, *, seq, final):
    rows, d = x2d.shape
    ne = wr.shape[1]
    r = ROWS_EXPERT
    h2, ridx, rw, cnt = _route(x2d, mod3, g, wr, seq=seq, tm=ROWS_ROUTE)
    counts = cnt[:, 0].astype(jnp.int32)
    tiles_e = (counts + r - 1) // r
    tile_end = jnp.cumsum(tiles_e)
    start = (tile_end - tiles_e) * r
    pos = (jnp.take(start, ridx[:, 0:2], axis=0) + ridx[:, 2:4]).reshape(-1).astype(jnp.int32)
    nt = (2 * rows) // r + ne
    n_used = tile_end[-1:]
    tile_idx = jnp.minimum(jnp.arange(nt), n_used[0] - 1)
    tile_expert = jnp.sum(tile_idx[:, None] >= tile_end[None, :], axis=1)
    y = _experts(h2.reshape(-1, LANES), tile_expert.astype(jnp.int32), pos, n_used.astype(jnp.int32),
                 wg, wu, wd, r=r, tf=EXPERT_FF_CHUNK)
    return _combine(pos, y, x2d, rw, mod3, final_g, seq=seq, tm=ROWS_COMBINE, final=final)


def _final_norm_kernel(x_ref, g_ref, o_ref):
    x = x_ref[...]
    o_ref[...] = (x * lax.rsqrt(jnp.mean(x * x, axis=-1, keepdims=True) + EPS)) * g_ref[...]


def _final_norm(x2d, g, *, tm):
    rows, d = x2d.shape
    return pl.pallas_call(
        _final_norm_kernel,
        grid=(rows // tm,),
        in_specs=[pl.BlockSpec((tm, d), lambda i: (i, 0)), pl.BlockSpec((1, d), lambda i: (0, 0))],
        out_specs=pl.BlockSpec((tm, d), lambda i: (i, 0)),
        out_shape=jax.ShapeDtypeStruct((rows, d), F32),
        compiler_params=_params("parallel"),
        name="final_norm",
    )(x2d, g)


def _t5_bucket(dist):
    n = jnp.maximum(dist, 0)
    max_exact = N_BUCKETS // 2
    nf = jnp.maximum(n, 1).astype(F32)
    large = max_exact + (jnp.log(nf / max_exact) / math.log(REL_MAX_DIST / max_exact)
                         * (N_BUCKETS - max_exact)).astype(jnp.int32)
    large = jnp.minimum(large, N_BUCKETS - 1)
    return jnp.where(n < max_exact, n, large)


def _bias_tiles_a(rel_bias, seq, t):
    nq = seq // t
    ncol = 2 * A_HEADS
    tab = rel_bias[:, :ncol][_t5_bucket(jnp.arange(seq))].astype(F32).T
    vneg = jnp.full((ncol, t), NEG, F32)
    v = jnp.concatenate([vneg, tab], axis=1)
    u = jnp.concatenate([v[:, 1:seq + 1][:, ::-1], vneg[:, :1], v[:, seq + 1:seq + t][:, ::-1]], axis=1)

    def toeplitz_kernel(u_ref, o_ref):
        x = jnp.broadcast_to(u_ref[0], (t, seq + t))
        r = pltpu.roll(x, 0, 1, stride=1, stride_axis=0)
        for delta in range(nq):
            c0 = (nq - 1 - delta) * t
            o_ref[0, delta] = r[:, c0:c0 + t]

    return pl.pallas_call(
        toeplitz_kernel,
        grid=(ncol,),
        in_specs=[pl.BlockSpec((1, 1, seq + t), lambda c: (c, 0, 0))],
        out_specs=pl.BlockSpec((1, nq, t, t), lambda c: (c // 2, 0, c % 2, 0)),
        out_shape=jax.ShapeDtypeStruct((A_HEADS, nq, 2 * t, t), F32),
        compiler_params=_params("parallel"),
        name="bias_tiles_diff",
    )(u.reshape(ncol, 1, seq + t))


def _bias_tiles_b(rel_bias):
    ng = len(B_GROUPS)
    period = 3 * BLOCK
    rows = []
    for g, (win, dil) in enumerate(B_GROUPS):
        n_back = win // dil
        tab = rel_bias[:, 2 * A_HEADS + g * B_HEADS:2 * A_HEADS + (g + 1) * B_HEADS]
        vals = tab[_t5_bucket(jnp.arange(n_back, -1, -1) * dil)].astype(F32).T
        rows.append(jnp.concatenate([vals, jnp.full((B_HEADS, period - n_back - 1), NEG, F32)], axis=1))
    u = jnp.concatenate(rows, axis=0)

    def toeplitz_kernel(u_ref, o_ref):
        x = jnp.broadcast_to(u_ref[0], (BLOCK, period))
        r = pltpu.roll(x, 0, 1, stride=1, stride_axis=0)[:, :2 * BLOCK]
        col = lax.broadcasted_iota(jnp.int32, r.shape, 1)
        o_ref[0, 0, 0] = jnp.where(col >= BLOCK, r, NEG)
        o_ref[0, 1, 0] = r

    return pl.pallas_call(
        toeplitz_kernel,
        grid=(ng * B_HEADS,),
        in_specs=[pl.BlockSpec((1, 1, period), lambda c: (c, 0, 0))],
        out_specs=pl.BlockSpec((1, 2, 1, BLOCK, 2 * BLOCK),
                               lambda c: (c // B_HEADS, 0, (c % B_HEADS) // 2, c % 2, 0)),
        out_shape=jax.ShapeDtypeStruct((ng, 2, B_HEADS // 2, 2 * BLOCK, 2 * BLOCK), F32),
        compiler_params=_params("parallel"),
        name="bias_tiles_dilated",
    )(u.reshape(ng * B_HEADS, 1, period))


def kernel(x, c, norm_mix_g, norm_ffn_g, w_mod, b_mod, w_in, b_forget, lam_q1, lam_k1, lam_q2, lam_k2,
           subln_g, rel_bias, w_br_a, w_br_b, w_br_c, w_out, w_ff_gate, w_ff_up, w_ff_down, w_router,
           w_exp_gate, w_exp_up, w_exp_down, final_norm_g):
    bsz, seq, d = x.shape
    depth = w_mod.shape[0]
    rows = bsz * seq
    x2d = x.reshape(rows, d)
    attn_tiles = dict(t=ATTN_BLOCK, nsb=ATTN_SUB_TILES, nbb=ATTN_BATCH_ROWS)

    mod = _modulation(c, w_mod, b_mod)
    bias_a = _bias_tiles_a(rel_bias, seq, ATTN_BLOCK)
    bias_b = _bias_tiles_b(rel_bias)
    w_qkv_all, w_gf_all = _prep_w_in(w_in)

    final_g = final_norm_g.reshape(1, d)
    fused_final = False
    for l in range(depth):
        lam_init = 0.8 - 0.6 * math.exp(-0.3 * l)
        mod3 = mod[l].reshape(bsz, 6, d)
        w_qkv, w_gate, w_f = w_qkv_all[l], w_gf_all[l, :, :GATE_W], w_gf_all[l, :, GATE_W:]

        g_mix = norm_mix_g[l].reshape(1, d)
        qkv_ac, qkv_b0, qkv_b1, qkv_b2, f_logit = _qkv_proj(x2d, mod3, g_mix, w_qkv, w_f,
                                                            bsz=bsz, seq=seq, tm=ROWS_PROJ)
        qkv_ac = qkv_ac.reshape(bsz, seq, -1)

        b_f8 = jnp.pad(b_forget[l], (0, 8 - C_HEADS)).reshape(8, 1)
        fcum = _forget_cumsum(f_logit.reshape(bsz, seq, LANES), b_f8, col_block=0)
        fcum = fcum[:, :C_HEADS].reshape(bsz, C_HEADS // 2, 2, seq)

        oa = _attn_a(qkv_ac, bias_a, lam_q1[l].reshape(1, -1), lam_k1[l].reshape(1, -1),
                     lam_q2[l].reshape(1, -1), lam_k2[l].reshape(1, -1), subln_g[l].reshape(1, -1),
                     lam_init=lam_init, **attn_tiles)
        oc = _attn_c(qkv_ac, fcum, **attn_tiles)
        obs, lses = [], []
        groups = (qkv_b0.reshape(bsz, 1, seq, 3 * B_W), qkv_b1, qkv_b2)
        for g, (qkv_g, nb) in enumerate(zip(groups, DILATED_BLOCKS)):
            o_g, lse_g = _attn_b_group(qkv_g, bias_b[g], g, nb)
            obs.append(o_g)
            lses.append(lse_g)

        x2d = _merge(oa.reshape(rows, A_W), obs, lses, oc.reshape(rows, C_W), x2d, mod3, g_mix, w_gate,
                     w_br_a[l].astype(BF16), w_br_b[l].astype(BF16), w_br_c[l].astype(BF16),
                     w_out[l].astype(BF16), seq=seq, tm=ROWS_PROJ)

        g_ffn = norm_ffn_g[l].reshape(1, d)
        if l % 2 == 0:
            j = l // 2
            x2d = _ffn(x2d, mod3, g_ffn, w_ff_gate[j].astype(BF16), w_ff_up[j].astype(BF16),
                       w_ff_down[j].astype(BF16), seq=seq, tm=ROWS_PROJ)
        else:
            j = l // 2
            fused_final = l == depth - 1
            x2d = _moe(x2d, mod3, g_ffn, w_router[j], w_exp_gate[j].astype(BF16),
                       w_exp_up[j].astype(BF16), w_exp_down[j].astype(BF16), final_g,
                       seq=seq, final=fused_final)

    if not fused_final:
        x2d = _final_norm(x2d, final_g, tm=ROWS_ROUTE)
    return x2d.reshape(bsz, seq, d)
```

```python
import functools
import math

import jax
import jax.numpy as jnp
from jax import lax
from jax.experimental import pallas as pl
from jax.experimental.pallas import tpu as pltpu

F32 = jnp.float32
BF16 = jnp.bfloat16

D_MODEL = 1024
HEAD_DIM = 64
LANES = 128
A_HEADS = 4
A_W = A_HEADS * 2 * HEAD_DIM
B_GROUPS = ((128, 1), (512, 4), (2048, 16))
B_HEADS = 6
B_W = B_HEADS * HEAD_DIM
B_QW = len(B_GROUPS) * B_W
C_HEADS = 6
C_W = C_HEADS * HEAD_DIM
N_BRANCH = 3
BLOCK = 128
N_BUCKETS = 32
REL_MAX_DIST = 2048
EPS = 1e-6
QKV_W = 3 * A_W + 3 * B_QW + 3 * C_W
GATE_W = N_BRANCH * D_MODEL
NEG = -1e30
SCALE = HEAD_DIM ** -0.5
VMEM_LIMIT = 56 * 1024 * 1024


MXU_TILE = 256

ROWS_PROJ = 512
ROWS_ROUTE = 1024
ROWS_EXPERT = 512
ROWS_COMBINE = 256
EXPERT_FF_CHUNK = 7 * MXU_TILE
ATTN_BLOCK = 256
ATTN_SUB_TILES = 8
ATTN_BATCH_ROWS = 2
DILATED_BLOCKS = (8, 2, 1)


def _params(*sem):
    return pltpu.CompilerParams(dimension_semantics=sem, vmem_limit_bytes=VMEM_LIMIT)


def _resident(a):
    return pl.BlockSpec(a.shape, lambda *_: (0, 0), pipeline_mode=pl.Buffered(1))


def _rms_mod(x, g, sc, sh):
    y = x * lax.rsqrt(jnp.mean(x * x, axis=-1, keepdims=True) + EPS)
    return (y * g) * (1.0 + sc) + sh


def _dot(a, b):
    return jnp.dot(a, b, preferred_element_type=F32)


def _dot_nt(a, b):
    return lax.dot_general(a, b, (((1,), (1,)), ((), ())), preferred_element_type=F32)


def _silu(a):
    return a * jax.nn.sigmoid(a)


def _mod_kernel(c_ref, w_ref, b_ref, o_ref):
    a = _silu(c_ref[...]).astype(BF16)
    o_ref[0] = _dot(a, w_ref[0].astype(BF16)) + b_ref[0]


def _modulation(c, w_mod, b_mod):
    depth, d, n = w_mod.shape
    bsz = c.shape[0]
    tn = 1536
    return pl.pallas_call(
        _mod_kernel,
        grid=(depth, n // tn),
        in_specs=[
            pl.BlockSpec((bsz, d), lambda l, j: (0, 0)),
            pl.BlockSpec((1, d, tn), lambda l, j: (l, 0, j)),
            pl.BlockSpec((1, 1, tn), lambda l, j: (l, 0, j)),
        ],
        out_specs=pl.BlockSpec((1, bsz, tn), lambda l, j: (l, 0, j)),
        out_shape=jax.ShapeDtypeStruct((depth, bsz, n), F32),
        compiler_params=_params("parallel", "parallel"),
        name="modulation",
    )(c, w_mod, b_mod.reshape(depth, 1, n))


_A_BLOCKS = 3 * A_W // B_W
_B_BLOCKS = 3 * B_QW // B_W
_QKV_SRC_BLOCKS = (list(range(_A_BLOCKS))
                   + list(range(_A_BLOCKS + _B_BLOCKS, QKV_W // B_W))
                   + [_A_BLOCKS + s * len(B_GROUPS) + g for g in range(len(B_GROUPS)) for s in range(3)])


def _prep_w_in_kernel(*refs):
    n = len(_QKV_SRC_BLOCKS)
    piece_refs, x_ref, y_ref, qkv_ref, gf_ref = refs[:n], refs[n], refs[n + 1], refs[n + 2], refs[n + 3]
    for j, ref in enumerate(piece_refs):
        qkv_ref[0, :, j * B_W:(j + 1) * B_W] = ref[0].astype(BF16)
    x = x_ref[0]
    rolled = pltpu.roll(x, GATE_W - C_HEADS, 1)
    tail = pltpu.roll(y_ref[0], LANES - C_HEADS, 1)
    lane = lax.broadcasted_iota(jnp.int32, tail.shape, 1)
    gf_ref[0, :, :GATE_W - LANES] = rolled[:, :GATE_W - LANES].astype(BF16)
    gf_ref[0, :, GATE_W - LANES:GATE_W] = jnp.where(lane < LANES - C_HEADS, rolled[:, GATE_W - LANES:],
                                                     tail).astype(BF16)
    gf_ref[0, :, GATE_W:] = jnp.where(lane < C_HEADS, x[:, :LANES], 0.0).astype(BF16)


def _prep_w_in(w_in):
    depth, d, _ = w_in.shape
    tr = 256
    piece = lambda c: pl.BlockSpec((1, tr, B_W), lambda l, i: (l, i, c))
    return pl.pallas_call(
        _prep_w_in_kernel,
        grid=(depth, d // tr),
        in_specs=[piece(c) for c in _QKV_SRC_BLOCKS]
        + [pl.BlockSpec((1, tr, GATE_W), lambda l, i: (l, i, QKV_W // GATE_W)),
           pl.BlockSpec((1, tr, LANES), lambda l, i: (l, i, (QKV_W + GATE_W) // LANES))],
        out_specs=[pl.BlockSpec((1, tr, QKV_W), lambda l, i: (l, i, 0)),
                   pl.BlockSpec((1, tr, GATE_W + LANES), lambda l, i: (l, i, 0))],
        out_shape=[jax.ShapeDtypeStruct((depth, d, QKV_W), BF16),
                   jax.ShapeDtypeStruct((depth, d, GATE_W + LANES), BF16)],
        compiler_params=_params("parallel", "parallel"),
        name="prep_w_in",
    )(*([w_in] * (len(_QKV_SRC_BLOCKS) + 2)))


def _qkv_kernel(x_ref, mod_ref, g_ref, w_ref, wf_ref, ac_ref, b0_ref, b1_ref, b2_ref, f_ref, h_scr, *, tm):
    m = mod_ref[0]
    h = _rms_mod(x_ref[...], g_ref[...], m[1:2], m[0:1])
    nc = h_scr.shape[0]
    for c in range(nc):
        h_scr[c] = h[:, c * LANES:(c + 1) * LANES]
    hb = h.astype(BF16)
    n_ac = ac_ref.shape[1]
    n_b = b0_ref.shape[1]
    f_ref[...] = _dot(hb, wf_ref[...])
    ac_ref[...] = _dot(hb, w_ref[:, 0:n_ac]).astype(BF16)
    b0_ref[...] = _dot(hb, w_ref[:, n_ac:n_ac + n_b]).astype(BF16)
    for gi, ref in ((1, b1_ref), (2, b2_ref)):
        dil = B_GROUPS[gi][1]
        per = tm // dil
        hp = jnp.concatenate(
            [jnp.concatenate([h_scr[c, pl.ds(r, per, stride=dil), :] for c in range(nc)], axis=1).astype(BF16)
             for r in range(dil)], axis=0)
        y = _dot(hp, w_ref[:, n_ac + gi * n_b:n_ac + (gi + 1) * n_b]).astype(BF16)
        for r in range(dil):
            ref[0, r] = y[r * per:(r + 1) * per]


def _qkv_proj(x2d, mod3, g, w, w_f, *, bsz, seq, tm):
    rows, d = x2d.shape
    per_b = seq // tm
    n_b = 3 * B_W
    n_ac = w.shape[1] - 3 * n_b
    dil1, dil2 = B_GROUPS[1][1], B_GROUPS[2][1]
    strided = lambda dil: pl.BlockSpec((1, dil, tm // dil, n_b), lambda i: (i // per_b, 0, i % per_b, 0))
    return pl.pallas_call(
        functools.partial(_qkv_kernel, tm=tm),
        grid=(rows // tm,),
        in_specs=[
            pl.BlockSpec((tm, d), lambda i: (i, 0)),
            pl.BlockSpec((1, 6, d), lambda i: (i // per_b, 0, 0)),
            pl.BlockSpec((1, d), lambda i: (0, 0)),
            _resident(w), _resident(w_f),
        ],
        out_specs=[pl.BlockSpec((tm, n_ac), lambda i: (i, 0)), pl.BlockSpec((tm, n_b), lambda i: (i, 0)),
                   strided(dil1), strided(dil2), pl.BlockSpec((tm, LANES), lambda i: (i, 0))],
        out_shape=[jax.ShapeDtypeStruct((rows, n_ac), BF16), jax.ShapeDtypeStruct((rows, n_b), BF16),
                   jax.ShapeDtypeStruct((bsz, dil1, seq // dil1, n_b), BF16),
                   jax.ShapeDtypeStruct((bsz, dil2, seq // dil2, n_b), BF16),
                   jax.ShapeDtypeStruct((rows, LANES), F32)],
        scratch_shapes=[pltpu.VMEM((d // LANES, tm, LANES), F32)],
        compiler_params=_params("parallel"),
        name="qkv_proj",
    )(x2d, mod3, g, w, w_f)


def _fcum_kernel(f_ref, b_ref, o_ref):
    z = f_ref[0].T[:8] + b_ref[...]
    x = jnp.minimum(z, 0.0) - jnp.log1p(jnp.exp(-jnp.abs(z)))
    s = x.shape[1]
    lane = lax.broadcasted_iota(jnp.int32, x.shape, 1)
    k = 1
    while k < s:
        x = x + jnp.where(lane >= k, pltpu.roll(x, k, 1), 0.0)
        k *= 2
    o_ref[0] = x


def _forget_cumsum(gf, b_f8, *, col_block):
    bsz, seq, _ = gf.shape
    return pl.pallas_call(
        _fcum_kernel,
        grid=(bsz,),
        in_specs=[
            pl.BlockSpec((1, seq, LANES), lambda b: (b, 0, col_block)),
            pl.BlockSpec((8, 1), lambda b: (0, 0)),
        ],
        out_specs=pl.BlockSpec((1, 8, seq), lambda b: (b, 0, 0)),
        out_shape=jax.ShapeDtypeStruct((bsz, 8, seq), F32),
        compiler_params=_params("parallel"),
        name="forget_cumsum",
    )(gf, b_f8)


def _half_masks(q):
    lane = lax.broadcasted_iota(jnp.int32, q.shape, 1)
    zero = jnp.zeros_like(q)
    return jnp.where(lane < HEAD_DIM, q, zero), jnp.where(lane >= HEAD_DIM, q, zero)


def _flash_init(first, v_ref, vext_scr, m_scr, acc_scr):
    @pl.when(first)
    def _():
        for bb in range(vext_scr.shape[0]):
            vext_scr[bb, :, :LANES] = v_ref[bb]
            vext_scr[bb, :, LANES:] = jnp.ones((vext_scr.shape[1], LANES), BF16)

    m_scr[...] = jnp.full(m_scr.shape, -jnp.inf, F32)
    acc_scr[...] = jnp.zeros(acc_scr.shape, F32)


def _lane_tile(a, n):
    return a if n == 1 else jnp.concatenate([a] * n, axis=1)


def _flash_update(s, vext, rows, m_scr, acc_scr):
    m_prev = m_scr[rows]
    m_new = jnp.maximum(m_prev, jnp.max(s, axis=-1, keepdims=True))
    alpha = jnp.exp(m_prev - m_new)
    p = jnp.exp(s - _lane_tile(m_new, s.shape[1] // LANES))
    acc_scr[rows] = _lane_tile(alpha, 2) * acc_scr[rows] + _dot(p.astype(BF16), vext)
    m_scr[rows] = m_new


def _flash_result(acc_scr):
    acc = acc_scr[...]
    return acc[:, :LANES] / acc[:, LANES:]


def _attn_a_kernel(q_ref, k_ref, v_ref, bias_ref, lq1, lk1, lq2, lk2, sg_ref, o_ref,
                   vext_scr, m_scr, acc_scr, *, t, nsb, nbb, lam_init):
    qi = pl.program_id(2)
    _flash_init(qi == 0, v_ref, vext_scr, m_scr, acc_scr)
    qh = [[_half_masks(q_ref[bb, sb * t:(sb + 1) * t, :] * SCALE) for sb in range(nsb)] for bb in range(nbb)]
    chain_rows = lambda bb, sb, hh: pl.ds(((bb * nsb + sb) * 2 + hh) * t, t)

    def step(kb, nkb, plan):
        off = pl.multiple_of(kb * t, t)
        for bb in range(nbb):
            kblk = k_ref[bb, pl.ds(off, nkb * t), :]
            vext = vext_scr[bb, pl.ds(off, nkb * t), :]
            for sb, deltas in plan:
                for hh in range(2):
                    bias = [bias_ref[0, d, hh * t:(hh + 1) * t, :] for d in deltas]
                    s = _dot_nt(qh[bb][sb][hh], kblk) + (bias[0] if nkb == 1 else jnp.concatenate(bias, axis=1))
                    _flash_update(s, vext, chain_rows(bb, sb, hh), m_scr, acc_scr)

    def body(kb2, carry):
        first = [qi * nsb + sb - 2 * kb2 for sb in range(nsb)]
        step(2 * kb2, 2, [(sb, (first[sb], first[sb] - 1)) for sb in range(nsb)])
        return carry

    if q_ref.shape[1] < k_ref.shape[1]:
        lax.fori_loop(0, qi * (nsb // 2), body, 0)
    for j in range(0, nsb, 2):
        step(qi * nsb + j, 1, [(j, (0,))])
        step(qi * nsb + j, 2, [(sb, (sb - j, sb - j - 1)) for sb in range(j + 1, nsb)])

    o = _flash_result(acc_scr)
    lam = (jnp.exp(jnp.sum(lq1[...] * lk1[...], axis=-1, keepdims=True))
           - jnp.exp(jnp.sum(lq2[...] * lk2[...], axis=-1, keepdims=True)) + lam_init)
    for bb in range(nbb):
        for sb in range(nsb):
            c0 = (bb * nsb + sb) * 2 * t
            d = o[c0:c0 + t] - lam * o[c0 + t:c0 + 2 * t]
            y = d * lax.rsqrt(jnp.mean(d * d, axis=-1, keepdims=True) + EPS)
            o_ref[bb, sb * t:(sb + 1) * t, :] = ((y * sg_ref[...]) * (1.0 - lam_init)).astype(o_ref.dtype)


def _flash_scratch(seq, t, nsb, nbb):
    chains = 2 * nsb * nbb
    return [pltpu.VMEM((nbb, seq, 2 * LANES), BF16), pltpu.VMEM((chains * t, LANES), F32),
            pltpu.VMEM((chains * t, 2 * LANES), F32)]


def _attn_a(qkv, bias_a, lq1, lk1, lq2, lk2, subln_g, *, lam_init, t, nsb, nbb):
    bsz, seq, _ = qkv.shape
    tq = t * nsb
    assert bsz % nbb == 0 and seq % tq == 0 and nsb % 2 == 0
    vec = lambda n: pl.BlockSpec((1, n), lambda h, b, i: (0, 0))
    return pl.pallas_call(
        functools.partial(_attn_a_kernel, t=t, nsb=nsb, nbb=nbb, lam_init=lam_init),
        grid=(A_HEADS, bsz // nbb, seq // tq),
        in_specs=[
            pl.BlockSpec((nbb, tq, LANES), lambda h, b, i: (b, i, h)),
            pl.BlockSpec((nbb, seq, LANES), lambda h, b, i: (b, 0, A_HEADS + h)),
            pl.BlockSpec((nbb, seq, LANES), lambda h, b, i: (b, 0, 2 * A_HEADS + h)),
            pl.BlockSpec((1, seq // t, 2 * t, t), lambda h, b, i: (h, 0, 0, 0)),
            vec(HEAD_DIM), vec(HEAD_DIM), vec(HEAD_DIM), vec(HEAD_DIM), vec(LANES),
        ],
        out_specs=pl.BlockSpec((nbb, tq, LANES), lambda h, b, i: (b, i, h)),
        out_shape=jax.ShapeDtypeStruct((bsz, seq, A_W), BF16),
        scratch_shapes=_flash_scratch(seq, t, nsb, nbb),
        compiler_params=_params("parallel", "parallel", "arbitrary"),
        name="attn_diff",
    )(qkv, qkv, qkv, bias_a, lq1, lk1, lq2, lk2, subln_g)


def _attn_c_kernel(q_ref, k_ref, v_ref, f_ref, o_ref, vext_scr, m_scr, acc_scr, *, t, nsb, nbb):
    qi = pl.program_id(2)
    _flash_init(qi == 0, v_ref, vext_scr, m_scr, acc_scr)
    qh = [[_half_masks(q_ref[bb, sb * t:(sb + 1) * t, :] * SCALE) for sb in range(nsb)] for bb in range(nbb)]
    chain_rows = lambda bb, sb, hh: pl.ds(((bb * nsb + sb) * 2 + hh) * t, t)
    q_off = pl.multiple_of(qi * (t * nsb), t * nsb)
    f_anchor = [[f_ref[bb, 0, :, pl.ds(pl.multiple_of(q_off + sb * t, t), LANES)][:, :1] for sb in range(nsb)]
                for bb in range(nbb)]

    def causal(nkb):
        r = lax.broadcasted_iota(jnp.int32, (t, nkb * t), 0)
        c = lax.broadcasted_iota(jnp.int32, (t, nkb * t), 1)
        return r + (nkb - 1) * t >= c

    def step(kb, nkb, sbs, diag_sb):
        off = pl.multiple_of(kb * t, t)
        for bb in range(nbb):
            kblk = k_ref[bb, pl.ds(off, nkb * t), :]
            vext = vext_scr[bb, pl.ds(off, nkb * t), :]
            f_keys = [f_ref[bb, 0, hh:hh + 1, pl.ds(off, nkb * t)] for hh in range(2)]
            for sb in sbs:
                for hh in range(2):
                    s = _dot_nt(qh[bb][sb][hh], kblk) + (f_anchor[bb][sb][hh:hh + 1] - f_keys[hh])
                    if sb == diag_sb:
                        s = jnp.where(causal(nkb), s, NEG)
                    _flash_update(s, vext, chain_rows(bb, sb, hh), m_scr, acc_scr)

    def body(kb2, carry):
        step(2 * kb2, 2, range(nsb), None)
        return carry

    if q_ref.shape[1] < k_ref.shape[1]:
        lax.fori_loop(0, qi * (nsb // 2), body, 0)
    for j in range(0, nsb, 2):
        step(qi * nsb + j, 1, [j], j)
        step(qi * nsb + j, 2, range(j + 1, nsb), j + 1)

    o = _flash_result(acc_scr)
    lane = lax.broadcasted_iota(jnp.int32, (t, LANES), 1)
    for bb in range(nbb):
        for sb in range(nsb):
            c0 = (bb * nsb + sb) * 2 * t
            pair = jnp.where(lane < HEAD_DIM, o[c0:c0 + t], o[c0 + t:c0 + 2 * t])
            o_ref[bb, sb * t:(sb + 1) * t, :] = pair.astype(o_ref.dtype)


def _attn_c(qkv, fcum, *, t, nsb, nbb):
    bsz, seq, _ = qkv.shape
    tq = t * nsb
    assert bsz % nbb == 0 and seq % tq == 0 and nsb % 2 == 0
    pairs = C_HEADS // 2
    q0 = 3 * A_W // LANES
    return pl.pallas_call(
        functools.partial(_attn_c_kernel, t=t, nsb=nsb, nbb=nbb),
        grid=(pairs, bsz // nbb, seq // tq),
        in_specs=[
            pl.BlockSpec((nbb, tq, LANES), lambda p, b, i: (b, i, q0 + p)),
            pl.BlockSpec((nbb, seq, LANES), lambda p, b, i: (b, 0, q0 + pairs + p)),
            pl.BlockSpec((nbb, seq, LANES), lambda p, b, i: (b, 0, q0 + 2 * pairs + p)),
            pl.BlockSpec((nbb, 1, 2, seq), lambda p, b, i: (b, p, 0, 0)),
        ],
        out_specs=pl.BlockSpec((nbb, tq, LANES), lambda p, b, i: (b, i, p)),
        out_shape=jax.ShapeDtypeStruct((bsz, seq, C_W), BF16),
        scratch_shapes=_flash_scratch(seq, t, nsb, nbb),
        compiler_params=_params("parallel", "parallel", "arbitrary"),
        name="attn_forget",
    )(qkv, qkv, qkv, fcum)


def _attn_b_kernel(q_ref, kp_ref, kc_ref, vp_ref, vc_ref, bias_ref, o_ref, lse_ref, *, dil, nb):
    n = pl.program_id(1)
    lane = lax.broadcasted_iota(jnp.int32, (BLOCK, LANES), 1)
    first_variant = jnp.minimum(n, 1)

    def residue(r):
        for hp in range(B_HEADS // 2):
            cols = slice(hp * LANES, (hp + 1) * LANES)
            kcat = jnp.concatenate([kp_ref[0, r, :, cols], kc_ref[0, r, :, cols]], axis=0)
            vcat = jnp.concatenate([vp_ref[0, r, :, cols], vc_ref[0, r, :, cols]], axis=0)
            for jb in range(nb):
                qh = _half_masks(q_ref[0, r, jb * BLOCK:(jb + 1) * BLOCK, cols] * SCALE)
                kwin = kcat[jb * BLOCK:(jb + 2) * BLOCK]
                vwin = vcat[jb * BLOCK:(jb + 2) * BLOCK]
                variant = first_variant if jb == 0 else 1
                outs, lses = [], []
                for hh in range(2):
                    s = _dot_nt(qh[hh], kwin) + bias_ref[variant, hp, hh * BLOCK:(hh + 1) * BLOCK, :]
                    m = jnp.max(s, axis=-1, keepdims=True)
                    e = jnp.exp(s - m)
                    den = jnp.sum(e, axis=-1, keepdims=True)
                    outs.append(_dot(e.astype(BF16), vwin) / den)
                    lses.append(jnp.broadcast_to(m + jnp.log(den), (BLOCK, LANES)))
                if dil == 1:
                    rows = pl.ds(jb * BLOCK, BLOCK)
                else:
                    rows = pl.ds(jb * BLOCK * dil + r, BLOCK, stride=dil)
                o_ref[0, hp, rows, :] = jnp.where(lane < HEAD_DIM, outs[0], outs[1])
                lse_ref[0, hp, rows, :] = jnp.where(lane < HEAD_DIM, lses[0], lses[1])

    unroll = min(dil, 4)
    if dil == unroll:
        for r in range(dil):
            residue(r)
    else:
        def body(i, carry):
            for j in range(unroll):
                residue(i * unroll + j)
            return carry

        lax.fori_loop(0, dil // unroll, body, 0)


def _attn_b_group(qkv_g, bias_g, g, nb):
    bsz, dil, m_len, _ = qkv_g.shape
    tb = BLOCK * nb
    cur = lambda c: pl.BlockSpec((1, dil, tb, B_W), lambda b, n: (b, 0, n, c))
    prev = lambda c: pl.BlockSpec((1, dil, BLOCK, B_W), lambda b, n: (b, 0, jnp.maximum(n * nb - 1, 0), c))
    pairs = B_HEADS // 2
    out_spec = pl.BlockSpec((1, pairs, tb * dil, LANES), lambda b, n: (b, 0, n, 0))
    out_sds = jax.ShapeDtypeStruct((bsz, pairs, m_len * dil, LANES), F32)
    return pl.pallas_call(
        functools.partial(_attn_b_kernel, dil=dil, nb=nb),
        grid=(bsz, m_len // tb),
        in_specs=[cur(0), prev(1), cur(1), prev(2), cur(2),
                  pl.BlockSpec(bias_g.shape, lambda b, n: (0, 0, 0, 0))],
        out_specs=[out_spec, out_spec],
        out_shape=[out_sds, out_sds],
        compiler_params=_params("parallel", "arbitrary"),
        name=f"attn_dilated_g{g}",
    )(qkv_g, qkv_g, qkv_g, qkv_g, qkv_g, bias_g)


def _merge_kernel(oa_ref, ob0, ob1, ob2, ls0, ls1, ls2, oc_ref, x_ref, mod_ref, g_ref,
                  wg_ref, wa_ref, wb_ref, wc_ref, wo_ref, o_ref):
    x = x_ref[...]
    m = mod_ref[0]
    hb = _rms_mod(x, g_ref[...], m[1:2], m[0:1]).astype(BF16)
    parts = []
    for hp in range(B_HEADS // 2):
        l0, l1, l2 = ls0[0, hp], ls1[0, hp], ls2[0, hp]
        mx = jnp.maximum(jnp.maximum(l0, l1), l2)
        e0, e1, e2 = jnp.exp(l0 - mx), jnp.exp(l1 - mx), jnp.exp(l2 - mx)
        den = e0 + e1 + e2
        parts.append((e0 / den) * ob0[0, hp] + (e1 / den) * ob1[0, hp] + (e2 / den) * ob2[0, hp])
    ob = jnp.concatenate(parts, axis=1)
    d = D_MODEL
    gate = lambda k: jax.nn.sigmoid(_dot(hb, wg_ref[:, k * d:(k + 1) * d]))
    merged = (gate(0) * _dot(oa_ref[...], wa_ref[...])
              + gate(1) * _dot(ob.astype(BF16), wb_ref[...])
              + gate(2) * _dot(oc_ref[...], wc_ref[...]))
    y = _dot(merged.astype(BF16), wo_ref[...])
    o_ref[...] = x + m[2:3] * y


def _merge(oa, obs, lses, oc, x2d, mod3, g, w_gate, wa, wb, wc, wo, *, seq, tm):
    rows, d = x2d.shape
    per_b = seq // tm
    row = lambda w: pl.BlockSpec((tm, w), lambda i: (i, 0))
    paired = pl.BlockSpec((1, B_HEADS // 2, tm, LANES), lambda i: (i // per_b, 0, i % per_b, 0))
    return pl.pallas_call(
        _merge_kernel,
        grid=(rows // tm,),
        in_specs=[row(A_W)] + [paired] * 6 + [row(C_W), row(d),
                  pl.BlockSpec((1, 6, d), lambda i: (i // per_b, 0, 0)),
                  pl.BlockSpec((1, d), lambda i: (0, 0)),
                  _resident(w_gate), _resident(wa), _resident(wb), _resident(wc), _resident(wo)],
        out_specs=row(d),
        out_shape=jax.ShapeDtypeStruct((rows, d), F32),
        compiler_params=_params("parallel"),
        name="merge_outproj",
    )(oa, *obs, *lses, oc, x2d, mod3, g, w_gate, wa, wb, wc, wo)


def _ffn_kernel(x_ref, mod_ref, g_ref, wg_ref, wu_ref, wd_ref, o_ref, *, bounds):
    x = x_ref[...]
    m = mod_ref[0]
    h = _rms_mod(x, g_ref[...], m[4:5], m[3:4]).astype(BF16)
    acc = None
    for lo, hi in bounds:
        act = (_silu(_dot(h, wg_ref[:, lo:hi])) * _dot(h, wu_ref[:, lo:hi])).astype(BF16)
        part = _dot(act, wd_ref[lo:hi, :])
        acc = part if acc is None else acc + part
    o_ref[...] = x + m[5:6] * acc


def _ffn(x2d, mod3, g, wg, wu, wd, *, seq, tm):
    rows, d = x2d.shape
    dff = wg.shape[1]
    half = (dff // MXU_TILE + 1) // 2 * MXU_TILE
    per_b = seq // tm
    return pl.pallas_call(
        functools.partial(_ffn_kernel, bounds=((0, half), (half, dff))),
        grid=(rows // tm,),
        in_specs=[
            pl.BlockSpec((tm, d), lambda i: (i, 0)),
            pl.BlockSpec((1, 6, d), lambda i: (i // per_b, 0, 0)),
            pl.BlockSpec((1, d), lambda i: (0, 0)),
            _resident(wg), _resident(wu), _resident(wd),
        ],
        out_specs=pl.BlockSpec((tm, d), lambda i: (i, 0)),
        out_shape=jax.ShapeDtypeStruct((rows, d), F32),
        compiler_params=_params("parallel"),
        name="ffn_dense",
    )(x2d, mod3, g, wg, wu, wd)


def _route_kernel(x_ref, mod_ref, g_ref, wr_ref, tri_ref, h_ref, ridx_ref, rw_ref, cnt_ref, carry_scr):
    i = pl.program_id(0)

    @pl.when(i == 0)
    def _():
        carry_scr[...] = jnp.zeros(carry_scr.shape, F32)

    m = mod_ref[0]
    h = _rms_mod(x_ref[...], g_ref[...], m[4:5], m[3:4])
    _store_row_tiles(h_ref, h)
    w = wr_ref[...]
    h_hi, w_hi = h.astype(BF16), w.astype(BF16)
    h_lo = (h - h_hi.astype(F32)).astype(BF16)
    w_lo = (w - w_hi.astype(F32)).astype(BF16)
    logits = _dot_nt(w_hi, h_hi) + (_dot_nt(w_hi, h_lo) + _dot_nt(w_lo, h_hi))
    idx = lax.broadcasted_iota(jnp.int32, logits.shape, 0)
    n = logits.shape[0]
    m1 = jnp.max(logits, axis=0, keepdims=True)
    i1 = jnp.min(jnp.where(logits == m1, idx, n), axis=0, keepdims=True)
    first = idx == i1
    rest = jnp.where(first, -jnp.inf, logits)
    m2 = jnp.max(rest, axis=0, keepdims=True)
    i2 = jnp.min(jnp.where(rest == m2, idx, n), axis=0, keepdims=True)
    second = idx == i2
    e = jnp.exp(m2 - m1)
    den = 1.0 + e
    onehot = jnp.where(first | second, 1.0, 0.0)
    before = _dot(onehot.astype(BF16), tri_ref[...]) - onehot + carry_scr[...]
    rank1 = jnp.sum(jnp.where(first, before, 0.0), axis=0, keepdims=True)
    rank2 = jnp.sum(jnp.where(second, before, 0.0), axis=0, keepdims=True)
    carry_scr[...] += jnp.sum(onehot, axis=1, keepdims=True)
    picks = jnp.where(idx == 0, i1.astype(F32), jnp.where(idx == 1, i2.astype(F32),
                      jnp.where(idx == 2, rank1, jnp.where(idx == 3, rank2, 0.0))))
    weights = jnp.where(idx == 0, 1.0 / den, jnp.where(idx == 1, e / den, 0.0))
    ridx_ref[...] = picks.T.astype(jnp.int32)
    rw_ref[...] = weights.T
    cnt_ref[...] = carry_scr[...]


def _route(x2d, mod3, g, wr, *, seq, tm):
    rows, d = x2d.shape
    ne = wr.shape[1]
    per_b = seq // tm
    tri = (jnp.arange(tm)[:, None] <= jnp.arange(tm)[None, :]).astype(BF16)
    row = lambda w: pl.BlockSpec((tm, w), lambda i: (i, 0))
    return pl.pallas_call(
        _route_kernel,
        grid=(rows // tm,),
        in_specs=[row(d), pl.BlockSpec((1, 6, d), lambda i: (i // per_b, 0, 0)),
                  pl.BlockSpec((1, d), lambda i: (0, 0)), pl.BlockSpec((ne, d), lambda i: (0, 0)),
                  pl.BlockSpec((tm, tm), lambda i: (0, 0))],
        out_specs=[pl.BlockSpec((tm, d // LANES, LANES), lambda i: (i, 0, 0)), row(ne), row(ne),
                   pl.BlockSpec((ne, 1), lambda i: (0, 0))],
        out_shape=[jax.ShapeDtypeStruct((rows, d // LANES, LANES), F32),
                   jax.ShapeDtypeStruct((rows, ne), jnp.int32),
                   jax.ShapeDtypeStruct((rows, ne), F32), jax.ShapeDtypeStruct((ne, 1), F32)],
        scratch_shapes=[pltpu.VMEM((ne, 1), F32)],
        compiler_params=_params("arbitrary"),
        name="moe_route",
    )(x2d, mod3, g, wr.T, tri)


def _store_row_tiles(ref, val):
    for c in range(ref.shape[-2]):
        ref[:, c, :] = val[:, c * LANES:(c + 1) * LANES]


ROW_PITCH = D_MODEL // LANES + 1


ROW_TILE = D_MODEL // LANES


def _gather_rows(idx_of, n, src_hbm, dst, sem):
    for j in range(n):
        start = pl.multiple_of(idx_of(j) * ROW_TILE, ROW_TILE)
        pltpu.make_async_copy(src_hbm.at[pl.ds(start, ROW_TILE)], dst.at[pl.ds(j * ROW_PITCH, ROW_TILE)],
                              sem).start()


def _gather_wait(n, src_hbm, dst, sem):
    pltpu.make_async_copy(src_hbm.at[pl.ds(0, n * ROW_TILE)], dst.at[pl.ds(0, n * ROW_TILE)], sem).wait()


def _load_gathered(buf, n):
    return jnp.concatenate([buf[pl.ds(c, n, stride=ROW_PITCH), :] for c in range(ROW_TILE)], axis=1)


def _expert_kernel(te_ref, pos_ref, nu_ref, h_hbm, zeros_hbm, wg_ref, wu_ref, wd_ref, y_ref,
                   xbuf, xb_scr, acc_scr, src_smem, sem, *, nf, r):
    i = pl.program_id(0)
    f = pl.program_id(1)
    active = i < nu_ref[0]
    slot = i % 2

    def gather(tile, s):
        _gather_rows(lambda j: src_smem[tile * r + j], r, h_hbm, xbuf.at[s], sem.at[s])

    @pl.when((i == 0) & (f == 0))
    def _():
        clear = pltpu.make_async_copy(zeros_hbm, src_smem, sem.at[0])
        clear.start()
        clear.wait()

        def place(p, carry):
            src_smem[pos_ref[p]] = p >> 1
            return carry

        lax.fori_loop(0, pos_ref.shape[0], place, 0, unroll=8)
        gather(0, 0)

    @pl.when(active & (f == 0))
    def _():
        _gather_wait(r, h_hbm, xbuf.at[slot], sem.at[slot])

        @pl.when(i + 1 < nu_ref[0])
        def _():
            gather(i + 1, 1 - slot)

        xb_scr[...] = _load_gathered(xbuf.at[slot], r).astype(BF16)
        acc_scr[...] = jnp.zeros(acc_scr.shape, F32)

    @pl.when(active)
    def _():
        xb = xb_scr[...]
        act = (_silu(_dot(xb, wg_ref[0])) * _dot(xb, wu_ref[0])).astype(BF16)
        acc_scr[...] += _dot(act, wd_ref[0])

    @pl.when(f == nf - 1)
    def _():
        y_ref[...] = jnp.where(active, acc_scr[...], 0.0)


def _experts(h2, tile_expert, pos, n_used, wg, wu, wd, *, r, tf):
    ne, d, dff = wg.shape
    nf = dff // tf
    nt = tile_expert.shape[0]
    p_rows = nt * r
    fsel = lambda i, f, nu: jnp.where(i < nu[0], f, nf - 1)
    grid_spec = pltpu.PrefetchScalarGridSpec(
        num_scalar_prefetch=3,
        grid=(nt, nf),
        in_specs=[
            pl.BlockSpec(memory_space=pl.ANY),
            pl.BlockSpec(memory_space=pl.ANY),
            pl.BlockSpec((1, d, tf), lambda i, f, te, sp, nu: (te[i], 0, fsel(i, f, nu))),
            pl.BlockSpec((1, d, tf), lambda i, f, te, sp, nu: (te[i], 0, fsel(i, f, nu))),
            pl.BlockSpec((1, tf, d), lambda i, f, te, sp, nu: (te[i], fsel(i, f, nu), 0)),
        ],
        out_specs=pl.BlockSpec((r, d), lambda i, f, te, sp, nu: (i, 0)),
        scratch_shapes=[pltpu.VMEM((2, r * ROW_PITCH, LANES), F32), pltpu.VMEM((r, d), BF16),
                        pltpu.VMEM((r, d), F32), pltpu.SMEM((p_rows,), jnp.int32),
                        pltpu.SemaphoreType.DMA((2,))],
    )
    return pl.pallas_call(
        functools.partial(_expert_kernel, nf=nf, r=r),
        grid_spec=grid_spec,
        out_shape=jax.ShapeDtypeStruct((p_rows, d), F32),
        compiler_params=_params("arbitrary", "arbitrary"),
        name="moe_experts",
    )(tile_expert, pos, n_used, h2, jnp.zeros((p_rows,), jnp.int32), wg, wu, wd)


COMBINE_SLOTS = 3


def _combine_kernel(pos_ref, y_hbm, x_ref, rw_ref, mod_ref, fg_ref, o_ref, ybuf, sem, *, tm, final):
    i = pl.program_id(0)
    n = pl.num_programs(0)
    ahead = COMBINE_SLOTS - 1
    slot = i % COMBINE_SLOTS

    def gather(tile, s):
        for k in range(2):
            for j in range(tm):
                row = pos_ref[(tile * tm + j) * 2 + k]
                pltpu.make_async_copy(y_hbm.at[pl.ds(row, 1)], ybuf.at[s, k, pl.ds(j, 1)], sem.at[s]).start()

    @pl.when(i == 0)
    def _():
        for t0 in range(ahead):
            @pl.when(t0 < n)
            def _():
                gather(t0, t0)

    for k in range(2):
        pltpu.make_async_copy(y_hbm.at[pl.ds(0, tm)], ybuf.at[slot, k], sem.at[slot]).wait()

    @pl.when(i + ahead < n)
    def _():
        gather(i + ahead, (i + ahead) % COMBINE_SLOTS)

    w = rw_ref[...]
    f = w[:, 0:1] * ybuf[slot, 0] + w[:, 1:2] * ybuf[slot, 1]
    x = x_ref[...] + mod_ref[0][5:6] * f
    if final:
        x = (x * lax.rsqrt(jnp.mean(x * x, axis=-1, keepdims=True) + EPS)) * fg_ref[...]
    o_ref[...] = x


def _combine(pos, y, x2d, rw, mod3, final_g, *, seq, tm, final):
    rows, d = x2d.shape
    per_b = seq // tm
    grid_spec = pltpu.PrefetchScalarGridSpec(
        num_scalar_prefetch=1,
        grid=(rows // tm,),
        in_specs=[
            pl.BlockSpec(memory_space=pl.ANY),
            pl.BlockSpec((tm, d), lambda i, p: (i, 0)),
            pl.BlockSpec((tm, rw.shape[1]), lambda i, p: (i, 0)),
            pl.BlockSpec((1, 6, d), lambda i, p: (i // per_b, 0, 0)),
            pl.BlockSpec((1, d), lambda i, p: (0, 0)),
        ],
        out_specs=pl.BlockSpec((tm, d), lambda i, p: (i, 0)),
        scratch_shapes=[pltpu.VMEM((COMBINE_SLOTS, 2, tm, d), F32),
                        pltpu.SemaphoreType.DMA((COMBINE_SLOTS,))],
    )
    return pl.pallas_call(
        functools.partial(_combine_kernel, tm=tm, final=final),
        grid_spec=grid_spec,
        out_shape=jax.ShapeDtypeStruct((rows, d), F32),
        compiler_params=_params("arbitrary"),
        name="moe_combine",
    )(pos, y, x2d, rw, mod3, final_g)


def _moe(x2d, mod3, g, wr, wg, wu, wd, final_g, *, seq, final):
    rows, d = x2d.shape
    ne = wr.shape[1]
    r = ROWS_EXPERT
    h2, ridx, rw, cnt = _route(x2d, mod3, g, wr, seq=seq, tm=ROWS_ROUTE)
    counts = cnt[:, 0].astype(jnp.int32)
    tiles_e = (counts + r - 1) // r
    tile_end = jnp.cumsum(tiles_e)
    start = (tile_end - tiles_e) * r
    pos = (jnp.take(start, ridx[:, 0:2], axis=0) + ridx[:, 2:4]).reshape(-1).astype(jnp.int32)
    nt = (2 * rows) // r + ne
    n_used = tile_end[-1:]
    tile_idx = jnp.minimum(jnp.arange(nt), n_used[0] - 1)
    tile_expert = jnp.sum(tile_idx[:, None] >= tile_end[None, :], axis=1)
    y = _experts(h2.reshape(-1, LANES), tile_expert.astype(jnp.int32), pos, n_used.astype(jnp.int32),
                 wg, wu, wd, r=r, tf=EXPERT_FF_CHUNK)
    return _combine(pos, y, x2d, rw, mod3, final_g, seq=seq, tm=ROWS_COMBINE, final=final)


def _final_norm_kernel(x_ref, g_ref, o_ref):
    x = x_ref[...]
    o_ref[...] = (x * lax.rsqrt(jnp.mean(x * x, axis=-1, keepdims=True) + EPS)) * g_ref[...]


def _final_norm(x2d, g, *, tm):
    rows, d = x2d.shape
    return pl.pallas_call(
        _final_norm_kernel,
        grid=(rows // tm,),
        in_specs=[pl.BlockSpec((tm, d), lambda i: (i, 0)), pl.BlockSpec((1, d), lambda i: (0, 0))],
        out_specs=pl.BlockSpec((tm, d), lambda i: (i, 0)),
        out_shape=jax.ShapeDtypeStruct((rows, d), F32),
        compiler_params=_params("parallel"),
        name="final_norm",
    )(x2d, g)


def _t5_bucket(dist):
    n = jnp.maximum(dist, 0)
    max_exact = N_BUCKETS // 2
    nf = jnp.maximum(n, 1).astype(F32)
    large = max_exact + (jnp.log(nf / max_exact) / math.log(REL_MAX_DIST / max_exact)
                         * (N_BUCKETS - max_exact)).astype(jnp.int32)
    large = jnp.minimum(large, N_BUCKETS - 1)
    return jnp.where(n < max_exact, n, large)


def _bias_tiles_a(rel_bias, seq, t):
    nq = seq // t
    ncol = 2 * A_HEADS
    tab = rel_bias[:, :ncol][_t5_bucket(jnp.arange(seq))].astype(F32).T
    vneg = jnp.full((ncol, t), NEG, F32)
    v = jnp.concatenate([vneg, tab], axis=1)
    u = jnp.concatenate([v[:, 1:seq + 1][:, ::-1], vneg[:, :1], v[:, seq + 1:seq + t][:, ::-1]], axis=1)

    def toeplitz_kernel(u_ref, o_ref):
        x = jnp.broadcast_to(u_ref[0], (t, seq + t))
        r = pltpu.roll(x, 0, 1, stride=1, stride_axis=0)
        for delta in range(nq):
            c0 = (nq - 1 - delta) * t
            o_ref[0, delta] = r[:, c0:c0 + t]

    return pl.pallas_call(
        toeplitz_kernel,
        grid=(ncol,),
        in_specs=[pl.BlockSpec((1, 1, seq + t), lambda c: (c, 0, 0))],
        out_specs=pl.BlockSpec((1, nq, t, t), lambda c: (c // 2, 0, c % 2, 0)),
        out_shape=jax.ShapeDtypeStruct((A_HEADS, nq, 2 * t, t), F32),
        compiler_params=_params("parallel"),
        name="bias_tiles_diff",
    )(u.reshape(ncol, 1, seq + t))


def _bias_tiles_b(rel_bias):
    ng = len(B_GROUPS)
    period = 3 * BLOCK
    rows = []
    for g, (win, dil) in enumerate(B_GROUPS):
        n_back = win // dil
        tab = rel_bias[:, 2 * A_HEADS + g * B_HEADS:2 * A_HEADS + (g + 1) * B_HEADS]
        vals = tab[_t5_bucket(jnp.arange(n_back, -1, -1) * dil)].astype(F32).T
        rows.append(jnp.concatenate([vals, jnp.full((B_HEADS, period - n_back - 1), NEG, F32)], axis=1))
    u = jnp.concatenate(rows, axis=0)

    def toeplitz_kernel(u_ref, o_ref):
        x = jnp.broadcast_to(u_ref[0], (BLOCK, period))
        r = pltpu.roll(x, 0, 1, stride=1, stride_axis=0)[:, :2 * BLOCK]
        col = lax.broadcasted_iota(jnp.int32, r.shape, 1)
        o_ref[0, 0, 0] = jnp.where(col >= BLOCK, r, NEG)
        o_ref[0, 1, 0] = r

    return pl.pallas_call(
        toeplitz_kernel,
        grid=(ng * B_HEADS,),
        in_specs=[pl.BlockSpec((1, 1, period), lambda c: (c, 0, 0))],
        out_specs=pl.BlockSpec((1, 2, 1, BLOCK, 2 * BLOCK),
                               lambda c: (c // B_HEADS, 0, (c % B_HEADS) // 2, c % 2, 0)),
        out_shape=jax.ShapeDtypeStruct((ng, 2, B_HEADS // 2, 2 * BLOCK, 2 * BLOCK), F32),
        compiler_params=_params("parallel"),
        name="bias_tiles_dilated",
    )(u.reshape(ng * B_HEADS, 1, period))


def kernel(x, c, norm_mix_g, norm_ffn_g, w_mod, b_mod, w_in, b_forget, lam_q1, lam_k1, lam_q2, lam_k2,
           subln_g, rel_bias, w_br_a, w_br_b, w_br_c, w_out, w_ff_gate, w_ff_up, w_ff_down, w_router,
           w_exp_gate, w_exp_up, w_exp_down, final_norm_g):
    bsz, seq, d = x.shape
    depth = w_mod.shape[0]
    rows = bsz * seq
    x2d = x.reshape(rows, d)
    attn_tiles = dict(t=ATTN_BLOCK, nsb=ATTN_SUB_TILES, nbb=ATTN_BATCH_ROWS)

    mod = _modulation(c, w_mod, b_mod)
    bias_a = _bias_tiles_a(rel_bias, seq, ATTN_BLOCK)
    bias_b = _bias_tiles_b(rel_bias)
    w_qkv_all, w_gf_all = _prep_w_in(w_in)

    final_g = final_norm_g.reshape(1, d)
    fused_final = False
    for l in range(depth):
        lam_init = 0.8 - 0.6 * math.exp(-0.3 * l)
        mod3 = mod[l].reshape(bsz, 6, d)
        w_qkv, w_gate, w_f = w_qkv_all[l], w_gf_all[l, :, :GATE_W], w_gf_all[l, :, GATE_W:]

        g_mix = norm_mix_g[l].reshape(1, d)
        qkv_ac, qkv_b0, qkv_b1, qkv_b2, f_logit = _qkv_proj(x2d, mod3, g_mix, w_qkv, w_f,
                                                            bsz=bsz, seq=seq, tm=ROWS_PROJ)
        qkv_ac = qkv_ac.reshape(bsz, seq, -1)

        b_f8 = jnp.pad(b_forget[l], (0, 8 - C_HEADS)).reshape(8, 1)
        fcum = _forget_cumsum(f_logit.reshape(bsz, seq, LANES), b_f8, col_block=0)
        fcum = fcum[:, :C_HEADS].reshape(bsz, C_HEADS // 2, 2, seq)

        oa = _attn_a(qkv_ac, bias_a, lam_q1[l].reshape(1, -1), lam_k1[l].reshape(1, -1),
                     lam_q2[l].reshape(1, -1), lam_k2[l].reshape(1, -1), subln_g[l].reshape(1, -1),
                     lam_init=lam_init, **attn_tiles)
        oc = _attn_c(qkv_ac, fcum, **attn_tiles)
        obs, lses = [], []
        groups = (qkv_b0.reshape(bsz, 1, seq, 3 * B_W), qkv_b1, qkv_b2)
        for g, (qkv_g, nb) in enumerate(zip(groups, DILATED_BLOCKS)):
            o_g, lse_g = _attn_b_group(qkv_g, bias_b[g], g, nb)
            obs.append(o_g)
            lses.append(lse_g)

        x2d = _merge(oa.reshape(rows, A_W), obs, lses, oc.reshape(rows, C_W), x2d, mod3, g_mix, w_gate,
                     w_br_a[l].astype(BF16), w_br_b[l].astype(BF16), w_br_c[l].astype(BF16),
                     w_out[l].astype(BF16), seq=seq, tm=ROWS_PROJ)

        g_ffn = norm_ffn_g[l].reshape(1, d)
        if l % 2 == 0:
            j = l // 2
            x2d = _ffn(x2d, mod3, g_ffn, w_ff_gate[j].astype(BF16), w_ff_up[j].astype(BF16),
                       w_ff_down[j].astype(BF16), seq=seq, tm=ROWS_PROJ)
        else:
            j = l // 2
            fused_final = l == depth - 1
            x2d = _moe(x2d, mod3, g_ffn, w_router[j], w_exp_gate[j].astype(BF16),
                       w_exp_up[j].astype(BF16), w_exp_down[j].astype(BF16), final_g,
                       seq=seq, final=fused_final)

    if not fused_final:
        x2d = _final_norm(x2d, final_g, tm=ROWS_ROUTE)
    return x2d.reshape(bsz, seq, d)
```

```python
import functools
import math

import jax
import jax.numpy as jnp
from jax import lax
from jax.experimental import pallas as pl
from jax.experimental.pallas import tpu as pltpu

F32 = jnp.float32
BF16 = jnp.bfloat16

D_MODEL = 1024
HEAD_DIM = 64
LANES = 128
A_HEADS = 4
A_W = A_HEADS * 2 * HEAD_DIM
B_GROUPS = ((128, 1), (512, 4), (2048, 16))
B_HEADS = 6
B_W = B_HEADS * HEAD_DIM
B_QW = len(B_GROUPS) * B_W
C_HEADS = 6
C_W = C_HEADS * HEAD_DIM
N_BRANCH = 3
BLOCK = 128
N_BUCKETS = 32
REL_MAX_DIST = 2048
EPS = 1e-6
QKV_W = 3 * A_W + 3 * B_QW + 3 * C_W
GATE_W = N_BRANCH * D_MODEL
NEG = -1e30
SCALE = HEAD_DIM ** -0.5
VMEM_LIMIT = 56 * 1024 * 1024


MXU_TILE = 256

ROWS_PROJ = 512
ROWS_ROUTE = 1024
ROWS_EXPERT = 512
ROWS_COMBINE = 256
EXPERT_FF_CHUNK = 7 * MXU_TILE
ATTN_BLOCK = 256
ATTN_SUB_TILES = 8
ATTN_BATCH_ROWS = 2
DILATED_BLOCKS = (16, 4, 1)


def _params(*sem):
    return pltpu.CompilerParams(dimension_semantics=sem, vmem_limit_bytes=VMEM_LIMIT)


def _resident(a):
    return pl.BlockSpec(a.shape, lambda *_: (0, 0), pipeline_mode=pl.Buffered(1))


def _rms_mod(x, g, sc, sh):
    y = x * lax.rsqrt(jnp.mean(x * x, axis=-1, keepdims=True) + EPS)
    return (y * g) * (1.0 + sc) + sh


def _dot(a, b):
    return jnp.dot(a, b, preferred_element_type=F32)


def _dot_nt(a, b):
    return lax.dot_general(a, b, (((1,), (1,)), ((), ())), preferred_element_type=F32)


def _silu(a):
    return a * jax.nn.sigmoid(a)


def _mod_kernel(c_ref, w_ref, b_ref, o_ref):
    a = _silu(c_ref[...]).astype(BF16)
    o_ref[0] = _dot(a, w_ref[0].astype(BF16)) + b_ref[0]


def _modulation(c, w_mod, b_mod):
    depth, d, n = w_mod.shape
    bsz = c.shape[0]
    tn = 1536
    return pl.pallas_call(
        _mod_kernel,
        grid=(depth, n // tn),
        in_specs=[
            pl.BlockSpec((bsz, d), lambda l, j: (0, 0)),
            pl.BlockSpec((1, d, tn), lambda l, j: (l, 0, j)),
            pl.BlockSpec((1, 1, tn), lambda l, j: (l, 0, j)),
        ],
        out_specs=pl.BlockSpec((1, bsz, tn), lambda l, j: (l, 0, j)),
        out_shape=jax.ShapeDtypeStruct((depth, bsz, n), F32),
        compiler_params=_params("parallel", "parallel"),
        name="modulation",
    )(c, w_mod, b_mod.reshape(depth, 1, n))


_A_BLOCKS = 3 * A_W // B_W
_B_BLOCKS = 3 * B_QW // B_W
_QKV_SRC_BLOCKS = (list(range(_A_BLOCKS))
                   + list(range(_A_BLOCKS + _B_BLOCKS, QKV_W // B_W))
                   + [_A_BLOCKS + s * len(B_GROUPS) + g for g in range(len(B_GROUPS)) for s in range(3)])


def _prep_w_in_kernel(*refs):
    n = len(_QKV_SRC_BLOCKS)
    piece_refs, x_ref, y_ref, qkv_ref, gf_ref = refs[:n], refs[n], refs[n + 1], refs[n + 2], refs[n + 3]
    for j, ref in enumerate(piece_refs):
        qkv_ref[0, :, j * B_W:(j + 1) * B_W] = ref[0].astype(BF16)
    x = x_ref[0]
    rolled = pltpu.roll(x, GATE_W - C_HEADS, 1)
    tail = pltpu.roll(y_ref[0], LANES - C_HEADS, 1)
    lane = lax.broadcasted_iota(jnp.int32, tail.shape, 1)
    gf_ref[0, :, :GATE_W - LANES] = rolled[:, :GATE_W - LANES].astype(BF16)
    gf_ref[0, :, GATE_W - LANES:GATE_W] = jnp.where(lane < LANES - C_HEADS, rolled[:, GATE_W - LANES:],
                                                     tail).astype(BF16)
    gf_ref[0, :, GATE_W:] = jnp.where(lane < C_HEADS, x[:, :LANES], 0.0).astype(BF16)


def _prep_w_in(w_in):
    depth, d, _ = w_in.shape
    tr = 256
    piece = lambda c: pl.BlockSpec((1, tr, B_W), lambda l, i: (l, i, c))
    return pl.pallas_call(
        _prep_w_in_kernel,
        grid=(depth, d // tr),
        in_specs=[piece(c) for c in _QKV_SRC_BLOCKS]
        + [pl.BlockSpec((1, tr, GATE_W), lambda l, i: (l, i, QKV_W // GATE_W)),
           pl.BlockSpec((1, tr, LANES), lambda l, i: (l, i, (QKV_W + GATE_W) // LANES))],
        out_specs=[pl.BlockSpec((1, tr, QKV_W), lambda l, i: (l, i, 0)),
                   pl.BlockSpec((1, tr, GATE_W + LANES), lambda l, i: (l, i, 0))],
        out_shape=[jax.ShapeDtypeStruct((depth, d, QKV_W), BF16),
                   jax.ShapeDtypeStruct((depth, d, GATE_W + LANES), BF16)],
        compiler_params=_params("parallel", "parallel"),
        name="prep_w_in",
    )(*([w_in] * (len(_QKV_SRC_BLOCKS) + 2)))


def _qkv_kernel(x_ref, mod_ref, g_ref, w_ref, wf_ref, ac_ref, b0_ref, b1_ref, b2_ref, f_ref, h_scr, *, tm):
    m = mod_ref[0]
    h = _rms_mod(x_ref[...], g_ref[...], m[1:2], m[0:1])
    nc = h_scr.shape[0]
    for c in range(nc):
        h_scr[c] = h[:, c * LANES:(c + 1) * LANES]
    hb = h.astype(BF16)
    n_ac = ac_ref.shape[1]
    n_b = b0_ref.shape[1]
    f_ref[...] = _dot(hb, wf_ref[...])
    ac_ref[...] = _dot(hb, w_ref[:, 0:n_ac]).astype(BF16)
    b0_ref[...] = _dot(hb, w_ref[:, n_ac:n_ac + n_b]).astype(BF16)
    for gi, ref in ((1, b1_ref), (2, b2_ref)):
        dil = B_GROUPS[gi][1]
        per = tm // dil
        hp = jnp.concatenate(
            [jnp.concatenate([h_scr[c, pl.ds(r, per, stride=dil), :] for c in range(nc)], axis=1).astype(BF16)
             for r in range(dil)], axis=0)
        y = _dot(hp, w_ref[:, n_ac + gi * n_b:n_ac + (gi + 1) * n_b]).astype(BF16)
        for r in range(dil):
            ref[0, r] = y[r * per:(r + 1) * per]


def _qkv_proj(x2d, mod3, g, w, w_f, *, bsz, seq, tm):
    rows, d = x2d.shape
    per_b = seq // tm
    n_b = 3 * B_W
    n_ac = w.shape[1] - 3 * n_b
    dil1, dil2 = B_GROUPS[1][1], B_GROUPS[2][1]
    strided = lambda dil: pl.BlockSpec((1, dil, tm // dil, n_b), lambda i: (i // per_b, 0, i % per_b, 0))
    return pl.pallas_call(
        functools.partial(_qkv_kernel, tm=tm),
        grid=(rows // tm,),
        in_specs=[
            pl.BlockSpec((tm, d), lambda i: (i, 0)),
            pl.BlockSpec((1, 6, d), lambda i: (i // per_b, 0, 0)),
            pl.BlockSpec((1, d), lambda i: (0, 0)),
            _resident(w), _resident(w_f),
        ],
        out_specs=[pl.BlockSpec((tm, n_ac), lambda i: (i, 0)), pl.BlockSpec((tm, n_b), lambda i: (i, 0)),
                   strided(dil1), strided(dil2), pl.BlockSpec((tm, LANES), lambda i: (i, 0))],
        out_shape=[jax.ShapeDtypeStruct((rows, n_ac), BF16), jax.ShapeDtypeStruct((rows, n_b), BF16),
                   jax.ShapeDtypeStruct((bsz, dil1, seq // dil1, n_b), BF16),
                   jax.ShapeDtypeStruct((bsz, dil2, seq // dil2, n_b), BF16),
                   jax.ShapeDtypeStruct((rows, LANES), F32)],
        scratch_shapes=[pltpu.VMEM((d // LANES, tm, LANES), F32)],
        compiler_params=_params("parallel"),
        name="qkv_proj",
    )(x2d, mod3, g, w, w_f)


def _fcum_kernel(f_ref, b_ref, o_ref):
    z = f_ref[0].T[:8] + b_ref[...]
    x = jnp.minimum(z, 0.0) - jnp.log1p(jnp.exp(-jnp.abs(z)))
    s = x.shape[1]
    lane = lax.broadcasted_iota(jnp.int32, x.shape, 1)
    k = 1
    while k < s:
        x = x + jnp.where(lane >= k, pltpu.roll(x, k, 1), 0.0)
        k *= 2
    o_ref[0] = x


def _forget_cumsum(gf, b_f8, *, col_block):
    bsz, seq, _ = gf.shape
    return pl.pallas_call(
        _fcum_kernel,
        grid=(bsz,),
        in_specs=[
            pl.BlockSpec((1, seq, LANES), lambda b: (b, 0, col_block)),
            pl.BlockSpec((8, 1), lambda b: (0, 0)),
        ],
        out_specs=pl.BlockSpec((1, 8, seq), lambda b: (b, 0, 0)),
        out_shape=jax.ShapeDtypeStruct((bsz, 8, seq), F32),
        compiler_params=_params("parallel"),
        name="forget_cumsum",
    )(gf, b_f8)


def _half_masks(q):
    lane = lax.broadcasted_iota(jnp.int32, q.shape, 1)
    zero = jnp.zeros_like(q)
    return jnp.where(lane < HEAD_DIM, q, zero), jnp.where(lane >= HEAD_DIM, q, zero)


def _flash_init(first, v_ref, vext_scr, m_scr, acc_scr):
    @pl.when(first)
    def _():
        for bb in range(vext_scr.shape[0]):
            vext_scr[bb, :, :LANES] = v_ref[bb]
            vext_scr[bb, :, LANES:] = jnp.ones((vext_scr.shape[1], LANES), BF16)

    m_scr[...] = jnp.full(m_scr.shape, -jnp.inf, F32)
    acc_scr[...] = jnp.zeros(acc_scr.shape, F32)


def _lane_tile(a, n):
    return a if n == 1 else jnp.concatenate([a] * n, axis=1)


def _flash_update(s, vext, rows, m_scr, acc_scr):
    m_prev = m_scr[rows]
    m_new = jnp.maximum(m_prev, jnp.max(s, axis=-1, keepdims=True))
    alpha = jnp.exp(m_prev - m_new)
    p = jnp.exp(s - _lane_tile(m_new, s.shape[1] // LANES))
    acc_scr[rows] = _lane_tile(alpha, 2) * acc_scr[rows] + _dot(p.astype(BF16), vext)
    m_scr[rows] = m_new


def _flash_result(acc_scr):
    acc = acc_scr[...]
    return acc[:, :LANES] / acc[:, LANES:]


def _attn_a_kernel(q_ref, k_ref, v_ref, bias_ref, lq1, lk1, lq2, lk2, sg_ref, o_ref,
                   vext_scr, m_scr, acc_scr, *, t, nsb, nbb, lam_init):
    qi = pl.program_id(2)
    _flash_init(qi == 0, v_ref, vext_scr, m_scr, acc_scr)
    qh = [[_half_masks(q_ref[bb, sb * t:(sb + 1) * t, :] * SCALE) for sb in range(nsb)] for bb in range(nbb)]
    chain_rows = lambda bb, sb, hh: pl.ds(((bb * nsb + sb) * 2 + hh) * t, t)

    def step(kb, nkb, plan):
        off = pl.multiple_of(kb * t, t)
        for bb in range(nbb):
            kblk = k_ref[bb, pl.ds(off, nkb * t), :]
            vext = vext_scr[bb, pl.ds(off, nkb * t), :]
            for sb, deltas in plan:
                for hh in range(2):
                    bias = [bias_ref[0, d, hh * t:(hh + 1) * t, :] for d in deltas]
                    s = _dot_nt(qh[bb][sb][hh], kblk) + (bias[0] if nkb == 1 else jnp.concatenate(bias, axis=1))
                    _flash_update(s, vext, chain_rows(bb, sb, hh), m_scr, acc_scr)

    def body(kb2, carry):
        first = [qi * nsb + sb - 2 * kb2 for sb in range(nsb)]
        step(2 * kb2, 2, [(sb, (first[sb], first[sb] - 1)) for sb in range(nsb)])
        return carry

    if q_ref.shape[1] < k_ref.shape[1]:
        lax.fori_loop(0, qi * (nsb // 2), body, 0)
    for j in range(0, nsb, 2):
        step(qi * nsb + j, 1, [(j, (0,))])
        step(qi * nsb + j, 2, [(sb, (sb - j, sb - j - 1)) for sb in range(j + 1, nsb)])

    o = _flash_result(acc_scr)
    lam = (jnp.exp(jnp.sum(lq1[...] * lk1[...], axis=-1, keepdims=True))
           - jnp.exp(jnp.sum(lq2[...] * lk2[...], axis=-1, keepdims=True)) + lam_init)
    for bb in range(nbb):
        for sb in range(nsb):
            c0 = (bb * nsb + sb) * 2 * t
            d = o[c0:c0 + t] - lam * o[c0 + t:c0 + 2 * t]
            y = d * lax.rsqrt(jnp.mean(d * d, axis=-1, keepdims=True) + EPS)
            o_ref[bb, sb * t:(sb + 1) * t, :] = ((y * sg_ref[...]) * (1.0 - lam_init)).astype(o_ref.dtype)


def _flash_scratch(seq, t, nsb, nbb):
    chains = 2 * nsb * nbb
    return [pltpu.VMEM((nbb, seq, 2 * LANES), BF16), pltpu.VMEM((chains * t, LANES), F32),
            pltpu.VMEM((chains * t, 2 * LANES), F32)]


def _attn_a(qkv, bias_a, lq1, lk1, lq2, lk2, subln_g, *, lam_init, t, nsb, nbb):
    bsz, seq, _ = qkv.shape
    tq = t * nsb
    assert bsz % nbb == 0 and seq % tq == 0 and nsb % 2 == 0
    vec = lambda n: pl.BlockSpec((1, n), lambda h, b, i: (0, 0))
    return pl.pallas_call(
        functools.partial(_attn_a_kernel, t=t, nsb=nsb, nbb=nbb, lam_init=lam_init),
        grid=(A_HEADS, bsz // nbb, seq // tq),
        in_specs=[
            pl.BlockSpec((nbb, tq, LANES), lambda h, b, i: (b, i, h)),
            pl.BlockSpec((nbb, seq, LANES), lambda h, b, i: (b, 0, A_HEADS + h)),
            pl.BlockSpec((nbb, seq, LANES), lambda h, b, i: (b, 0, 2 * A_HEADS + h)),
            pl.BlockSpec((1, seq // t, 2 * t, t), lambda h, b, i: (h, 0, 0, 0)),
            vec(HEAD_DIM), vec(HEAD_DIM), vec(HEAD_DIM), vec(HEAD_DIM), vec(LANES),
        ],
        out_specs=pl.BlockSpec((nbb, tq, LANES), lambda h, b, i: (b, i, h)),
        out_shape=jax.ShapeDtypeStruct((bsz, seq, A_W), BF16),
        scratch_shapes=_flash_scratch(seq, t, nsb, nbb),
        compiler_params=_params("parallel", "parallel", "arbitrary"),
        name="attn_diff",
    )(qkv, qkv, qkv, bias_a, lq1, lk1, lq2, lk2, subln_g)


def _attn_c_kernel(q_ref, k_ref, v_ref, f_ref, o_ref, vext_scr, m_scr, acc_scr, *, t, nsb, nbb):
    qi = pl.program_id(2)
    _flash_init(qi == 0, v_ref, vext_scr, m_scr, acc_scr)
    qh = [[_half_masks(q_ref[bb, sb * t:(sb + 1) * t, :] * SCALE) for sb in range(nsb)] for bb in range(nbb)]
    chain_rows = lambda bb, sb, hh: pl.ds(((bb * nsb + sb) * 2 + hh) * t, t)
    q_off = pl.multiple_of(qi * (t * nsb), t * nsb)
    f_anchor = [[f_ref[bb, 0, :, pl.ds(pl.multiple_of(q_off + sb * t, t), LANES)][:, :1] for sb in range(nsb)]
                for bb in range(nbb)]

    def causal(nkb):
        r = lax.broadcasted_iota(jnp.int32, (t, nkb * t), 0)
        c = lax.broadcasted_iota(jnp.int32, (t, nkb * t), 1)
        return r + (nkb - 1) * t >= c

    def step(kb, nkb, sbs, diag_sb):
        off = pl.multiple_of(kb * t, t)
        for bb in range(nbb):
            kblk = k_ref[bb, pl.ds(off, nkb * t), :]
            vext = vext_scr[bb, pl.ds(off, nkb * t), :]
            f_keys = [f_ref[bb, 0, hh:hh + 1, pl.ds(off, nkb * t)] for hh in range(2)]
            for sb in sbs:
                for hh in range(2):
                    s = _dot_nt(qh[bb][sb][hh], kblk) + (f_anchor[bb][sb][hh:hh + 1] - f_keys[hh])
                    if sb == diag_sb:
                        s = jnp.where(causal(nkb), s, NEG)
                    _flash_update(s, vext, chain_rows(bb, sb, hh), m_scr, acc_scr)

    def body(kb2, carry):
        step(2 * kb2, 2, range(nsb), None)
        return carry

    if q_ref.shape[1] < k_ref.shape[1]:
        lax.fori_loop(0, qi * (nsb // 2), body, 0)
    for j in range(0, nsb, 2):
        step(qi * nsb + j, 1, [j], j)
        step(qi * nsb + j, 2, range(j + 1, nsb), j + 1)

    o = _flash_result(acc_scr)
    lane = lax.broadcasted_iota(jnp.int32, (t, LANES), 1)
    for bb in range(nbb):
        for sb in range(nsb):
            c0 = (bb * nsb + sb) * 2 * t
            pair = jnp.where(lane < HEAD_DIM, o[c0:c0 + t], o[c0 + t:c0 + 2 * t])
            o_ref[bb, sb * t:(sb + 1) * t, :] = pair.astype(o_ref.dtype)


def _attn_c(qkv, fcum, *, t, nsb, nbb):
    bsz, seq, _ = qkv.shape
    tq = t * nsb
    assert bsz % nbb == 0 and seq % tq == 0 and nsb % 2 == 0
    pairs = C_HEADS // 2
    q0 = 3 * A_W // LANES
    return pl.pallas_call(
        functools.partial(_attn_c_kernel, t=t, nsb=nsb, nbb=nbb),
        grid=(pairs, bsz // nbb, seq // tq),
        in_specs=[
            pl.BlockSpec((nbb, tq, LANES), lambda p, b, i: (b, i, q0 + p)),
            pl.BlockSpec((nbb, seq, LANES), lambda p, b, i: (b, 0, q0 + pairs + p)),
            pl.BlockSpec((nbb, seq, LANES), lambda p, b, i: (b, 0, q0 + 2 * pairs + p)),
            pl.BlockSpec((nbb, 1, 2, seq), lambda p, b, i: (b, p, 0, 0)),
        ],
        out_specs=pl.BlockSpec((nbb, tq, LANES), lambda p, b, i: (b, i, p)),
        out_shape=jax.ShapeDtypeStruct((bsz, seq, C_W), BF16),
        scratch_shapes=_flash_scratch(seq, t, nsb, nbb),
        compiler_params=_params("parallel", "parallel", "arbitrary"),
        name="attn_forget",
    )(qkv, qkv, qkv, fcum)


def _attn_b_kernel(q_ref, kp_ref, kc_ref, vp_ref, vc_ref, bias_ref, o_ref, lse_ref, *, dil, nb):
    n = pl.program_id(1)
    lane = lax.broadcasted_iota(jnp.int32, (BLOCK, LANES), 1)
    first_variant = jnp.minimum(n, 1)

    def residue(r):
        for hp in range(B_HEADS // 2):
            cols = slice(hp * LANES, (hp + 1) * LANES)
            kcat = jnp.concatenate([kp_ref[0, r, :, cols], kc_ref[0, r, :, cols]], axis=0)
            vcat = jnp.concatenate([vp_ref[0, r, :, cols], vc_ref[0, r, :, cols]], axis=0)
            for jb in range(nb):
                qh = _half_masks(q_ref[0, r, jb * BLOCK:(jb + 1) * BLOCK, cols] * SCALE)
                kwin = kcat[jb * BLOCK:(jb + 2) * BLOCK]
                vwin = vcat[jb * BLOCK:(jb + 2) * BLOCK]
                variant = first_variant if jb == 0 else 1
                outs, lses = [], []
                for hh in range(2):
                    s = _dot_nt(qh[hh], kwin) + bias_ref[variant, hp, hh * BLOCK:(hh + 1) * BLOCK, :]
                    m = jnp.max(s, axis=-1, keepdims=True)
                    e = jnp.exp(s - m)
                    den = jnp.sum(e, axis=-1, keepdims=True)
                    outs.append(_dot(e.astype(BF16), vwin) / den)
                    lses.append(jnp.broadcast_to(m + jnp.log(den), (BLOCK, LANES)))
                if dil == 1:
                    rows = pl.ds(jb * BLOCK, BLOCK)
                else:
                    rows = pl.ds(jb * BLOCK * dil + r, BLOCK, stride=dil)
                o_ref[0, hp, rows, :] = jnp.where(lane < HEAD_DIM, outs[0], outs[1])
                lse_ref[0, hp, rows, :] = jnp.where(lane < HEAD_DIM, lses[0], lses[1])

    unroll = min(dil, 4)
    if dil == unroll:
        for r in range(dil):
            residue(r)
    else:
        def body(i, carry):
            for j in range(unroll):
                residue(i * unroll + j)
            return carry

        lax.fori_loop(0, dil // unroll, body, 0)


def _attn_b_group(qkv_g, bias_g, g, nb):
    bsz, dil, m_len, _ = qkv_g.shape
    tb = BLOCK * nb
    cur = lambda c: pl.BlockSpec((1, dil, tb, B_W), lambda b, n: (b, 0, n, c))
    prev = lambda c: pl.BlockSpec((1, dil, BLOCK, B_W), lambda b, n: (b, 0, jnp.maximum(n * nb - 1, 0), c))
    pairs = B_HEADS // 2
    out_spec = pl.BlockSpec((1, pairs, tb * dil, LANES), lambda b, n: (b, 0, n, 0))
    out_sds = jax.ShapeDtypeStruct((bsz, pairs, m_len * dil, LANES), F32)
    return pl.pallas_call(
        functools.partial(_attn_b_kernel, dil=dil, nb=nb),
        grid=(bsz, m_len // tb),
        in_specs=[cur(0), prev(1), cur(1), prev(2), cur(2),
                  pl.BlockSpec(bias_g.shape, lambda b, n: (0, 0, 0, 0))],
        out_specs=[out_spec, out_spec],
        out_shape=[out_sds, out_sds],
        compiler_params=_params("parallel", "arbitrary"),
        name=f"attn_dilated_g{g}",
    )(qkv_g, qkv_g, qkv_g, qkv_g, qkv_g, bias_g)


def _merge_kernel(oa_ref, ob0, ob1, ob2, ls0, ls1, ls2, oc_ref, x_ref, mod_ref, g_ref,
                  wg_ref, wa_ref, wb_ref, wc_ref, wo_ref, o_ref):
    x = x_ref[...]
    m = mod_ref[0]
    hb = _rms_mod(x, g_ref[...], m[1:2], m[0:1]).astype(BF16)
    parts = []
    for hp in range(B_HEADS // 2):
        l0, l1, l2 = ls0[0, hp], ls1[0, hp], ls2[0, hp]
        mx = jnp.maximum(jnp.maximum(l0, l1), l2)
        e0, e1, e2 = jnp.exp(l0 - mx), jnp.exp(l1 - mx), jnp.exp(l2 - mx)
        den = e0 + e1 + e2
        parts.append((e0 / den) * ob0[0, hp] + (e1 / den) * ob1[0, hp] + (e2 / den) * ob2[0, hp])
    ob = jnp.concatenate(parts, axis=1)
    d = D_MODEL
    gate = lambda k: jax.nn.sigmoid(_dot(hb, wg_ref[:, k * d:(k + 1) * d]))
    merged = (gate(0) * _dot(oa_ref[...], wa_ref[...])
              + gate(1) * _dot(ob.astype(BF16), wb_ref[...])
              + gate(2) * _dot(oc_ref[...], wc_ref[...]))
    y = _dot(merged.astype(BF16), wo_ref[...])
    o_ref[...] = x + m[2:3] * y


def _merge(oa, obs, lses, oc, x2d, mod3, g, w_gate, wa, wb, wc, wo, *, seq, tm):
    rows, d = x2d.shape
    per_b = seq // tm
    row = lambda w: pl.BlockSpec((tm, w), lambda i: (i, 0))
    paired = pl.BlockSpec((1, B_HEADS // 2, tm, LANES), lambda i: (i // per_b, 0, i % per_b, 0))
    return pl.pallas_call(
        _merge_kernel,
        grid=(rows // tm,),
        in_specs=[row(A_W)] + [paired] * 6 + [row(C_W), row(d),
                  pl.BlockSpec((1, 6, d), lambda i: (i // per_b, 0, 0)),
                  pl.BlockSpec((1, d), lambda i: (0, 0)),
                  _resident(w_gate), _resident(wa), _resident(wb), _resident(wc), _resident(wo)],
        out_specs=row(d),
        out_shape=jax.ShapeDtypeStruct((rows, d), F32),
        compiler_params=_params("parallel"),
        name="merge_outproj",
    )(oa, *obs, *lses, oc, x2d, mod3, g, w_gate, wa, wb, wc, wo)


def _ffn_kernel(x_ref, mod_ref, g_ref, wg_ref, wu_ref, wd_ref, o_ref, *, bounds):
    x = x_ref[...]
    m = mod_ref[0]
    h = _rms_mod(x, g_ref[...], m[4:5], m[3:4]).astype(BF16)
    acc = None
    for lo, hi in bounds:
        act = (_silu(_dot(h, wg_ref[:, lo:hi])) * _dot(h, wu_ref[:, lo:hi])).astype(BF16)
        part = _dot(act, wd_ref[lo:hi, :])
        acc = part if acc is None else acc + part
    o_ref[...] = x + m[5:6] * acc


def _ffn(x2d, mod3, g, wg, wu, wd, *, seq, tm):
    rows, d = x2d.shape
    dff = wg.shape[1]
    half = (dff // MXU_TILE + 1) // 2 * MXU_TILE
    per_b = seq // tm
    return pl.pallas_call(
        functools.partial(_ffn_kernel, bounds=((0, half), (half, dff))),
        grid=(rows // tm,),
        in_specs=[
            pl.BlockSpec((tm, d), lambda i: (i, 0)),
            pl.BlockSpec((1, 6, d), lambda i: (i // per_b, 0, 0)),
            pl.BlockSpec((1, d), lambda i: (0, 0)),
            _resident(wg), _resident(wu), _resident(wd),
        ],
        out_specs=pl.BlockSpec((tm, d), lambda i: (i, 0)),
        out_shape=jax.ShapeDtypeStruct((rows, d), F32),
        compiler_params=_params("parallel"),
        name="ffn_dense",
    )(x2d, mod3, g, wg, wu, wd)


def _route_kernel(x_ref, mod_ref, g_ref, wr_ref, tri_ref, h_ref, ridx_ref, rw_ref, cnt_ref, carry_scr):
    i = pl.program_id(0)

    @pl.when(i == 0)
    def _():
        carry_scr[...] = jnp.zeros(carry_scr.shape, F32)

    m = mod_ref[0]
    h = _rms_mod(x_ref[...], g_ref[...], m[4:5], m[3:4])
    _store_row_tiles(h_ref, h)
    w = wr_ref[...]
    h_hi, w_hi = h.astype(BF16), w.astype(BF16)
    h_lo = (h - h_hi.astype(F32)).astype(BF16)
    w_lo = (w - w_hi.astype(F32)).astype(BF16)
    logits = _dot_nt(w_hi, h_hi) + (_dot_nt(w_hi, h_lo) + _dot_nt(w_lo, h_hi))
    idx = lax.broadcasted_iota(jnp.int32, logits.shape, 0)
    n = logits.shape[0]
    m1 = jnp.max(logits, axis=0, keepdims=True)
    i1 = jnp.min(jnp.where(logits == m1, idx, n), axis=0, keepdims=True)
    first = idx == i1
    rest = jnp.where(first, -jnp.inf, logits)
    m2 = jnp.max(rest, axis=0, keepdims=True)
    i2 = jnp.min(jnp.where(rest == m2, idx, n), axis=0, keepdims=True)
    second = idx == i2
    e = jnp.exp(m2 - m1)
    den = 1.0 + e
    onehot = jnp.where(first | second, 1.0, 0.0)
    before = _dot(onehot.astype(BF16), tri_ref[...]) - onehot + carry_scr[...]
    rank1 = jnp.sum(jnp.where(first, before, 0.0), axis=0, keepdims=True)
    rank2 = jnp.sum(jnp.where(second, before, 0.0), axis=0, keepdims=True)
    carry_scr[...] += jnp.sum(onehot, axis=1, keepdims=True)
    picks = jnp.where(idx == 0, i1.astype(F32), jnp.where(idx == 1, i2.astype(F32),
                      jnp.where(idx == 2, rank1, jnp.where(idx == 3, rank2, 0.0))))
    weights = jnp.where(idx == 0, 1.0 / den, jnp.where(idx == 1, e / den, 0.0))
    ridx_ref[...] = picks.T.astype(jnp.int32)
    rw_ref[...] = weights.T
    cnt_ref[...] = carry_scr[...]


def _route(x2d, mod3, g, wr, *, seq, tm):
    rows, d = x2d.shape
    ne = wr.shape[1]
    per_b = seq // tm
    tri = (jnp.arange(tm)[:, None] <= jnp.arange(tm)[None, :]).astype(BF16)
    row = lambda w: pl.BlockSpec((tm, w), lambda i: (i, 0))
    return pl.pallas_call(
        _route_kernel,
        grid=(rows // tm,),
        in_specs=[row(d), pl.BlockSpec((1, 6, d), lambda i: (i // per_b, 0, 0)),
                  pl.BlockSpec((1, d), lambda i: (0, 0)), pl.BlockSpec((ne, d), lambda i: (0, 0)),
                  pl.BlockSpec((tm, tm), lambda i: (0, 0))],
        out_specs=[pl.BlockSpec((tm, d // LANES, LANES), lambda i: (i, 0, 0)), row(ne), row(ne),
                   pl.BlockSpec((ne, 1), lambda i: (0, 0))],
        out_shape=[jax.ShapeDtypeStruct((rows, d // LANES, LANES), F32),
                   jax.ShapeDtypeStruct((rows, ne), jnp.int32),
                   jax.ShapeDtypeStruct((rows, ne), F32), jax.ShapeDtypeStruct((ne, 1), F32)],
        scratch_shapes=[pltpu.VMEM((ne, 1), F32)],
        compiler_params=_params("arbitrary"),
        name="moe_route",
    )(x2d, mod3, g, wr.T, tri)


def _store_row_tiles(ref, val):
    for c in range(ref.shape[-2]):
        ref[:, c, :] = val[:, c * LANES:(c + 1) * LANES]


ROW_PITCH = D_MODEL // LANES + 1


ROW_TILE = D_MODEL // LANES


def _gather_rows(idx_of, n, src_hbm, dst, sem):
    for j in range(n):
        start = pl.multiple_of(idx_of(j) * ROW_TILE, ROW_TILE)
        pltpu.make_async_copy(src_hbm.at[pl.ds(start, ROW_TILE)], dst.at[pl.ds(j * ROW_PITCH, ROW_TILE)],
                              sem).start()


def _gather_wait(n, src_hbm, dst, sem):
    pltpu.make_async_copy(src_hbm.at[pl.ds(0, n * ROW_TILE)], dst.at[pl.ds(0, n * ROW_TILE)], sem).wait()


def _load_gathered(buf, n):
    return jnp.concatenate([buf[pl.ds(c, n, stride=ROW_PITCH), :] for c in range(ROW_TILE)], axis=1)


def _expert_kernel(te_ref, pos_ref, nu_ref, h_hbm, zeros_hbm, wg_ref, wu_ref, wd_ref, y_ref,
                   xbuf, xb_scr, acc_scr, src_smem, sem, *, nf, r):
    i = pl.program_id(0)
    f = pl.program_id(1)
    active = i < nu_ref[0]
    slot = i % 2

    def gather(tile, s):
        _gather_rows(lambda j: src_smem[tile * r + j], r, h_hbm, xbuf.at[s], sem.at[s])

    @pl.when((i == 0) & (f == 0))
    def _():
        clear = pltpu.make_async_copy(zeros_hbm, src_smem, sem.at[0])
        clear.start()
        clear.wait()

        def place(p, carry):
            src_smem[pos_ref[p]] = p >> 1
            return carry

        lax.fori_loop(0, pos_ref.shape[0], place, 0, unroll=8)
        gather(0, 0)

    @pl.when(active & (f == 0))
    def _():
        _gather_wait(r, h_hbm, xbuf.at[slot], sem.at[slot])

        @pl.when(i + 1 < nu_ref[0])
        def _():
            gather(i + 1, 1 - slot)

        xb_scr[...] = _load_gathered(xbuf.at[slot], r).astype(BF16)
        acc_scr[...] = jnp.zeros(acc_scr.shape, F32)

    @pl.when(active)
    def _():
        xb = xb_scr[...]
        act = (_silu(_dot(xb, wg_ref[0])) * _dot(xb, wu_ref[0])).astype(BF16)
        acc_scr[...] += _dot(act, wd_ref[0])

    @pl.when(f == nf - 1)
    def _():
        y_ref[...] = jnp.where(active, acc_scr[...], 0.0)


def _experts(h2, tile_expert, pos, n_used, wg, wu, wd, *, r, tf):
    ne, d, dff = wg.shape
    nf = dff // tf
    nt = tile_expert.shape[0]
    p_rows = nt * r
    fsel = lambda i, f, nu: jnp.where(i < nu[0], f, nf - 1)
    grid_spec = pltpu.PrefetchScalarGridSpec(
        num_scalar_prefetch=3,
        grid=(nt, nf),
        in_specs=[
            pl.BlockSpec(memory_space=pl.ANY),
            pl.BlockSpec(memory_space=pl.ANY),
            pl.BlockSpec((1, d, tf), lambda i, f, te, sp, nu: (te[i], 0, fsel(i, f, nu))),
            pl.BlockSpec((1, d, tf), lambda i, f, te, sp, nu: (te[i], 0, fsel(i, f, nu))),
            pl.BlockSpec((1, tf, d), lambda i, f, te, sp, nu: (te[i], fsel(i, f, nu), 0)),
        ],
        out_specs=pl.BlockSpec((r, d), lambda i, f, te, sp, nu: (i, 0)),
        scratch_shapes=[pltpu.VMEM((2, r * ROW_PITCH, LANES), F32), pltpu.VMEM((r, d), BF16),
                        pltpu.VMEM((r, d), F32), pltpu.SMEM((p_rows,), jnp.int32),
                        pltpu.SemaphoreType.DMA((2,))],
    )
    return pl.pallas_call(
        functools.partial(_expert_kernel, nf=nf, r=r),
        grid_spec=grid_spec,
        out_shape=jax.ShapeDtypeStruct((p_rows, d), F32),
        compiler_params=_params("arbitrary", "arbitrary"),
        name="moe_experts",
    )(tile_expert, pos, n_used, h2, jnp.zeros((p_rows,), jnp.int32), wg, wu, wd)


COMBINE_SLOTS = 3


def _combine_kernel(pos_ref, y_hbm, x_ref, rw_ref, mod_ref, fg_ref, o_ref, ybuf, sem, *, tm, final):
    i = pl.program_id(0)
    n = pl.num_programs(0)
    ahead = COMBINE_SLOTS - 1
    slot = i % COMBINE_SLOTS

    def gather(tile, s):
        for k in range(2):
            for j in range(tm):
                row = pos_ref[(tile * tm + j) * 2 + k]
                pltpu.make_async_copy(y_hbm.at[pl.ds(row, 1)], ybuf.at[s, k, pl.ds(j, 1)], sem.at[s]).start()

    @pl.when(i == 0)
    def _():
        for t0 in range(ahead):
            @pl.when(t0 < n)
            def _():
                gather(t0, t0)

    for k in range(2):
        pltpu.make_async_copy(y_hbm.at[pl.ds(0, tm)], ybuf.at[slot, k], sem.at[slot]).wait()

    @pl.when(i + ahead < n)
    def _():
        gather(i + ahead, (i + ahead) % COMBINE_SLOTS)

    w = rw_ref[...]
    f = w[:, 0:1] * ybuf[slot, 0] + w[:, 1:2] * ybuf[slot, 1]
    x = x_ref[...] + mod_ref[0][5:6] * f
    if final:
        x = (x * lax.rsqrt(jnp.mean(x * x, axis=-1, keepdims=True) + EPS)) * fg_ref[...]
    o_ref[...] = x


def _combine(pos, y, x2d, rw, mod3, final_g, *, seq, tm, final):
    rows, d = x2d.shape
    per_b = seq // tm
    grid_spec = pltpu.PrefetchScalarGridSpec(
        num_scalar_prefetch=1,
        grid=(rows // tm,),
        in_specs=[
            pl.BlockSpec(memory_space=pl.ANY),
            pl.BlockSpec((tm, d), lambda i, p: (i, 0)),
            pl.BlockSpec((tm, rw.shape[1]), lambda i, p: (i, 0)),
            pl.BlockSpec((1, 6, d), lambda i, p: (i // per_b, 0, 0)),
            pl.BlockSpec((1, d), lambda i, p: (0, 0)),
        ],
        out_specs=pl.BlockSpec((tm, d), lambda i, p: (i, 0)),
        scratch_shapes=[pltpu.VMEM((COMBINE_SLOTS, 2, tm, d), F32),
                        pltpu.SemaphoreType.DMA((COMBINE_SLOTS,))],
    )
    return pl.pallas_call(
        functools.partial(_combine_kernel, tm=tm, final=final),
        grid_spec=grid_spec,
        out_shape=jax.ShapeDtypeStruct((rows, d), F32),
        compiler_params=_params("arbitrary"),
        name="moe_combine",
    )(pos, y, x2d, rw, mod3, final_g)


def _moe(x2d, mod3, g, wr, wg, wu, wd, final_g, *, seq, final):
    rows, d = x2d.shape
    ne = wr.shape[1]
    r = ROWS_EXPERT
    h2, ridx, rw, cnt = _route(x2d, mod3, g, wr, seq=seq, tm=ROWS_ROUTE)
    counts = cnt[:, 0].astype(jnp.int32)
    tiles_e = (counts + r - 1) // r
    tile_end = jnp.cumsum(tiles_e)
    start = (tile_end - tiles_e) * r
    pos = (jnp.take(start, ridx[:, 0:2], axis=0) + ridx[:, 2:4]).reshape(-1).astype(jnp.int32)
    nt = (2 * rows) // r + ne
    n_used = tile_end[-1:]
    tile_idx = jnp.minimum(jnp.arange(nt), n_used[0] - 1)
    tile_expert = jnp.sum(tile_idx[:, None] >= tile_end[None, :], axis=1)
    y = _experts(h2.reshape(-1, LANES), tile_expert.astype(jnp.int32), pos, n_used.astype(jnp.int32),
                 wg, wu, wd, r=r, tf=EXPERT_FF_CHUNK)
    return _combine(pos, y, x2d, rw, mod3, final_g, seq=seq, tm=ROWS_COMBINE, final=final)


def _final_norm_kernel(x_ref, g_ref, o_ref):
    x = x_ref[...]
    o_ref[...] = (x * lax.rsqrt(jnp.mean(x * x, axis=-1, keepdims=True) + EPS)) * g_ref[...]


def _final_norm(x2d, g, *, tm):
    rows, d = x2d.shape
    return pl.pallas_call(
        _final_norm_kernel,
        grid=(rows // tm,),
        in_specs=[pl.BlockSpec((tm, d), lambda i: (i, 0)), pl.BlockSpec((1, d), lambda i: (0, 0))],
        out_specs=pl.BlockSpec((tm, d), lambda i: (i, 0)),
        out_shape=jax.ShapeDtypeStruct((rows, d), F32),
        compiler_params=_params("parallel"),
        name="final_norm",
    )(x2d, g)


def _t5_bucket(dist):
    n = jnp.maximum(dist, 0)
    max_exact = N_BUCKETS // 2
    nf = jnp.maximum(n, 1).astype(F32)
    large = max_exact + (jnp.log(nf / max_exact) / math.log(REL_MAX_DIST / max_exact)
                         * (N_BUCKETS - max_exact)).astype(jnp.int32)
    large = jnp.minimum(large, N_BUCKETS - 1)
    return jnp.where(n < max_exact, n, large)


def _bias_tiles_a(rel_bias, seq, t):
    nq = seq // t
    ncol = 2 * A_HEADS
    tab = rel_bias[:, :ncol][_t5_bucket(jnp.arange(seq))].astype(F32).T
    vneg = jnp.full((ncol, t), NEG, F32)
    v = jnp.concatenate([vneg, tab], axis=1)
    u = jnp.concatenate([v[:, 1:seq + 1][:, ::-1], vneg[:, :1], v[:, seq + 1:seq + t][:, ::-1]], axis=1)

    def toeplitz_kernel(u_ref, o_ref):
        x = jnp.broadcast_to(u_ref[0], (t, seq + t))
        r = pltpu.roll(x, 0, 1, stride=1, stride_axis=0)
        for delta in range(nq):
            c0 = (nq - 1 - delta) * t
            o_ref[0, delta] = r[:, c0:c0 + t]

    return pl.pallas_call(
        toeplitz_kernel,
        grid=(ncol,),
        in_specs=[pl.BlockSpec((1, 1, seq + t), lambda c: (c, 0, 0))],
        out_specs=pl.BlockSpec((1, nq, t, t), lambda c: (c // 2, 0, c % 2, 0)),
        out_shape=jax.ShapeDtypeStruct((A_HEADS, nq, 2 * t, t), F32),
        compiler_params=_params("parallel"),
        name="bias_tiles_diff",
    )(u.reshape(ncol, 1, seq + t))


def _bias_tiles_b(rel_bias):
    ng = len(B_GROUPS)
    period = 3 * BLOCK
    rows = []
    for g, (win, dil) in enumerate(B_GROUPS):
        n_back = win // dil
        tab = rel_bias[:, 2 * A_HEADS + g * B_HEADS:2 * A_HEADS + (g + 1) * B_HEADS]
        vals = tab[_t5_bucket(jnp.arange(n_back, -1, -1) * dil)].astype(F32).T
        rows.append(jnp.concatenate([vals, jnp.full((B_HEADS, period - n_back - 1), NEG, F32)], axis=1))
    u = jnp.concatenate(rows, axis=0)

    def toeplitz_kernel(u_ref, o_ref):
        x = jnp.broadcast_to(u_ref[0], (BLOCK, period))
        r = pltpu.roll(x, 0, 1, stride=1, stride_axis=0)[:, :2 * BLOCK]
        col = lax.broadcasted_iota(jnp.int32, r.shape, 1)
        o_ref[0, 0, 0] = jnp.where(col >= BLOCK, r, NEG)
        o_ref[0, 1, 0] = r

    return pl.pallas_call(
        toeplitz_kernel,
        grid=(ng * B_HEADS,),
        in_specs=[pl.BlockSpec((1, 1, period), lambda c: (c, 0, 0))],
        out_specs=pl.BlockSpec((1, 2, 1, BLOCK, 2 * BLOCK),
                               lambda c: (c // B_HEADS, 0, (c % B_HEADS) // 2, c % 2, 0)),
        out_shape=jax.ShapeDtypeStruct((ng, 2, B_HEADS // 2, 2 * BLOCK, 2 * BLOCK), F32),
        compiler_params=_params("parallel"),
        name="bias_tiles_dilated",
    )(u.reshape(ng * B_HEADS, 1, period))


def kernel(x, c, norm_mix_g, norm_ffn_g, w_mod, b_mod, w_in, b_forget, lam_q1, lam_k1, lam_q2, lam_k2,
           subln_g, rel_bias, w_br_a, w_br_b, w_br_c, w_out, w_ff_gate, w_ff_up, w_ff_down, w_router,
           w_exp_gate, w_exp_up, w_exp_down, final_norm_g):
    bsz, seq, d = x.shape
    depth = w_mod.shape[0]
    rows = bsz * seq
    x2d = x.reshape(rows, d)
    attn_tiles = dict(t=ATTN_BLOCK, nsb=ATTN_SUB_TILES, nbb=ATTN_BATCH_ROWS)

    mod = _modulation(c, w_mod, b_mod)
    bias_a = _bias_tiles_a(rel_bias, seq, ATTN_BLOCK)
    bias_b = _bias_tiles_b(rel_bias)
    w_qkv_all, w_gf_all = _prep_w_in(w_in)

    final_g = final_norm_g.reshape(1, d)
    fused_final = False
    for l in range(depth):
        lam_init = 0.8 - 0.6 * math.exp(-0.3 * l)
        mod3 = mod[l].reshape(bsz, 6, d)
        w_qkv, w_gate, w_f = w_qkv_all[l], w_gf_all[l, :, :GATE_W], w_gf_all[l, :, GATE_W:]

        g_mix = norm_mix_g[l].reshape(1, d)
        qkv_ac, qkv_b0, qkv_b1, qkv_b2, f_logit = _qkv_proj(x2d, mod3, g_mix, w_qkv, w_f,
                                                            bsz=bsz, seq=seq, tm=ROWS_PROJ)
        qkv_ac = qkv_ac.reshape(bsz, seq, -1)

        b_f8 = jnp.pad(b_forget[l], (0, 8 - C_HEADS)).reshape(8, 1)
        fcum = _forget_cumsum(f_logit.reshape(bsz, seq, LANES), b_f8, col_block=0)
        fcum = fcum[:, :C_HEADS].reshape(bsz, C_HEADS // 2, 2, seq)

        oa = _attn_a(qkv_ac, bias_a, lam_q1[l].reshape(1, -1), lam_k1[l].reshape(1, -1),
                     lam_q2[l].reshape(1, -1), lam_k2[l].reshape(1, -1), subln_g[l].reshape(1, -1),
                     lam_init=lam_init, **attn_tiles)
        oc = _attn_c(qkv_ac, fcum, **attn_tiles)
        obs, lses = [], []
        groups = (qkv_b0.reshape(bsz, 1, seq, 3 * B_W), qkv_b1, qkv_b2)
        for g, (qkv_g, nb) in enumerate(zip(groups, DILATED_BLOCKS)):
            o_g, lse_g = _attn_b_group(qkv_g, bias_b[g], g, nb)
            obs.append(o_g)
            lses.append(lse_g)

        x2d = _merge(oa.reshape(rows, A_W), obs, lses, oc.reshape(rows, C_W), x2d, mod3, g_mix, w_gate,
                     w_br_a[l].astype(BF16), w_br_b[l].astype(BF16), w_br_c[l].astype(BF16),
                     w_out[l].astype(BF16), seq=seq, tm=ROWS_PROJ)

        g_ffn = norm_ffn_g[l].reshape(1, d)
        if l % 2 == 0:
            j = l // 2
            x2d = _ffn(x2d, mod3, g_ffn, w_ff_gate[j].astype(BF16), w_ff_up[j].astype(BF16),
                       w_ff_down[j].astype(BF16), seq=seq, tm=ROWS_PROJ)
        else:
            j = l // 2
            fused_final = l == depth - 1
            x2d = _moe(x2d, mod3, g_ffn, w_router[j], w_exp_gate[j].astype(BF16),
                       w_exp_up[j].astype(BF16), w_exp_down[j].astype(BF16), final_g,
                       seq=seq, final=fused_final)

    if not fused_final:
        x2d = _final_norm(x2d, final_g, tm=ROWS_ROUTE)
    return x2d.reshape(bsz, seq, d)
```
